```python
import jax, jax.numpy as jnp
from jax import lax
import numpy as np

D_MODEL = 1024
BATCH = 8
SEQ = 8192
DEPTH = 2

PLE_DIM = 256
D_FF = 2816
D_POOL = D_MODEL
N_POOL_GROUPS = 4
POOL_GROUP = D_POOL // N_POOL_GROUPS
POOL_WINDOWS = (2, 4, 8, 16)
D_CONV = D_MODEL
CONV_K = 31
N_IN = D_POOL + 2 * D_CONV + 2 * D_MODEL
RMS_EPS = 1e-6
LN_EPS = 1e-5

kernel_name = "hybrid_pool_conformer_macaron_ple"


def rmsnorm(x, g):
    x32 = x.astype(jnp.float32)
    y = x32 * lax.rsqrt(jnp.mean(x32 * x32, axis=-1, keepdims=True) + RMS_EPS)
    return (y * g.astype(jnp.float32)).astype(x.dtype)


def layernorm(x, g, b):
    x32 = x.astype(jnp.float32)
    mu = jnp.mean(x32, axis=-1, keepdims=True)
    var = jnp.mean(jnp.square(x32 - mu), axis=-1, keepdims=True)
    y = (x32 - mu) * lax.rsqrt(var + LN_EPS)
    return (y * g.astype(jnp.float32) + b.astype(jnp.float32)).astype(x.dtype)


def swiglu(x, w_gate, w_up, w_down):
    return (jax.nn.silu(x @ w_gate) * (x @ w_up)) @ w_down


def causal_multiscale_pool(z):
    S = z.shape[1]
    cs = jnp.cumsum(z.astype(jnp.float32), axis=1)
    pos = jnp.arange(S, dtype=jnp.int32)
    outs = []
    for g, w in enumerate(POOL_WINDOWS):
        sl = slice(g * POOL_GROUP, (g + 1) * POOL_GROUP)
        cs_g = cs[..., sl]
        lower = jnp.pad(cs_g, ((0, 0), (w, 0), (0, 0)))[:, :S]
        count = jnp.minimum(pos + 1, w).astype(jnp.float32)[None, :, None]
        mean = (cs_g - lower) / count
        outs.append(mean - z[..., sl].astype(jnp.float32))
    return jnp.concatenate(outs, axis=-1).astype(z.dtype)


def causal_depthwise_conv(x, w, b):
    K, C = w.shape
    y = lax.conv_general_dilated(
        x, w[:, None, :].astype(x.dtype), window_strides=(1,), padding=[(K - 1, 0)],
        dimension_numbers=("NWC", "WIO", "NWC"), feature_group_count=C)
    return y + b


def _fwd_setup_inputs(seed: int = 0) -> dict:
    key = jax.random.key(seed)
    ks = iter(jax.random.split(key, 32))

    def nrm(shape, fan_in):
        return jax.random.normal(next(ks), shape, jnp.float32) * (fan_in ** -0.5)

    def gain(shape):
        return 1.0 + 0.02 * jax.random.normal(next(ks), shape, jnp.float32)

    def bias(shape):
        return 0.02 * jax.random.normal(next(ks), shape, jnp.float32)

    L = DEPTH
    return {
        "x": jax.random.normal(next(ks), (BATCH, SEQ, D_MODEL), jnp.float32),
        "p": jax.random.normal(next(ks), (DEPTH, BATCH, SEQ, PLE_DIM), jnp.float32),
        "ffn1_norm": gain((L, D_MODEL)),
        "ffn1_w_gate": nrm((L, D_MODEL, D_FF), D_MODEL),
        "ffn1_w_up": nrm((L, D_MODEL, D_FF), D_MODEL),
        "ffn1_w_down": nrm((L, D_FF, D_MODEL), D_FF),
        "mix_norm": gain((L, D_MODEL)),
        "w_in": nrm((L, D_MODEL, N_IN), D_MODEL),
        "pool_w": nrm((L, N_POOL_GROUPS, POOL_GROUP, POOL_GROUP), POOL_GROUP),
        "pool_scale": gain((L, D_POOL)),
        "conv_dw_w": nrm((L, CONV_K, D_CONV), CONV_K),
        "conv_dw_b": bias((L, D_CONV)),
        "conv_ln_g": gain((L, D_CONV)),
        "conv_ln_b": bias((L, D_CONV)),
        "conv_w_out": nrm((L, D_CONV, D_MODEL), D_CONV),
        "w_out": nrm((L, D_MODEL, D_MODEL), D_MODEL),
        "ffn2_norm": gain((L, D_MODEL)),
        "ffn2_w_gate": nrm((L, D_MODEL, D_FF), D_MODEL),
        "ffn2_w_up": nrm((L, D_MODEL, D_FF), D_MODEL),
        "ffn2_w_down": nrm((L, D_FF, D_MODEL), D_FF),
        "ple_norm": gain((L, D_MODEL)),
        "ple_w_gate": nrm((L, D_MODEL, D_MODEL), D_MODEL),
        "ple_w_proj": nrm((L, PLE_DIM, D_MODEL), PLE_DIM),
        "final_norm": gain((D_MODEL,)),
    }


def _fwd_reference(x, p, ffn1_norm, ffn1_w_gate, ffn1_w_up, ffn1_w_down, mix_norm, w_in,
              pool_w, pool_scale, conv_dw_w, conv_dw_b, conv_ln_g, conv_ln_b, conv_w_out,
              w_out, ffn2_norm, ffn2_w_gate, ffn2_w_up, ffn2_w_down, ple_norm, ple_w_gate,
              ple_w_proj, final_norm):
    B, S, _ = x.shape
    h = x
    split_pts = [D_POOL, D_POOL + D_CONV, D_POOL + 2 * D_CONV, D_POOL + 2 * D_CONV + D_MODEL]
    for i in range(DEPTH):
        h = h + 0.5 * swiglu(rmsnorm(h, ffn1_norm[i]), ffn1_w_gate[i], ffn1_w_up[i], ffn1_w_down[i])

        u = rmsnorm(h, mix_norm[i])
        z = u @ w_in[i]
        z_pool, z_glu_a, z_glu_g, g_pool, g_conv = jnp.split(z, split_pts, axis=-1)

        pooled = causal_multiscale_pool(z_pool).reshape(B, S, N_POOL_GROUPS, POOL_GROUP)
        a = jnp.einsum("bsgc,gcd->bsgd", pooled, pool_w[i]).reshape(B, S, D_POOL)
        a = a * pool_scale[i]

        c = z_glu_a * jax.nn.sigmoid(z_glu_g)
        c = causal_depthwise_conv(c, conv_dw_w[i], conv_dw_b[i])
        c = jax.nn.silu(layernorm(c, conv_ln_g[i], conv_ln_b[i]))
        c = c @ conv_w_out[i]

        m = jax.nn.sigmoid(g_pool) * a + jax.nn.sigmoid(g_conv) * c
        h = h + m @ w_out[i]

        h = h + 0.5 * swiglu(rmsnorm(h, ffn2_norm[i]), ffn2_w_gate[i], ffn2_w_up[i], ffn2_w_down[i])

        gate = jax.nn.sigmoid(rmsnorm(h, ple_norm[i]) @ ple_w_gate[i])
        h = h + gate * (p[i] @ ple_w_proj[i])
    return rmsnorm(h, final_norm)


import jax as _jax
import jax.numpy as _jnp

TWIN_FORMAT = 'train_step'
FWD_PARAMS = ['x', 'p', 'ffn1_norm', 'ffn1_w_gate', 'ffn1_w_up', 'ffn1_w_down', 'mix_norm', 'w_in', 'pool_w', 'pool_scale', 'conv_dw_w', 'conv_dw_b', 'conv_ln_g', 'conv_ln_b', 'conv_w_out', 'w_out', 'ffn2_norm', 'ffn2_w_gate', 'ffn2_w_up', 'ffn2_w_down', 'ple_norm', 'ple_w_gate', 'ple_w_proj', 'final_norm']
TWIN_WEIGHTS = ['ffn1_norm', 'ffn1_w_gate', 'ffn1_w_up', 'ffn1_w_down', 'mix_norm', 'w_in', 'pool_w', 'pool_scale', 'conv_dw_w', 'conv_dw_b', 'conv_ln_g', 'conv_ln_b', 'conv_w_out', 'w_out', 'ffn2_norm', 'ffn2_w_gate', 'ffn2_w_up', 'ffn2_w_down', 'ple_norm', 'ple_w_gate', 'ple_w_proj', 'final_norm']
TWIN_DIFF_INPUT = 'x'
TWIN_INPUTS = ['x', 'p', 'ffn1_norm', 'ffn1_w_gate', 'ffn1_w_up', 'ffn1_w_down', 'mix_norm', 'w_in', 'pool_w', 'pool_scale', 'conv_dw_w', 'conv_dw_b', 'conv_ln_g', 'conv_ln_b', 'conv_w_out', 'w_out', 'ffn2_norm', 'ffn2_w_gate', 'ffn2_w_up', 'ffn2_w_down', 'ple_norm', 'ple_w_gate', 'ple_w_proj', 'final_norm', 'loss_target', 'm_ffn1_norm', 'm_ffn1_w_gate', 'm_ffn1_w_up', 'm_ffn1_w_down', 'm_mix_norm', 'm_w_in', 'm_pool_w', 'm_pool_scale', 'm_conv_dw_w', 'm_conv_dw_b', 'm_conv_ln_g', 'm_conv_ln_b', 'm_conv_w_out', 'm_w_out', 'm_ffn2_norm', 'm_ffn2_w_gate', 'm_ffn2_w_up', 'm_ffn2_w_down', 'm_ple_norm', 'm_ple_w_gate', 'm_ple_w_proj', 'm_final_norm', 'v_ffn1_norm', 'v_ffn1_w_gate', 'v_ffn1_w_up', 'v_ffn1_w_down', 'v_mix_norm', 'v_w_in', 'v_pool_w', 'v_pool_scale', 'v_conv_dw_w', 'v_conv_dw_b', 'v_conv_ln_g', 'v_conv_ln_b', 'v_conv_w_out', 'v_w_out', 'v_ffn2_norm', 'v_ffn2_w_gate', 'v_ffn2_w_up', 'v_ffn2_w_down', 'v_ple_norm', 'v_ple_w_gate', 'v_ple_w_proj', 'v_final_norm']
TWIN_OUTPUTS = ['loss', 'grad_x', 'grad_ffn1_norm', 'grad_ffn1_w_gate', 'grad_ffn1_w_up', 'grad_ffn1_w_down', 'grad_mix_norm', 'grad_w_in', 'grad_pool_w', 'grad_pool_scale', 'grad_conv_dw_w', 'grad_conv_dw_b', 'grad_conv_ln_g', 'grad_conv_ln_b', 'grad_conv_w_out', 'grad_w_out', 'grad_ffn2_norm', 'grad_ffn2_w_gate', 'grad_ffn2_w_up', 'grad_ffn2_w_down', 'grad_ple_norm', 'grad_ple_w_gate', 'grad_ple_w_proj', 'grad_final_norm', 'delta_ffn1_norm', 'delta_ffn1_w_gate', 'delta_ffn1_w_up', 'delta_ffn1_w_down', 'delta_mix_norm', 'delta_w_in', 'delta_pool_w', 'delta_pool_scale', 'delta_conv_dw_w', 'delta_conv_dw_b', 'delta_conv_ln_g', 'delta_conv_ln_b', 'delta_conv_w_out', 'delta_w_out', 'delta_ffn2_norm', 'delta_ffn2_w_gate', 'delta_ffn2_w_up', 'delta_ffn2_w_down', 'delta_ple_norm', 'delta_ple_w_gate', 'delta_ple_w_proj', 'delta_final_norm', 'new_m_ffn1_norm', 'new_m_ffn1_w_gate', 'new_m_ffn1_w_up', 'new_m_ffn1_w_down', 'new_m_mix_norm', 'new_m_w_in', 'new_m_pool_w', 'new_m_pool_scale', 'new_m_conv_dw_w', 'new_m_conv_dw_b', 'new_m_conv_ln_g', 'new_m_conv_ln_b', 'new_m_conv_w_out', 'new_m_w_out', 'new_m_ffn2_norm', 'new_m_ffn2_w_gate', 'new_m_ffn2_w_up', 'new_m_ffn2_w_down', 'new_m_ple_norm', 'new_m_ple_w_gate', 'new_m_ple_w_proj', 'new_m_final_norm', 'new_v_ffn1_norm', 'new_v_ffn1_w_gate', 'new_v_ffn1_w_up', 'new_v_ffn1_w_down', 'new_v_mix_norm', 'new_v_w_in', 'new_v_pool_w', 'new_v_pool_scale', 'new_v_conv_dw_w', 'new_v_conv_dw_b', 'new_v_conv_ln_g', 'new_v_conv_ln_b', 'new_v_conv_w_out', 'new_v_w_out', 'new_v_ffn2_norm', 'new_v_ffn2_w_gate', 'new_v_ffn2_w_up', 'new_v_ffn2_w_down', 'new_v_ple_norm', 'new_v_ple_w_gate', 'new_v_ple_w_proj', 'new_v_final_norm']
TWIN_LEAF_KINDS = {'loss': 'loss', 'grad_x': 'grad_x', 'grad_ffn1_norm': 'grad_w', 'grad_ffn1_w_gate': 'grad_w', 'grad_ffn1_w_up': 'grad_w', 'grad_ffn1_w_down': 'grad_w', 'grad_mix_norm': 'grad_w', 'grad_w_in': 'grad_w', 'grad_pool_w': 'grad_w', 'grad_pool_scale': 'grad_w', 'grad_conv_dw_w': 'grad_w', 'grad_conv_dw_b': 'grad_w', 'grad_conv_ln_g': 'grad_w', 'grad_conv_ln_b': 'grad_w', 'grad_conv_w_out': 'grad_w', 'grad_w_out': 'grad_w', 'grad_ffn2_norm': 'grad_w', 'grad_ffn2_w_gate': 'grad_w', 'grad_ffn2_w_up': 'grad_w', 'grad_ffn2_w_down': 'grad_w', 'grad_ple_norm': 'grad_w', 'grad_ple_w_gate': 'grad_w', 'grad_ple_w_proj': 'grad_w', 'grad_final_norm': 'grad_w', 'delta_ffn1_norm': 'delta_w', 'delta_ffn1_w_gate': 'delta_w', 'delta_ffn1_w_up': 'delta_w', 'delta_ffn1_w_down': 'delta_w', 'delta_mix_norm': 'delta_w', 'delta_w_in': 'delta_w', 'delta_pool_w': 'delta_w', 'delta_pool_scale': 'delta_w', 'delta_conv_dw_w': 'delta_w', 'delta_conv_dw_b': 'delta_w', 'delta_conv_ln_g': 'delta_w', 'delta_conv_ln_b': 'delta_w', 'delta_conv_w_out': 'delta_w', 'delta_w_out': 'delta_w', 'delta_ffn2_norm': 'delta_w', 'delta_ffn2_w_gate': 'delta_w', 'delta_ffn2_w_up': 'delta_w', 'delta_ffn2_w_down': 'delta_w', 'delta_ple_norm': 'delta_w', 'delta_ple_w_gate': 'delta_w', 'delta_ple_w_proj': 'delta_w', 'delta_final_norm': 'delta_w', 'new_m_ffn1_norm': 'new_m', 'new_m_ffn1_w_gate': 'new_m', 'new_m_ffn1_w_up': 'new_m', 'new_m_ffn1_w_down': 'new_m', 'new_m_mix_norm': 'new_m', 'new_m_w_in': 'new_m', 'new_m_pool_w': 'new_m', 'new_m_pool_scale': 'new_m', 'new_m_conv_dw_w': 'new_m', 'new_m_conv_dw_b': 'new_m', 'new_m_conv_ln_g': 'new_m', 'new_m_conv_ln_b': 'new_m', 'new_m_conv_w_out': 'new_m', 'new_m_w_out': 'new_m', 'new_m_ffn2_norm': 'new_m', 'new_m_ffn2_w_gate': 'new_m', 'new_m_ffn2_w_up': 'new_m', 'new_m_ffn2_w_down': 'new_m', 'new_m_ple_norm': 'new_m', 'new_m_ple_w_gate': 'new_m', 'new_m_ple_w_proj': 'new_m', 'new_m_final_norm': 'new_m', 'new_v_ffn1_norm': 'new_v', 'new_v_ffn1_w_gate': 'new_v', 'new_v_ffn1_w_up': 'new_v', 'new_v_ffn1_w_down': 'new_v', 'new_v_mix_norm': 'new_v', 'new_v_w_in': 'new_v', 'new_v_pool_w': 'new_v', 'new_v_pool_scale': 'new_v', 'new_v_conv_dw_w': 'new_v', 'new_v_conv_dw_b': 'new_v', 'new_v_conv_ln_g': 'new_v', 'new_v_conv_ln_b': 'new_v', 'new_v_conv_w_out': 'new_v', 'new_v_w_out': 'new_v', 'new_v_ffn2_norm': 'new_v', 'new_v_ffn2_w_gate': 'new_v', 'new_v_ffn2_w_up': 'new_v', 'new_v_ffn2_w_down': 'new_v', 'new_v_ple_norm': 'new_v', 'new_v_ple_w_gate': 'new_v', 'new_v_ple_w_proj': 'new_v', 'new_v_final_norm': 'new_v'}


def _forward(args):
    return _fwd_reference(*[args[k] for k in FWD_PARAMS])


def _output_shape():
    out = _jax.eval_shape(lambda: _forward(_fwd_setup_inputs(0)))
    return out.shape, out.dtype

N_MICROBATCH = 1
ADAM_LR = 0.001
ADAM_B1 = 0.9
ADAM_B2 = 0.999
ADAM_EPS = 1e-08
ADAM_WD = 0.01
ADAM_STEP = 10
PER_EXAMPLE_BATCH_AXIS = {'x': 0, 'p': 1, 'loss_target': 0}
SHARED_INPUTS = []
_WEIGHT_DTYPES = {'ffn1_norm': _jnp.float32, 'ffn1_w_gate': _jnp.float32, 'ffn1_w_up': _jnp.float32, 'ffn1_w_down': _jnp.float32, 'mix_norm': _jnp.float32, 'w_in': _jnp.float32, 'pool_w': _jnp.float32, 'pool_scale': _jnp.float32, 'conv_dw_w': _jnp.float32, 'conv_dw_b': _jnp.float32, 'conv_ln_g': _jnp.float32, 'conv_ln_b': _jnp.float32, 'conv_w_out': _jnp.float32, 'w_out': _jnp.float32, 'ffn2_norm': _jnp.float32, 'ffn2_w_gate': _jnp.float32, 'ffn2_w_up': _jnp.float32, 'ffn2_w_down': _jnp.float32, 'ple_norm': _jnp.float32, 'ple_w_gate': _jnp.float32, 'ple_w_proj': _jnp.float32, 'final_norm': _jnp.float32}
MOMENT_SCALE = {'ffn1_norm': 9.376261e-02, 'ffn1_w_gate': 3.931854e-02, 'ffn1_w_up': 3.811361e-02, 'ffn1_w_down': 6.299401e-02, 'mix_norm': 1.204944e-01, 'w_in': 5.268713e-02, 'pool_w': 8.836557e-02, 'pool_scale': 8.711621e-02, 'conv_dw_w': 6.215057e-02, 'conv_dw_b': 1.338184e-01, 'conv_ln_g': 7.721788e-02, 'conv_ln_b': 7.141174e-02, 'conv_w_out': 6.075029e-02, 'w_out': 1.069699e-01, 'ffn2_norm': 7.586465e-02, 'ffn2_w_gate': 3.259182e-02, 'ffn2_w_up': 3.153845e-02, 'ffn2_w_down': 5.232333e-02, 'ple_norm': 3.698060e-02, 'ple_w_gate': 3.665746e-02, 'ple_w_proj': 9.364406e-02, 'final_norm': 6.379361e+01}


def _to_microbatches(a, axis):
    t = _jnp.moveaxis(a, axis, 0)
    t = t.reshape((N_MICROBATCH, t.shape[0] // N_MICROBATCH) + t.shape[1:])
    return _jnp.moveaxis(t, 1, axis + 1)


def setup_inputs(seed: int = 0) -> dict:
    inp = _fwd_setup_inputs(seed)
    key = _jax.random.fold_in(_jax.random.key(seed), 7919)
    shape, _ = _output_shape()
    out = dict(inp)
    out["loss_target"] = _jax.random.normal(_jax.random.fold_in(key, 0), shape, _jnp.float32)
    for i, name in enumerate(TWIN_WEIGHTS):
        w = inp[name].astype(_jnp.float32)
        if MOMENT_SCALE is None:
            s = _jnp.sqrt(_jnp.mean(_jnp.square(w)) + 1e-30)
        else:
            s = MOMENT_SCALE[name]
        km, kv = _jax.random.split(_jax.random.fold_in(key, i + 1))
        out[name] = w
        out["m_" + name] = s * _jax.random.normal(km, w.shape, _jnp.float32)
        out["v_" + name] = (s * s) * _jax.random.uniform(kv, w.shape, _jnp.float32, 0.5, 1.5)
    if N_MICROBATCH > 1:
        for name, axis in PER_EXAMPLE_BATCH_AXIS.items():
            out[name] = _to_microbatches(out[name], axis)
    return {'x': out['x'], 'p': out['p'], 'ffn1_norm': out['ffn1_norm'], 'ffn1_w_gate': out['ffn1_w_gate'], 'ffn1_w_up': out['ffn1_w_up'], 'ffn1_w_down': out['ffn1_w_down'], 'mix_norm': out['mix_norm'], 'w_in': out['w_in'], 'pool_w': out['pool_w'], 'pool_scale': out['pool_scale'], 'conv_dw_w': out['conv_dw_w'], 'conv_dw_b': out['conv_dw_b'], 'conv_ln_g': out['conv_ln_g'], 'conv_ln_b': out['conv_ln_b'], 'conv_w_out': out['conv_w_out'], 'w_out': out['w_out'], 'ffn2_norm': out['ffn2_norm'], 'ffn2_w_gate': out['ffn2_w_gate'], 'ffn2_w_up': out['ffn2_w_up'], 'ffn2_w_down': out['ffn2_w_down'], 'ple_norm': out['ple_norm'], 'ple_w_gate': out['ple_w_gate'], 'ple_w_proj': out['ple_w_proj'], 'final_norm': out['final_norm'], 'loss_target': out['loss_target'], 'm_ffn1_norm': out['m_ffn1_norm'], 'm_ffn1_w_gate': out['m_ffn1_w_gate'], 'm_ffn1_w_up': out['m_ffn1_w_up'], 'm_ffn1_w_down': out['m_ffn1_w_down'], 'm_mix_norm': out['m_mix_norm'], 'm_w_in': out['m_w_in'], 'm_pool_w': out['m_pool_w'], 'm_pool_scale': out['m_pool_scale'], 'm_conv_dw_w': out['m_conv_dw_w'], 'm_conv_dw_b': out['m_conv_dw_b'], 'm_conv_ln_g': out['m_conv_ln_g'], 'm_conv_ln_b': out['m_conv_ln_b'], 'm_conv_w_out': out['m_conv_w_out'], 'm_w_out': out['m_w_out'], 'm_ffn2_norm': out['m_ffn2_norm'], 'm_ffn2_w_gate': out['m_ffn2_w_gate'], 'm_ffn2_w_up': out['m_ffn2_w_up'], 'm_ffn2_w_down': out['m_ffn2_w_down'], 'm_ple_norm': out['m_ple_norm'], 'm_ple_w_gate': out['m_ple_w_gate'], 'm_ple_w_proj': out['m_ple_w_proj'], 'm_final_norm': out['m_final_norm'], 'v_ffn1_norm': out['v_ffn1_norm'], 'v_ffn1_w_gate': out['v_ffn1_w_gate'], 'v_ffn1_w_up': out['v_ffn1_w_up'], 'v_ffn1_w_down': out['v_ffn1_w_down'], 'v_mix_norm': out['v_mix_norm'], 'v_w_in': out['v_w_in'], 'v_pool_w': out['v_pool_w'], 'v_pool_scale': out['v_pool_scale'], 'v_conv_dw_w': out['v_conv_dw_w'], 'v_conv_dw_b': out['v_conv_dw_b'], 'v_conv_ln_g': out['v_conv_ln_g'], 'v_conv_ln_b': out['v_conv_ln_b'], 'v_conv_w_out': out['v_conv_w_out'], 'v_w_out': out['v_w_out'], 'v_ffn2_norm': out['v_ffn2_norm'], 'v_ffn2_w_gate': out['v_ffn2_w_gate'], 'v_ffn2_w_up': out['v_ffn2_w_up'], 'v_ffn2_w_down': out['v_ffn2_w_down'], 'v_ple_norm': out['v_ple_norm'], 'v_ple_w_gate': out['v_ple_w_gate'], 'v_ple_w_proj': out['v_ple_w_proj'], 'v_final_norm': out['v_final_norm']}


def _loss(weights, diff, rest, loss_target):
    with _jax.named_scope("forward"):
        args = {**rest, TWIN_DIFF_INPUT: diff, **{k: w.astype(_WEIGHT_DTYPES[k]) for k, w in weights.items()}}
        y = _forward(args)
    with _jax.named_scope("loss_head"):
        err = _jnp.square(y.astype(_jnp.float32) - loss_target)
        return 0.5 * _jnp.sum(_jnp.mean(err, axis=-1)) if err.ndim else 0.5 * err


def _adamw(w, g, m, v):
    m = ADAM_B1 * m + (1.0 - ADAM_B1) * g
    v = ADAM_B2 * v + (1.0 - ADAM_B2) * _jnp.square(g)
    m_hat = m / (1.0 - ADAM_B1 ** ADAM_STEP)
    v_hat = v / (1.0 - ADAM_B2 ** ADAM_STEP)
    delta = -ADAM_LR * (m_hat / (_jnp.sqrt(v_hat) + ADAM_EPS) + ADAM_WD * w)
    return delta, m, v


def reference(x, p, ffn1_norm, ffn1_w_gate, ffn1_w_up, ffn1_w_down, mix_norm, w_in, pool_w, pool_scale, conv_dw_w, conv_dw_b, conv_ln_g, conv_ln_b, conv_w_out, w_out, ffn2_norm, ffn2_w_gate, ffn2_w_up, ffn2_w_down, ple_norm, ple_w_gate, ple_w_proj, final_norm, loss_target, m_ffn1_norm, m_ffn1_w_gate, m_ffn1_w_up, m_ffn1_w_down, m_mix_norm, m_w_in, m_pool_w, m_pool_scale, m_conv_dw_w, m_conv_dw_b, m_conv_ln_g, m_conv_ln_b, m_conv_w_out, m_w_out, m_ffn2_norm, m_ffn2_w_gate, m_ffn2_w_up, m_ffn2_w_down, m_ple_norm, m_ple_w_gate, m_ple_w_proj, m_final_norm, v_ffn1_norm, v_ffn1_w_gate, v_ffn1_w_up, v_ffn1_w_down, v_mix_norm, v_w_in, v_pool_w, v_pool_scale, v_conv_dw_w, v_conv_dw_b, v_conv_ln_g, v_conv_ln_b, v_conv_w_out, v_w_out, v_ffn2_norm, v_ffn2_w_gate, v_ffn2_w_up, v_ffn2_w_down, v_ple_norm, v_ple_w_gate, v_ple_w_proj, v_final_norm):
    given = dict(x=x, p=p, ffn1_norm=ffn1_norm, ffn1_w_gate=ffn1_w_gate, ffn1_w_up=ffn1_w_up, ffn1_w_down=ffn1_w_down, mix_norm=mix_norm, w_in=w_in, pool_w=pool_w, pool_scale=pool_scale, conv_dw_w=conv_dw_w, conv_dw_b=conv_dw_b, conv_ln_g=conv_ln_g, conv_ln_b=conv_ln_b, conv_w_out=conv_w_out, w_out=w_out, ffn2_norm=ffn2_norm, ffn2_w_gate=ffn2_w_gate, ffn2_w_up=ffn2_w_up, ffn2_w_down=ffn2_w_down, ple_norm=ple_norm, ple_w_gate=ple_w_gate, ple_w_proj=ple_w_proj, final_norm=final_norm, loss_target=loss_target, m_ffn1_norm=m_ffn1_norm, m_ffn1_w_gate=m_ffn1_w_gate, m_ffn1_w_up=m_ffn1_w_up, m_ffn1_w_down=m_ffn1_w_down, m_mix_norm=m_mix_norm, m_w_in=m_w_in, m_pool_w=m_pool_w, m_pool_scale=m_pool_scale, m_conv_dw_w=m_conv_dw_w, m_conv_dw_b=m_conv_dw_b, m_conv_ln_g=m_conv_ln_g, m_conv_ln_b=m_conv_ln_b, m_conv_w_out=m_conv_w_out, m_w_out=m_w_out, m_ffn2_norm=m_ffn2_norm, m_ffn2_w_gate=m_ffn2_w_gate, m_ffn2_w_up=m_ffn2_w_up, m_ffn2_w_down=m_ffn2_w_down, m_ple_norm=m_ple_norm, m_ple_w_gate=m_ple_w_gate, m_ple_w_proj=m_ple_w_proj, m_final_norm=m_final_norm, v_ffn1_norm=v_ffn1_norm, v_ffn1_w_gate=v_ffn1_w_gate, v_ffn1_w_up=v_ffn1_w_up, v_ffn1_w_down=v_ffn1_w_down, v_mix_norm=v_mix_norm, v_w_in=v_w_in, v_pool_w=v_pool_w, v_pool_scale=v_pool_scale, v_conv_dw_w=v_conv_dw_w, v_conv_dw_b=v_conv_dw_b, v_conv_ln_g=v_conv_ln_g, v_conv_ln_b=v_conv_ln_b, v_conv_w_out=v_conv_w_out, v_w_out=v_w_out, v_ffn2_norm=v_ffn2_norm, v_ffn2_w_gate=v_ffn2_w_gate, v_ffn2_w_up=v_ffn2_w_up, v_ffn2_w_down=v_ffn2_w_down, v_ple_norm=v_ple_norm, v_ple_w_gate=v_ple_w_gate, v_ple_w_proj=v_ple_w_proj, v_final_norm=v_final_norm)
    weights = {n: given[n] for n in TWIN_WEIGHTS}
    shared = {n: given[n] for n in SHARED_INPUTS}
    per_example = {n: given[n] for n in ['x', 'p']}
    grad_fn = _jax.value_and_grad(_loss, argnums=(0, 1))

    def one_microbatch(ex, loss_target):
        ex = dict(ex)
        diff = ex.pop(TWIN_DIFF_INPUT)
        return grad_fn(weights, diff, {**shared, **ex}, loss_target)

    if N_MICROBATCH == 1:
        loss, (grad_w, grad_x) = one_microbatch(per_example, given["loss_target"])
    else:
        def body(carry, xs):
            loss_sum, grad_sum = carry
            l_k, (gw_k, gx_k) = one_microbatch(xs[0], xs[1])
            with _jax.named_scope("update"):
                return (loss_sum + l_k, _jax.tree.map(_jnp.add, grad_sum, gw_k)), gx_k

        init = (_jnp.zeros((), _jnp.float32), _jax.tree.map(_jnp.zeros_like, weights))
        (loss, grad_w), grad_x = _jax.lax.scan(body, init, (per_example, given["loss_target"]))
    with _jax.named_scope("update"):
        delta_w, new_m, new_v = {}, {}, {}
        for n in TWIN_WEIGHTS:
            delta_w[n], new_m[n], new_v[n] = _adamw(weights[n], grad_w[n], given["m_" + n], given["v_" + n])
    return (loss, grad_x, *[grad_w[n] for n in TWIN_WEIGHTS], *[delta_w[n] for n in TWIN_WEIGHTS],
            *[new_m[n] for n in TWIN_WEIGHTS], *[new_v[n] for n in TWIN_WEIGHTS])
```

```python
import functools

import jax
import jax.numpy as jnp
from jax import lax
from jax.experimental import pallas as pl
from jax.experimental.pallas import tpu as pltpu

F32, BF16 = jnp.float32, jnp.bfloat16
NDEV = 8
AXES = ("x", "y", "c")
MESH = pl.DeviceIdType.MESH
HALO = 32
POOL_WINDOWS = (2, 4, 8, 16)
CONV_K = 31
CONV_KP = 32
RMS_EPS, LN_EPS = 1e-6, 1e-5
ADAM_LR, ADAM_B1, ADAM_B2, ADAM_EPS, ADAM_WD, ADAM_STEP = 0.001, 0.9, 0.999, 1e-08, 0.01, 10
LANE = 128
VMEM_LIMIT = 56 * 1024 * 1024
ANY = pl.BlockSpec(memory_space=pl.ANY)


def _nn(a, b):
    return jnp.dot(a, b, preferred_element_type=F32)


def _nt(a, b):
    return lax.dot_general(a, b, (((1,), (1,)), ((), ())), preferred_element_type=F32)


def _tn(a, b):
    return lax.dot_general(a, b, (((0,), (0,)), ((), ())), preferred_element_type=F32)


def _colsum8(v):
    return jnp.sum(v.reshape(v.shape[0] // 8, 8, v.shape[1]), axis=0)


def _sig(v):
    return jax.nn.sigmoid(v)


def _rms(h):
    r = lax.rsqrt(jnp.mean(h * h, axis=-1, keepdims=True) + RMS_EPS)
    return h * r, r


def _rms_bwd(dn, xh, r, gain):
    dxh = dn * gain
    return r * (dxh - xh * jnp.mean(dxh * xh, axis=-1, keepdims=True))


def _ln(c1):
    mu = jnp.mean(c1, axis=-1, keepdims=True)
    cen = c1 - mu
    rstd = lax.rsqrt(jnp.mean(cen * cen, axis=-1, keepdims=True) + LN_EPS)
    return cen * rstd, rstd


def _rows(tm, c):
    return pl.BlockSpec((tm, c), lambda i: (i, 0))


def _const(shape):
    return pl.BlockSpec(shape, lambda i: (0,) * len(shape))


def _params(n_grid=1):
    return pltpu.CompilerParams(dimension_semantics=("arbitrary",) * n_grid, vmem_limit_bytes=VMEM_LIMIT)


def _tile(n, want):
    t = min(n, want)
    while n % t:
        t //= 2
    return t


def _fetch(g_hbm, specs, sems):
    cps = []
    for wi, (off, rows, dst) in enumerate(specs):
        for dev in range(NDEV):
            cps.append(pltpu.make_async_copy(g_hbm.at[dev, pl.ds(off, rows), :],
                                             dst.at[pl.ds(dev * rows, rows), :], sems.at[wi, dev]))
    for cp in cps:
        cp.start()
    for cp in cps:
        cp.wait()


class _Layout:
    def __init__(self, D, FS, NS, PD, PG, PGS):
        self.D, self.FS, self.NS, self.PD, self.PG, self.PGS = D, FS, NS, PD, PG, PGS
        self.FB = -(-FS // LANE) * LANE
        self.DS = D // NDEV
        names = [("g1", self.FB), ("u1", self.FB), ("d1", self.FB), ("g2", self.FB), ("u2", self.FB), ("d2", self.FB),
                 ("win", NS), ("wco", self.DS), ("wout", self.DS), ("wpg", self.DS),
                 ("wpp", self.DS * PD // D), ("pool", 4 * PGS * PG // D)]
        self.off, self.rows, o = {}, {}, 0
        for n, r in names:
            self.off[n], self.rows[n] = o, r
            o += r
        self.R = o


def _pad_rows(a, r):
    return jnp.pad(a, ((0, r - a.shape[0]), (0, 0)))


def _pack_layer(w, l, lay, dtype):
    D, FB = lay.D, lay.FB
    parts = [_pad_rows(w["ffn1_w_gate"][l].T, FB), _pad_rows(w["ffn1_w_up"][l].T, FB), _pad_rows(w["ffn1_w_down"][l], FB),
             _pad_rows(w["ffn2_w_gate"][l].T, FB), _pad_rows(w["ffn2_w_up"][l].T, FB), _pad_rows(w["ffn2_w_down"][l], FB),
             w["w_in"][l].T, w["conv_w_out"][l], w["w_out"][l], w["ple_w_gate"][l],
             w["ple_w_proj"][l].T.reshape(-1, D), w["pool_w"][l].reshape(-1, D)]
    return jnp.concatenate(parts, axis=0).astype(dtype)


def _unpack_layer(slab, lay):
    D, FS, PD, PG, PGS, DS = lay.D, lay.FS, lay.PD, lay.PG, lay.PGS, lay.DS

    def sl(n):
        return slab[lay.off[n]:lay.off[n] + lay.rows[n]]

    return {"ffn1_w_gate": sl("g1")[:FS].T, "ffn1_w_up": sl("u1")[:FS].T, "ffn1_w_down": sl("d1")[:FS],
            "ffn2_w_gate": sl("g2")[:FS].T, "ffn2_w_up": sl("u2")[:FS].T, "ffn2_w_down": sl("d2")[:FS],
            "w_in": sl("win").T, "conv_w_out": sl("wco"), "w_out": sl("wout"), "ple_w_gate": sl("wpg"),
            "ple_w_proj": sl("wpp").reshape(DS, PD).T, "pool_w": sl("pool").reshape(4, PGS, PG)}


def _place():
    return lax.axis_index("x"), lax.axis_index("y"), lax.axis_index("c")


def _all_gather_slab(slab):
    R, D = slab.shape

    def body(x_ref, out_ref, send_sems, recv_sems, local_sem):
        x, y, c = _place()
        me, sibling = (x, y, c), (x, y, 1 - c)
        chips = [(1 - x, y), (x, 1 - y), (1 - x, 1 - y)]

        def rows(px, py, pc):
            return out_ref.at[4 * px + 2 * py + pc]

        def copy(k, block, to, src=None):
            return pltpu.make_async_remote_copy(
                src_ref=rows(*block) if src is None else src, dst_ref=rows(*block),
                send_sem=send_sems.at[k], recv_sem=recv_sems.at[k], device_id=to, device_id_type=MESH)

        mine = pltpu.make_async_copy(x_ref, rows(*me), local_sem)
        mine.start()
        first = [copy(0, me, sibling, src=x_ref)]
        first += [copy(1 + j, me, (*chip, c), src=x_ref) for j, chip in enumerate(chips)]
        for cp in first:
            cp.start()
        passed = [copy(4 + j, (*chip, c), sibling) for j, chip in enumerate(chips)]
        for j, chip in enumerate(chips):
            copy(1 + j, (*chip, c), me).wait_recv()
            passed[j].start()
        copy(0, sibling, me).wait_recv()
        for j, chip in enumerate(chips):
            copy(4 + j, (*chip, 1 - c), me).wait_recv()
        for cp in first + passed:
            cp.wait_send()
        mine.wait()

    return pl.pallas_call(
        body, name="ag_slab",
        out_shape=jax.ShapeDtypeStruct((NDEV, R, D), slab.dtype),
        in_specs=[ANY], out_specs=ANY,
        scratch_shapes=[pltpu.SemaphoreType.DMA((7,)), pltpu.SemaphoreType.DMA((7,)), pltpu.SemaphoreType.DMA],
    )(slab)


def _all_gather_small(rows_in):
    S, D = rows_in.shape

    def body(x_ref, out_ref, sum_ref, send_sems, recv_sems):
        x, y, c = _place()
        me = 4 * x + 2 * y + c
        out_ref[me] = x_ref[...]
        flips = [(a, b, d) for a in (0, 1) for b in (0, 1) for d in (0, 1)][1:]

        def copy(k):
            a, b, d = flips[k]
            px, py, pc = x ^ a, y ^ b, c ^ d
            peer = 4 * px + 2 * py + pc
            send = pltpu.make_async_remote_copy(
                src_ref=x_ref, dst_ref=out_ref.at[me], send_sem=send_sems.at[k], recv_sem=recv_sems.at[k],
                device_id=(px, py, pc), device_id_type=MESH)
            recv = pltpu.make_async_remote_copy(
                src_ref=x_ref, dst_ref=out_ref.at[peer], send_sem=send_sems.at[k], recv_sem=recv_sems.at[k],
                device_id=(px, py, pc), device_id_type=MESH)
            return send, recv

        cps = [copy(k) for k in range(7)]
        for send, _ in cps:
            send.start()
        for _, recv in cps:
            recv.wait_recv()
        for send, _ in cps:
            send.wait_send()
        acc = out_ref[0]
        for j in range(1, NDEV):
            acc = acc + out_ref[j]
        sum_ref[...] = acc

    vm = pl.BlockSpec(memory_space=pltpu.VMEM)
    return pl.pallas_call(
        body, name="ag_small",
        out_shape=(jax.ShapeDtypeStruct((NDEV, S, D), F32), jax.ShapeDtypeStruct((S, D), F32)),
        in_specs=[vm], out_specs=(vm, vm),
        scratch_shapes=[pltpu.SemaphoreType.DMA((7,)), pltpu.SemaphoreType.DMA((7,))],
    )(rows_in)


def _rs_pair(gr):
    _, RR, D = gr.shape

    def body(gr_ref, out_ref, send_sems, recv_sems):
        x, y, c = _place()
        cps = [pltpu.make_async_remote_copy(
            src_ref=gr_ref.at[2 * q + (1 - c)], dst_ref=out_ref.at[q], send_sem=send_sems.at[q],
            recv_sem=recv_sems.at[q], device_id=(x, y, 1 - c), device_id_type=MESH) for q in range(4)]
        for cp in cps:
            cp.start()
        for cp in cps:
            cp.wait_recv()
        for cp in cps:
            cp.wait_send()

    return pl.pallas_call(
        body, name="rs_pair", out_shape=jax.ShapeDtypeStruct((4, RR, D), gr.dtype),
        in_specs=[ANY], out_specs=ANY,
        scratch_shapes=[pltpu.SemaphoreType.DMA((4,)), pltpu.SemaphoreType.DMA((4,))],
    )(gr)


def _rs_chips(part):
    _, RR, D = part.shape

    def body(p_ref, out_ref, send_sems, recv_sems):
        x, y, c = _place()
        chips = [(1 - x, y), (x, 1 - y), (1 - x, 1 - y)]
        cps = [pltpu.make_async_remote_copy(
            src_ref=p_ref.at[j], dst_ref=out_ref.at[j], send_sem=send_sems.at[j], recv_sem=recv_sems.at[j],
            device_id=(*chips[j], c), device_id_type=MESH) for j in range(3)]
        for cp in cps:
            cp.start()
        for cp in cps:
            cp.wait_recv()
        for cp in cps:
            cp.wait_send()

    return pl.pallas_call(
        body, name="rs_chips", out_shape=jax.ShapeDtypeStruct((3, RR, D), part.dtype),
        in_specs=[ANY], out_specs=ANY,
        scratch_shapes=[pltpu.SemaphoreType.DMA((3,)), pltpu.SemaphoreType.DMA((3,))],
    )(part)


def _rs_add(gr, got, dev_idx, chip_idx):
    _, RR, D = gr.shape
    tr = _tile(RR, 512)

    def body(di_ref, qi_ref, a_ref, b_ref, o_ref):
        o_ref[...] = (a_ref[...].astype(F32) + b_ref[...].astype(F32)).astype(o_ref.dtype)

    grid_spec = pltpu.PrefetchScalarGridSpec(
        num_scalar_prefetch=2, grid=(3, RR // tr),
        in_specs=[pl.BlockSpec((None, tr, D), lambda j, i, di, qi: (di[j], i, 0)),
                  pl.BlockSpec((None, tr, D), lambda j, i, di, qi: (qi[j], i, 0))],
        out_specs=pl.BlockSpec((None, tr, D), lambda j, i, di, qi: (j, i, 0)))
    return pl.pallas_call(
        body, name="rs_add", grid_spec=grid_spec, out_shape=jax.ShapeDtypeStruct((3, RR, D), gr.dtype),
        compiler_params=_params(2),
    )(dev_idx, chip_idx, gr, got)


def _adamw(parts, w, m, v, name):
    rows, D = w.shape
    tr = _tile(rows, 256)
    n = len(parts)

    def body(*refs):
        part_refs, (w_ref, m_ref, v_ref, g_out, d_out, m_out, v_out) = refs[:n], refs[n:]
        g = part_refs[0][...].astype(F32)
        for pr in part_refs[1:]:
            g = g + pr[...].astype(F32)
        m_new = ADAM_B1 * m_ref[...] + (1.0 - ADAM_B1) * g
        v_new = ADAM_B2 * v_ref[...] + (1.0 - ADAM_B2) * (g * g)
        m_hat = m_new / (1.0 - ADAM_B1 ** ADAM_STEP)
        v_hat = v_new / (1.0 - ADAM_B2 ** ADAM_STEP)
        g_out[...] = g
        d_out[...] = -ADAM_LR * (m_hat / (jnp.sqrt(v_hat) + ADAM_EPS) + ADAM_WD * w_ref[...])
        m_out[...] = m_new
        v_out[...] = v_new

    return pl.pallas_call(
        body, name=name, grid=(rows // tr,),
        in_specs=[_rows(tr, D)] * (n + 3), out_specs=[_rows(tr, D)] * 4,
        out_shape=[jax.ShapeDtypeStruct((rows, D), F32)] * 4, compiler_params=_params(),
    )(*parts, w, m, v)


def _tn_matmul(xa, ya, name, tmm, y_follows=False):
    T, M = xa.shape
    tn = tmm if y_follows else ya.shape[1]
    tt = _tile(T, 1024)
    nb = M // tmm

    def body(x_ref, y_ref, o_ref, acc):
        k = pl.program_id(1)

        @pl.when(k == 0)
        def _():
            acc[...] = jnp.zeros_like(acc)

        acc[...] += _tn(x_ref[...], y_ref[...])

        @pl.when(k == pl.num_programs(1) - 1)
        def _():
            o_ref[...] = acc[...].astype(o_ref.dtype)

    y_map = (lambda b, k: (k, b)) if y_follows else (lambda b, k: (k, 0))
    return pl.pallas_call(
        body, name=name, grid=(nb, T // tt),
        in_specs=[pl.BlockSpec((tt, tmm), lambda b, k: (k, b)), pl.BlockSpec((tt, tn), y_map)],
        out_specs=pl.BlockSpec((tmm, tn), lambda b, k: (b, 0)),
        out_shape=jax.ShapeDtypeStruct((M, tn), BF16),
        scratch_shapes=[pltpu.VMEM((tmm, tn), F32)], compiler_params=_params(2),
    )(xa, ya)


def _ffn_fwd(h, gain, G, offs, lay, name):
    T, D = h.shape
    FB = lay.FB
    FP, CH = NDEV * FB, 2 * FB
    tm = _tile(T, 512)

    def body(h_ref, gain_ref, g_hbm, ho_ref, go_ref, uo_ref, wg, wu, wd, sems):
        @pl.when(pl.program_id(0) == 0)
        def _():
            _fetch(g_hbm, [(offs[0], FB, wg), (offs[1], FB, wu), (offs[2], FB, wd)], sems)

        h = h_ref[...]
        xh, _ = _rms(h)
        n = (xh * gain_ref[...]).astype(BF16)
        acc = jnp.zeros((tm, D), F32)
        for j in range(FP // CH):
            sl = pl.ds(j * CH, CH)
            g = _nt(n, wg[sl, :])
            u = _nt(n, wu[sl, :])
            go_ref[:, sl] = g.astype(BF16)
            uo_ref[:, sl] = u.astype(BF16)
            a = g * _sig(g) * u
            acc = acc + _nn(a.astype(BF16), wd[sl, :])
        ho_ref[...] = h + 0.5 * acc

    return pl.pallas_call(
        body, name=name, grid=(T // tm,),
        in_specs=[_rows(tm, D), _const((1, D)), ANY],
        out_specs=[_rows(tm, D), _rows(tm, FP), _rows(tm, FP)],
        out_shape=[jax.ShapeDtypeStruct((T, D), F32), jax.ShapeDtypeStruct((T, FP), BF16),
                   jax.ShapeDtypeStruct((T, FP), BF16)],
        scratch_shapes=[pltpu.VMEM((FP, D), BF16)] * 3 + [pltpu.SemaphoreType.DMA((3, NDEV))],
        compiler_params=_params(),
    )(h, gain, G)


def _ffn_bwd(d, h, gain, ga, ua, G, offs, lay, name):
    T, D = h.shape
    FB = lay.FB
    FP, CH = NDEV * FB, 2 * FB
    tm = _tile(T, 256)

    def body(d_ref, h_ref, gain_ref, ga_ref, ua_ref, g_hbm,
             do_ref, dg_ref, du_ref, a_ref, n_ref, dh_ref, gg_ref, wg, wu, wd, sems):
        @pl.when(pl.program_id(0) == 0)
        def _():
            _fetch(g_hbm, [(offs[0], FB, wg), (offs[1], FB, wu), (offs[2], FB, wd)], sems)
            gg_ref[...] = jnp.zeros_like(gg_ref)

        d = d_ref[...]
        gain_v = gain_ref[...]
        xh, r = _rms(h_ref[...])
        n_ref[...] = (xh * gain_v).astype(BF16)
        dh = (0.5 * d).astype(BF16)
        dh_ref[...] = dh
        dn = jnp.zeros((tm, D), F32)
        for j in range(FP // CH):
            sl = pl.ds(j * CH, CH)
            g = ga_ref[:, sl].astype(F32)
            u = ua_ref[:, sl].astype(F32)
            da = _nt(dh, wd[sl, :])
            s = _sig(g)
            silu = g * s
            a_ref[:, sl] = (silu * u).astype(BF16)
            dgv = (da * u * (s * (1.0 + g * (1.0 - s)))).astype(BF16)
            duv = (da * silu).astype(BF16)
            dg_ref[:, sl] = dgv
            du_ref[:, sl] = duv
            dn = dn + _nn(dgv, wg[sl, :]) + _nn(duv, wu[sl, :])
        gg_ref[...] += _colsum8(dn * xh)
        do_ref[...] = d + _rms_bwd(dn, xh, r, gain_v)

    wide = jax.ShapeDtypeStruct((T, FP), BF16)
    return pl.pallas_call(
        body, name=name, grid=(T // tm,),
        in_specs=[_rows(tm, D), _rows(tm, D), _const((1, D)), _rows(tm, FP), _rows(tm, FP), ANY],
        out_specs=[_rows(tm, D), _rows(tm, FP), _rows(tm, FP), _rows(tm, FP), _rows(tm, D), _rows(tm, D),
                   _const((8, D))],
        out_shape=[jax.ShapeDtypeStruct((T, D), F32), wide, wide, wide, jax.ShapeDtypeStruct((T, D), BF16),
                   jax.ShapeDtypeStruct((T, D), BF16), jax.ShapeDtypeStruct((8, D), F32)],
        scratch_shapes=[pltpu.VMEM((FP, D), BF16)] * 3 + [pltpu.SemaphoreType.DMA((3, NDEV))],
        compiler_params=_params(),
    )(d, h, gain, ga, ua, G)


def _inproj_fwd(h, gain, G, lay):
    T, D = h.shape
    NS = lay.NS
    NIN, CH = NDEV * NS, 2 * NS
    tm = _tile(T, 512)

    def body(h_ref, gain_ref, g_hbm, z_ref, win, sems):
        @pl.when(pl.program_id(0) == 0)
        def _():
            _fetch(g_hbm, [(lay.off["win"], NS, win)], sems)

        xh, _ = _rms(h_ref[...])
        n = (xh * gain_ref[...]).astype(BF16)
        for j in range(NIN // CH):
            sl = pl.ds(j * CH, CH)
            z_ref[:, sl] = _nt(n, win[sl, :]).astype(BF16)

    return pl.pallas_call(
        body, name="inproj_fwd", grid=(T // tm,),
        in_specs=[_rows(tm, D), _const((1, D)), ANY], out_specs=_rows(tm, NIN),
        out_shape=jax.ShapeDtypeStruct((T, NIN), BF16),
        scratch_shapes=[pltpu.VMEM((NIN, D), BF16), pltpu.SemaphoreType.DMA((1, NDEV))],
        compiler_params=_params(),
    )(h, gain, G)


def _conv_chunks(tm, D):
    rb, lc = min(tm, 64), min(D, 256)
    return [(r0, l0, rb, lc) for r0 in range(0, tm, rb) for l0 in range(0, D, lc)]


def _mixer_fwd(z, h, pool_w, pscale, cw, cb, lng, lnb, G, lay):
    T, D = h.shape
    PG, DS = lay.PG, lay.DS
    tm = _tile(T, 256)
    hb = tm // HALO

    def body(z_ref, zp_ref, h_ref, pw_ref, ps_ref, cw_ref, cb_ref, lg_ref, lb_ref, g_hbm,
             ho_ref, c1_ref, q_ref, cc_ref, pool_ref, ext_p, ext_c, a_s, wco, wout, sems):
        i = pl.program_id(0)

        @pl.when(i == 0)
        def _():
            _fetch(g_hbm, [(lay.off["wco"], DS, wco), (lay.off["wout"], DS, wout)], sems)

        live = jnp.where(i > 0, 1.0, 0.0).astype(F32)
        ext_p[pl.ds(0, HALO), :] = zp_ref[:, pl.ds(0, D)].astype(F32) * live
        ext_p[pl.ds(HALO, tm), :] = z_ref[:, pl.ds(0, D)].astype(F32)
        ext_c[pl.ds(0, HALO), :] = (zp_ref[:, pl.ds(D, D)].astype(F32)
                                    * _sig(zp_ref[:, pl.ds(2 * D, D)].astype(F32)) * live)
        ext_c[pl.ds(HALO, tm), :] = z_ref[:, pl.ds(D, D)].astype(F32) * _sig(z_ref[:, pl.ds(2 * D, D)].astype(F32))

        t = i * tm + lax.broadcasted_iota(jnp.int32, (tm, 1), 0)
        for g, w in enumerate(POOL_WINDOWS):
            sl = pl.ds(g * PG, PG)
            s = ext_p[pl.ds(HALO, tm), sl]
            zc = s
            for j in range(1, w):
                s = s + ext_p[pl.ds(HALO - j, tm), sl]
            inv = 1.0 / jnp.minimum(t + 1, w).astype(F32)
            pooled = (s * inv - zc).astype(BF16)
            pool_ref[:, sl] = pooled
            qv = _nn(pooled, pw_ref[g])
            q_ref[:, sl] = qv.astype(BF16)
            a_s[:, sl] = qv * ps_ref[:, sl]

        for r0, l0, rb, lc in _conv_chunks(tm, D):
            ls = pl.ds(l0, lc)
            acc = jnp.zeros((rb, lc), F32) + cb_ref[:, ls]
            for k in range(CONV_K):
                acc = acc + cw_ref[pl.ds(k, 1), ls] * ext_c[pl.ds(r0 + HALO - (CONV_K - 1) + k, rb), ls]
            c1_ref[pl.ds(r0, rb), ls] = acc

        xhat, _ = _ln(c1_ref[...])
        c2 = xhat * lg_ref[...] + lb_ref[...]
        c3 = (c2 * _sig(c2)).astype(BF16)
        cc = _nn(c3, wco[...])
        cc_ref[...] = cc.astype(BF16)
        gp = z_ref[:, pl.ds(3 * D, D)].astype(F32)
        gc = z_ref[:, pl.ds(4 * D, D)].astype(F32)
        m = (_sig(gp) * a_s[...] + _sig(gc) * cc).astype(BF16)
        ho_ref[...] = h_ref[...] + _nn(m, wout[...])

    act = jax.ShapeDtypeStruct((T, D), BF16)
    return pl.pallas_call(
        body, name="mixer_fwd", grid=(T // tm,),
        in_specs=[_rows(tm, 5 * D), pl.BlockSpec((HALO, 5 * D), lambda i: (jnp.maximum(i * hb - 1, 0), 0)),
                  _rows(tm, D), _const((4, PG, PG)), _const((1, D)), _const((CONV_KP, D)), _const((1, D)),
                  _const((1, D)), _const((1, D)), ANY],
        out_specs=[_rows(tm, D)] * 5,
        out_shape=[jax.ShapeDtypeStruct((T, D), F32), jax.ShapeDtypeStruct((T, D), F32), act, act, act],
        scratch_shapes=[pltpu.VMEM((HALO + tm, D), F32), pltpu.VMEM((HALO + tm, D), F32), pltpu.VMEM((tm, D), F32),
                        pltpu.VMEM((D, D), BF16), pltpu.VMEM((D, D), BF16), pltpu.SemaphoreType.DMA((2, NDEV))],
        compiler_params=_params(),
    )(z, z, h, pool_w, pscale, cw, cb, lng, lnb, G)


def _mixer_bwd_rows(d, z, c1, qa, cca, pool_w, pscale, lng, lnb, G, lay):
    T, D = d.shape
    PG, DS = lay.PG, lay.DS
    tm = _tile(T, 256)

    def body(d_ref, z_ref, c1_ref, q_ref, cc_ref, pw_ref, ps_ref, lg_ref, lb_ref, g_hbm,
             m_ref, db_ref, dcc_ref, c3_ref, dq_ref, dpool_ref, dc1_ref, dzg_ref, gps_ref, glg_ref, glb_ref, gcb_ref,
             wco, wout, sems):
        @pl.when(pl.program_id(0) == 0)
        def _():
            _fetch(g_hbm, [(lay.off["wco"], DS, wco), (lay.off["wout"], DS, wout)], sems)
            for ref in (gps_ref, glg_ref, glb_ref, gcb_ref):
                ref[...] = jnp.zeros_like(ref)

        db = d_ref[...].astype(BF16)
        db_ref[...] = db
        dm = _nt(db, wout[...])
        q = q_ref[...].astype(F32)
        cc = cc_ref[...].astype(F32)
        ps = ps_ref[...]
        sp = _sig(z_ref[:, pl.ds(3 * D, D)].astype(F32))
        sc = _sig(z_ref[:, pl.ds(4 * D, D)].astype(F32))
        a = q * ps
        m_ref[...] = (sp * a + sc * cc).astype(BF16)
        da = dm * sp
        dzg_ref[:, pl.ds(0, D)] = (dm * a * sp * (1.0 - sp)).astype(BF16)
        dzg_ref[:, pl.ds(D, D)] = (dm * cc * sc * (1.0 - sc)).astype(BF16)
        gps_ref[...] += _colsum8(da * q)
        dq = (da * ps).astype(BF16)
        dq_ref[...] = dq
        for g in range(len(POOL_WINDOWS)):
            sl = pl.ds(g * PG, PG)
            dpool_ref[:, sl] = _nt(dq_ref[:, sl], pw_ref[g])

        dcc = (dm * sc).astype(BF16)
        dcc_ref[...] = dcc
        xhat, rstd = _ln(c1_ref[...])
        lg = lg_ref[...]
        c2 = xhat * lg + lb_ref[...]
        s2 = _sig(c2)
        c3_ref[...] = (c2 * s2).astype(BF16)
        dc2 = _nt(dcc, wco[...]) * (s2 * (1.0 + c2 * (1.0 - s2)))
        glg_ref[...] += _colsum8(dc2 * xhat)
        glb_ref[...] += _colsum8(dc2)
        dxh = dc2 * lg
        dc1 = rstd * (dxh - jnp.mean(dxh, axis=-1, keepdims=True)
                      - xhat * jnp.mean(dxh * xhat, axis=-1, keepdims=True))
        dc1_ref[...] = dc1
        gcb_ref[...] += _colsum8(dc1)

    act = jax.ShapeDtypeStruct((T, D), BF16)
    full = jax.ShapeDtypeStruct((T, D), F32)
    vec = jax.ShapeDtypeStruct((8, D), F32)
    return pl.pallas_call(
        body, name="mixer_bwd_rows", grid=(T // tm,),
        in_specs=[_rows(tm, D), _rows(tm, 5 * D), _rows(tm, D), _rows(tm, D), _rows(tm, D),
                  _const((4, PG, PG)), _const((1, D)), _const((1, D)), _const((1, D)), ANY],
        out_specs=[_rows(tm, D)] * 7 + [_rows(tm, 2 * D)] + [_const((8, D))] * 4,
        out_shape=[act, act, act, act, act, full, full, jax.ShapeDtypeStruct((T, 2 * D), BF16), vec, vec, vec, vec],
        scratch_shapes=[pltpu.VMEM((D, D), BF16), pltpu.VMEM((D, D), BF16), pltpu.SemaphoreType.DMA((2, NDEV))],
        compiler_params=_params(),
    )(d, z, c1, qa, cca, pool_w, pscale, lng, lnb, G)


def _mixer_bwd_time(dc1, dpool, z, cw, lay):
    T, D = dc1.shape
    PG = lay.PG
    tm = _tile(T, 256)
    hb = tm // HALO
    nt = T // tm

    def body(dc_ref, dcn_ref, dp_ref, dpn_ref, z_ref, zp_ref, cw_ref, dz_ref, gcw_ref, ext_d, ext_q, ext_c, dc0_s):
        i = pl.program_id(0)

        @pl.when(i == 0)
        def _():
            gcw_ref[...] = jnp.zeros_like(gcw_ref)

        live_p = jnp.where(i > 0, 1.0, 0.0).astype(F32)
        live_n = jnp.where(i < nt - 1, 1.0, 0.0).astype(F32)
        ext_d[pl.ds(0, tm), :] = dc_ref[...]
        ext_d[pl.ds(tm, HALO), :] = dcn_ref[...] * live_n
        zg = z_ref[:, pl.ds(2 * D, D)].astype(F32)
        za = z_ref[:, pl.ds(D, D)].astype(F32)
        sg = _sig(zg)
        ext_c[pl.ds(0, HALO), :] = (zp_ref[:, pl.ds(D, D)].astype(F32)
                                    * _sig(zp_ref[:, pl.ds(2 * D, D)].astype(F32)) * live_p)
        ext_c[pl.ds(HALO, tm), :] = za * sg

        t = i * tm + lax.broadcasted_iota(jnp.int32, (tm, 1), 0)
        tn = (i + 1) * tm + lax.broadcasted_iota(jnp.int32, (HALO, 1), 0)
        for g, w in enumerate(POOL_WINDOWS):
            sl = pl.ds(g * PG, PG)
            ext_q[pl.ds(0, tm), sl] = dp_ref[:, sl] * (1.0 / jnp.minimum(t + 1, w).astype(F32))
            ext_q[pl.ds(tm, HALO), sl] = dpn_ref[:, sl] * (live_n / jnp.minimum(tn + 1, w).astype(F32))
        for g, w in enumerate(POOL_WINDOWS):
            sl = pl.ds(g * PG, PG)
            s = ext_q[pl.ds(0, tm), sl]
            for j in range(1, w):
                s = s + ext_q[pl.ds(j, tm), sl]
            dz_ref[:, sl] = (s - dp_ref[:, sl]).astype(BF16)

        for r0, l0, rb, lc in _conv_chunks(tm, D):
            ls = pl.ds(l0, lc)
            acc = jnp.zeros((rb, lc), F32)
            for j in range(CONV_K):
                acc = acc + cw_ref[pl.ds(CONV_K - 1 - j, 1), ls] * ext_d[pl.ds(r0 + j, rb), ls]
            dc0_s[pl.ds(r0, rb), ls] = acc
        dc0 = dc0_s[...]
        dz_ref[:, pl.ds(D, D)] = (dc0 * sg).astype(BF16)
        dz_ref[:, pl.ds(2 * D, D)] = (dc0 * za * sg * (1.0 - sg)).astype(BF16)

        lc = min(D, 256)
        for l0 in range(0, D, lc):
            ls = pl.ds(l0, lc)
            dcv = dc_ref[:, ls]
            for k in range(CONV_K):
                gcw_ref[pl.ds(8 * k, 8), ls] += _colsum8(dcv * ext_c[pl.ds(HALO - (CONV_K - 1) + k, tm), ls])

    nxt = lambda i: (jnp.minimum((i + 1) * hb, T // HALO - 1), 0)
    return pl.pallas_call(
        body, name="mixer_bwd_time", grid=(nt,),
        in_specs=[_rows(tm, D), pl.BlockSpec((HALO, D), nxt), _rows(tm, D), pl.BlockSpec((HALO, D), nxt),
                  _rows(tm, 5 * D), pl.BlockSpec((HALO, 5 * D), lambda i: (jnp.maximum(i * hb - 1, 0), 0)),
                  _const((CONV_KP, D))],
        out_specs=[_rows(tm, 3 * D), _const((CONV_KP * 8, D))],
        out_shape=[jax.ShapeDtypeStruct((T, 3 * D), BF16), jax.ShapeDtypeStruct((CONV_KP * 8, D), F32)],
        scratch_shapes=[pltpu.VMEM((tm + HALO, D), F32), pltpu.VMEM((tm + HALO, D), F32),
                        pltpu.VMEM((HALO + tm, D), F32), pltpu.VMEM((tm, D), F32)],
        compiler_params=_params(),
    )(dc1, dc1, dpool, dpool, z, z, cw)


def _inproj_bwd(d, h, gain, dzm, dzg, G, lay):
    T, D = h.shape
    NS = lay.NS
    NIN = NDEV * NS
    tm = _tile(T, 512)

    def body(d_ref, h_ref, gain_ref, dzm_ref, dzg_ref, g_hbm, do_ref, u_ref, gg_ref, win, sems):
        @pl.when(pl.program_id(0) == 0)
        def _():
            _fetch(g_hbm, [(lay.off["win"], NS, win)], sems)
            gg_ref[...] = jnp.zeros_like(gg_ref)

        gain_v = gain_ref[...]
        xh, r = _rms(h_ref[...])
        u_ref[...] = (xh * gain_v).astype(BF16)
        dn = jnp.zeros((tm, D), F32)
        for j in range(3):
            dn = dn + _nn(dzm_ref[:, pl.ds(j * D, D)], win[pl.ds(j * D, D), :])
        for j in range(2):
            dn = dn + _nn(dzg_ref[:, pl.ds(j * D, D)], win[pl.ds((3 + j) * D, D), :])
        gg_ref[...] += _colsum8(dn * xh)
        do_ref[...] = d_ref[...] + _rms_bwd(dn, xh, r, gain_v)

    return pl.pallas_call(
        body, name="inproj_bwd", grid=(T // tm,),
        in_specs=[_rows(tm, D), _rows(tm, D), _const((1, D)), _rows(tm, 3 * D), _rows(tm, 2 * D), ANY],
        out_specs=[_rows(tm, D), _rows(tm, D), _const((8, D))],
        out_shape=[jax.ShapeDtypeStruct((T, D), F32), jax.ShapeDtypeStruct((T, D), BF16),
                   jax.ShapeDtypeStruct((8, D), F32)],
        scratch_shapes=[pltpu.VMEM((NIN, D), BF16), pltpu.SemaphoreType.DMA((1, NDEV))],
        compiler_params=_params(),
    )(d, h, gain, dzm, dzg, G)


def _ple_fwd(h, pe, gain, wppt, G, lay):
    T, D = h.shape
    PD, DS = lay.PD, lay.DS
    tm = _tile(T, 512)

    def body(h_ref, p_ref, gain_ref, wpp_ref, g_hbm, ho_ref, gate_ref, wpg, sems):
        @pl.when(pl.program_id(0) == 0)
        def _():
            _fetch(g_hbm, [(lay.off["wpg"], DS, wpg)], sems)

        h = h_ref[...]
        xh, _ = _rms(h)
        n = (xh * gain_ref[...]).astype(BF16)
        gate = _sig(_nn(n, wpg[...]))
        gate_ref[...] = gate.astype(BF16)
        e = _nt(p_ref[...].astype(BF16), wpp_ref[...])
        ho_ref[...] = h + gate * e

    return pl.pallas_call(
        body, name="ple_fwd", grid=(T // tm,),
        in_specs=[_rows(tm, D), _rows(tm, PD), _const((1, D)), _const((D, PD)), ANY],
        out_specs=[_rows(tm, D), _rows(tm, D)],
        out_shape=[jax.ShapeDtypeStruct((T, D), F32), jax.ShapeDtypeStruct((T, D), BF16)],
        scratch_shapes=[pltpu.VMEM((D, D), BF16), pltpu.SemaphoreType.DMA((1, NDEV))],
        compiler_params=_params(),
    )(h, pe, gain, wppt, G)


def _ple_bwd(d, h, pe, gate_a, gain, wppt, G, lay):
    T, D = h.shape
    PD, DS = lay.PD, lay.DS
    tm = _tile(T, 512)

    def body(d_ref, h_ref, p_ref, gate_ref, gain_ref, wpp_ref, g_hbm,
             do_ref, dpre_ref, de_ref, n_ref, pb_ref, gg_ref, wpg, sems):
        @pl.when(pl.program_id(0) == 0)
        def _():
            _fetch(g_hbm, [(lay.off["wpg"], DS, wpg)], sems)
            gg_ref[...] = jnp.zeros_like(gg_ref)

        d = d_ref[...]
        gain_v = gain_ref[...]
        xh, r = _rms(h_ref[...])
        n_ref[...] = (xh * gain_v).astype(BF16)
        pb = p_ref[...].astype(BF16)
        pb_ref[...] = pb
        e = _nt(pb, wpp_ref[...])
        gate = gate_ref[...].astype(F32)
        de_ref[...] = (d * gate).astype(BF16)
        dpre = (d * e * gate * (1.0 - gate)).astype(BF16)
        dpre_ref[...] = dpre
        dn = _nt(dpre, wpg[...])
        gg_ref[...] += _colsum8(dn * xh)
        do_ref[...] = d + _rms_bwd(dn, xh, r, gain_v)

    act = jax.ShapeDtypeStruct((T, D), BF16)
    return pl.pallas_call(
        body, name="ple_bwd", grid=(T // tm,),
        in_specs=[_rows(tm, D), _rows(tm, D), _rows(tm, PD), _rows(tm, D), _const((1, D)), _const((D, PD)), ANY],
        out_specs=[_rows(tm, D), _rows(tm, D), _rows(tm, D), _rows(tm, D), _rows(tm, PD), _const((8, D))],
        out_shape=[jax.ShapeDtypeStruct((T, D), F32), act, act, act, jax.ShapeDtypeStruct((T, PD), BF16),
                   jax.ShapeDtypeStruct((8, D), F32)],
        scratch_shapes=[pltpu.VMEM((D, D), BF16), pltpu.SemaphoreType.DMA((1, NDEV))],
        compiler_params=_params(),
    )(d, h, pe, gate_a, gain, wppt, G)


def _head(h, target, gain):
    T, D = h.shape
    tm = _tile(T, 512)

    def body(h_ref, t_ref, gain_ref, do_ref, loss_ref, gg_ref):
        @pl.when(pl.program_id(0) == 0)
        def _():
            loss_ref[...] = jnp.zeros_like(loss_ref)
            gg_ref[...] = jnp.zeros_like(gg_ref)

        gain_v = gain_ref[...]
        xh, r = _rms(h_ref[...])
        err = xh * gain_v - t_ref[...]
        loss_ref[...] += _colsum8(err * err)
        dy = err * (1.0 / D)
        gg_ref[...] += _colsum8(dy * xh)
        do_ref[...] = _rms_bwd(dy, xh, r, gain_v)

    vec = jax.ShapeDtypeStruct((8, D), F32)
    return pl.pallas_call(
        body, name="head", grid=(T // tm,),
        in_specs=[_rows(tm, D), _rows(tm, D), _const((1, D))],
        out_specs=[_rows(tm, D), _const((8, D)), _const((8, D))],
        out_shape=[jax.ShapeDtypeStruct((T, D), F32), vec, vec], compiler_params=_params(),
    )(h, target, gain)


_BIG = ["ffn1_w_gate", "ffn1_w_up", "ffn1_w_down", "w_in", "pool_w", "conv_w_out", "w_out", "ffn2_w_gate",
        "ffn2_w_up", "ffn2_w_down", "ple_w_gate", "ple_w_proj"]
_VECS = ["ffn1_norm", "mix_norm", "pool_scale", "conv_dw_b", "conv_ln_g", "conv_ln_b", "ffn2_norm", "ple_norm"]
_WEIGHTS = ["ffn1_norm", "ffn1_w_gate", "ffn1_w_up", "ffn1_w_down", "mix_norm", "w_in", "pool_w", "pool_scale",
            "conv_dw_w", "conv_dw_b", "conv_ln_g", "conv_ln_b", "conv_w_out", "w_out", "ffn2_norm", "ffn2_w_gate",
            "ffn2_w_up", "ffn2_w_down", "ple_norm", "ple_w_gate", "ple_w_proj", "final_norm"]


def _step(x, p, loss_target, w, mom, var):
    T, D = x.shape[1], x.shape[2]
    L = p.shape[0]
    FS, NS = w["ffn1_w_gate"].shape[2], w["w_in"].shape[2]
    PD = p.shape[3]
    PGS, PG = w["pool_w"].shape[2], w["pool_w"].shape[3]
    CS = w["conv_dw_w"].shape[2]
    lay = _Layout(D, FS, NS, PD, PG, PGS)
    R, FB, DS = lay.R, lay.FB, lay.DS
    ax, ay, ac = _place()
    me = 4 * ax + 2 * ay + ac

    gathered = [_all_gather_slab(_pack_layer(w, l, lay, BF16)) for l in range(L)]
    cw_mine = jnp.pad(w["conv_dw_w"], ((0, 0), (0, CONV_KP - CONV_K), (0, 0)))
    cw_rows = L * CONV_KP * CS // D
    cw_all, _ = _all_gather_small(cw_mine.reshape(cw_rows, D))
    cw_full = cw_all.reshape(NDEV, L, CONV_KP, CS).transpose(1, 2, 0, 3).reshape(L, CONV_KP, D)

    def small_mats(l):
        G = gathered[l]
        wppt = G[:, lay.off["wpp"]:lay.off["wpp"] + lay.rows["wpp"]].reshape(D, PD)
        pw = G[:, lay.off["pool"]:lay.off["pool"] + lay.rows["pool"]].reshape(NDEV, 4, PGS, PG)
        return wppt, pw.transpose(1, 0, 2, 3).reshape(4, PG, PG)

    def vec(name, l):
        return w[name][l].reshape(1, D)

    ffn_offs = {1: (lay.off["g1"], lay.off["u1"], lay.off["d1"]), 2: (lay.off["g2"], lay.off["u2"], lay.off["d2"])}

    h = x[0]
    saved = []
    for l in range(L):
        G = gathered[l]
        wppt, pw = small_mats(l)
        s = {"h0": h, "wppt": wppt, "pw": pw}
        h, s["g1"], s["u1"] = _ffn_fwd(h, vec("ffn1_norm", l), G, ffn_offs[1], lay, "ffn_fwd")
        s["h1"] = h
        s["z"] = _inproj_fwd(h, vec("mix_norm", l), G, lay)
        h, s["c1"], s["q"], s["cc"], s["pooled"] = _mixer_fwd(
            s["z"], h, pw, vec("pool_scale", l), cw_full[l], vec("conv_dw_b", l), vec("conv_ln_g", l),
            vec("conv_ln_b", l), G, lay)
        s["h2"] = h
        h, s["g2"], s["u2"] = _ffn_fwd(h, vec("ffn2_norm", l), G, ffn_offs[2], lay, "ffn_fwd")
        s["h3"] = h
        h, s["gate"] = _ple_fwd(h, p[l, 0], vec("ple_norm", l), wppt, G, lay)
        saved.append(s)

    d, loss_part, g_final = _head(h, loss_target[0], w["final_norm"].reshape(1, D))
    loss = lax.psum((0.5 / D) * jnp.sum(loss_part), AXES)

    big_grads = [None] * L
    small_parts = [None] * L
    for l in reversed(range(L)):
        G, s = gathered[l], saved[l]
        wppt, pw = s["wppt"], s["pw"]
        d, dpre, de, n_ple, pb, g_ple = _ple_bwd(d, s["h3"], p[l, 0], s["gate"], vec("ple_norm", l), wppt, G, lay)
        g_wpg = _tn_matmul(n_ple, dpre, "tn_sq", _tile(D, 512))
        g_wppt = _tn_matmul(de, pb, "tn_proj", _tile(D, 512))

        d, dg2, du2, a2, n2, dh2, g_n2 = _ffn_bwd(d, s["h2"], vec("ffn2_norm", l), s["g2"], s["u2"], G, ffn_offs[2],
                                                  lay, "ffn_bwd")
        g_g2 = _tn_matmul(dg2, n2, "tn_ffn", 2 * FB)
        g_u2 = _tn_matmul(du2, n2, "tn_ffn", 2 * FB)
        g_d2 = _tn_matmul(a2, dh2, "tn_ffn", 2 * FB)

        (m_b, d_b, dcc, c3, dq, dpool, dc1, dzg, g_ps, g_lg, g_lb, g_cb) = _mixer_bwd_rows(
            d, s["z"], s["c1"], s["q"], s["cc"], pw, vec("pool_scale", l), vec("conv_ln_g", l), vec("conv_ln_b", l),
            G, lay)
        g_wout = _tn_matmul(m_b, d_b, "tn_sq", _tile(D, 512))
        g_wco = _tn_matmul(c3, dcc, "tn_sq", _tile(D, 512))
        g_pool = _tn_matmul(s["pooled"], dq, "tn_pool", PG, y_follows=True)
        dzm, g_cw = _mixer_bwd_time(dc1, dpool, s["z"], cw_full[l], lay)
        d, u_b, g_mix = _inproj_bwd(d, s["h1"], vec("mix_norm", l), dzm, dzg, G, lay)
        g_win = jnp.concatenate([_tn_matmul(dzm, u_b, "tn_in3", _tile(D, 512)),
                                 _tn_matmul(dzg, u_b, "tn_in2", _tile(D, 512))], axis=0)

        d, dg1, du1, a1, n1, dh1, g_n1 = _ffn_bwd(d, s["h0"], vec("ffn1_norm", l), s["g1"], s["u1"], G, ffn_offs[1],
                                                  lay, "ffn_bwd")
        g_g1 = _tn_matmul(dg1, n1, "tn_ffn", 2 * FB)
        g_u1 = _tn_matmul(du1, n1, "tn_ffn", 2 * FB)
        g_d1 = _tn_matmul(a1, dh1, "tn_ffn", 2 * FB)

        blocks = [g_g1.reshape(NDEV, FB, D), g_u1.reshape(NDEV, FB, D), g_d1.reshape(NDEV, FB, D),
                  g_g2.reshape(NDEV, FB, D), g_u2.reshape(NDEV, FB, D), g_d2.reshape(NDEV, FB, D),
                  g_win.reshape(NDEV, NS, D), g_wco.reshape(NDEV, DS, D), g_wout.reshape(NDEV, DS, D),
                  g_wpg.reshape(NDEV, DS, D), g_wppt.reshape(NDEV, DS * PD // D, D),
                  g_pool.reshape(4, NDEV, PGS, PG).transpose(1, 0, 2, 3).reshape(NDEV, 4 * PGS * PG // D, D)]
        big_grads[l] = jnp.concatenate(blocks, axis=1)
        small_parts[l] = [g_n1, g_mix, g_ps, g_cb, g_lg, g_lb, g_n2, g_ple, g_cw]
    grad_x = d[None]

    gr = jnp.concatenate(big_grads, axis=1)
    got_pair = _rs_pair(gr)
    chips = [(1 - ax, ay), (ax, 1 - ay), (1 - ax, 1 - ay)]
    chip_idx = jnp.stack([2 * cx + cy for cx, cy in chips]).astype(jnp.int32)
    part = _rs_add(gr, got_pair, 2 * chip_idx + ac, chip_idx)
    got_chips = _rs_chips(part)
    own = [lax.dynamic_index_in_dim(gr, me, 0, keepdims=False),
           lax.dynamic_index_in_dim(got_pair, 2 * ax + ay, 0, keepdims=False)]
    own += [got_chips[j] for j in range(3)]

    def slab32(src):
        return jnp.concatenate([_pack_layer(src, l, lay, F32) for l in range(L)], axis=0)

    big_out = _adamw(own, slab32(w), slab32(mom), slab32(var), "adamw_big")

    rows = []
    for l in range(L):
        rows += [v.sum(axis=0, keepdims=True) for v in small_parts[l][:-1]]
        rows.append(small_parts[l][-1].reshape(CONV_KP, 8, D).sum(axis=1))
    rows.append(g_final.sum(axis=0, keepdims=True))
    n_rows = sum(r.shape[0] for r in rows)
    pad = -n_rows % 8
    _, small_sum = _all_gather_small(jnp.pad(jnp.concatenate(rows, axis=0), ((0, pad), (0, 0))))
    per_layer = len(_VECS) + CONV_KP
    g_vecs = {n: jnp.stack([small_sum[l * per_layer + k] for l in range(L)]) for k, n in enumerate(_VECS)}
    g_cw_full = jnp.stack([small_sum[l * per_layer + len(_VECS):(l + 1) * per_layer] for l in range(L)])
    g_cw_mine = lax.dynamic_slice_in_dim(g_cw_full, me * CS, CS, axis=2)
    g_fin = small_sum[L * per_layer]

    def small_slab(src, conv, fin):
        parts = [src[n] for n in _VECS] + [fin.reshape(1, D), conv.reshape(cw_rows, D)]
        sl = jnp.concatenate(parts, axis=0)
        return jnp.pad(sl, ((0, -sl.shape[0] % 8), (0, 0)))

    padk = ((0, 0), (0, CONV_KP - CONV_K), (0, 0))
    small_out = _adamw(
        [small_slab(g_vecs, g_cw_mine, g_fin)],
        small_slab(w, jnp.pad(w["conv_dw_w"], padk), w["final_norm"]),
        small_slab(mom, jnp.pad(mom["conv_dw_w"], padk), mom["final_norm"]),
        small_slab(var, jnp.pad(var["conv_dw_w"], padk), var["final_norm"]), "adamw_small")

    outs = []
    nv = len(_VECS)
    for big, small in zip(big_out, small_out):
        per = [_unpack_layer(big[l * R:(l + 1) * R], lay) for l in range(L)]
        res = {n: jnp.stack([per[l][n] for l in range(L)]) for n in _BIG}
        for k, n in enumerate(_VECS):
            res[n] = small[k * L:(k + 1) * L]
        res["final_norm"] = small[nv * L]
        res["conv_dw_w"] = small[nv * L + 1:nv * L + 1 + cw_rows].reshape(L, CONV_KP, CS)[:, :CONV_K]
        outs.append(res)
    return loss, grad_x, outs


def kernel(x, p, ffn1_norm, ffn1_w_gate, ffn1_w_up, ffn1_w_down, mix_norm, w_in, pool_w, pool_scale, conv_dw_w, conv_dw_b, conv_ln_g, conv_ln_b, conv_w_out, w_out, ffn2_norm, ffn2_w_gate, ffn2_w_up, ffn2_w_down, ple_norm, ple_w_gate, ple_w_proj, final_norm, loss_target, m_ffn1_norm, m_ffn1_w_gate, m_ffn1_w_up, m_ffn1_w_down, m_mix_norm, m_w_in, m_pool_w, m_pool_scale, m_conv_dw_w, m_conv_dw_b, m_conv_ln_g, m_conv_ln_b, m_conv_w_out, m_w_out, m_ffn2_norm, m_ffn2_w_gate, m_ffn2_w_up, m_ffn2_w_down, m_ple_norm, m_ple_w_gate, m_ple_w_proj, m_final_norm, v_ffn1_norm, v_ffn1_w_gate, v_ffn1_w_up, v_ffn1_w_down, v_mix_norm, v_w_in, v_pool_w, v_pool_scale, v_conv_dw_w, v_conv_dw_b, v_conv_ln_g, v_conv_ln_b, v_conv_w_out, v_w_out, v_ffn2_norm, v_ffn2_w_gate, v_ffn2_w_up, v_ffn2_w_down, v_ple_norm, v_ple_w_gate, v_ple_w_proj, v_final_norm):
    given = dict(locals())
    w = {n: given[n] for n in _WEIGHTS}
    mom = {n: given["m_" + n] for n in _WEIGHTS}
    var = {n: given["v_" + n] for n in _WEIGHTS}
    loss, grad_x, (grads, deltas, new_m, new_v) = _step(x, p, loss_target, w, mom, var)
    out = [loss, grad_x]
    for res in (grads, deltas, new_m, new_v):
        out += [res[n] for n in _WEIGHTS]
    return tuple(out)
```

```python
import functools

import jax
import jax.numpy as jnp
from jax import lax
from jax.experimental import pallas as pl
from jax.experimental.pallas import tpu as pltpu

F32, BF16 = jnp.float32, jnp.bfloat16
NDEV = 8
MESH = pl.DeviceIdType.MESH
HALO = 32
POOL_WINDOWS = (2, 4, 8, 16)
CONV_K = 31
CONV_KP = 32
RMS_EPS, LN_EPS = 1e-6, 1e-5
ADAM_LR, ADAM_B1, ADAM_B2, ADAM_EPS, ADAM_WD, ADAM_STEP = 0.001, 0.9, 0.999, 1e-08, 0.01, 10
LANE = 128
VMEM_LIMIT = 56 * 1024 * 1024
ANY = pl.BlockSpec(memory_space=pl.ANY)


def _nn(a, b):
    return jnp.dot(a, b, preferred_element_type=F32)


def _nt(a, b):
    return lax.dot_general(a, b, (((1,), (1,)), ((), ())), preferred_element_type=F32)


def _tn(a, b):
    return lax.dot_general(a, b, (((0,), (0,)), ((), ())), preferred_element_type=F32)


def _colsum8(v):
    return jnp.sum(v.reshape(v.shape[0] // 8, 8, v.shape[1]), axis=0)


def _sig(v):
    return jax.nn.sigmoid(v)


def _rms(h):
    r = lax.rsqrt(jnp.mean(h * h, axis=-1, keepdims=True) + RMS_EPS)
    return h * r, r


def _rms_bwd(dn, xh, r, gain):
    dxh = dn * gain
    return r * (dxh - xh * jnp.mean(dxh * xh, axis=-1, keepdims=True))


def _ln(c1):
    mu = jnp.mean(c1, axis=-1, keepdims=True)
    cen = c1 - mu
    rstd = lax.rsqrt(jnp.mean(cen * cen, axis=-1, keepdims=True) + LN_EPS)
    return cen * rstd, rstd


def _rows(tm, c):
    return pl.BlockSpec((tm, c), lambda i: (i, 0))


def _const(shape):
    return pl.BlockSpec(shape, lambda i: (0,) * len(shape))


def _params(n_grid=1):
    return pltpu.CompilerParams(dimension_semantics=("arbitrary",) * n_grid, vmem_limit_bytes=VMEM_LIMIT)


def _tile(n, want):
    t = min(n, want)
    while n % t:
        t //= 2
    return t


def _fetch(g_hbm, specs, sems):
    cps = []
    for wi, (off, rows, dst) in enumerate(specs):
        for dev in range(NDEV):
            cps.append(pltpu.make_async_copy(g_hbm.at[dev, pl.ds(off, rows), :],
                                             dst.at[pl.ds(dev * rows, rows), :], sems.at[wi, dev]))
    for cp in cps:
        cp.start()
    for cp in cps:
        cp.wait()


PIECES = ("A", "B", "C", "D")


class _Layout:
    def __init__(self, D, FS, NS, PD, PG, PGS):
        self.D, self.FS, self.NS, self.PD, self.PG, self.PGS = D, FS, NS, PD, PG, PGS
        self.FB = -(-FS // LANE) * LANE
        self.DS = D // NDEV
        self.pieces = {
            "A": [("g1", self.FB), ("u1", self.FB), ("d1", self.FB)],
            "B": [("g2", self.FB), ("u2", self.FB), ("d2", self.FB)],
            "C": [("win", NS)],
            "D": [("wco", self.DS), ("wout", self.DS), ("wpg", self.DS), ("wpp", self.DS * PD // D),
                  ("pool", 4 * PGS * PG // D)]}
        self.off, self.rows, self.piece_rows = {}, {}, {}
        for pc, entries in self.pieces.items():
            o = 0
            for n, r in entries:
                self.off[n], self.rows[n] = o, r
                o += r
            self.piece_rows[pc] = o


def _pad_rows(a, r):
    return jnp.pad(a, ((0, r - a.shape[0]), (0, 0)))


def _pack_piece(w, l, lay, pc, dtype):
    D, FB = lay.D, lay.FB
    make = {"g1": lambda: _pad_rows(w["ffn1_w_gate"][l].T, FB), "u1": lambda: _pad_rows(w["ffn1_w_up"][l].T, FB),
            "d1": lambda: _pad_rows(w["ffn1_w_down"][l], FB), "g2": lambda: _pad_rows(w["ffn2_w_gate"][l].T, FB),
            "u2": lambda: _pad_rows(w["ffn2_w_up"][l].T, FB), "d2": lambda: _pad_rows(w["ffn2_w_down"][l], FB),
            "win": lambda: w["w_in"][l].T, "wco": lambda: w["conv_w_out"][l], "wout": lambda: w["w_out"][l],
            "wpg": lambda: w["ple_w_gate"][l], "wpp": lambda: w["ple_w_proj"][l].T.reshape(-1, D),
            "pool": lambda: w["pool_w"][l].reshape(-1, D)}
    return jnp.concatenate([make[n]() for n, _ in lay.pieces[pc]], axis=0).astype(dtype)


def _unpack_piece(slab, lay, pc):
    FS, PD, PG, PGS, DS = lay.FS, lay.PD, lay.PG, lay.PGS, lay.DS
    undo = {"g1": ("ffn1_w_gate", lambda a: a[:FS].T), "u1": ("ffn1_w_up", lambda a: a[:FS].T),
            "d1": ("ffn1_w_down", lambda a: a[:FS]), "g2": ("ffn2_w_gate", lambda a: a[:FS].T),
            "u2": ("ffn2_w_up", lambda a: a[:FS].T), "d2": ("ffn2_w_down", lambda a: a[:FS]),
            "win": ("w_in", lambda a: a.T), "wco": ("conv_w_out", lambda a: a), "wout": ("w_out", lambda a: a),
            "wpg": ("ple_w_gate", lambda a: a), "wpp": ("ple_w_proj", lambda a: a.reshape(DS, PD).T),
            "pool": ("pool_w", lambda a: a.reshape(4, PGS, PG))}
    out = {}
    for n, r in lay.pieces[pc]:
        name, fn = undo[n]
        out[name] = fn(slab[lay.off[n]:lay.off[n] + r])
    return out


def _place():
    return lax.axis_index("x"), lax.axis_index("y"), lax.axis_index("c")


FLIPS = [(a, b, d) for a in (0, 1) for b in (0, 1) for d in (0, 1)][1:]


class _GatherJob:
    def __init__(self, tag, slab):
        self.tag, self.ins = tag, [slab]
        self.outs = [jax.ShapeDtypeStruct((NDEV,) + slab.shape, slab.dtype)]
        self.scratch = [pltpu.SemaphoreType.DMA((7,)), pltpu.SemaphoreType.DMA((7,)), pltpu.SemaphoreType.DMA]
        self.results = None

    def _plan(self, ins, outs, sems):
        (x_ref,), (out_ref,), (send_sems, recv_sems, local_sem) = ins, outs, sems
        x, y, c = _place()
        me, sibling = (x, y, c), (x, y, 1 - c)
        chips = [(1 - x, y), (x, 1 - y), (1 - x, 1 - y)]

        def rows(px, py, pc):
            return out_ref.at[4 * px + 2 * py + pc]

        def copy(k, block, to, src=None):
            return pltpu.make_async_remote_copy(
                src_ref=rows(*block) if src is None else src, dst_ref=rows(*block),
                send_sem=send_sems.at[k], recv_sem=recv_sems.at[k], device_id=to, device_id_type=MESH)

        mine = pltpu.make_async_copy(x_ref, rows(*me), local_sem)
        first = [copy(0, me, sibling, src=x_ref)]
        first += [copy(1 + j, me, (*chip, c), src=x_ref) for j, chip in enumerate(chips)]
        passed = [copy(4 + j, (*chip, c), sibling) for j, chip in enumerate(chips)]
        landed = [copy(1 + j, (*chip, c), me) for j, chip in enumerate(chips)]
        late = [copy(0, sibling, me)] + [copy(4 + j, (*chip, 1 - c), me) for j, chip in enumerate(chips)]
        return mine, first, passed, landed, late

    def start(self, ins, outs, sems):
        mine, first, _, _, _ = self._plan(ins, outs, sems)
        mine.start()
        for cp in first:
            cp.start()

    def middle(self, ins, outs, sems):
        _, _, passed, landed, _ = self._plan(ins, outs, sems)
        for got, cp in zip(landed, passed):
            got.wait_recv()
            cp.start()

    def finish(self, ins, outs, sems):
        mine, first, passed, _, late = self._plan(ins, outs, sems)
        for got in late:
            got.wait_recv()
        for cp in first + passed:
            cp.wait_send()
        mine.wait()


class _ScatterJob:
    def __init__(self, tag, grads):
        self.tag, self.ins = tag, [grads]
        self.outs = [jax.ShapeDtypeStruct((7,) + grads.shape[1:], grads.dtype)]
        self.scratch = [pltpu.SemaphoreType.DMA((7,)), pltpu.SemaphoreType.DMA((7,))]
        self.results = None

    def _plan(self, ins, outs, sems):
        (g_ref,), (out_ref,), (send_sems, recv_sems) = ins, outs, sems
        x, y, c = _place()
        cps = []
        for k, (a, b, d) in enumerate(FLIPS):
            px, py, pc = x ^ a, y ^ b, c ^ d
            cps.append(pltpu.make_async_remote_copy(
                src_ref=g_ref.at[4 * px + 2 * py + pc], dst_ref=out_ref.at[k], send_sem=send_sems.at[k],
                recv_sem=recv_sems.at[k], device_id=(px, py, pc), device_id_type=MESH))
        return cps

    def start(self, ins, outs, sems):
        for cp in self._plan(ins, outs, sems):
            cp.start()

    def middle(self, ins, outs, sems):
        pass

    def finish(self, ins, outs, sems):
        cps = self._plan(ins, outs, sems)
        for cp in cps:
            cp.wait_recv()
        for cp in cps:
            cp.wait_send()


def _launch(name, body, steps, args, in_specs, out_specs, out_shape, scratch, jobs=()):
    n_in, n_out, n_sc = len(args), len(out_shape), len(scratch)
    j_in = [a for jb in jobs for a in jb.ins]
    j_out = [o for jb in jobs for o in jb.outs]
    j_sc = [s for jb in jobs for s in jb.scratch]
    mid = (3 * steps) // 5

    def wrapped(*refs):
        c_in, refs = refs[:n_in], refs[n_in:]
        m_in, refs = refs[:len(j_in)], refs[len(j_in):]
        c_out, refs = refs[:n_out], refs[n_out:]
        m_out, refs = refs[:len(j_out)], refs[len(j_out):]
        c_sc, m_sc = refs[:n_sc], refs[n_sc:]
        bound, a, b, c = [], 0, 0, 0
        for jb in jobs:
            bound.append((jb, m_in[a:a + len(jb.ins)], m_out[b:b + len(jb.outs)], m_sc[c:c + len(jb.scratch)]))
            a, b, c = a + len(jb.ins), b + len(jb.outs), c + len(jb.scratch)
        i = pl.program_id(0)

        def phase(step, which):
            if jobs:
                @pl.when(i == step)
                def _():
                    for jb, ins, outs, sems in bound:
                        getattr(jb, which)(ins, outs, sems)

        phase(0, "start")
        if body is not None:
            body(*c_in, *c_out, *c_sc)
        phase(mid, "middle")
        phase(steps - 1, "finish")

    outs = pl.pallas_call(
        wrapped, name=name + "".join("_" + jb.tag for jb in jobs), grid=(steps,),
        in_specs=list(in_specs) + [ANY] * len(j_in), out_specs=list(out_specs) + [ANY] * len(j_out),
        out_shape=list(out_shape) + j_out, scratch_shapes=list(scratch) + j_sc, compiler_params=_params(),
    )(*args, *j_in)
    pos = n_out
    for jb in jobs:
        jb.results = list(outs[pos:pos + len(jb.outs)])
        pos += len(jb.outs)
    return list(outs[:n_out])


def _run_jobs(name, jobs):
    _launch(name, None, 1, [], [], [], [], [], jobs)


def _all_gather_small(rows_in):
    S, D = rows_in.shape

    def body(x_ref, out_ref, sum_ref, send_sems, recv_sems):
        x, y, c = _place()
        me = 4 * x + 2 * y + c
        out_ref[me] = x_ref[...]
        flips = [(a, b, d) for a in (0, 1) for b in (0, 1) for d in (0, 1)][1:]

        def copy(k):
            a, b, d = flips[k]
            px, py, pc = x ^ a, y ^ b, c ^ d
            peer = 4 * px + 2 * py + pc
            send = pltpu.make_async_remote_copy(
                src_ref=x_ref, dst_ref=out_ref.at[me], send_sem=send_sems.at[k], recv_sem=recv_sems.at[k],
                device_id=(px, py, pc), device_id_type=MESH)
            recv = pltpu.make_async_remote_copy(
                src_ref=x_ref, dst_ref=out_ref.at[peer], send_sem=send_sems.at[k], recv_sem=recv_sems.at[k],
                device_id=(px, py, pc), device_id_type=MESH)
            return send, recv

        cps = [copy(k) for k in range(7)]
        for send, _ in cps:
            send.start()
        for _, recv in cps:
            recv.wait_recv()
        for send, _ in cps:
            send.wait_send()
        acc = out_ref[0]
        for j in range(1, NDEV):
            acc = acc + out_ref[j]
        sum_ref[...] = acc

    vm = pl.BlockSpec(memory_space=pltpu.VMEM)
    return pl.pallas_call(
        body, name="ag_small",
        out_shape=(jax.ShapeDtypeStruct((NDEV, S, D), F32), jax.ShapeDtypeStruct((S, D), F32)),
        in_specs=[vm], out_specs=(vm, vm),
        scratch_shapes=[pltpu.SemaphoreType.DMA((7,)), pltpu.SemaphoreType.DMA((7,))],
    )(rows_in)


def _adamw(own, got, w, m, v, name):
    rows, D = w.shape
    tr = _tile(rows, 256)
    n_got = 0 if got is None else got.shape[0]
    n = 1 + n_got
    parts = [own] + [got] * n_got
    part_specs = [_rows(tr, D)] + [pl.BlockSpec((None, tr, D), functools.partial(lambda k, i: (k, i, 0), k))
                                   for k in range(n_got)]

    def body(*refs):
        part_refs, (w_ref, m_ref, v_ref, g_out, d_out, m_out, v_out) = refs[:n], refs[n:]
        g = part_refs[0][...].astype(F32)
        for pr in part_refs[1:]:
            g = g + pr[...].astype(F32)
        m_new = ADAM_B1 * m_ref[...] + (1.0 - ADAM_B1) * g
        v_new = ADAM_B2 * v_ref[...] + (1.0 - ADAM_B2) * (g * g)
        m_hat = m_new / (1.0 - ADAM_B1 ** ADAM_STEP)
        v_hat = v_new / (1.0 - ADAM_B2 ** ADAM_STEP)
        g_out[...] = g
        d_out[...] = -ADAM_LR * (m_hat / (jnp.sqrt(v_hat) + ADAM_EPS) + ADAM_WD * w_ref[...])
        m_out[...] = m_new
        v_out[...] = v_new

    return pl.pallas_call(
        body, name=name, grid=(rows // tr,),
        in_specs=part_specs + [_rows(tr, D)] * 3, out_specs=[_rows(tr, D)] * 4,
        out_shape=[jax.ShapeDtypeStruct((rows, D), F32)] * 4, compiler_params=_params(),
    )(*parts, w, m, v)


def _tn_matmul(xa, ya, name, tmm, y_follows=False):
    T, M = xa.shape
    tn = tmm if y_follows else ya.shape[1]
    tt = _tile(T, 1024)
    nb = M // tmm

    def body(x_ref, y_ref, o_ref, acc):
        k = pl.program_id(1)

        @pl.when(k == 0)
        def _():
            acc[...] = jnp.zeros_like(acc)

        acc[...] += _tn(x_ref[...], y_ref[...])

        @pl.when(k == pl.num_programs(1) - 1)
        def _():
            o_ref[...] = acc[...].astype(o_ref.dtype)

    y_map = (lambda b, k: (k, b)) if y_follows else (lambda b, k: (k, 0))
    return pl.pallas_call(
        body, name=name, grid=(nb, T // tt),
        in_specs=[pl.BlockSpec((tt, tmm), lambda b, k: (k, b)), pl.BlockSpec((tt, tn), y_map)],
        out_specs=pl.BlockSpec((tmm, tn), lambda b, k: (b, 0)),
        out_shape=jax.ShapeDtypeStruct((M, tn), BF16),
        scratch_shapes=[pltpu.VMEM((tmm, tn), F32)], compiler_params=_params(2),
    )(xa, ya)


def _ffn_fwd(h, gain, G, lay, jobs=()):
    T, D = h.shape
    FB = lay.FB
    FP, CH = NDEV * FB, 2 * FB
    offs = (0, FB, 2 * FB)
    tm = _tile(T, 512)

    def body(h_ref, gain_ref, g_hbm, ho_ref, go_ref, uo_ref, wg, wu, wd, sems):
        @pl.when(pl.program_id(0) == 0)
        def _():
            _fetch(g_hbm, [(offs[0], FB, wg), (offs[1], FB, wu), (offs[2], FB, wd)], sems)

        h = h_ref[...]
        xh, _ = _rms(h)
        n = (xh * gain_ref[...]).astype(BF16)
        acc = jnp.zeros((tm, D), F32)
        for j in range(FP // CH):
            sl = pl.ds(j * CH, CH)
            g = _nt(n, wg[sl, :])
            u = _nt(n, wu[sl, :])
            go_ref[:, sl] = g.astype(BF16)
            uo_ref[:, sl] = u.astype(BF16)
            a = g * _sig(g) * u
            acc = acc + _nn(a.astype(BF16), wd[sl, :])
        ho_ref[...] = h + 0.5 * acc

    return _launch(
        "ffn_fwd", body, T // tm, [h, gain, G],
        [_rows(tm, D), _const((1, D)), ANY], [_rows(tm, D), _rows(tm, FP), _rows(tm, FP)],
        [jax.ShapeDtypeStruct((T, D), F32), jax.ShapeDtypeStruct((T, FP), BF16), jax.ShapeDtypeStruct((T, FP), BF16)],
        [pltpu.VMEM((FP, D), BF16)] * 3 + [pltpu.SemaphoreType.DMA((3, NDEV))], jobs)


def _ffn_bwd(d, h, gain, ga, ua, G, lay, jobs=()):
    T, D = h.shape
    FB = lay.FB
    FP, CH = NDEV * FB, 2 * FB
    offs = (0, FB, 2 * FB)
    tm = _tile(T, 256)

    def body(d_ref, h_ref, gain_ref, ga_ref, ua_ref, g_hbm,
             do_ref, dg_ref, du_ref, a_ref, n_ref, dh_ref, gg_ref, wg, wu, wd, sems):
        @pl.when(pl.program_id(0) == 0)
        def _():
            _fetch(g_hbm, [(offs[0], FB, wg), (offs[1], FB, wu), (offs[2], FB, wd)], sems)
            gg_ref[...] = jnp.zeros_like(gg_ref)

        d = d_ref[...]
        gain_v = gain_ref[...]
        xh, r = _rms(h_ref[...])
        n_ref[...] = (xh * gain_v).astype(BF16)
        dh = (0.5 * d).astype(BF16)
        dh_ref[...] = dh
        dn = jnp.zeros((tm, D), F32)
        for j in range(FP // CH):
            sl = pl.ds(j * CH, CH)
            g = ga_ref[:, sl].astype(F32)
            u = ua_ref[:, sl].astype(F32)
            da = _nt(dh, wd[sl, :])
            s = _sig(g)
            silu = g * s
            a_ref[:, sl] = (silu * u).astype(BF16)
            dgv = (da * u * (s * (1.0 + g * (1.0 - s)))).astype(BF16)
            duv = (da * silu).astype(BF16)
            dg_ref[:, sl] = dgv
            du_ref[:, sl] = duv
            dn = dn + _nn(dgv, wg[sl, :]) + _nn(duv, wu[sl, :])
        gg_ref[...] += _colsum8(dn * xh)
        do_ref[...] = d + _rms_bwd(dn, xh, r, gain_v)

    wide = jax.ShapeDtypeStruct((T, FP), BF16)
    return _launch(
        "ffn_bwd", body, T // tm, [d, h, gain, ga, ua, G],
        [_rows(tm, D), _rows(tm, D), _const((1, D)), _rows(tm, FP), _rows(tm, FP), ANY],
        [_rows(tm, D), _rows(tm, FP), _rows(tm, FP), _rows(tm, FP), _rows(tm, D), _rows(tm, D), _const((8, D))],
        [jax.ShapeDtypeStruct((T, D), F32), wide, wide, wide, jax.ShapeDtypeStruct((T, D), BF16),
         jax.ShapeDtypeStruct((T, D), BF16), jax.ShapeDtypeStruct((8, D), F32)],
        [pltpu.VMEM((FP, D), BF16)] * 3 + [pltpu.SemaphoreType.DMA((3, NDEV))], jobs)


def _inproj_fwd(h, gain, G, lay):
    T, D = h.shape
    NS = lay.NS
    NIN, CH = NDEV * NS, 2 * NS
    tm = _tile(T, 512)

    def body(h_ref, gain_ref, g_hbm, z_ref, win, sems):
        @pl.when(pl.program_id(0) == 0)
        def _():
            _fetch(g_hbm, [(lay.off["win"], NS, win)], sems)

        xh, _ = _rms(h_ref[...])
        n = (xh * gain_ref[...]).astype(BF16)
        for j in range(NIN // CH):
            sl = pl.ds(j * CH, CH)
            z_ref[:, sl] = _nt(n, win[sl, :]).astype(BF16)

    return pl.pallas_call(
        body, name="inproj_fwd", grid=(T // tm,),
        in_specs=[_rows(tm, D), _const((1, D)), ANY], out_specs=_rows(tm, NIN),
        out_shape=jax.ShapeDtypeStruct((T, NIN), BF16),
        scratch_shapes=[pltpu.VMEM((NIN, D), BF16), pltpu.SemaphoreType.DMA((1, NDEV))],
        compiler_params=_params(),
    )(h, gain, G)


def _conv_chunks(tm, D):
    rb, lc = min(tm, 64), min(D, 256)
    return [(r0, l0, rb, lc) for r0 in range(0, tm, rb) for l0 in range(0, D, lc)]


def _mixer_fwd(z, h, pool_w, pscale, cw, cb, lng, lnb, G, lay, jobs=()):
    T, D = h.shape
    PG, DS = lay.PG, lay.DS
    tm = _tile(T, 256)
    hb = tm // HALO

    def body(z_ref, zp_ref, h_ref, pw_ref, ps_ref, cw_ref, cb_ref, lg_ref, lb_ref, g_hbm,
             ho_ref, c1_ref, q_ref, cc_ref, pool_ref, ext_p, ext_c, a_s, wco, wout, sems):
        i = pl.program_id(0)

        @pl.when(i == 0)
        def _():
            _fetch(g_hbm, [(lay.off["wco"], DS, wco), (lay.off["wout"], DS, wout)], sems)

        live = jnp.where(i > 0, 1.0, 0.0).astype(F32)
        ext_p[pl.ds(0, HALO), :] = zp_ref[:, pl.ds(0, D)].astype(F32) * live
        ext_p[pl.ds(HALO, tm), :] = z_ref[:, pl.ds(0, D)].astype(F32)
        ext_c[pl.ds(0, HALO), :] = (zp_ref[:, pl.ds(D, D)].astype(F32)
                                    * _sig(zp_ref[:, pl.ds(2 * D, D)].astype(F32)) * live)
        ext_c[pl.ds(HALO, tm), :] = z_ref[:, pl.ds(D, D)].astype(F32) * _sig(z_ref[:, pl.ds(2 * D, D)].astype(F32))

        t = i * tm + lax.broadcasted_iota(jnp.int32, (tm, 1), 0)
        for g, w in enumerate(POOL_WINDOWS):
            sl = pl.ds(g * PG, PG)
            s = ext_p[pl.ds(HALO, tm), sl]
            zc = s
            for j in range(1, w):
                s = s + ext_p[pl.ds(HALO - j, tm), sl]
            inv = 1.0 / jnp.minimum(t + 1, w).astype(F32)
            pooled = (s * inv - zc).astype(BF16)
            pool_ref[:, sl] = pooled
            qv = _nn(pooled, pw_ref[g])
            q_ref[:, sl] = qv.astype(BF16)
            a_s[:, sl] = qv * ps_ref[:, sl]

        for r0, l0, rb, lc in _conv_chunks(tm, D):
            ls = pl.ds(l0, lc)
            acc = jnp.zeros((rb, lc), F32) + cb_ref[:, ls]
            for k in range(CONV_K):
                acc = acc + cw_ref[pl.ds(k, 1), ls] * ext_c[pl.ds(r0 + HALO - (CONV_K - 1) + k, rb), ls]
            c1_ref[pl.ds(r0, rb), ls] = acc

        xhat, _ = _ln(c1_ref[...])
        c2 = xhat * lg_ref[...] + lb_ref[...]
        c3 = (c2 * _sig(c2)).astype(BF16)
        cc = _nn(c3, wco[...])
        cc_ref[...] = cc.astype(BF16)
        gp = z_ref[:, pl.ds(3 * D, D)].astype(F32)
        gc = z_ref[:, pl.ds(4 * D, D)].astype(F32)
        m = (_sig(gp) * a_s[...] + _sig(gc) * cc).astype(BF16)
        ho_ref[...] = h_ref[...] + _nn(m, wout[...])

    act = jax.ShapeDtypeStruct((T, D), BF16)
    return _launch(
        "mixer_fwd", body, T // tm, [z, z, h, pool_w, pscale, cw, cb, lng, lnb, G],
        [_rows(tm, 5 * D), pl.BlockSpec((HALO, 5 * D), lambda i: (jnp.maximum(i * hb - 1, 0), 0)),
         _rows(tm, D), _const((4, PG, PG)), _const((1, D)), _const((CONV_KP, D)), _const((1, D)),
         _const((1, D)), _const((1, D)), ANY],
        [_rows(tm, D)] * 5,
        [jax.ShapeDtypeStruct((T, D), F32), jax.ShapeDtypeStruct((T, D), F32), act, act, act],
        [pltpu.VMEM((HALO + tm, D), F32), pltpu.VMEM((HALO + tm, D), F32), pltpu.VMEM((tm, D), F32),
         pltpu.VMEM((D, D), BF16), pltpu.VMEM((D, D), BF16), pltpu.SemaphoreType.DMA((2, NDEV))], jobs)


def _mixer_bwd_rows(d, z, c1, qa, cca, pool_w, pscale, lng, lnb, G, lay):
    T, D = d.shape
    PG, DS = lay.PG, lay.DS
    tm = _tile(T, 256)

    def body(d_ref, z_ref, c1_ref, q_ref, cc_ref, pw_ref, ps_ref, lg_ref, lb_ref, g_hbm,
             m_ref, db_ref, dcc_ref, c3_ref, dq_ref, dpool_ref, dc1_ref, dzg_ref, gps_ref, glg_ref, glb_ref, gcb_ref,
             wco, wout, sems):
        @pl.when(pl.program_id(0) == 0)
        def _():
            _fetch(g_hbm, [(lay.off["wco"], DS, wco), (lay.off["wout"], DS, wout)], sems)
            for ref in (gps_ref, glg_ref, glb_ref, gcb_ref):
                ref[...] = jnp.zeros_like(ref)

        db = d_ref[...].astype(BF16)
        db_ref[...] = db
        dm = _nt(db, wout[...])
        q = q_ref[...].astype(F32)
        cc = cc_ref[...].astype(F32)
        ps = ps_ref[...]
        sp = _sig(z_ref[:, pl.ds(3 * D, D)].astype(F32))
        sc = _sig(z_ref[:, pl.ds(4 * D, D)].astype(F32))
        a = q * ps
        m_ref[...] = (sp * a + sc * cc).astype(BF16)
        da = dm * sp
        dzg_ref[:, pl.ds(0, D)] = (dm * a * sp * (1.0 - sp)).astype(BF16)
        dzg_ref[:, pl.ds(D, D)] = (dm * cc * sc * (1.0 - sc)).astype(BF16)
        gps_ref[...] += _colsum8(da * q)
        dq = (da * ps).astype(BF16)
        dq_ref[...] = dq
        for g in range(len(POOL_WINDOWS)):
            sl = pl.ds(g * PG, PG)
            dpool_ref[:, sl] = _nt(dq_ref[:, sl], pw_ref[g])

        dcc = (dm * sc).astype(BF16)
        dcc_ref[...] = dcc
        xhat, rstd = _ln(c1_ref[...])
        lg = lg_ref[...]
        c2 = xhat * lg + lb_ref[...]
        s2 = _sig(c2)
        c3_ref[...] = (c2 * s2).astype(BF16)
        dc2 = _nt(dcc, wco[...]) * (s2 * (1.0 + c2 * (1.0 - s2)))
        glg_ref[...] += _colsum8(dc2 * xhat)
        glb_ref[...] += _colsum8(dc2)
        dxh = dc2 * lg
        dc1 = rstd * (dxh - jnp.mean(dxh, axis=-1, keepdims=True)
                      - xhat * jnp.mean(dxh * xhat, axis=-1, keepdims=True))
        dc1_ref[...] = dc1
        gcb_ref[...] += _colsum8(dc1)

    act = jax.ShapeDtypeStruct((T, D), BF16)
    full = jax.ShapeDtypeStruct((T, D), F32)
    vec = jax.ShapeDtypeStruct((8, D), F32)
    return pl.pallas_call(
        body, name="mixer_bwd_rows", grid=(T // tm,),
        in_specs=[_rows(tm, D), _rows(tm, 5 * D), _rows(tm, D), _rows(tm, D), _rows(tm, D),
                  _const((4, PG, PG)), _const((1, D)), _const((1, D)), _const((1, D)), ANY],
        out_specs=[_rows(tm, D)] * 7 + [_rows(tm, 2 * D)] + [_const((8, D))] * 4,
        out_shape=[act, act, act, act, act, full, full, jax.ShapeDtypeStruct((T, 2 * D), BF16), vec, vec, vec, vec],
        scratch_shapes=[pltpu.VMEM((D, D), BF16), pltpu.VMEM((D, D), BF16), pltpu.SemaphoreType.DMA((2, NDEV))],
        compiler_params=_params(),
    )(d, z, c1, qa, cca, pool_w, pscale, lng, lnb, G)


def _mixer_bwd_time(dc1, dpool, z, cw, lay, jobs=()):
    T, D = dc1.shape
    PG = lay.PG
    tm = _tile(T, 256)
    hb = tm // HALO
    nt = T // tm

    def body(dc_ref, dcn_ref, dp_ref, dpn_ref, z_ref, zp_ref, cw_ref, dz_ref, gcw_ref, ext_d, ext_q, ext_c, dc0_s):
        i = pl.program_id(0)

        @pl.when(i == 0)
        def _():
            gcw_ref[...] = jnp.zeros_like(gcw_ref)

        live_p = jnp.where(i > 0, 1.0, 0.0).astype(F32)
        live_n = jnp.where(i < nt - 1, 1.0, 0.0).astype(F32)
        ext_d[pl.ds(0, tm), :] = dc_ref[...]
        ext_d[pl.ds(tm, HALO), :] = dcn_ref[...] * live_n
        zg = z_ref[:, pl.ds(2 * D, D)].astype(F32)
        za = z_ref[:, pl.ds(D, D)].astype(F32)
        sg = _sig(zg)
        ext_c[pl.ds(0, HALO), :] = (zp_ref[:, pl.ds(D, D)].astype(F32)
                                    * _sig(zp_ref[:, pl.ds(2 * D, D)].astype(F32)) * live_p)
        ext_c[pl.ds(HALO, tm), :] = za * sg

        t = i * tm + lax.broadcasted_iota(jnp.int32, (tm, 1), 0)
        tn = (i + 1) * tm + lax.broadcasted_iota(jnp.int32, (HALO, 1), 0)
        for g, w in enumerate(POOL_WINDOWS):
            sl = pl.ds(g * PG, PG)
            ext_q[pl.ds(0, tm), sl] = dp_ref[:, sl] * (1.0 / jnp.minimum(t + 1, w).astype(F32))
            ext_q[pl.ds(tm, HALO), sl] = dpn_ref[:, sl] * (live_n / jnp.minimum(tn + 1, w).astype(F32))
        for g, w in enumerate(POOL_WINDOWS):
            sl = pl.ds(g * PG, PG)
            s = ext_q[pl.ds(0, tm), sl]
            for j in range(1, w):
                s = s + ext_q[pl.ds(j, tm), sl]
            dz_ref[:, sl] = (s - dp_ref[:, sl]).astype(BF16)

        for r0, l0, rb, lc in _conv_chunks(tm, D):
            ls = pl.ds(l0, lc)
            acc = jnp.zeros((rb, lc), F32)
            for j in range(CONV_K):
                acc = acc + cw_ref[pl.ds(CONV_K - 1 - j, 1), ls] * ext_d[pl.ds(r0 + j, rb), ls]
            dc0_s[pl.ds(r0, rb), ls] = acc
        dc0 = dc0_s[...]
        dz_ref[:, pl.ds(D, D)] = (dc0 * sg).astype(BF16)
        dz_ref[:, pl.ds(2 * D, D)] = (dc0 * za * sg * (1.0 - sg)).astype(BF16)

        lc = min(D, 256)
        for l0 in range(0, D, lc):
            ls = pl.ds(l0, lc)
            dcv = dc_ref[:, ls]
            for k in range(CONV_K):
                gcw_ref[pl.ds(8 * k, 8), ls] += _colsum8(dcv * ext_c[pl.ds(HALO - (CONV_K - 1) + k, tm), ls])

    nxt = lambda i: (jnp.minimum((i + 1) * hb, T // HALO - 1), 0)
    return _launch(
        "mixer_bwd_time", body, nt, [dc1, dc1, dpool, dpool, z, z, cw],
        [_rows(tm, D), pl.BlockSpec((HALO, D), nxt), _rows(tm, D), pl.BlockSpec((HALO, D), nxt),
         _rows(tm, 5 * D), pl.BlockSpec((HALO, 5 * D), lambda i: (jnp.maximum(i * hb - 1, 0), 0)),
         _const((CONV_KP, D))],
        [_rows(tm, 3 * D), _const((CONV_KP * 8, D))],
        [jax.ShapeDtypeStruct((T, 3 * D), BF16), jax.ShapeDtypeStruct((CONV_KP * 8, D), F32)],
        [pltpu.VMEM((tm + HALO, D), F32), pltpu.VMEM((tm + HALO, D), F32),
         pltpu.VMEM((HALO + tm, D), F32), pltpu.VMEM((tm, D), F32)], jobs)


def _inproj_bwd(d, h, gain, dzm, dzg, G, lay):
    T, D = h.shape
    NS = lay.NS
    NIN = NDEV * NS
    tm = _tile(T, 512)

    def body(d_ref, h_ref, gain_ref, dzm_ref, dzg_ref, g_hbm, do_ref, u_ref, gg_ref, win, sems):
        @pl.when(pl.program_id(0) == 0)
        def _():
            _fetch(g_hbm, [(lay.off["win"], NS, win)], sems)
            gg_ref[...] = jnp.zeros_like(gg_ref)

        gain_v = gain_ref[...]
        xh, r = _rms(h_ref[...])
        u_ref[...] = (xh * gain_v).astype(BF16)
        dn = jnp.zeros((tm, D), F32)
        for j in range(3):
            dn = dn + _nn(dzm_ref[:, pl.ds(j * D, D)], win[pl.ds(j * D, D), :])
        for j in range(2):
            dn = dn + _nn(dzg_ref[:, pl.ds(j * D, D)], win[pl.ds((3 + j) * D, D), :])
        gg_ref[...] += _colsum8(dn * xh)
        do_ref[...] = d_ref[...] + _rms_bwd(dn, xh, r, gain_v)

    return pl.pallas_call(
        body, name="inproj_bwd", grid=(T // tm,),
        in_specs=[_rows(tm, D), _rows(tm, D), _const((1, D)), _rows(tm, 3 * D), _rows(tm, 2 * D), ANY],
        out_specs=[_rows(tm, D), _rows(tm, D), _const((8, D))],
        out_shape=[jax.ShapeDtypeStruct((T, D), F32), jax.ShapeDtypeStruct((T, D), BF16),
                   jax.ShapeDtypeStruct((8, D), F32)],
        scratch_shapes=[pltpu.VMEM((NIN, D), BF16), pltpu.SemaphoreType.DMA((1, NDEV))],
        compiler_params=_params(),
    )(d, h, gain, dzm, dzg, G)


def _ple_fwd(h, pe, gain, wppt, G, lay):
    T, D = h.shape
    PD, DS = lay.PD, lay.DS
    tm = _tile(T, 512)

    def body(h_ref, p_ref, gain_ref, wpp_ref, g_hbm, ho_ref, gate_ref, wpg, sems):
        @pl.when(pl.program_id(0) == 0)
        def _():
            _fetch(g_hbm, [(lay.off["wpg"], DS, wpg)], sems)

        h = h_ref[...]
        xh, _ = _rms(h)
        n = (xh * gain_ref[...]).astype(BF16)
        gate = _sig(_nn(n, wpg[...]))
        gate_ref[...] = gate.astype(BF16)
        e = _nt(p_ref[...].astype(BF16), wpp_ref[...])
        ho_ref[...] = h + gate * e

    return pl.pallas_call(
        body, name="ple_fwd", grid=(T // tm,),
        in_specs=[_rows(tm, D), _rows(tm, PD), _const((1, D)), _const((D, PD)), ANY],
        out_specs=[_rows(tm, D), _rows(tm, D)],
        out_shape=[jax.ShapeDtypeStruct((T, D), F32), jax.ShapeDtypeStruct((T, D), BF16)],
        scratch_shapes=[pltpu.VMEM((D, D), BF16), pltpu.SemaphoreType.DMA((1, NDEV))],
        compiler_params=_params(),
    )(h, pe, gain, wppt, G)


def _ple_bwd(d, h, pe, gate_a, gain, wppt, G, lay):
    T, D = h.shape
    PD, DS = lay.PD, lay.DS
    tm = _tile(T, 512)

    def body(d_ref, h_ref, p_ref, gate_ref, gain_ref, wpp_ref, g_hbm,
             do_ref, dpre_ref, de_ref, n_ref, pb_ref, gg_ref, wpg, sems):
        @pl.when(pl.program_id(0) == 0)
        def _():
            _fetch(g_hbm, [(lay.off["wpg"], DS, wpg)], sems)
            gg_ref[...] = jnp.zeros_like(gg_ref)

        d = d_ref[...]
        gain_v = gain_ref[...]
        xh, r = _rms(h_ref[...])
        n_ref[...] = (xh * gain_v).astype(BF16)
        pb = p_ref[...].astype(BF16)
        pb_ref[...] = pb
        e = _nt(pb, wpp_ref[...])
        gate = gate_ref[...].astype(F32)
        de_ref[...] = (d * gate).astype(BF16)
        dpre = (d * e * gate * (1.0 - gate)).astype(BF16)
        dpre_ref[...] = dpre
        dn = _nt(dpre, wpg[...])
        gg_ref[...] += _colsum8(dn * xh)
        do_ref[...] = d + _rms_bwd(dn, xh, r, gain_v)

    act = jax.ShapeDtypeStruct((T, D), BF16)
    return pl.pallas_call(
        body, name="ple_bwd", grid=(T // tm,),
        in_specs=[_rows(tm, D), _rows(tm, D), _rows(tm, PD), _rows(tm, D), _const((1, D)), _const((D, PD)), ANY],
        out_specs=[_rows(tm, D), _rows(tm, D), _rows(tm, D), _rows(tm, D), _rows(tm, PD), _const((8, D))],
        out_shape=[jax.ShapeDtypeStruct((T, D), F32), act, act, act, jax.ShapeDtypeStruct((T, PD), BF16),
                   jax.ShapeDtypeStruct((8, D), F32)],
        scratch_shapes=[pltpu.VMEM((D, D), BF16), pltpu.SemaphoreType.DMA((1, NDEV))],
        compiler_params=_params(),
    )(d, h, pe, gate_a, gain, wppt, G)


def _head(h, target, gain):
    T, D = h.shape
    tm = _tile(T, 512)

    def body(h_ref, t_ref, gain_ref, do_ref, loss_ref, gg_ref):
        @pl.when(pl.program_id(0) == 0)
        def _():
            loss_ref[...] = jnp.zeros_like(loss_ref)
            gg_ref[...] = jnp.zeros_like(gg_ref)

        gain_v = gain_ref[...]
        xh, r = _rms(h_ref[...])
        err = xh * gain_v - t_ref[...]
        loss_ref[...] += _colsum8(err * err)
        dy = err * (1.0 / D)
        gg_ref[...] += _colsum8(dy * xh)
        do_ref[...] = _rms_bwd(dy, xh, r, gain_v)

    vec = jax.ShapeDtypeStruct((8, D), F32)
    return pl.pallas_call(
        body, name="head", grid=(T // tm,),
        in_specs=[_rows(tm, D), _rows(tm, D), _const((1, D))],
        out_specs=[_rows(tm, D), _const((8, D)), _const((8, D))],
        out_shape=[jax.ShapeDtypeStruct((T, D), F32), vec, vec], compiler_params=_params(),
    )(h, target, gain)


_BIG = ["ffn1_w_gate", "ffn1_w_up", "ffn1_w_down", "w_in", "pool_w", "conv_w_out", "w_out", "ffn2_w_gate",
        "ffn2_w_up", "ffn2_w_down", "ple_w_gate", "ple_w_proj"]
_VECS = ["ffn1_norm", "mix_norm", "pool_scale", "conv_dw_b", "conv_ln_g", "conv_ln_b", "ffn2_norm", "ple_norm"]
_WEIGHTS = ["ffn1_norm", "ffn1_w_gate", "ffn1_w_up", "ffn1_w_down", "mix_norm", "w_in", "pool_w", "pool_scale",
            "conv_dw_w", "conv_dw_b", "conv_ln_g", "conv_ln_b", "conv_w_out", "w_out", "ffn2_norm", "ffn2_w_gate",
            "ffn2_w_up", "ffn2_w_down", "ple_norm", "ple_w_gate", "ple_w_proj", "final_norm"]


def _step(x, p, loss_target, w, mom, var):
    T, D = x.shape[1], x.shape[2]
    L = p.shape[0]
    FS, NS = w["ffn1_w_gate"].shape[2], w["w_in"].shape[2]
    PD = p.shape[3]
    PGS, PG = w["pool_w"].shape[2], w["pool_w"].shape[3]
    CS = w["conv_dw_w"].shape[2]
    lay = _Layout(D, FS, NS, PD, PG, PGS)
    FB = lay.FB
    ax, ay, ac = _place()
    me = 4 * ax + 2 * ay + ac

    assert L == 2, "the exchange schedule below is written for two layers"
    gather = {(l, pc): _GatherJob(f"ag{pc}", _pack_piece(w, l, lay, pc, BF16)) for l in range(L) for pc in PIECES}
    fwd_jobs = {("ffn1", 0): [gather[0, "C"], gather[0, "D"]], ("mixer", 0): [gather[0, "B"], gather[1, "A"]],
                ("ffn2", 0): [gather[1, "C"], gather[1, "D"]], ("ffn1", 1): [gather[1, "B"]]}
    _run_jobs("gather_first", [gather[0, "A"]])
    cw_mine = jnp.pad(w["conv_dw_w"], ((0, 0), (0, CONV_KP - CONV_K), (0, 0)))
    cw_rows = L * CONV_KP * CS // D
    cw_all, _ = _all_gather_small(cw_mine.reshape(cw_rows, D))
    cw_full = cw_all.reshape(NDEV, L, CONV_KP, CS).transpose(1, 2, 0, 3).reshape(L, CONV_KP, D)

    def gathered(l, pc):
        return gather[l, pc].results[0]

    def small_mats(l):
        G = gathered(l, "D")
        wppt = G[:, lay.off["wpp"]:lay.off["wpp"] + lay.rows["wpp"]].reshape(D, PD)
        pw = G[:, lay.off["pool"]:lay.off["pool"] + lay.rows["pool"]].reshape(NDEV, 4, PGS, PG)
        return wppt, pw.transpose(1, 0, 2, 3).reshape(4, PG, PG)

    def vec(name, l):
        return w[name][l].reshape(1, D)

    h = x[0]
    saved = []
    for l in range(L):
        s = {"h0": h}
        h, s["g1"], s["u1"] = _ffn_fwd(h, vec("ffn1_norm", l), gathered(l, "A"), lay, fwd_jobs.get(("ffn1", l), ()))
        wppt, pw = small_mats(l)
        s["wppt"], s["pw"], s["h1"] = wppt, pw, h
        s["z"] = _inproj_fwd(h, vec("mix_norm", l), gathered(l, "C"), lay)
        h, s["c1"], s["q"], s["cc"], s["pooled"] = _mixer_fwd(
            s["z"], h, pw, vec("pool_scale", l), cw_full[l], vec("conv_dw_b", l), vec("conv_ln_g", l),
            vec("conv_ln_b", l), gathered(l, "D"), lay, fwd_jobs.get(("mixer", l), ()))
        s["h2"] = h
        h, s["g2"], s["u2"] = _ffn_fwd(h, vec("ffn2_norm", l), gathered(l, "B"), lay, fwd_jobs.get(("ffn2", l), ()))
        s["h3"] = h
        h, s["gate"] = _ple_fwd(h, p[l, 0], vec("ple_norm", l), wppt, gathered(l, "D"), lay)
        saved.append(s)

    d, loss_part, g_final = _head(h, loss_target[0], w["final_norm"].reshape(1, D))

    scatter = {}
    small_parts = [None] * L

    def send(l, pc, blocks):
        scatter[l, pc] = _ScatterJob(f"rs{pc}", jnp.concatenate(
            [blocks[n].reshape(NDEV, lay.rows[n], D) for n, _ in lay.pieces[pc]], axis=1))
        return scatter[l, pc]

    held = []
    for l in reversed(range(L)):
        s = saved[l]
        wppt, pw = s["wppt"], s["pw"]
        g = {}
        d, dpre, de, n_ple, pb, g_ple = _ple_bwd(d, s["h3"], p[l, 0], s["gate"], vec("ple_norm", l), wppt,
                                                 gathered(l, "D"), lay)
        g["wpg"] = _tn_matmul(n_ple, dpre, "tn_sq", _tile(D, 512))
        g["wpp"] = _tn_matmul(de, pb, "tn_proj", _tile(D, 512))

        d, dg2, du2, a2, n2, dh2, g_n2 = _ffn_bwd(d, s["h2"], vec("ffn2_norm", l), s["g2"], s["u2"],
                                                  gathered(l, "B"), lay, held)
        held = []
        g["g2"] = _tn_matmul(dg2, n2, "tn_ffn", 2 * FB)
        g["u2"] = _tn_matmul(du2, n2, "tn_ffn", 2 * FB)
        g["d2"] = _tn_matmul(a2, dh2, "tn_ffn", 2 * FB)
        held.append(send(l, "B", g))

        (m_b, d_b, dcc, c3, dq, dpool, dc1, dzg, g_ps, g_lg, g_lb, g_cb) = _mixer_bwd_rows(
            d, s["z"], s["c1"], s["q"], s["cc"], pw, vec("pool_scale", l), vec("conv_ln_g", l), vec("conv_ln_b", l),
            gathered(l, "D"), lay)
        g["wout"] = _tn_matmul(m_b, d_b, "tn_sq", _tile(D, 512))
        g["wco"] = _tn_matmul(c3, dcc, "tn_sq", _tile(D, 512))
        g_pool = _tn_matmul(s["pooled"], dq, "tn_pool", PG, y_follows=True)
        g["pool"] = g_pool.reshape(4, NDEV, PGS, PG).transpose(1, 0, 2, 3)
        held.append(send(l, "D", g))
        dzm, g_cw = _mixer_bwd_time(dc1, dpool, s["z"], cw_full[l], lay, held)
        held = []
        d, u_b, g_mix = _inproj_bwd(d, s["h1"], vec("mix_norm", l), dzm, dzg, gathered(l, "C"), lay)
        g["win"] = jnp.concatenate([_tn_matmul(dzm, u_b, "tn_in3", _tile(D, 512)),
                                    _tn_matmul(dzg, u_b, "tn_in2", _tile(D, 512))], axis=0)
        held.append(send(l, "C", g))

        d, dg1, du1, a1, n1, dh1, g_n1 = _ffn_bwd(d, s["h0"], vec("ffn1_norm", l), s["g1"], s["u1"],
                                                  gathered(l, "A"), lay, held)
        held = []
        g["g1"] = _tn_matmul(dg1, n1, "tn_ffn", 2 * FB)
        g["u1"] = _tn_matmul(du1, n1, "tn_ffn", 2 * FB)
        g["d1"] = _tn_matmul(a1, dh1, "tn_ffn", 2 * FB)
        held.append(send(l, "A", g))
        small_parts[l] = [g_n1, g_mix, g_ps, g_cb, g_lg, g_lb, g_n2, g_ple, g_cw]
    _run_jobs("scatter_last", held)
    grad_x = d[None]

    big_out = {}
    for (l, pc), job in scatter.items():
        own = lax.dynamic_index_in_dim(job.ins[0], me, 0, keepdims=False)
        big_out[l, pc] = _adamw(own, job.results[0], _pack_piece(w, l, lay, pc, F32), _pack_piece(mom, l, lay, pc, F32),
                                _pack_piece(var, l, lay, pc, F32), "adamw_" + pc)

    rows = []
    for l in range(L):
        rows += [v.sum(axis=0, keepdims=True) for v in small_parts[l][:-1]]
        rows.append(small_parts[l][-1].reshape(CONV_KP, 8, D).sum(axis=1))
    rows.append(g_final.sum(axis=0, keepdims=True))
    rows.append(loss_part.sum(axis=0, keepdims=True))
    n_rows = sum(r.shape[0] for r in rows)
    pad = -n_rows % 8
    _, small_sum = _all_gather_small(jnp.pad(jnp.concatenate(rows, axis=0), ((0, pad), (0, 0))))
    per_layer = len(_VECS) + CONV_KP
    g_vecs = {n: jnp.stack([small_sum[l * per_layer + k] for l in range(L)]) for k, n in enumerate(_VECS)}
    g_cw_full = jnp.stack([small_sum[l * per_layer + len(_VECS):(l + 1) * per_layer] for l in range(L)])
    g_cw_mine = lax.dynamic_slice_in_dim(g_cw_full, me * CS, CS, axis=2)
    g_fin = small_sum[L * per_layer]
    loss = (0.5 / D) * jnp.sum(small_sum[L * per_layer + 1])

    def small_slab(src, conv, fin):
        parts = [src[n] for n in _VECS] + [fin.reshape(1, D), conv.reshape(cw_rows, D)]
        sl = jnp.concatenate(parts, axis=0)
        return jnp.pad(sl, ((0, -sl.shape[0] % 8), (0, 0)))

    padk = ((0, 0), (0, CONV_KP - CONV_K), (0, 0))
    small_out = _adamw(
        small_slab(g_vecs, g_cw_mine, g_fin), None,
        small_slab(w, jnp.pad(w["conv_dw_w"], padk), w["final_norm"]),
        small_slab(mom, jnp.pad(mom["conv_dw_w"], padk), mom["final_norm"]),
        small_slab(var, jnp.pad(var["conv_dw_w"], padk), var["final_norm"]), "adamw_small")

    outs = []
    nv = len(_VECS)
    for which, small in enumerate(small_out):
        per = [{} for _ in range(L)]
        for (l, pc), slabs in big_out.items():
            per[l].update(_unpack_piece(slabs[which], lay, pc))
        res = {n: jnp.stack([per[l][n] for l in range(L)]) for n in _BIG}
        for k, n in enumerate(_VECS):
            res[n] = small[k * L:(k + 1) * L]
        res["final_norm"] = small[nv * L]
        res["conv_dw_w"] = small[nv * L + 1:nv * L + 1 + cw_rows].reshape(L, CONV_KP, CS)[:, :CONV_K]
        outs.append(res)
    return loss, grad_x, outs


def kernel(x, p, ffn1_norm, ffn1_w_gate, ffn1_w_up, ffn1_w_down, mix_norm, w_in, pool_w, pool_scale, conv_dw_w, conv_dw_b, conv_ln_g, conv_ln_b, conv_w_out, w_out, ffn2_norm, ffn2_w_gate, ffn2_w_up, ffn2_w_down, ple_norm, ple_w_gate, ple_w_proj, final_norm, loss_target, m_ffn1_norm, m_ffn1_w_gate, m_ffn1_w_up, m_ffn1_w_down, m_mix_norm, m_w_in, m_pool_w, m_pool_scale, m_conv_dw_w, m_conv_dw_b, m_conv_ln_g, m_conv_ln_b, m_conv_w_out, m_w_out, m_ffn2_norm, m_ffn2_w_gate, m_ffn2_w_up, m_ffn2_w_down, m_ple_norm, m_ple_w_gate, m_ple_w_proj, m_final_norm, v_ffn1_norm, v_ffn1_w_gate, v_ffn1_w_up, v_ffn1_w_down, v_mix_norm, v_w_in, v_pool_w, v_pool_scale, v_conv_dw_w, v_conv_dw_b, v_conv_ln_g, v_conv_ln_b, v_conv_w_out, v_w_out, v_ffn2_norm, v_ffn2_w_gate, v_ffn2_w_up, v_ffn2_w_down, v_ple_norm, v_ple_w_gate, v_ple_w_proj, v_final_norm):
    given = dict(locals())
    w = {n: given[n] for n in _WEIGHTS}
    mom = {n: given["m_" + n] for n in _WEIGHTS}
    var = {n: given["v_" + n] for n in _WEIGHTS}
    loss, grad_x, (grads, deltas, new_m, new_v) = _step(x, p, loss_target, w, mom, var)
    out = [loss, grad_x]
    for res in (grads, deltas, new_m, new_v):
        out += [res[n] for n in _WEIGHTS]
    return tuple(out)
```

```python
import functools

import jax
import jax.numpy as jnp
from jax import lax
from jax.experimental import pallas as pl
from jax.experimental.pallas import tpu as pltpu

F32, BF16 = jnp.float32, jnp.bfloat16
NDEV = 8
MESH = pl.DeviceIdType.MESH
HALO = 32
POOL_WINDOWS = (2, 4, 8, 16)
CONV_K = 31
CONV_KP = 32
RMS_EPS, LN_EPS = 1e-6, 1e-5
ADAM_LR, ADAM_B1, ADAM_B2, ADAM_EPS, ADAM_WD, ADAM_STEP = 0.001, 0.9, 0.999, 1e-08, 0.01, 10
LANE = 128
VMEM_LIMIT = 56 * 1024 * 1024
ANY = pl.BlockSpec(memory_space=pl.ANY)


def _nn(a, b):
    return jnp.dot(a, b, preferred_element_type=F32)


def _nt(a, b):
    return lax.dot_general(a, b, (((1,), (1,)), ((), ())), preferred_element_type=F32)


def _tn(a, b):
    return lax.dot_general(a, b, (((0,), (0,)), ((), ())), preferred_element_type=F32)


def _colsum8(v):
    return jnp.sum(v.reshape(v.shape[0] // 8, 8, v.shape[1]), axis=0)


def _sig(v):
    return jax.nn.sigmoid(v)


def _rms(h):
    r = lax.rsqrt(jnp.mean(h * h, axis=-1, keepdims=True) + RMS_EPS)
    return h * r, r


def _rms_bwd(dn, xh, r, gain):
    dxh = dn * gain
    return r * (dxh - xh * jnp.mean(dxh * xh, axis=-1, keepdims=True))


def _ln(c1):
    mu = jnp.mean(c1, axis=-1, keepdims=True)
    cen = c1 - mu
    rstd = lax.rsqrt(jnp.mean(cen * cen, axis=-1, keepdims=True) + LN_EPS)
    return cen * rstd, rstd


def _rows(tm, c):
    return pl.BlockSpec((tm, c), lambda i: (i, 0))


def _const(shape):
    return pl.BlockSpec(shape, lambda i: (0,) * len(shape))


def _params(n_grid=1):
    return pltpu.CompilerParams(dimension_semantics=("arbitrary",) * n_grid, vmem_limit_bytes=VMEM_LIMIT)


def _tile(n, want):
    t = min(n, want)
    while n % t:
        t //= 2
    return t


def _fetch(g_hbm, specs, sems):
    cps = []
    for wi, (off, rows, dst) in enumerate(specs):
        for dev in range(NDEV):
            cps.append(pltpu.make_async_copy(g_hbm.at[dev, pl.ds(off, rows), :],
                                             dst.at[pl.ds(dev * rows, rows), :], sems.at[wi, dev]))
    for cp in cps:
        cp.start()
    for cp in cps:
        cp.wait()


PIECES = ("A", "B", "C", "D")


class _Layout:
    def __init__(self, D, FS, NS, PD, PG, PGS):
        self.D, self.FS, self.NS, self.PD, self.PG, self.PGS = D, FS, NS, PD, PG, PGS
        self.FB = -(-FS // LANE) * LANE
        self.DS = D // NDEV
        self.pieces = {
            "A": [("g1", self.FB), ("u1", self.FB), ("d1", self.FB)],
            "B": [("g2", self.FB), ("u2", self.FB), ("d2", self.FB)],
            "C": [("win", NS)],
            "D": [("wco", self.DS), ("wout", self.DS), ("wpg", self.DS), ("wpp", self.DS * PD // D),
                  ("pool", 4 * PGS * PG // D)],
            "A1": [("g1", self.FB)], "A2": [("u1", self.FB)], "A3": [("d1", self.FB)]}
        self.off, self.rows = {}, {}
        for pc in PIECES:
            o = 0
            for n, r in self.pieces[pc]:
                self.off[n], self.rows[n] = o, r
                o += r


def _pad_rows(a, r):
    return jnp.pad(a, ((0, r - a.shape[0]), (0, 0)))


def _pack_piece(w, l, lay, pc, dtype):
    D, FB = lay.D, lay.FB
    make = {"g1": lambda: _pad_rows(w["ffn1_w_gate"][l].T, FB), "u1": lambda: _pad_rows(w["ffn1_w_up"][l].T, FB),
            "d1": lambda: _pad_rows(w["ffn1_w_down"][l], FB), "g2": lambda: _pad_rows(w["ffn2_w_gate"][l].T, FB),
            "u2": lambda: _pad_rows(w["ffn2_w_up"][l].T, FB), "d2": lambda: _pad_rows(w["ffn2_w_down"][l], FB),
            "win": lambda: w["w_in"][l].T, "wco": lambda: w["conv_w_out"][l], "wout": lambda: w["w_out"][l],
            "wpg": lambda: w["ple_w_gate"][l], "wpp": lambda: w["ple_w_proj"][l].T.reshape(-1, D),
            "pool": lambda: w["pool_w"][l].reshape(-1, D)}
    return jnp.concatenate([make[n]() for n, _ in lay.pieces[pc]], axis=0).astype(dtype)


def _unpack_piece(slab, lay, pc):
    FS, PD, PG, PGS, DS = lay.FS, lay.PD, lay.PG, lay.PGS, lay.DS
    undo = {"g1": ("ffn1_w_gate", lambda a: a[:FS].T), "u1": ("ffn1_w_up", lambda a: a[:FS].T),
            "d1": ("ffn1_w_down", lambda a: a[:FS]), "g2": ("ffn2_w_gate", lambda a: a[:FS].T),
            "u2": ("ffn2_w_up", lambda a: a[:FS].T), "d2": ("ffn2_w_down", lambda a: a[:FS]),
            "win": ("w_in", lambda a: a.T), "wco": ("conv_w_out", lambda a: a), "wout": ("w_out", lambda a: a),
            "wpg": ("ple_w_gate", lambda a: a), "wpp": ("ple_w_proj", lambda a: a.reshape(DS, PD).T),
            "pool": ("pool_w", lambda a: a.reshape(4, PGS, PG))}
    out, o = {}, 0
    for n, r in lay.pieces[pc]:
        name, fn = undo[n]
        out[name] = fn(slab[o:o + r])
        o += r
    return out


def _place():
    return lax.axis_index("x"), lax.axis_index("y"), lax.axis_index("c")


FLIPS = [(a, b, d) for a in (0, 1) for b in (0, 1) for d in (0, 1)][1:]


class _GatherJob:
    def __init__(self, tag, slab):
        self.tag, self.ins = tag, [slab]
        self.outs = [jax.ShapeDtypeStruct((NDEV,) + slab.shape, slab.dtype)]
        self.scratch = [pltpu.SemaphoreType.DMA((7,)), pltpu.SemaphoreType.DMA((7,)), pltpu.SemaphoreType.DMA]
        self.results = None

    def _plan(self, ins, outs, sems):
        (x_ref,), (out_ref,), (send_sems, recv_sems, local_sem) = ins, outs, sems
        x, y, c = _place()
        me, sibling = (x, y, c), (x, y, 1 - c)
        chips = [(1 - x, y), (x, 1 - y), (1 - x, 1 - y)]

        def rows(px, py, pc):
            return out_ref.at[4 * px + 2 * py + pc]

        def copy(k, block, to, src=None):
            return pltpu.make_async_remote_copy(
                src_ref=rows(*block) if src is None else src, dst_ref=rows(*block),
                send_sem=send_sems.at[k], recv_sem=recv_sems.at[k], device_id=to, device_id_type=MESH)

        mine = pltpu.make_async_copy(x_ref, rows(*me), local_sem)
        first = [copy(0, me, sibling, src=x_ref)]
        first += [copy(1 + j, me, (*chip, c), src=x_ref) for j, chip in enumerate(chips)]
        passed = [copy(4 + j, (*chip, c), sibling) for j, chip in enumerate(chips)]
        landed = [copy(1 + j, (*chip, c), me) for j, chip in enumerate(chips)]
        late = [copy(0, sibling, me)] + [copy(4 + j, (*chip, 1 - c), me) for j, chip in enumerate(chips)]
        return mine, first, passed, landed, late

    def start(self, ins, outs, sems):
        mine, first, _, _, _ = self._plan(ins, outs, sems)
        mine.start()
        for cp in first:
            cp.start()

    def middle(self, ins, outs, sems):
        _, _, passed, landed, _ = self._plan(ins, outs, sems)
        for got, cp in zip(landed, passed):
            got.wait_recv()
            cp.start()

    def finish(self, ins, outs, sems):
        mine, first, passed, _, late = self._plan(ins, outs, sems)
        for got in late:
            got.wait_recv()
        for cp in first + passed:
            cp.wait_send()
        mine.wait()


class _ScatterJob:
    def __init__(self, tag, grads):
        self.tag, self.ins = tag, [grads]
        self.outs = [jax.ShapeDtypeStruct((7,) + grads.shape[1:], grads.dtype)]
        self.scratch = [pltpu.SemaphoreType.DMA((7,)), pltpu.SemaphoreType.DMA((7,))]
        self.results = None

    def _plan(self, ins, outs, sems):
        (g_ref,), (out_ref,), (send_sems, recv_sems) = ins, outs, sems
        x, y, c = _place()
        cps = []
        for k, (a, b, d) in enumerate(FLIPS):
            px, py, pc = x ^ a, y ^ b, c ^ d
            cps.append(pltpu.make_async_remote_copy(
                src_ref=g_ref.at[4 * px + 2 * py + pc], dst_ref=out_ref.at[k], send_sem=send_sems.at[k],
                recv_sem=recv_sems.at[k], device_id=(px, py, pc), device_id_type=MESH))
        return cps

    def start(self, ins, outs, sems):
        for cp in self._plan(ins, outs, sems):
            cp.start()

    def middle(self, ins, outs, sems):
        pass

    def finish(self, ins, outs, sems):
        cps = self._plan(ins, outs, sems)
        for cp in cps:
            cp.wait_recv()
        for cp in cps:
            cp.wait_send()


class _RowsGatherJob:
    def __init__(self, tag, rows):
        self.tag, self.ins = tag, [rows]
        self.outs = [jax.ShapeDtypeStruct((NDEV,) + rows.shape, rows.dtype)]
        self.scratch = [pltpu.SemaphoreType.DMA((7,)), pltpu.SemaphoreType.DMA((7,)), pltpu.SemaphoreType.DMA]
        self.results = None

    def _plan(self, ins, outs, sems):
        (x_ref,), (out_ref,), (send_sems, recv_sems, local_sem) = ins, outs, sems
        x, y, c = _place()
        me = 4 * x + 2 * y + c
        mine = pltpu.make_async_copy(x_ref, out_ref.at[me], local_sem)
        sends, lands = [], []
        for k, (a, b, d) in enumerate(FLIPS):
            px, py, pc = x ^ a, y ^ b, c ^ d
            for dst, keep in ((out_ref.at[me], sends), (out_ref.at[4 * px + 2 * py + pc], lands)):
                keep.append(pltpu.make_async_remote_copy(
                    src_ref=x_ref, dst_ref=dst, send_sem=send_sems.at[k], recv_sem=recv_sems.at[k],
                    device_id=(px, py, pc), device_id_type=MESH))
        return mine, sends, lands

    def start(self, ins, outs, sems):
        mine, sends, _ = self._plan(ins, outs, sems)
        mine.start()
        for cp in sends:
            cp.start()

    def middle(self, ins, outs, sems):
        pass

    def finish(self, ins, outs, sems):
        mine, sends, lands = self._plan(ins, outs, sems)
        for cp in lands:
            cp.wait_recv()
        for cp in sends:
            cp.wait_send()
        mine.wait()


def _sum_slots(slots):
    n, S, D = slots.shape

    def body(s_ref, o_ref):
        acc = s_ref[0]
        for j in range(1, n):
            acc = acc + s_ref[j]
        o_ref[...] = acc

    vm = pl.BlockSpec(memory_space=pltpu.VMEM)
    return pl.pallas_call(body, name="sum_slots", out_shape=jax.ShapeDtypeStruct((S, D), slots.dtype),
                          in_specs=[vm], out_specs=vm)(slots)


def _launch(name, body, grid, args, in_specs, out_specs, out_shape, scratch, jobs=()):
    grid = (grid,) if isinstance(grid, int) else tuple(grid)
    steps = grid[0] * (grid[1] if len(grid) == 2 else 1)
    n_in, n_out, n_sc = len(args), len(out_shape), len(scratch)
    j_in = [a for jb in jobs for a in jb.ins]
    j_out = [o for jb in jobs for o in jb.outs]
    j_sc = [s for jb in jobs for s in jb.scratch]
    mid = (3 * steps) // 5

    def wrapped(*refs):
        c_in, refs = refs[:n_in], refs[n_in:]
        m_in, refs = refs[:len(j_in)], refs[len(j_in):]
        c_out, refs = refs[:n_out], refs[n_out:]
        m_out, refs = refs[:len(j_out)], refs[len(j_out):]
        c_sc, m_sc = refs[:n_sc], refs[n_sc:]
        bound, a, b, c = [], 0, 0, 0
        for jb in jobs:
            bound.append((jb, m_in[a:a + len(jb.ins)], m_out[b:b + len(jb.outs)], m_sc[c:c + len(jb.scratch)]))
            a, b, c = a + len(jb.ins), b + len(jb.outs), c + len(jb.scratch)
        i = pl.program_id(0) if len(grid) == 1 else pl.program_id(0) * grid[1] + pl.program_id(1)

        def phase(step, which):
            if jobs:
                @pl.when(i == step)
                def _():
                    for jb, ins, outs, sems in bound:
                        getattr(jb, which)(ins, outs, sems)

        phase(0, "start")
        if body is not None:
            body(*c_in, *c_out, *c_sc)
        phase(mid, "middle")
        phase(steps - 1, "finish")

    outs = pl.pallas_call(
        wrapped, name=name + "".join("_" + jb.tag for jb in jobs), grid=grid,
        in_specs=list(in_specs) + [ANY] * len(j_in), out_specs=list(out_specs) + [ANY] * len(j_out),
        out_shape=list(out_shape) + j_out, scratch_shapes=list(scratch) + j_sc, compiler_params=_params(len(grid)),
    )(*args, *j_in)
    pos = n_out
    for jb in jobs:
        jb.results = list(outs[pos:pos + len(jb.outs)])
        pos += len(jb.outs)
    return list(outs[:n_out])


def _run_jobs(name, jobs):
    _launch(name, None, 1, [], [], [], [], [], jobs)


def _all_gather_small(rows_in):
    S, D = rows_in.shape

    def body(x_ref, out_ref, sum_ref, send_sems, recv_sems):
        x, y, c = _place()
        me = 4 * x + 2 * y + c
        out_ref[me] = x_ref[...]
        flips = [(a, b, d) for a in (0, 1) for b in (0, 1) for d in (0, 1)][1:]

        def copy(k):
            a, b, d = flips[k]
            px, py, pc = x ^ a, y ^ b, c ^ d
            peer = 4 * px + 2 * py + pc
            send = pltpu.make_async_remote_copy(
                src_ref=x_ref, dst_ref=out_ref.at[me], send_sem=send_sems.at[k], recv_sem=recv_sems.at[k],
                device_id=(px, py, pc), device_id_type=MESH)
            recv = pltpu.make_async_remote_copy(
                src_ref=x_ref, dst_ref=out_ref.at[peer], send_sem=send_sems.at[k], recv_sem=recv_sems.at[k],
                device_id=(px, py, pc), device_id_type=MESH)
            return send, recv

        cps = [copy(k) for k in range(7)]
        for send, _ in cps:
            send.start()
        for _, recv in cps:
            recv.wait_recv()
        for send, _ in cps:
            send.wait_send()
        acc = out_ref[0]
        for j in range(1, NDEV):
            acc = acc + out_ref[j]
        sum_ref[...] = acc

    vm = pl.BlockSpec(memory_space=pltpu.VMEM)
    return pl.pallas_call(
        body, name="ag_small",
        out_shape=(jax.ShapeDtypeStruct((NDEV, S, D), F32), jax.ShapeDtypeStruct((S, D), F32)),
        in_specs=[vm], out_specs=(vm, vm),
        scratch_shapes=[pltpu.SemaphoreType.DMA((7,)), pltpu.SemaphoreType.DMA((7,))],
    )(rows_in)


def _adamw(own, got, w, m, v, name):
    rows, D = w.shape
    tr = _tile(rows, 256)
    n_got = 0 if got is None else got.shape[0]
    n = 1 + n_got
    parts = [own] + [got] * n_got
    part_specs = [_rows(tr, D)] + [pl.BlockSpec((None, tr, D), functools.partial(lambda k, i: (k, i, 0), k))
                                   for k in range(n_got)]

    def body(*refs):
        part_refs, (w_ref, m_ref, v_ref, g_out, d_out, m_out, v_out) = refs[:n], refs[n:]
        g = part_refs[0][...].astype(F32)
        for pr in part_refs[1:]:
            g = g + pr[...].astype(F32)
        m_new = ADAM_B1 * m_ref[...] + (1.0 - ADAM_B1) * g
        v_new = ADAM_B2 * v_ref[...] + (1.0 - ADAM_B2) * (g * g)
        m_hat = m_new / (1.0 - ADAM_B1 ** ADAM_STEP)
        v_hat = v_new / (1.0 - ADAM_B2 ** ADAM_STEP)
        g_out[...] = g
        d_out[...] = -ADAM_LR * (m_hat / (jnp.sqrt(v_hat) + ADAM_EPS) + ADAM_WD * w_ref[...])
        m_out[...] = m_new
        v_out[...] = v_new

    return pl.pallas_call(
        body, name=name, grid=(rows // tr,),
        in_specs=part_specs + [_rows(tr, D)] * 3, out_specs=[_rows(tr, D)] * 4,
        out_shape=[jax.ShapeDtypeStruct((rows, D), F32)] * 4, compiler_params=_params(),
    )(*parts, w, m, v)


def _tn_matmul(xa, ya, name, tmm, y_follows=False, jobs=()):
    T, M = xa.shape
    tn = tmm if y_follows else ya.shape[1]
    tt = _tile(T, 1024)
    nb = M // tmm

    def body(x_ref, y_ref, o_ref, acc):
        k = pl.program_id(1)

        @pl.when(k == 0)
        def _():
            acc[...] = jnp.zeros_like(acc)

        acc[...] += _tn(x_ref[...], y_ref[...])

        @pl.when(k == pl.num_programs(1) - 1)
        def _():
            o_ref[...] = acc[...].astype(o_ref.dtype)

    y_map = (lambda b, k: (k, b)) if y_follows else (lambda b, k: (k, 0))
    return _launch(
        name, body, (nb, T // tt), [xa, ya],
        [pl.BlockSpec((tt, tmm), lambda b, k: (k, b)), pl.BlockSpec((tt, tn), y_map)],
        [pl.BlockSpec((tmm, tn), lambda b, k: (b, 0))], [jax.ShapeDtypeStruct((M, tn), BF16)],
        [pltpu.VMEM((tmm, tn), F32)], jobs)[0]


def _ffn_fwd(h, gain, G, lay, jobs=()):
    T, D = h.shape
    FB = lay.FB
    FP, CH = NDEV * FB, 2 * FB
    offs = (0, FB, 2 * FB)
    tm = _tile(T, 512)

    def body(h_ref, gain_ref, g_hbm, ho_ref, go_ref, uo_ref, wg, wu, wd, sems):
        @pl.when(pl.program_id(0) == 0)
        def _():
            _fetch(g_hbm, [(offs[0], FB, wg), (offs[1], FB, wu), (offs[2], FB, wd)], sems)

        h = h_ref[...]
        xh, _ = _rms(h)
        n = (xh * gain_ref[...]).astype(BF16)
        acc = jnp.zeros((tm, D), F32)
        for j in range(FP // CH):
            sl = pl.ds(j * CH, CH)
            g = _nt(n, wg[sl, :])
            u = _nt(n, wu[sl, :])
            go_ref[:, sl] = g.astype(BF16)
            uo_ref[:, sl] = u.astype(BF16)
            a = g * _sig(g) * u
            acc = acc + _nn(a.astype(BF16), wd[sl, :])
        ho_ref[...] = h + 0.5 * acc

    return _launch(
        "ffn_fwd", body, T // tm, [h, gain, G],
        [_rows(tm, D), _const((1, D)), ANY], [_rows(tm, D), _rows(tm, FP), _rows(tm, FP)],
        [jax.ShapeDtypeStruct((T, D), F32), jax.ShapeDtypeStruct((T, FP), BF16), jax.ShapeDtypeStruct((T, FP), BF16)],
        [pltpu.VMEM((FP, D), BF16)] * 3 + [pltpu.SemaphoreType.DMA((3, NDEV))], jobs)


def _ffn_bwd(d, h, gain, ga, ua, G, lay, jobs=()):
    T, D = h.shape
    FB = lay.FB
    FP, CH = NDEV * FB, 2 * FB
    offs = (0, FB, 2 * FB)
    tm = _tile(T, 256)

    def body(d_ref, h_ref, gain_ref, ga_ref, ua_ref, g_hbm,
             do_ref, dg_ref, du_ref, a_ref, n_ref, dh_ref, gg_ref, wg, wu, wd, sems):
        @pl.when(pl.program_id(0) == 0)
        def _():
            _fetch(g_hbm, [(offs[0], FB, wg), (offs[1], FB, wu), (offs[2], FB, wd)], sems)
            gg_ref[...] = jnp.zeros_like(gg_ref)

        d = d_ref[...]
        gain_v = gain_ref[...]
        xh, r = _rms(h_ref[...])
        n_ref[...] = (xh * gain_v).astype(BF16)
        dh = (0.5 * d).astype(BF16)
        dh_ref[...] = dh
        dn = jnp.zeros((tm, D), F32)
        for j in range(FP // CH):
            sl = pl.ds(j * CH, CH)
            g = ga_ref[:, sl].astype(F32)
            u = ua_ref[:, sl].astype(F32)
            da = _nt(dh, wd[sl, :])
            s = _sig(g)
            silu = g * s
            a_ref[:, sl] = (silu * u).astype(BF16)
            dgv = (da * u * (s * (1.0 + g * (1.0 - s)))).astype(BF16)
            duv = (da * silu).astype(BF16)
            dg_ref[:, sl] = dgv
            du_ref[:, sl] = duv
            dn = dn + _nn(dgv, wg[sl, :]) + _nn(duv, wu[sl, :])
        gg_ref[...] += _colsum8(dn * xh)
        do_ref[...] = d + _rms_bwd(dn, xh, r, gain_v)

    wide = jax.ShapeDtypeStruct((T, FP), BF16)
    return _launch(
        "ffn_bwd", body, T // tm, [d, h, gain, ga, ua, G],
        [_rows(tm, D), _rows(tm, D), _const((1, D)), _rows(tm, FP), _rows(tm, FP), ANY],
        [_rows(tm, D), _rows(tm, FP), _rows(tm, FP), _rows(tm, FP), _rows(tm, D), _rows(tm, D), _const((8, D))],
        [jax.ShapeDtypeStruct((T, D), F32), wide, wide, wide, jax.ShapeDtypeStruct((T, D), BF16),
         jax.ShapeDtypeStruct((T, D), BF16), jax.ShapeDtypeStruct((8, D), F32)],
        [pltpu.VMEM((FP, D), BF16)] * 3 + [pltpu.SemaphoreType.DMA((3, NDEV))], jobs)


def _inproj_fwd(h, gain, G, lay):
    T, D = h.shape
    NS = lay.NS
    NIN, CH = NDEV * NS, 2 * NS
    tm = _tile(T, 512)

    def body(h_ref, gain_ref, g_hbm, z_ref, win, sems):
        @pl.when(pl.program_id(0) == 0)
        def _():
            _fetch(g_hbm, [(lay.off["win"], NS, win)], sems)

        xh, _ = _rms(h_ref[...])
        n = (xh * gain_ref[...]).astype(BF16)
        for j in range(NIN // CH):
            sl = pl.ds(j * CH, CH)
            z_ref[:, sl] = _nt(n, win[sl, :]).astype(BF16)

    return pl.pallas_call(
        body, name="inproj_fwd", grid=(T // tm,),
        in_specs=[_rows(tm, D), _const((1, D)), ANY], out_specs=_rows(tm, NIN),
        out_shape=jax.ShapeDtypeStruct((T, NIN), BF16),
        scratch_shapes=[pltpu.VMEM((NIN, D), BF16), pltpu.SemaphoreType.DMA((1, NDEV))],
        compiler_params=_params(),
    )(h, gain, G)


def _conv_chunks(tm, D):
    rb, lc = min(tm, 64), min(D, 256)
    return [(r0, l0, rb, lc) for r0 in range(0, tm, rb) for l0 in range(0, D, lc)]


SUBLANES = 8


def _preshift(sh, n_rows):
    for r in range(1, SUBLANES):
        sh[r, pl.ds(0, n_rows), :] = sh[0, pl.ds(r, n_rows), :]


def _window(sh, start, rows, lanes):
    r = start % SUBLANES
    return sh[r, pl.ds(start - r, rows), lanes]


def _mixer_fwd(z, h, pool_w, pscale, cw, cb, lng, lnb, G, lay, jobs=()):
    T, D = h.shape
    PG, DS = lay.PG, lay.DS
    tm = _tile(T, 256)
    hb = tm // HALO

    def body(z_ref, zp_ref, h_ref, pw_ref, ps_ref, cw_ref, cb_ref, lg_ref, lb_ref, g_hbm,
             ho_ref, c1_ref, q_ref, cc_ref, pool_ref, ext_p, sh_c, a_s, wco, wout, sems):
        i = pl.program_id(0)

        @pl.when(i == 0)
        def _():
            _fetch(g_hbm, [(lay.off["wco"], DS, wco), (lay.off["wout"], DS, wout)], sems)

        live = jnp.where(i > 0, 1.0, 0.0).astype(F32)
        ext_p[pl.ds(0, HALO), :] = zp_ref[:, pl.ds(0, D)].astype(F32) * live
        ext_p[pl.ds(HALO, tm), :] = z_ref[:, pl.ds(0, D)].astype(F32)
        sh_c[0, pl.ds(0, HALO), :] = (zp_ref[:, pl.ds(D, D)].astype(F32)
                                      * _sig(zp_ref[:, pl.ds(2 * D, D)].astype(F32)) * live)
        sh_c[0, pl.ds(HALO, tm), :] = z_ref[:, pl.ds(D, D)].astype(F32) * _sig(z_ref[:, pl.ds(2 * D, D)].astype(F32))
        _preshift(sh_c, tm + HALO - SUBLANES)

        t = i * tm + lax.broadcasted_iota(jnp.int32, (tm, 1), 0)
        for g, w in enumerate(POOL_WINDOWS):
            sl = pl.ds(g * PG, PG)
            s = ext_p[pl.ds(HALO, tm), sl]
            zc = s
            for j in range(1, w):
                s = s + ext_p[pl.ds(HALO - j, tm), sl]
            inv = 1.0 / jnp.minimum(t + 1, w).astype(F32)
            pooled = (s * inv - zc).astype(BF16)
            pool_ref[:, sl] = pooled
            qv = _nn(pooled, pw_ref[g])
            q_ref[:, sl] = qv.astype(BF16)
            a_s[:, sl] = qv * ps_ref[:, sl]

        for r0, l0, rb, lc in _conv_chunks(tm, D):
            ls = pl.ds(l0, lc)
            acc = jnp.zeros((rb, lc), F32) + cb_ref[:, ls]
            for k in range(CONV_K):
                acc = acc + cw_ref[pl.ds(k, 1), ls] * _window(sh_c, r0 + HALO - (CONV_K - 1) + k, rb, ls)
            c1_ref[pl.ds(r0, rb), ls] = acc

        xhat, _ = _ln(c1_ref[...])
        c2 = xhat * lg_ref[...] + lb_ref[...]
        c3 = (c2 * _sig(c2)).astype(BF16)
        cc = _nn(c3, wco[...])
        cc_ref[...] = cc.astype(BF16)
        gp = z_ref[:, pl.ds(3 * D, D)].astype(F32)
        gc = z_ref[:, pl.ds(4 * D, D)].astype(F32)
        m = (_sig(gp) * a_s[...] + _sig(gc) * cc).astype(BF16)
        ho_ref[...] = h_ref[...] + _nn(m, wout[...])

    act = jax.ShapeDtypeStruct((T, D), BF16)
    return _launch(
        "mixer_fwd", body, T // tm, [z, z, h, pool_w, pscale, cw, cb, lng, lnb, G],
        [_rows(tm, 5 * D), pl.BlockSpec((HALO, 5 * D), lambda i: (jnp.maximum(i * hb - 1, 0), 0)),
         _rows(tm, D), _const((4, PG, PG)), _const((1, D)), _const((CONV_KP, D)), _const((1, D)),
         _const((1, D)), _const((1, D)), ANY],
        [_rows(tm, D)] * 5,
        [jax.ShapeDtypeStruct((T, D), F32), jax.ShapeDtypeStruct((T, D), F32), act, act, act],
        [pltpu.VMEM((HALO + tm, D), F32), pltpu.VMEM((SUBLANES, HALO + tm, D), F32), pltpu.VMEM((tm, D), F32),
         pltpu.VMEM((D, D), BF16), pltpu.VMEM((D, D), BF16), pltpu.SemaphoreType.DMA((2, NDEV))], jobs)


def _mixer_bwd_rows(d, z, c1, qa, cca, pool_w, pscale, lng, lnb, G, lay):
    T, D = d.shape
    PG, DS = lay.PG, lay.DS
    tm = _tile(T, 256)

    def body(d_ref, z_ref, c1_ref, q_ref, cc_ref, pw_ref, ps_ref, lg_ref, lb_ref, g_hbm,
             m_ref, db_ref, dcc_ref, c3_ref, dq_ref, dpool_ref, dc1_ref, dzg_ref, gps_ref, glg_ref, glb_ref, gcb_ref,
             wco, wout, sems):
        @pl.when(pl.program_id(0) == 0)
        def _():
            _fetch(g_hbm, [(lay.off["wco"], DS, wco), (lay.off["wout"], DS, wout)], sems)
            for ref in (gps_ref, glg_ref, glb_ref, gcb_ref):
                ref[...] = jnp.zeros_like(ref)

        db = d_ref[...].astype(BF16)
        db_ref[...] = db
        dm = _nt(db, wout[...])
        q = q_ref[...].astype(F32)
        cc = cc_ref[...].astype(F32)
        ps = ps_ref[...]
        sp = _sig(z_ref[:, pl.ds(3 * D, D)].astype(F32))
        sc = _sig(z_ref[:, pl.ds(4 * D, D)].astype(F32))
        a = q * ps
        m_ref[...] = (sp * a + sc * cc).astype(BF16)
        da = dm * sp
        dzg_ref[:, pl.ds(0, D)] = (dm * a * sp * (1.0 - sp)).astype(BF16)
        dzg_ref[:, pl.ds(D, D)] = (dm * cc * sc * (1.0 - sc)).astype(BF16)
        gps_ref[...] += _colsum8(da * q)
        dq = (da * ps).astype(BF16)
        dq_ref[...] = dq
        for g in range(len(POOL_WINDOWS)):
            sl = pl.ds(g * PG, PG)
            dpool_ref[:, sl] = _nt(dq_ref[:, sl], pw_ref[g])

        dcc = (dm * sc).astype(BF16)
        dcc_ref[...] = dcc
        xhat, rstd = _ln(c1_ref[...])
        lg = lg_ref[...]
        c2 = xhat * lg + lb_ref[...]
        s2 = _sig(c2)
        c3_ref[...] = (c2 * s2).astype(BF16)
        dc2 = _nt(dcc, wco[...]) * (s2 * (1.0 + c2 * (1.0 - s2)))
        glg_ref[...] += _colsum8(dc2 * xhat)
        glb_ref[...] += _colsum8(dc2)
        dxh = dc2 * lg
        dc1 = rstd * (dxh - jnp.mean(dxh, axis=-1, keepdims=True)
                      - xhat * jnp.mean(dxh * xhat, axis=-1, keepdims=True))
        dc1_ref[...] = dc1
        gcb_ref[...] += _colsum8(dc1)

    act = jax.ShapeDtypeStruct((T, D), BF16)
    full = jax.ShapeDtypeStruct((T, D), F32)
    vec = jax.ShapeDtypeStruct((8, D), F32)
    return pl.pallas_call(
        body, name="mixer_bwd_rows", grid=(T // tm,),
        in_specs=[_rows(tm, D), _rows(tm, 5 * D), _rows(tm, D), _rows(tm, D), _rows(tm, D),
                  _const((4, PG, PG)), _const((1, D)), _const((1, D)), _const((1, D)), ANY],
        out_specs=[_rows(tm, D)] * 7 + [_rows(tm, 2 * D)] + [_const((8, D))] * 4,
        out_shape=[act, act, act, act, act, full, full, jax.ShapeDtypeStruct((T, 2 * D), BF16), vec, vec, vec, vec],
        scratch_shapes=[pltpu.VMEM((D, D), BF16), pltpu.VMEM((D, D), BF16), pltpu.SemaphoreType.DMA((2, NDEV))],
        compiler_params=_params(),
    )(d, z, c1, qa, cca, pool_w, pscale, lng, lnb, G)


def _mixer_bwd_time(dc1, dpool, z, cw, lay, jobs=()):
    T, D = dc1.shape
    PG = lay.PG
    tm = _tile(T, 256)
    hb = tm // HALO
    nt = T // tm

    def body(dc_ref, dcn_ref, dp_ref, dpn_ref, z_ref, zp_ref, cw_ref, dz_ref, gcw_ref, sh_d, ext_q, sh_c, dc0_s):
        i = pl.program_id(0)

        @pl.when(i == 0)
        def _():
            gcw_ref[...] = jnp.zeros_like(gcw_ref)

        live_p = jnp.where(i > 0, 1.0, 0.0).astype(F32)
        live_n = jnp.where(i < nt - 1, 1.0, 0.0).astype(F32)
        sh_d[0, pl.ds(0, tm), :] = dc_ref[...]
        sh_d[0, pl.ds(tm, HALO), :] = dcn_ref[...] * live_n
        _preshift(sh_d, tm + HALO - SUBLANES)
        zg = z_ref[:, pl.ds(2 * D, D)].astype(F32)
        za = z_ref[:, pl.ds(D, D)].astype(F32)
        sg = _sig(zg)
        sh_c[0, pl.ds(0, HALO), :] = (zp_ref[:, pl.ds(D, D)].astype(F32)
                                      * _sig(zp_ref[:, pl.ds(2 * D, D)].astype(F32)) * live_p)
        sh_c[0, pl.ds(HALO, tm), :] = za * sg
        _preshift(sh_c, tm + HALO - SUBLANES)

        t = i * tm + lax.broadcasted_iota(jnp.int32, (tm, 1), 0)
        tn = (i + 1) * tm + lax.broadcasted_iota(jnp.int32, (HALO, 1), 0)
        for g, w in enumerate(POOL_WINDOWS):
            sl = pl.ds(g * PG, PG)
            ext_q[pl.ds(0, tm), sl] = dp_ref[:, sl] * (1.0 / jnp.minimum(t + 1, w).astype(F32))
            ext_q[pl.ds(tm, HALO), sl] = dpn_ref[:, sl] * (live_n / jnp.minimum(tn + 1, w).astype(F32))
        for g, w in enumerate(POOL_WINDOWS):
            sl = pl.ds(g * PG, PG)
            s = ext_q[pl.ds(0, tm), sl]
            for j in range(1, w):
                s = s + ext_q[pl.ds(j, tm), sl]
            dz_ref[:, sl] = (s - dp_ref[:, sl]).astype(BF16)

        for r0, l0, rb, lc in _conv_chunks(tm, D):
            ls = pl.ds(l0, lc)
            acc = jnp.zeros((rb, lc), F32)
            for j in range(CONV_K):
                acc = acc + cw_ref[pl.ds(CONV_K - 1 - j, 1), ls] * _window(sh_d, r0 + j, rb, ls)
            dc0_s[pl.ds(r0, rb), ls] = acc
        dc0 = dc0_s[...]
        dz_ref[:, pl.ds(D, D)] = (dc0 * sg).astype(BF16)
        dz_ref[:, pl.ds(2 * D, D)] = (dc0 * za * sg * (1.0 - sg)).astype(BF16)

        lc = min(D, 256)
        for l0 in range(0, D, lc):
            ls = pl.ds(l0, lc)
            dcv = dc_ref[:, ls]
            for k in range(CONV_K):
                gcw_ref[pl.ds(8 * k, 8), ls] += _colsum8(dcv * _window(sh_c, HALO - (CONV_K - 1) + k, tm, ls))

    nxt = lambda i: (jnp.minimum((i + 1) * hb, T // HALO - 1), 0)
    return _launch(
        "mixer_bwd_time", body, nt, [dc1, dc1, dpool, dpool, z, z, cw],
        [_rows(tm, D), pl.BlockSpec((HALO, D), nxt), _rows(tm, D), pl.BlockSpec((HALO, D), nxt),
         _rows(tm, 5 * D), pl.BlockSpec((HALO, 5 * D), lambda i: (jnp.maximum(i * hb - 1, 0), 0)),
         _const((CONV_KP, D))],
        [_rows(tm, 3 * D), _const((CONV_KP * 8, D))],
        [jax.ShapeDtypeStruct((T, 3 * D), BF16), jax.ShapeDtypeStruct((CONV_KP * 8, D), F32)],
        [pltpu.VMEM((SUBLANES, tm + HALO, D), F32), pltpu.VMEM((tm + HALO, D), F32),
         pltpu.VMEM((SUBLANES, HALO + tm, D), F32), pltpu.VMEM((tm, D), F32)], jobs)


def _inproj_bwd(d, h, gain, dzm, dzg, G, lay):
    T, D = h.shape
    NS = lay.NS
    NIN = NDEV * NS
    tm = _tile(T, 512)

    def body(d_ref, h_ref, gain_ref, dzm_ref, dzg_ref, g_hbm, do_ref, u_ref, gg_ref, win, sems):
        @pl.when(pl.program_id(0) == 0)
        def _():
            _fetch(g_hbm, [(lay.off["win"], NS, win)], sems)
            gg_ref[...] = jnp.zeros_like(gg_ref)

        gain_v = gain_ref[...]
        xh, r = _rms(h_ref[...])
        u_ref[...] = (xh * gain_v).astype(BF16)
        dn = jnp.zeros((tm, D), F32)
        for j in range(3):
            dn = dn + _nn(dzm_ref[:, pl.ds(j * D, D)], win[pl.ds(j * D, D), :])
        for j in range(2):
            dn = dn + _nn(dzg_ref[:, pl.ds(j * D, D)], win[pl.ds((3 + j) * D, D), :])
        gg_ref[...] += _colsum8(dn * xh)
        do_ref[...] = d_ref[...] + _rms_bwd(dn, xh, r, gain_v)

    return pl.pallas_call(
        body, name="inproj_bwd", grid=(T // tm,),
        in_specs=[_rows(tm, D), _rows(tm, D), _const((1, D)), _rows(tm, 3 * D), _rows(tm, 2 * D), ANY],
        out_specs=[_rows(tm, D), _rows(tm, D), _const((8, D))],
        out_shape=[jax.ShapeDtypeStruct((T, D), F32), jax.ShapeDtypeStruct((T, D), BF16),
                   jax.ShapeDtypeStruct((8, D), F32)],
        scratch_shapes=[pltpu.VMEM((NIN, D), BF16), pltpu.SemaphoreType.DMA((1, NDEV))],
        compiler_params=_params(),
    )(d, h, gain, dzm, dzg, G)


def _ple_fwd(h, pe, gain, wppt, G, lay):
    T, D = h.shape
    PD, DS = lay.PD, lay.DS
    tm = _tile(T, 512)

    def body(h_ref, p_ref, gain_ref, wpp_ref, g_hbm, ho_ref, gate_ref, wpg, sems):
        @pl.when(pl.program_id(0) == 0)
        def _():
            _fetch(g_hbm, [(lay.off["wpg"], DS, wpg)], sems)

        h = h_ref[...]
        xh, _ = _rms(h)
        n = (xh * gain_ref[...]).astype(BF16)
        gate = _sig(_nn(n, wpg[...]))
        gate_ref[...] = gate.astype(BF16)
        e = _nt(p_ref[...].astype(BF16), wpp_ref[...])
        ho_ref[...] = h + gate * e

    return pl.pallas_call(
        body, name="ple_fwd", grid=(T // tm,),
        in_specs=[_rows(tm, D), _rows(tm, PD), _const((1, D)), _const((D, PD)), ANY],
        out_specs=[_rows(tm, D), _rows(tm, D)],
        out_shape=[jax.ShapeDtypeStruct((T, D), F32), jax.ShapeDtypeStruct((T, D), BF16)],
        scratch_shapes=[pltpu.VMEM((D, D), BF16), pltpu.SemaphoreType.DMA((1, NDEV))],
        compiler_params=_params(),
    )(h, pe, gain, wppt, G)


def _ple_bwd(d, h, pe, gate_a, gain, wppt, G, lay):
    T, D = h.shape
    PD, DS = lay.PD, lay.DS
    tm = _tile(T, 512)

    def body(d_ref, h_ref, p_ref, gate_ref, gain_ref, wpp_ref, g_hbm,
             do_ref, dpre_ref, de_ref, n_ref, pb_ref, gg_ref, wpg, sems):
        @pl.when(pl.program_id(0) == 0)
        def _():
            _fetch(g_hbm, [(lay.off["wpg"], DS, wpg)], sems)
            gg_ref[...] = jnp.zeros_like(gg_ref)

        d = d_ref[...]
        gain_v = gain_ref[...]
        xh, r = _rms(h_ref[...])
        n_ref[...] = (xh * gain_v).astype(BF16)
        pb = p_ref[...].astype(BF16)
        pb_ref[...] = pb
        e = _nt(pb, wpp_ref[...])
        gate = gate_ref[...].astype(F32)
        de_ref[...] = (d * gate).astype(BF16)
        dpre = (d * e * gate * (1.0 - gate)).astype(BF16)
        dpre_ref[...] = dpre
        dn = _nt(dpre, wpg[...])
        gg_ref[...] += _colsum8(dn * xh)
        do_ref[...] = d + _rms_bwd(dn, xh, r, gain_v)

    act = jax.ShapeDtypeStruct((T, D), BF16)
    return pl.pallas_call(
        body, name="ple_bwd", grid=(T // tm,),
        in_specs=[_rows(tm, D), _rows(tm, D), _rows(tm, PD), _rows(tm, D), _const((1, D)), _const((D, PD)), ANY],
        out_specs=[_rows(tm, D), _rows(tm, D), _rows(tm, D), _rows(tm, D), _rows(tm, PD), _const((8, D))],
        out_shape=[jax.ShapeDtypeStruct((T, D), F32), act, act, act, jax.ShapeDtypeStruct((T, PD), BF16),
                   jax.ShapeDtypeStruct((8, D), F32)],
        scratch_shapes=[pltpu.VMEM((D, D), BF16), pltpu.SemaphoreType.DMA((1, NDEV))],
        compiler_params=_params(),
    )(d, h, pe, gate_a, gain, wppt, G)


def _head(h, target, gain):
    T, D = h.shape
    tm = _tile(T, 512)

    def body(h_ref, t_ref, gain_ref, do_ref, loss_ref, gg_ref):
        @pl.when(pl.program_id(0) == 0)
        def _():
            loss_ref[...] = jnp.zeros_like(loss_ref)
            gg_ref[...] = jnp.zeros_like(gg_ref)

        gain_v = gain_ref[...]
        xh, r = _rms(h_ref[...])
        err = xh * gain_v - t_ref[...]
        loss_ref[...] += _colsum8(err * err)
        dy = err * (1.0 / D)
        gg_ref[...] += _colsum8(dy * xh)
        do_ref[...] = _rms_bwd(dy, xh, r, gain_v)

    vec = jax.ShapeDtypeStruct((8, D), F32)
    return pl.pallas_call(
        body, name="head", grid=(T // tm,),
        in_specs=[_rows(tm, D), _rows(tm, D), _const((1, D))],
        out_specs=[_rows(tm, D), _const((8, D)), _const((8, D))],
        out_shape=[jax.ShapeDtypeStruct((T, D), F32), vec, vec], compiler_params=_params(),
    )(h, target, gain)


_BIG = ["ffn1_w_gate", "ffn1_w_up", "ffn1_w_down", "w_in", "pool_w", "conv_w_out", "w_out", "ffn2_w_gate",
        "ffn2_w_up", "ffn2_w_down", "ple_w_gate", "ple_w_proj"]
_VECS = ["ffn1_norm", "mix_norm", "pool_scale", "conv_dw_b", "conv_ln_g", "conv_ln_b", "ffn2_norm", "ple_norm"]
_WEIGHTS = ["ffn1_norm", "ffn1_w_gate", "ffn1_w_up", "ffn1_w_down", "mix_norm", "w_in", "pool_w", "pool_scale",
            "conv_dw_w", "conv_dw_b", "conv_ln_g", "conv_ln_b", "conv_w_out", "w_out", "ffn2_norm", "ffn2_w_gate",
            "ffn2_w_up", "ffn2_w_down", "ple_norm", "ple_w_gate", "ple_w_proj", "final_norm"]


def _step(x, p, loss_target, w, mom, var):
    T, D = x.shape[1], x.shape[2]
    L = p.shape[0]
    FS, NS = w["ffn1_w_gate"].shape[2], w["w_in"].shape[2]
    PD = p.shape[3]
    PGS, PG = w["pool_w"].shape[2], w["pool_w"].shape[3]
    CS = w["conv_dw_w"].shape[2]
    lay = _Layout(D, FS, NS, PD, PG, PGS)
    FB = lay.FB
    ax, ay, ac = _place()
    me = 4 * ax + 2 * ay + ac

    assert L == 2, "the exchange schedule below is written for two layers"
    gather = {(l, pc): _GatherJob(f"ag{pc}", _pack_piece(w, l, lay, pc, BF16)) for l in range(L) for pc in PIECES}
    fwd_jobs = {("ffn1", 0): [gather[0, "C"], gather[0, "D"]], ("mixer", 0): [gather[0, "B"], gather[1, "A"]],
                ("ffn2", 0): [gather[1, "C"], gather[1, "D"]], ("ffn1", 1): [gather[1, "B"]]}
    _run_jobs("gather_first", [gather[0, "A"]])
    cw_mine = jnp.pad(w["conv_dw_w"], ((0, 0), (0, CONV_KP - CONV_K), (0, 0)))
    cw_rows = L * CONV_KP * CS // D
    cw_all, _ = _all_gather_small(cw_mine.reshape(cw_rows, D))
    cw_full = cw_all.reshape(NDEV, L, CONV_KP, CS).transpose(1, 2, 0, 3).reshape(L, CONV_KP, D)

    def gathered(l, pc):
        return gather[l, pc].results[0]

    def small_mats(l):
        G = gathered(l, "D")
        wppt = G[:, lay.off["wpp"]:lay.off["wpp"] + lay.rows["wpp"]].reshape(D, PD)
        pw = G[:, lay.off["pool"]:lay.off["pool"] + lay.rows["pool"]].reshape(NDEV, 4, PGS, PG)
        return wppt, pw.transpose(1, 0, 2, 3).reshape(4, PG, PG)

    def vec(name, l):
        return w[name][l].reshape(1, D)

    h = x[0]
    saved = []
    for l in range(L):
        s = {"h0": h}
        h, s["g1"], s["u1"] = _ffn_fwd(h, vec("ffn1_norm", l), gathered(l, "A"), lay, fwd_jobs.get(("ffn1", l), ()))
        wppt, pw = small_mats(l)
        s["wppt"], s["pw"], s["h1"] = wppt, pw, h
        s["z"] = _inproj_fwd(h, vec("mix_norm", l), gathered(l, "C"), lay)
        h, s["c1"], s["q"], s["cc"], s["pooled"] = _mixer_fwd(
            s["z"], h, pw, vec("pool_scale", l), cw_full[l], vec("conv_dw_b", l), vec("conv_ln_g", l),
            vec("conv_ln_b", l), gathered(l, "D"), lay, fwd_jobs.get(("mixer", l), ()))
        s["h2"] = h
        h, s["g2"], s["u2"] = _ffn_fwd(h, vec("ffn2_norm", l), gathered(l, "B"), lay, fwd_jobs.get(("ffn2", l), ()))
        s["h3"] = h
        h, s["gate"] = _ple_fwd(h, p[l, 0], vec("ple_norm", l), wppt, gathered(l, "D"), lay)
        saved.append(s)

    d, loss_part, g_final = _head(h, loss_target[0], w["final_norm"].reshape(1, D))

    scatter = {}
    small_parts = [None] * L

    def send(l, pc, blocks):
        scatter[l, pc] = _ScatterJob(f"rs{pc}", jnp.concatenate(
            [blocks[n].reshape(NDEV, lay.rows[n], D) for n, _ in lay.pieces[pc]], axis=1))
        return scatter[l, pc]

    def small_rows():
        rows = []
        for l in range(L):
            rows += [v.sum(axis=0, keepdims=True) for v in small_parts[l][:-1]]
            rows.append(small_parts[l][-1].reshape(CONV_KP, 8, D).sum(axis=1))
        rows.append(g_final.sum(axis=0, keepdims=True))
        rows.append(loss_part.sum(axis=0, keepdims=True))
        n_rows = sum(r.shape[0] for r in rows)
        return jnp.pad(jnp.concatenate(rows, axis=0), ((0, -n_rows % 8), (0, 0)))

    held = []
    for l in reversed(range(L)):
        s = saved[l]
        wppt, pw = s["wppt"], s["pw"]
        g = {}
        d, dpre, de, n_ple, pb, g_ple = _ple_bwd(d, s["h3"], p[l, 0], s["gate"], vec("ple_norm", l), wppt,
                                                 gathered(l, "D"), lay)
        g["wpg"] = _tn_matmul(n_ple, dpre, "tn_sq", _tile(D, 512))
        g["wpp"] = _tn_matmul(de, pb, "tn_proj", _tile(D, 512))

        d, dg2, du2, a2, n2, dh2, g_n2 = _ffn_bwd(d, s["h2"], vec("ffn2_norm", l), s["g2"], s["u2"],
                                                  gathered(l, "B"), lay, held)
        held = []
        g["g2"] = _tn_matmul(dg2, n2, "tn_ffn", 2 * FB)
        g["u2"] = _tn_matmul(du2, n2, "tn_ffn", 2 * FB)
        g["d2"] = _tn_matmul(a2, dh2, "tn_ffn", 2 * FB)
        held.append(send(l, "B", g))

        (m_b, d_b, dcc, c3, dq, dpool, dc1, dzg, g_ps, g_lg, g_lb, g_cb) = _mixer_bwd_rows(
            d, s["z"], s["c1"], s["q"], s["cc"], pw, vec("pool_scale", l), vec("conv_ln_g", l), vec("conv_ln_b", l),
            gathered(l, "D"), lay)
        g["wout"] = _tn_matmul(m_b, d_b, "tn_sq", _tile(D, 512))
        g["wco"] = _tn_matmul(c3, dcc, "tn_sq", _tile(D, 512))
        g_pool = _tn_matmul(s["pooled"], dq, "tn_pool", PG, y_follows=True)
        g["pool"] = g_pool.reshape(4, NDEV, PGS, PG).transpose(1, 0, 2, 3)
        held.append(send(l, "D", g))
        dzm, g_cw = _mixer_bwd_time(dc1, dpool, s["z"], cw_full[l], lay, held)
        held = []
        d, u_b, g_mix = _inproj_bwd(d, s["h1"], vec("mix_norm", l), dzm, dzg, gathered(l, "C"), lay)
        g["win"] = jnp.concatenate([_tn_matmul(dzm, u_b, "tn_in3", _tile(D, 512)),
                                    _tn_matmul(dzg, u_b, "tn_in2", _tile(D, 512))], axis=0)
        held.append(send(l, "C", g))

        d, dg1, du1, a1, n1, dh1, g_n1 = _ffn_bwd(d, s["h0"], vec("ffn1_norm", l), s["g1"], s["u1"],
                                                  gathered(l, "A"), lay, held)
        held = []
        small_parts[l] = [g_n1, g_mix, g_ps, g_cb, g_lg, g_lb, g_n2, g_ple, g_cw]
        if l > 0:
            g["g1"] = _tn_matmul(dg1, n1, "tn_ffn", 2 * FB)
            g["u1"] = _tn_matmul(du1, n1, "tn_ffn", 2 * FB)
            g["d1"] = _tn_matmul(a1, dh1, "tn_ffn", 2 * FB)
            held.append(send(l, "A", g))
        else:
            rows_job = _RowsGatherJob("agS", small_rows())
            g["g1"] = _tn_matmul(dg1, n1, "tn_ffn", 2 * FB, jobs=[rows_job])
            g["u1"] = _tn_matmul(du1, n1, "tn_ffn", 2 * FB, jobs=[send(l, "A1", g)])
            g["d1"] = _tn_matmul(a1, dh1, "tn_ffn", 2 * FB, jobs=[send(l, "A2", g)])
            held.append(send(l, "A3", g))
    _run_jobs("scatter_last", held)
    grad_x = d[None]

    big_out = {}
    for (l, pc), job in scatter.items():
        own = lax.dynamic_index_in_dim(job.ins[0], me, 0, keepdims=False)
        big_out[l, pc] = _adamw(own, job.results[0], _pack_piece(w, l, lay, pc, F32), _pack_piece(mom, l, lay, pc, F32),
                                _pack_piece(var, l, lay, pc, F32), "adamw_" + pc)

    small_sum = _sum_slots(rows_job.results[0])
    per_layer = len(_VECS) + CONV_KP
    g_vecs = {n: jnp.stack([small_sum[l * per_layer + k] for l in range(L)]) for k, n in enumerate(_VECS)}
    g_cw_full = jnp.stack([small_sum[l * per_layer + len(_VECS):(l + 1) * per_layer] for l in range(L)])
    g_cw_mine = lax.dynamic_slice_in_dim(g_cw_full, me * CS, CS, axis=2)
    g_fin = small_sum[L * per_layer]
    loss = (0.5 / D) * jnp.sum(small_sum[L * per_layer + 1])

    def small_slab(src, conv, fin):
        parts = [src[n] for n in _VECS] + [fin.reshape(1, D), conv.reshape(cw_rows, D)]
        sl = jnp.concatenate(parts, axis=0)
        return jnp.pad(sl, ((0, -sl.shape[0] % 8), (0, 0)))

    padk = ((0, 0), (0, CONV_KP - CONV_K), (0, 0))
    small_out = _adamw(
        small_slab(g_vecs, g_cw_mine, g_fin), None,
        small_slab(w, jnp.pad(w["conv_dw_w"], padk), w["final_norm"]),
        small_slab(mom, jnp.pad(mom["conv_dw_w"], padk), mom["final_norm"]),
        small_slab(var, jnp.pad(var["conv_dw_w"], padk), var["final_norm"]), "adamw_small")

    outs = []
    nv = len(_VECS)
    for which, small in enumerate(small_out):
        per = [{} for _ in range(L)]
        for (l, pc), slabs in big_out.items():
            per[l].update(_unpack_piece(slabs[which], lay, pc))
        res = {n: jnp.stack([per[l][n] for l in range(L)]) for n in _BIG}
        for k, n in enumerate(_VECS):
            res[n] = small[k * L:(k + 1) * L]
        res["final_norm"] = small[nv * L]
        res["conv_dw_w"] = small[nv * L + 1:nv * L + 1 + cw_rows].reshape(L, CONV_KP, CS)[:, :CONV_K]
        outs.append(res)
    return loss, grad_x, outs


def kernel(x, p, ffn1_norm, ffn1_w_gate, ffn1_w_up, ffn1_w_down, mix_norm, w_in, pool_w, pool_scale, conv_dw_w, conv_dw_b, conv_ln_g, conv_ln_b, conv_w_out, w_out, ffn2_norm, ffn2_w_gate, ffn2_w_up, ffn2_w_down, ple_norm, ple_w_gate, ple_w_proj, final_norm, loss_target, m_ffn1_norm, m_ffn1_w_gate, m_ffn1_w_up, m_ffn1_w_down, m_mix_norm, m_w_in, m_pool_w, m_pool_scale, m_conv_dw_w, m_conv_dw_b, m_conv_ln_g, m_conv_ln_b, m_conv_w_out, m_w_out, m_ffn2_norm, m_ffn2_w_gate, m_ffn2_w_up, m_ffn2_w_down, m_ple_norm, m_ple_w_gate, m_ple_w_proj, m_final_norm, v_ffn1_norm, v_ffn1_w_gate, v_ffn1_w_up, v_ffn1_w_down, v_mix_norm, v_w_in, v_pool_w, v_pool_scale, v_conv_dw_w, v_conv_dw_b, v_conv_ln_g, v_conv_ln_b, v_conv_w_out, v_w_out, v_ffn2_norm, v_ffn2_w_gate, v_ffn2_w_up, v_ffn2_w_down, v_ple_norm, v_ple_w_gate, v_ple_w_proj, v_final_norm):
    given = dict(locals())
    w = {n: given[n] for n in _WEIGHTS}
    mom = {n: given["m_" + n] for n in _WEIGHTS}
    var = {n: given["v_" + n] for n in _WEIGHTS}
    loss, grad_x, (grads, deltas, new_m, new_v) = _step(x, p, loss_target, w, mom, var)
    out = [loss, grad_x]
    for res in (grads, deltas, new_m, new_v):
        out += [res[n] for n in _WEIGHTS]
    return tuple(out)
```

```python
import functools

import jax
import jax.numpy as jnp
from jax import lax
from jax.experimental import pallas as pl
from jax.experimental.pallas import tpu as pltpu

F32, BF16 = jnp.float32, jnp.bfloat16
NDEV = 8
MESH = pl.DeviceIdType.MESH
HALO = 32
POOL_WINDOWS = (2, 4, 8, 16)
CONV_K = 31
CONV_KP = 32
RMS_EPS, LN_EPS = 1e-6, 1e-5
ADAM_LR, ADAM_B1, ADAM_B2, ADAM_EPS, ADAM_WD, ADAM_STEP = 0.001, 0.9, 0.999, 1e-08, 0.01, 10
LANE = 128
VMEM_LIMIT = 56 * 1024 * 1024
ANY = pl.BlockSpec(memory_space=pl.ANY)


def _nn(a, b):
    return jnp.dot(a, b, preferred_element_type=F32)


def _nt(a, b):
    return lax.dot_general(a, b, (((1,), (1,)), ((), ())), preferred_element_type=F32)


def _tn(a, b):
    return lax.dot_general(a, b, (((0,), (0,)), ((), ())), preferred_element_type=F32)


def _colsum8(v):
    return jnp.sum(v.reshape(v.shape[0] // 8, 8, v.shape[1]), axis=0)


def _sig(v):
    return jax.nn.sigmoid(v)


def _rms(h):
    r = lax.rsqrt(jnp.mean(h * h, axis=-1, keepdims=True) + RMS_EPS)
    return h * r, r


def _rms_bwd(dn, xh, r, gain):
    dxh = dn * gain
    return r * (dxh - xh * jnp.mean(dxh * xh, axis=-1, keepdims=True))


def _ln(c1):
    mu = jnp.mean(c1, axis=-1, keepdims=True)
    cen = c1 - mu
    rstd = lax.rsqrt(jnp.mean(cen * cen, axis=-1, keepdims=True) + LN_EPS)
    return cen * rstd, rstd


def _rows(tm, c):
    return pl.BlockSpec((tm, c), lambda i: (i, 0))


def _const(shape):
    return pl.BlockSpec(shape, lambda i: (0,) * len(shape))


def _params(n_grid=1):
    return pltpu.CompilerParams(dimension_semantics=("arbitrary",) * n_grid, vmem_limit_bytes=VMEM_LIMIT)


BF16_ROWS = 16
MXU_CHUNK = 768


def _tile(n, want):
    if n <= want:
        return n
    return max(t for t in range(BF16_ROWS, want + 1, BF16_ROWS) if n % t == 0)


def _hidden_chunks(width):
    return [(o, min(MXU_CHUNK, width - o)) for o in range(0, width, MXU_CHUNK)]


def _fetch(g_hbm, specs, sems):
    cps = []
    for wi, (off, rows, dst) in enumerate(specs):
        for dev in range(NDEV):
            cps.append(pltpu.make_async_copy(g_hbm.at[dev, pl.ds(off, rows), :],
                                             dst.at[pl.ds(dev * rows, rows), :], sems.at[wi, dev]))
    for cp in cps:
        cp.start()
    for cp in cps:
        cp.wait()


PIECES = ("A", "B", "C", "D")


class _Layout:
    def __init__(self, D, FS, NS, PD, PG, PGS):
        self.D, self.FS, self.NS, self.PD, self.PG, self.PGS = D, FS, NS, PD, PG, PGS
        assert FS % BF16_ROWS == 0 and (NDEV * FS) % (2 * LANE) == 0, "FFN shard rows must tile as bf16 row blocks"
        self.FB = FS
        self.DS = D // NDEV
        self.pieces = {
            "A": [("g1", self.FB), ("u1", self.FB), ("d1", self.FB)],
            "B": [("g2", self.FB), ("u2", self.FB), ("d2", self.FB)],
            "C": [("win", NS)],
            "D": [("wco", self.DS), ("wout", self.DS), ("wpg", self.DS), ("wpp", self.DS * PD // D),
                  ("pool", 4 * PGS * PG // D)],
            "A1": [("g1", self.FB)], "A2": [("u1", self.FB)], "A3": [("d1", self.FB)]}
        self.off, self.rows = {}, {}
        for pc in PIECES:
            o = 0
            for n, r in self.pieces[pc]:
                self.off[n], self.rows[n] = o, r
                o += r


def _pack_piece(w, l, lay, pc, dtype):
    D = lay.D
    make = {"g1": lambda: w["ffn1_w_gate"][l].T, "u1": lambda: w["ffn1_w_up"][l].T,
            "d1": lambda: w["ffn1_w_down"][l], "g2": lambda: w["ffn2_w_gate"][l].T,
            "u2": lambda: w["ffn2_w_up"][l].T, "d2": lambda: w["ffn2_w_down"][l],
            "win": lambda: w["w_in"][l].T, "wco": lambda: w["conv_w_out"][l], "wout": lambda: w["w_out"][l],
            "wpg": lambda: w["ple_w_gate"][l], "wpp": lambda: w["ple_w_proj"][l].T.reshape(-1, D),
            "pool": lambda: w["pool_w"][l].reshape(-1, D)}
    return jnp.concatenate([make[n]() for n, _ in lay.pieces[pc]], axis=0).astype(dtype)


def _unpack_piece(slab, lay, pc):
    PD, PG, PGS, DS = lay.PD, lay.PG, lay.PGS, lay.DS
    undo = {"g1": ("ffn1_w_gate", lambda a: a.T), "u1": ("ffn1_w_up", lambda a: a.T),
            "d1": ("ffn1_w_down", lambda a: a), "g2": ("ffn2_w_gate", lambda a: a.T),
            "u2": ("ffn2_w_up", lambda a: a.T), "d2": ("ffn2_w_down", lambda a: a),
            "win": ("w_in", lambda a: a.T), "wco": ("conv_w_out", lambda a: a), "wout": ("w_out", lambda a: a),
            "wpg": ("ple_w_gate", lambda a: a), "wpp": ("ple_w_proj", lambda a: a.reshape(DS, PD).T),
            "pool": ("pool_w", lambda a: a.reshape(4, PGS, PG))}
    out, o = {}, 0
    for n, r in lay.pieces[pc]:
        name, fn = undo[n]
        out[name] = fn(slab[o:o + r])
        o += r
    return out


def _place():
    return lax.axis_index("x"), lax.axis_index("y"), lax.axis_index("c")


FLIPS = [(a, b, d) for a in (0, 1) for b in (0, 1) for d in (0, 1)][1:]


class _GatherJob:
    def __init__(self, tag, slab):
        self.tag, self.ins = tag, [slab]
        self.outs = [jax.ShapeDtypeStruct((NDEV,) + slab.shape, slab.dtype)]
        self.scratch = [pltpu.SemaphoreType.DMA((7,)), pltpu.SemaphoreType.DMA((7,)), pltpu.SemaphoreType.DMA]
        self.results = None

    def _plan(self, ins, outs, sems):
        (x_ref,), (out_ref,), (send_sems, recv_sems, local_sem) = ins, outs, sems
        x, y, c = _place()
        me, sibling = (x, y, c), (x, y, 1 - c)
        chips = [(1 - x, y), (x, 1 - y), (1 - x, 1 - y)]

        def rows(px, py, pc):
            return out_ref.at[4 * px + 2 * py + pc]

        def copy(k, block, to, src=None):
            return pltpu.make_async_remote_copy(
                src_ref=rows(*block) if src is None else src, dst_ref=rows(*block),
                send_sem=send_sems.at[k], recv_sem=recv_sems.at[k], device_id=to, device_id_type=MESH)

        mine = pltpu.make_async_copy(x_ref, rows(*me), local_sem)
        first = [copy(0, me, sibling, src=x_ref)]
        first += [copy(1 + j, me, (*chip, c), src=x_ref) for j, chip in enumerate(chips)]
        passed = [copy(4 + j, (*chip, c), sibling) for j, chip in enumerate(chips)]
        landed = [copy(1 + j, (*chip, c), me) for j, chip in enumerate(chips)]
        late = [copy(0, sibling, me)] + [copy(4 + j, (*chip, 1 - c), me) for j, chip in enumerate(chips)]
        return mine, first, passed, landed, late

    def start(self, ins, outs, sems):
        mine, first, _, _, _ = self._plan(ins, outs, sems)
        mine.start()
        for cp in first:
            cp.start()

    def middle(self, ins, outs, sems):
        _, _, passed, landed, _ = self._plan(ins, outs, sems)
        for got, cp in zip(landed, passed):
            got.wait_recv()
            cp.start()

    def finish(self, ins, outs, sems):
        mine, first, passed, _, late = self._plan(ins, outs, sems)
        for got in late:
            got.wait_recv()
        for cp in first + passed:
            cp.wait_send()
        mine.wait()


class _ScatterJob:
    def __init__(self, tag, grads):
        self.tag, self.ins = tag, [grads]
        self.outs = [jax.ShapeDtypeStruct((7,) + grads.shape[1:], grads.dtype)]
        self.scratch = [pltpu.SemaphoreType.DMA((7,)), pltpu.SemaphoreType.DMA((7,))]
        self.results = None

    def _plan(self, ins, outs, sems):
        (g_ref,), (out_ref,), (send_sems, recv_sems) = ins, outs, sems
        x, y, c = _place()
        cps = []
        for k, (a, b, d) in enumerate(FLIPS):
            px, py, pc = x ^ a, y ^ b, c ^ d
            cps.append(pltpu.make_async_remote_copy(
                src_ref=g_ref.at[4 * px + 2 * py + pc], dst_ref=out_ref.at[k], send_sem=send_sems.at[k],
                recv_sem=recv_sems.at[k], device_id=(px, py, pc), device_id_type=MESH))
        return cps

    def start(self, ins, outs, sems):
        for cp in self._plan(ins, outs, sems):
            cp.start()

    def middle(self, ins, outs, sems):
        pass

    def finish(self, ins, outs, sems):
        cps = self._plan(ins, outs, sems)
        for cp in cps:
            cp.wait_recv()
        for cp in cps:
            cp.wait_send()


class _RowsGatherJob:
    def __init__(self, tag, rows):
        self.tag, self.ins = tag, [rows]
        self.outs = [jax.ShapeDtypeStruct((NDEV,) + rows.shape, rows.dtype)]
        self.scratch = [pltpu.SemaphoreType.DMA((7,)), pltpu.SemaphoreType.DMA((7,)), pltpu.SemaphoreType.DMA]
        self.results = None

    def _plan(self, ins, outs, sems):
        (x_ref,), (out_ref,), (send_sems, recv_sems, local_sem) = ins, outs, sems
        x, y, c = _place()
        me = 4 * x + 2 * y + c
        mine = pltpu.make_async_copy(x_ref, out_ref.at[me], local_sem)
        sends, lands = [], []
        for k, (a, b, d) in enumerate(FLIPS):
            px, py, pc = x ^ a, y ^ b, c ^ d
            for dst, keep in ((out_ref.at[me], sends), (out_ref.at[4 * px + 2 * py + pc], lands)):
                keep.append(pltpu.make_async_remote_copy(
                    src_ref=x_ref, dst_ref=dst, send_sem=send_sems.at[k], recv_sem=recv_sems.at[k],
                    device_id=(px, py, pc), device_id_type=MESH))
        return mine, sends, lands

    def start(self, ins, outs, sems):
        mine, sends, _ = self._plan(ins, outs, sems)
        mine.start()
        for cp in sends:
            cp.start()

    def middle(self, ins, outs, sems):
        pass

    def finish(self, ins, outs, sems):
        mine, sends, lands = self._plan(ins, outs, sems)
        for cp in lands:
            cp.wait_recv()
        for cp in sends:
            cp.wait_send()
        mine.wait()


def _sum_slots(slots):
    n, S, D = slots.shape

    def body(s_ref, o_ref):
        acc = s_ref[0]
        for j in range(1, n):
            acc = acc + s_ref[j]
        o_ref[...] = acc

    vm = pl.BlockSpec(memory_space=pltpu.VMEM)
    return pl.pallas_call(body, name="sum_slots", out_shape=jax.ShapeDtypeStruct((S, D), slots.dtype),
                          in_specs=[vm], out_specs=vm)(slots)


def _launch(name, body, grid, args, in_specs, out_specs, out_shape, scratch, jobs=()):
    grid = (grid,) if isinstance(grid, int) else tuple(grid)
    steps = grid[0] * (grid[1] if len(grid) == 2 else 1)
    n_in, n_out, n_sc = len(args), len(out_shape), len(scratch)
    j_in = [a for jb in jobs for a in jb.ins]
    j_out = [o for jb in jobs for o in jb.outs]
    j_sc = [s for jb in jobs for s in jb.scratch]
    mid = (3 * steps) // 5

    def wrapped(*refs):
        c_in, refs = refs[:n_in], refs[n_in:]
        m_in, refs = refs[:len(j_in)], refs[len(j_in):]
        c_out, refs = refs[:n_out], refs[n_out:]
        m_out, refs = refs[:len(j_out)], refs[len(j_out):]
        c_sc, m_sc = refs[:n_sc], refs[n_sc:]
        bound, a, b, c = [], 0, 0, 0
        for jb in jobs:
            bound.append((jb, m_in[a:a + len(jb.ins)], m_out[b:b + len(jb.outs)], m_sc[c:c + len(jb.scratch)]))
            a, b, c = a + len(jb.ins), b + len(jb.outs), c + len(jb.scratch)
        i = pl.program_id(0) if len(grid) == 1 else pl.program_id(0) * grid[1] + pl.program_id(1)

        def phase(step, which):
            if jobs:
                @pl.when(i == step)
                def _():
                    for jb, ins, outs, sems in bound:
                        getattr(jb, which)(ins, outs, sems)

        phase(0, "start")
        if body is not None:
            body(*c_in, *c_out, *c_sc)
        phase(mid, "middle")
        phase(steps - 1, "finish")

    outs = pl.pallas_call(
        wrapped, name=name + "".join("_" + jb.tag for jb in jobs), grid=grid,
        in_specs=list(in_specs) + [ANY] * len(j_in), out_specs=list(out_specs) + [ANY] * len(j_out),
        out_shape=list(out_shape) + j_out, scratch_shapes=list(scratch) + j_sc, compiler_params=_params(len(grid)),
    )(*args, *j_in)
    pos = n_out
    for jb in jobs:
        jb.results = list(outs[pos:pos + len(jb.outs)])
        pos += len(jb.outs)
    return list(outs[:n_out])


def _run_jobs(name, jobs):
    _launch(name, None, 1, [], [], [], [], [], jobs)


def _all_gather_small(rows_in):
    S, D = rows_in.shape

    def body(x_ref, out_ref, sum_ref, send_sems, recv_sems):
        x, y, c = _place()
        me = 4 * x + 2 * y + c
        out_ref[me] = x_ref[...]
        flips = [(a, b, d) for a in (0, 1) for b in (0, 1) for d in (0, 1)][1:]

        def copy(k):
            a, b, d = flips[k]
            px, py, pc = x ^ a, y ^ b, c ^ d
            peer = 4 * px + 2 * py + pc
            send = pltpu.make_async_remote_copy(
                src_ref=x_ref, dst_ref=out_ref.at[me], send_sem=send_sems.at[k], recv_sem=recv_sems.at[k],
                device_id=(px, py, pc), device_id_type=MESH)
            recv = pltpu.make_async_remote_copy(
                src_ref=x_ref, dst_ref=out_ref.at[peer], send_sem=send_sems.at[k], recv_sem=recv_sems.at[k],
                device_id=(px, py, pc), device_id_type=MESH)
            return send, recv

        cps = [copy(k) for k in range(7)]
        for send, _ in cps:
            send.start()
        for _, recv in cps:
            recv.wait_recv()
        for send, _ in cps:
            send.wait_send()
        acc = out_ref[0]
        for j in range(1, NDEV):
            acc = acc + out_ref[j]
        sum_ref[...] = acc

    vm = pl.BlockSpec(memory_space=pltpu.VMEM)
    return pl.pallas_call(
        body, name="ag_small",
        out_shape=(jax.ShapeDtypeStruct((NDEV, S, D), F32), jax.ShapeDtypeStruct((S, D), F32)),
        in_specs=[vm], out_specs=(vm, vm),
        scratch_shapes=[pltpu.SemaphoreType.DMA((7,)), pltpu.SemaphoreType.DMA((7,))],
    )(rows_in)


def _adamw(own, got, w, m, v, name):
    rows, D = w.shape
    tr = _tile(rows, 256)
    n_got = 0 if got is None else got.shape[0]
    n = 1 + n_got
    parts = [own] + [got] * n_got
    part_specs = [_rows(tr, D)] + [pl.BlockSpec((None, tr, D), functools.partial(lambda k, i: (k, i, 0), k))
                                   for k in range(n_got)]

    def body(*refs):
        part_refs, (w_ref, m_ref, v_ref, g_out, d_out, m_out, v_out) = refs[:n], refs[n:]
        g = part_refs[0][...].astype(F32)
        for pr in part_refs[1:]:
            g = g + pr[...].astype(F32)
        m_new = ADAM_B1 * m_ref[...] + (1.0 - ADAM_B1) * g
        v_new = ADAM_B2 * v_ref[...] + (1.0 - ADAM_B2) * (g * g)
        m_hat = m_new / (1.0 - ADAM_B1 ** ADAM_STEP)
        v_hat = v_new / (1.0 - ADAM_B2 ** ADAM_STEP)
        g_out[...] = g
        d_out[...] = -ADAM_LR * (m_hat / (jnp.sqrt(v_hat) + ADAM_EPS) + ADAM_WD * w_ref[...])
        m_out[...] = m_new
        v_out[...] = v_new

    return pl.pallas_call(
        body, name=name, grid=(rows // tr,),
        in_specs=part_specs + [_rows(tr, D)] * 3, out_specs=[_rows(tr, D)] * 4,
        out_shape=[jax.ShapeDtypeStruct((rows, D), F32)] * 4, compiler_params=_params(),
    )(*parts, w, m, v)


def _tn_matmul(xa, ya, name, tmm, y_follows=False, jobs=()):
    T, M = xa.shape
    tn = tmm if y_follows else ya.shape[1]
    tt = _tile(T, 1024)
    nb = M // tmm

    def body(x_ref, y_ref, o_ref, acc):
        k = pl.program_id(1)

        @pl.when(k == 0)
        def _():
            acc[...] = jnp.zeros_like(acc)

        acc[...] += _tn(x_ref[...], y_ref[...])

        @pl.when(k == pl.num_programs(1) - 1)
        def _():
            o_ref[...] = acc[...].astype(o_ref.dtype)

    y_map = (lambda b, k: (k, b)) if y_follows else (lambda b, k: (k, 0))
    return _launch(
        name, body, (nb, T // tt), [xa, ya],
        [pl.BlockSpec((tt, tmm), lambda b, k: (k, b)), pl.BlockSpec((tt, tn), y_map)],
        [pl.BlockSpec((tmm, tn), lambda b, k: (b, 0))], [jax.ShapeDtypeStruct((M, tn), BF16)],
        [pltpu.VMEM((tmm, tn), F32)], jobs)[0]


def _ffn_fwd(h, gain, G, lay, jobs=()):
    T, D = h.shape
    FB = lay.FB
    FP = NDEV * FB
    offs = (0, FB, 2 * FB)
    tm = _tile(T, 512)

    def body(h_ref, gain_ref, g_hbm, ho_ref, go_ref, uo_ref, wg, wu, wd, sems):
        @pl.when(pl.program_id(0) == 0)
        def _():
            _fetch(g_hbm, [(offs[0], FB, wg), (offs[1], FB, wu), (offs[2], FB, wd)], sems)

        h = h_ref[...]
        xh, _ = _rms(h)
        n = (xh * gain_ref[...]).astype(BF16)
        acc = jnp.zeros((tm, D), F32)
        for start, width in _hidden_chunks(FP):
            sl = pl.ds(start, width)
            g = _nt(n, wg[sl, :])
            u = _nt(n, wu[sl, :])
            go_ref[:, sl] = g.astype(BF16)
            uo_ref[:, sl] = u.astype(BF16)
            a = g * _sig(g) * u
            acc = acc + _nn(a.astype(BF16), wd[sl, :])
        ho_ref[...] = h + 0.5 * acc

    return _launch(
        "ffn_fwd", body, T // tm, [h, gain, G],
        [_rows(tm, D), _const((1, D)), ANY], [_rows(tm, D), _rows(tm, FP), _rows(tm, FP)],
        [jax.ShapeDtypeStruct((T, D), F32), jax.ShapeDtypeStruct((T, FP), BF16), jax.ShapeDtypeStruct((T, FP), BF16)],
        [pltpu.VMEM((FP, D), BF16)] * 3 + [pltpu.SemaphoreType.DMA((3, NDEV))], jobs)


def _ffn_bwd(d, h, gain, ga, ua, G, lay, jobs=()):
    T, D = h.shape
    FB = lay.FB
    FP = NDEV * FB
    offs = (0, FB, 2 * FB)
    tm = _tile(T, 256)

    def body(d_ref, h_ref, gain_ref, ga_ref, ua_ref, g_hbm,
             do_ref, dg_ref, du_ref, a_ref, n_ref, dh_ref, gg_ref, wg, wu, wd, sems):
        @pl.when(pl.program_id(0) == 0)
        def _():
            _fetch(g_hbm, [(offs[0], FB, wg), (offs[1], FB, wu), (offs[2], FB, wd)], sems)
            gg_ref[...] = jnp.zeros_like(gg_ref)

        d = d_ref[...]
        gain_v = gain_ref[...]
        xh, r = _rms(h_ref[...])
        n_ref[...] = (xh * gain_v).astype(BF16)
        dh = (0.5 * d).astype(BF16)
        dh_ref[...] = dh
        dn = jnp.zeros((tm, D), F32)
        for start, width in _hidden_chunks(FP):
            sl = pl.ds(start, width)
            g = ga_ref[:, sl].astype(F32)
            u = ua_ref[:, sl].astype(F32)
            da = _nt(dh, wd[sl, :])
            s = _sig(g)
            silu = g * s
            a_ref[:, sl] = (silu * u).astype(BF16)
            dgv = (da * u * (s * (1.0 + g * (1.0 - s)))).astype(BF16)
            duv = (da * silu).astype(BF16)
            dg_ref[:, sl] = dgv
            du_ref[:, sl] = duv
            dn = dn + _nn(dgv, wg[sl, :]) + _nn(duv, wu[sl, :])
        gg_ref[...] += _colsum8(dn * xh)
        do_ref[...] = d + _rms_bwd(dn, xh, r, gain_v)

    wide = jax.ShapeDtypeStruct((T, FP), BF16)
    return _launch(
        "ffn_bwd", body, T // tm, [d, h, gain, ga, ua, G],
        [_rows(tm, D), _rows(tm, D), _const((1, D)), _rows(tm, FP), _rows(tm, FP), ANY],
        [_rows(tm, D), _rows(tm, FP), _rows(tm, FP), _rows(tm, FP), _rows(tm, D), _rows(tm, D), _const((8, D))],
        [jax.ShapeDtypeStruct((T, D), F32), wide, wide, wide, jax.ShapeDtypeStruct((T, D), BF16),
         jax.ShapeDtypeStruct((T, D), BF16), jax.ShapeDtypeStruct((8, D), F32)],
        [pltpu.VMEM((FP, D), BF16)] * 3 + [pltpu.SemaphoreType.DMA((3, NDEV))], jobs)


def _inproj_fwd(h, gain, G, lay):
    T, D = h.shape
    NS = lay.NS
    NIN, CH = NDEV * NS, 2 * NS
    tm = _tile(T, 512)

    def body(h_ref, gain_ref, g_hbm, z_ref, win, sems):
        @pl.when(pl.program_id(0) == 0)
        def _():
            _fetch(g_hbm, [(lay.off["win"], NS, win)], sems)

        xh, _ = _rms(h_ref[...])
        n = (xh * gain_ref[...]).astype(BF16)
        for j in range(NIN // CH):
            sl = pl.ds(j * CH, CH)
            z_ref[:, sl] = _nt(n, win[sl, :]).astype(BF16)

    return pl.pallas_call(
        body, name="inproj_fwd", grid=(T // tm,),
        in_specs=[_rows(tm, D), _const((1, D)), ANY], out_specs=_rows(tm, NIN),
        out_shape=jax.ShapeDtypeStruct((T, NIN), BF16),
        scratch_shapes=[pltpu.VMEM((NIN, D), BF16), pltpu.SemaphoreType.DMA((1, NDEV))],
        compiler_params=_params(),
    )(h, gain, G)


def _conv_chunks(tm, D):
    rb, lc = min(tm, 64), min(D, 256)
    return [(r0, l0, rb, lc) for r0 in range(0, tm, rb) for l0 in range(0, D, lc)]


SUBLANES = 8


def _preshift(sh, n_rows):
    for r in range(1, SUBLANES):
        sh[r, pl.ds(0, n_rows), :] = sh[0, pl.ds(r, n_rows), :]


def _window(sh, start, rows, lanes):
    r = start % SUBLANES
    return sh[r, pl.ds(start - r, rows), lanes]


def _mixer_fwd(z, h, pool_w, pscale, cw, cb, lng, lnb, G, lay, jobs=()):
    T, D = h.shape
    PG, DS = lay.PG, lay.DS
    tm = _tile(T, 256)
    hb = tm // HALO

    def body(z_ref, zp_ref, h_ref, pw_ref, ps_ref, cw_ref, cb_ref, lg_ref, lb_ref, g_hbm,
             ho_ref, c1_ref, q_ref, cc_ref, pool_ref, ext_p, sh_c, a_s, wco, wout, sems):
        i = pl.program_id(0)

        @pl.when(i == 0)
        def _():
            _fetch(g_hbm, [(lay.off["wco"], DS, wco), (lay.off["wout"], DS, wout)], sems)

        live = jnp.where(i > 0, 1.0, 0.0).astype(F32)
        ext_p[pl.ds(0, HALO), :] = zp_ref[:, pl.ds(0, D)].astype(F32) * live
        ext_p[pl.ds(HALO, tm), :] = z_ref[:, pl.ds(0, D)].astype(F32)
        sh_c[0, pl.ds(0, HALO), :] = (zp_ref[:, pl.ds(D, D)].astype(F32)
                                      * _sig(zp_ref[:, pl.ds(2 * D, D)].astype(F32)) * live)
        sh_c[0, pl.ds(HALO, tm), :] = z_ref[:, pl.ds(D, D)].astype(F32) * _sig(z_ref[:, pl.ds(2 * D, D)].astype(F32))
        _preshift(sh_c, tm + HALO - SUBLANES)

        t = i * tm + lax.broadcasted_iota(jnp.int32, (tm, 1), 0)
        for g, w in enumerate(POOL_WINDOWS):
            sl = pl.ds(g * PG, PG)
            s = ext_p[pl.ds(HALO, tm), sl]
            zc = s
            for j in range(1, w):
                s = s + ext_p[pl.ds(HALO - j, tm), sl]
            inv = 1.0 / jnp.minimum(t + 1, w).astype(F32)
            pooled = (s * inv - zc).astype(BF16)
            pool_ref[:, sl] = pooled
            qv = _nn(pooled, pw_ref[g])
            q_ref[:, sl] = qv.astype(BF16)
            a_s[:, sl] = qv * ps_ref[:, sl]

        for r0, l0, rb, lc in _conv_chunks(tm, D):
            ls = pl.ds(l0, lc)
            acc = jnp.zeros((rb, lc), F32) + cb_ref[:, ls]
            for k in range(CONV_K):
                acc = acc + cw_ref[pl.ds(k, 1), ls] * _window(sh_c, r0 + HALO - (CONV_K - 1) + k, rb, ls)
            c1_ref[pl.ds(r0, rb), ls] = acc

        xhat, _ = _ln(c1_ref[...])
        c2 = xhat * lg_ref[...] + lb_ref[...]
        c3 = (c2 * _sig(c2)).astype(BF16)
        cc = _nn(c3, wco[...])
        cc_ref[...] = cc.astype(BF16)
        gp = z_ref[:, pl.ds(3 * D, D)].astype(F32)
        gc = z_ref[:, pl.ds(4 * D, D)].astype(F32)
        m = (_sig(gp) * a_s[...] + _sig(gc) * cc).astype(BF16)
        ho_ref[...] = h_ref[...] + _nn(m, wout[...])

    act = jax.ShapeDtypeStruct((T, D), BF16)
    return _launch(
        "mixer_fwd", body, T // tm, [z, z, h, pool_w, pscale, cw, cb, lng, lnb, G],
        [_rows(tm, 5 * D), pl.BlockSpec((HALO, 5 * D), lambda i: (jnp.maximum(i * hb - 1, 0), 0)),
         _rows(tm, D), _const((4, PG, PG)), _const((1, D)), _const((CONV_KP, D)), _const((1, D)),
         _const((1, D)), _const((1, D)), ANY],
        [_rows(tm, D)] * 5,
        [jax.ShapeDtypeStruct((T, D), F32), jax.ShapeDtypeStruct((T, D), F32), act, act, act],
        [pltpu.VMEM((HALO + tm, D), F32), pltpu.VMEM((SUBLANES, HALO + tm, D), F32), pltpu.VMEM((tm, D), F32),
         pltpu.VMEM((D, D), BF16), pltpu.VMEM((D, D), BF16), pltpu.SemaphoreType.DMA((2, NDEV))], jobs)


def _mixer_bwd_rows(d, z, c1, qa, cca, pool_w, pscale, lng, lnb, G, lay):
    T, D = d.shape
    PG, DS = lay.PG, lay.DS
    tm = _tile(T, 256)

    def body(d_ref, z_ref, c1_ref, q_ref, cc_ref, pw_ref, ps_ref, lg_ref, lb_ref, g_hbm,
             m_ref, db_ref, dcc_ref, c3_ref, dq_ref, dpool_ref, dc1_ref, dzg_ref, gps_ref, glg_ref, glb_ref, gcb_ref,
             wco, wout, sems):
        @pl.when(pl.program_id(0) == 0)
        def _():
            _fetch(g_hbm, [(lay.off["wco"], DS, wco), (lay.off["wout"], DS, wout)], sems)
            for ref in (gps_ref, glg_ref, glb_ref, gcb_ref):
                ref[...] = jnp.zeros_like(ref)

        db = d_ref[...].astype(BF16)
        db_ref[...] = db
        dm = _nt(db, wout[...])
        q = q_ref[...].astype(F32)
        cc = cc_ref[...].astype(F32)
        ps = ps_ref[...]
        sp = _sig(z_ref[:, pl.ds(3 * D, D)].astype(F32))
        sc = _sig(z_ref[:, pl.ds(4 * D, D)].astype(F32))
        a = q * ps
        m_ref[...] = (sp * a + sc * cc).astype(BF16)
        da = dm * sp
        dzg_ref[:, pl.ds(0, D)] = (dm * a * sp * (1.0 - sp)).astype(BF16)
        dzg_ref[:, pl.ds(D, D)] = (dm * cc * sc * (1.0 - sc)).astype(BF16)
        gps_ref[...] += _colsum8(da * q)
        dq = (da * ps).astype(BF16)
        dq_ref[...] = dq
        for g in range(len(POOL_WINDOWS)):
            sl = pl.ds(g * PG, PG)
            dpool_ref[:, sl] = _nt(dq_ref[:, sl], pw_ref[g])

        dcc = (dm * sc).astype(BF16)
        dcc_ref[...] = dcc
        xhat, rstd = _ln(c1_ref[...])
        lg = lg_ref[...]
        c2 = xhat * lg + lb_ref[...]
        s2 = _sig(c2)
        c3_ref[...] = (c2 * s2).astype(BF16)
        dc2 = _nt(dcc, wco[...]) * (s2 * (1.0 + c2 * (1.0 - s2)))
        glg_ref[...] += _colsum8(dc2 * xhat)
        glb_ref[...] += _colsum8(dc2)
        dxh = dc2 * lg
        dc1 = rstd * (dxh - jnp.mean(dxh, axis=-1, keepdims=True)
                      - xhat * jnp.mean(dxh * xhat, axis=-1, keepdims=True))
        dc1_ref[...] = dc1
        gcb_ref[...] += _colsum8(dc1)

    act = jax.ShapeDtypeStruct((T, D), BF16)
    full = jax.ShapeDtypeStruct((T, D), F32)
    vec = jax.ShapeDtypeStruct((8, D), F32)
    return pl.pallas_call(
        body, name="mixer_bwd_rows", grid=(T // tm,),
        in_specs=[_rows(tm, D), _rows(tm, 5 * D), _rows(tm, D), _rows(tm, D), _rows(tm, D),
                  _const((4, PG, PG)), _const((1, D)), _const((1, D)), _const((1, D)), ANY],
        out_specs=[_rows(tm, D)] * 7 + [_rows(tm, 2 * D)] + [_const((8, D))] * 4,
        out_shape=[act, act, act, act, act, full, full, jax.ShapeDtypeStruct((T, 2 * D), BF16), vec, vec, vec, vec],
        scratch_shapes=[pltpu.VMEM((D, D), BF16), pltpu.VMEM((D, D), BF16), pltpu.SemaphoreType.DMA((2, NDEV))],
        compiler_params=_params(),
    )(d, z, c1, qa, cca, pool_w, pscale, lng, lnb, G)


def _mixer_bwd_time(dc1, dpool, z, cw, lay, jobs=()):
    T, D = dc1.shape
    PG = lay.PG
    tm = _tile(T, 256)
    hb = tm // HALO
    nt = T // tm

    def body(dc_ref, dcn_ref, dp_ref, dpn_ref, z_ref, zp_ref, cw_ref, dz_ref, gcw_ref, sh_d, ext_q, sh_c, dc0_s):
        i = pl.program_id(0)

        @pl.when(i == 0)
        def _():
            gcw_ref[...] = jnp.zeros_like(gcw_ref)

        live_p = jnp.where(i > 0, 1.0, 0.0).astype(F32)
        live_n = jnp.where(i < nt - 1, 1.0, 0.0).astype(F32)
        sh_d[0, pl.ds(0, tm), :] = dc_ref[...]
        sh_d[0, pl.ds(tm, HALO), :] = dcn_ref[...] * live_n
        _preshift(sh_d, tm + HALO - SUBLANES)
        zg = z_ref[:, pl.ds(2 * D, D)].astype(F32)
        za = z_ref[:, pl.ds(D, D)].astype(F32)
        sg = _sig(zg)
        sh_c[0, pl.ds(0, HALO), :] = (zp_ref[:, pl.ds(D, D)].astype(F32)
                                      * _sig(zp_ref[:, pl.ds(2 * D, D)].astype(F32)) * live_p)
        sh_c[0, pl.ds(HALO, tm), :] = za * sg
        _preshift(sh_c, tm + HALO - SUBLANES)

        t = i * tm + lax.broadcasted_iota(jnp.int32, (tm, 1), 0)
        tn = (i + 1) * tm + lax.broadcasted_iota(jnp.int32, (HALO, 1), 0)
        for g, w in enumerate(POOL_WINDOWS):
            sl = pl.ds(g * PG, PG)
            ext_q[pl.ds(0, tm), sl] = dp_ref[:, sl] * (1.0 / jnp.minimum(t + 1, w).astype(F32))
            ext_q[pl.ds(tm, HALO), sl] = dpn_ref[:, sl] * (live_n / jnp.minimum(tn + 1, w).astype(F32))
        for g, w in enumerate(POOL_WINDOWS):
            sl = pl.ds(g * PG, PG)
            s = ext_q[pl.ds(0, tm), sl]
            for j in range(1, w):
                s = s + ext_q[pl.ds(j, tm), sl]
            dz_ref[:, sl] = (s - dp_ref[:, sl]).astype(BF16)

        for r0, l0, rb, lc in _conv_chunks(tm, D):
            ls = pl.ds(l0, lc)
            acc = jnp.zeros((rb, lc), F32)
            for j in range(CONV_K):
                acc = acc + cw_ref[pl.ds(CONV_K - 1 - j, 1), ls] * _window(sh_d, r0 + j, rb, ls)
            dc0_s[pl.ds(r0, rb), ls] = acc
        dc0 = dc0_s[...]
        dz_ref[:, pl.ds(D, D)] = (dc0 * sg).astype(BF16)
        dz_ref[:, pl.ds(2 * D, D)] = (dc0 * za * sg * (1.0 - sg)).astype(BF16)

        lc = min(D, 256)
        for l0 in range(0, D, lc):
            ls = pl.ds(l0, lc)
            dcv = dc_ref[:, ls]
            for k in range(CONV_K):
                gcw_ref[pl.ds(8 * k, 8), ls] += _colsum8(dcv * _window(sh_c, HALO - (CONV_K - 1) + k, tm, ls))

    nxt = lambda i: (jnp.minimum((i + 1) * hb, T // HALO - 1), 0)
    return _launch(
        "mixer_bwd_time", body, nt, [dc1, dc1, dpool, dpool, z, z, cw],
        [_rows(tm, D), pl.BlockSpec((HALO, D), nxt), _rows(tm, D), pl.BlockSpec((HALO, D), nxt),
         _rows(tm, 5 * D), pl.BlockSpec((HALO, 5 * D), lambda i: (jnp.maximum(i * hb - 1, 0), 0)),
         _const((CONV_KP, D))],
        [_rows(tm, 3 * D), _const((CONV_KP * 8, D))],
        [jax.ShapeDtypeStruct((T, 3 * D), BF16), jax.ShapeDtypeStruct((CONV_KP * 8, D), F32)],
        [pltpu.VMEM((SUBLANES, tm + HALO, D), F32), pltpu.VMEM((tm + HALO, D), F32),
         pltpu.VMEM((SUBLANES, HALO + tm, D), F32), pltpu.VMEM((tm, D), F32)], jobs)


def _inproj_bwd(d, h, gain, dzm, dzg, G, lay):
    T, D = h.shape
    NS = lay.NS
    NIN = NDEV * NS
    tm = _tile(T, 512)

    def body(d_ref, h_ref, gain_ref, dzm_ref, dzg_ref, g_hbm, do_ref, u_ref, gg_ref, win, sems):
        @pl.when(pl.program_id(0) == 0)
        def _():
            _fetch(g_hbm, [(lay.off["win"], NS, win)], sems)
            gg_ref[...] = jnp.zeros_like(gg_ref)

        gain_v = gain_ref[...]
        xh, r = _rms(h_ref[...])
        u_ref[...] = (xh * gain_v).astype(BF16)
        dn = jnp.zeros((tm, D), F32)
        for j in range(3):
            dn = dn + _nn(dzm_ref[:, pl.ds(j * D, D)], win[pl.ds(j * D, D), :])
        for j in range(2):
            dn = dn + _nn(dzg_ref[:, pl.ds(j * D, D)], win[pl.ds((3 + j) * D, D), :])
        gg_ref[...] += _colsum8(dn * xh)
        do_ref[...] = d_ref[...] + _rms_bwd(dn, xh, r, gain_v)

    return pl.pallas_call(
        body, name="inproj_bwd", grid=(T // tm,),
        in_specs=[_rows(tm, D), _rows(tm, D), _const((1, D)), _rows(tm, 3 * D), _rows(tm, 2 * D), ANY],
        out_specs=[_rows(tm, D), _rows(tm, D), _const((8, D))],
        out_shape=[jax.ShapeDtypeStruct((T, D), F32), jax.ShapeDtypeStruct((T, D), BF16),
                   jax.ShapeDtypeStruct((8, D), F32)],
        scratch_shapes=[pltpu.VMEM((NIN, D), BF16), pltpu.SemaphoreType.DMA((1, NDEV))],
        compiler_params=_params(),
    )(d, h, gain, dzm, dzg, G)


def _ple_fwd(h, pe, gain, wppt, G, lay):
    T, D = h.shape
    PD, DS = lay.PD, lay.DS
    tm = _tile(T, 512)

    def body(h_ref, p_ref, gain_ref, wpp_ref, g_hbm, ho_ref, gate_ref, wpg, sems):
        @pl.when(pl.program_id(0) == 0)
        def _():
            _fetch(g_hbm, [(lay.off["wpg"], DS, wpg)], sems)

        h = h_ref[...]
        xh, _ = _rms(h)
        n = (xh * gain_ref[...]).astype(BF16)
        gate = _sig(_nn(n, wpg[...]))
        gate_ref[...] = gate.astype(BF16)
        e = _nt(p_ref[...].astype(BF16), wpp_ref[...])
        ho_ref[...] = h + gate * e

    return pl.pallas_call(
        body, name="ple_fwd", grid=(T // tm,),
        in_specs=[_rows(tm, D), _rows(tm, PD), _const((1, D)), _const((D, PD)), ANY],
        out_specs=[_rows(tm, D), _rows(tm, D)],
        out_shape=[jax.ShapeDtypeStruct((T, D), F32), jax.ShapeDtypeStruct((T, D), BF16)],
        scratch_shapes=[pltpu.VMEM((D, D), BF16), pltpu.SemaphoreType.DMA((1, NDEV))],
        compiler_params=_params(),
    )(h, pe, gain, wppt, G)


def _ple_bwd(d, h, pe, gate_a, gain, wppt, G, lay):
    T, D = h.shape
    PD, DS = lay.PD, lay.DS
    tm = _tile(T, 512)

    def body(d_ref, h_ref, p_ref, gate_ref, gain_ref, wpp_ref, g_hbm,
             do_ref, dpre_ref, de_ref, n_ref, pb_ref, gg_ref, wpg, sems):
        @pl.when(pl.program_id(0) == 0)
        def _():
            _fetch(g_hbm, [(lay.off["wpg"], DS, wpg)], sems)
            gg_ref[...] = jnp.zeros_like(gg_ref)

        d = d_ref[...]
        gain_v = gain_ref[...]
        xh, r = _rms(h_ref[...])
        n_ref[...] = (xh * gain_v).astype(BF16)
        pb = p_ref[...].astype(BF16)
        pb_ref[...] = pb
        e = _nt(pb, wpp_ref[...])
        gate = gate_ref[...].astype(F32)
        de_ref[...] = (d * gate).astype(BF16)
        dpre = (d * e * gate * (1.0 - gate)).astype(BF16)
        dpre_ref[...] = dpre
        dn = _nt(dpre, wpg[...])
        gg_ref[...] += _colsum8(dn * xh)
        do_ref[...] = d + _rms_bwd(dn, xh, r, gain_v)

    act = jax.ShapeDtypeStruct((T, D), BF16)
    return pl.pallas_call(
        body, name="ple_bwd", grid=(T // tm,),
        in_specs=[_rows(tm, D), _rows(tm, D), _rows(tm, PD), _rows(tm, D), _const((1, D)), _const((D, PD)), ANY],
        out_specs=[_rows(tm, D), _rows(tm, D), _rows(tm, D), _rows(tm, D), _rows(tm, PD), _const((8, D))],
        out_shape=[jax.ShapeDtypeStruct((T, D), F32), act, act, act, jax.ShapeDtypeStruct((T, PD), BF16),
                   jax.ShapeDtypeStruct((8, D), F32)],
        scratch_shapes=[pltpu.VMEM((D, D), BF16), pltpu.SemaphoreType.DMA((1, NDEV))],
        compiler_params=_params(),
    )(d, h, pe, gate_a, gain, wppt, G)


def _head(h, target, gain):
    T, D = h.shape
    tm = _tile(T, 512)

    def body(h_ref, t_ref, gain_ref, do_ref, loss_ref, gg_ref):
        @pl.when(pl.program_id(0) == 0)
        def _():
            loss_ref[...] = jnp.zeros_like(loss_ref)
            gg_ref[...] = jnp.zeros_like(gg_ref)

        gain_v = gain_ref[...]
        xh, r = _rms(h_ref[...])
        err = xh * gain_v - t_ref[...]
        loss_ref[...] += _colsum8(err * err)
        dy = err * (1.0 / D)
        gg_ref[...] += _colsum8(dy * xh)
        do_ref[...] = _rms_bwd(dy, xh, r, gain_v)

    vec = jax.ShapeDtypeStruct((8, D), F32)
    return pl.pallas_call(
        body, name="head", grid=(T // tm,),
        in_specs=[_rows(tm, D), _rows(tm, D), _const((1, D))],
        out_specs=[_rows(tm, D), _const((8, D)), _const((8, D))],
        out_shape=[jax.ShapeDtypeStruct((T, D), F32), vec, vec], compiler_params=_params(),
    )(h, target, gain)


_BIG = ["ffn1_w_gate", "ffn1_w_up", "ffn1_w_down", "w_in", "pool_w", "conv_w_out", "w_out", "ffn2_w_gate",
        "ffn2_w_up", "ffn2_w_down", "ple_w_gate", "ple_w_proj"]
_VECS = ["ffn1_norm", "mix_norm", "pool_scale", "conv_dw_b", "conv_ln_g", "conv_ln_b", "ffn2_norm", "ple_norm"]
_WEIGHTS = ["ffn1_norm", "ffn1_w_gate", "ffn1_w_up", "ffn1_w_down", "mix_norm", "w_in", "pool_w", "pool_scale",
            "conv_dw_w", "conv_dw_b", "conv_ln_g", "conv_ln_b", "conv_w_out", "w_out", "ffn2_norm", "ffn2_w_gate",
            "ffn2_w_up", "ffn2_w_down", "ple_norm", "ple_w_gate", "ple_w_proj", "final_norm"]


def _step(x, p, loss_target, w, mom, var):
    T, D = x.shape[1], x.shape[2]
    L = p.shape[0]
    FS, NS = w["ffn1_w_gate"].shape[2], w["w_in"].shape[2]
    PD = p.shape[3]
    PGS, PG = w["pool_w"].shape[2], w["pool_w"].shape[3]
    CS = w["conv_dw_w"].shape[2]
    lay = _Layout(D, FS, NS, PD, PG, PGS)
    FB = lay.FB
    ax, ay, ac = _place()
    me = 4 * ax + 2 * ay + ac

    assert L == 2, "the exchange schedule below is written for two layers"
    gather = {(l, pc): _GatherJob(f"ag{pc}", _pack_piece(w, l, lay, pc, BF16)) for l in range(L) for pc in PIECES}
    fwd_jobs = {("ffn1", 0): [gather[0, "C"], gather[0, "D"]], ("mixer", 0): [gather[0, "B"]],
                ("ffn2", 0): [gather[1, "A"], gather[1, "D"]], ("ffn1", 1): [gather[1, "C"]],
                ("mixer", 1): [gather[1, "B"]]}
    _run_jobs("gather_first", [gather[0, "A"]])
    cw_mine = jnp.pad(w["conv_dw_w"], ((0, 0), (0, CONV_KP - CONV_K), (0, 0)))
    cw_rows = L * CONV_KP * CS // D
    cw_all, _ = _all_gather_small(cw_mine.reshape(cw_rows, D))
    cw_full = cw_all.reshape(NDEV, L, CONV_KP, CS).transpose(1, 2, 0, 3).reshape(L, CONV_KP, D)

    def gathered(l, pc):
        return gather[l, pc].results[0]

    def small_mats(l):
        G = gathered(l, "D")
        wppt = G[:, lay.off["wpp"]:lay.off["wpp"] + lay.rows["wpp"]].reshape(D, PD)
        pw = G[:, lay.off["pool"]:lay.off["pool"] + lay.rows["pool"]].reshape(NDEV, 4, PGS, PG)
        return wppt, pw.transpose(1, 0, 2, 3).reshape(4, PG, PG)

    def vec(name, l):
        return w[name][l].reshape(1, D)

    h = x[0]
    saved = []
    for l in range(L):
        s = {"h0": h}
        h, s["g1"], s["u1"] = _ffn_fwd(h, vec("ffn1_norm", l), gathered(l, "A"), lay, fwd_jobs.get(("ffn1", l), ()))
        wppt, pw = small_mats(l)
        s["wppt"], s["pw"], s["h1"] = wppt, pw, h
        s["z"] = _inproj_fwd(h, vec("mix_norm", l), gathered(l, "C"), lay)
        h, s["c1"], s["q"], s["cc"], s["pooled"] = _mixer_fwd(
            s["z"], h, pw, vec("pool_scale", l), cw_full[l], vec("conv_dw_b", l), vec("conv_ln_g", l),
            vec("conv_ln_b", l), gathered(l, "D"), lay, fwd_jobs.get(("mixer", l), ()))
        s["h2"] = h
        h, s["g2"], s["u2"] = _ffn_fwd(h, vec("ffn2_norm", l), gathered(l, "B"), lay, fwd_jobs.get(("ffn2", l), ()))
        s["h3"] = h
        h, s["gate"] = _ple_fwd(h, p[l, 0], vec("ple_norm", l), wppt, gathered(l, "D"), lay)
        saved.append(s)

    d, loss_part, g_final = _head(h, loss_target[0], w["final_norm"].reshape(1, D))

    scatter = {}
    small_parts = [None] * L

    def send(l, pc, blocks):
        scatter[l, pc] = _ScatterJob(f"rs{pc}", jnp.concatenate(
            [blocks[n].reshape(NDEV, lay.rows[n], D) for n, _ in lay.pieces[pc]], axis=1))
        return scatter[l, pc]

    def small_rows():
        rows = []
        for l in range(L):
            rows += [v.sum(axis=0, keepdims=True) for v in small_parts[l][:-1]]
            rows.append(small_parts[l][-1].reshape(CONV_KP, 8, D).sum(axis=1))
        rows.append(g_final.sum(axis=0, keepdims=True))
        rows.append(loss_part.sum(axis=0, keepdims=True))
        n_rows = sum(r.shape[0] for r in rows)
        return jnp.pad(jnp.concatenate(rows, axis=0), ((0, -n_rows % 8), (0, 0)))

    held = []
    for l in reversed(range(L)):
        s = saved[l]
        wppt, pw = s["wppt"], s["pw"]
        g = {}
        d, dpre, de, n_ple, pb, g_ple = _ple_bwd(d, s["h3"], p[l, 0], s["gate"], vec("ple_norm", l), wppt,
                                                 gathered(l, "D"), lay)
        g["wpg"] = _tn_matmul(n_ple, dpre, "tn_sq", _tile(D, 512))
        g["wpp"] = _tn_matmul(de, pb, "tn_proj", _tile(D, 512))

        d, dg2, du2, a2, n2, dh2, g_n2 = _ffn_bwd(d, s["h2"], vec("ffn2_norm", l), s["g2"], s["u2"],
                                                  gathered(l, "B"), lay, held)
        held = []
        g["g2"] = _tn_matmul(dg2, n2, "tn_ffn", NDEV * FB // 2)
        g["u2"] = _tn_matmul(du2, n2, "tn_ffn", NDEV * FB // 2)
        g["d2"] = _tn_matmul(a2, dh2, "tn_ffn", NDEV * FB // 2)
        held.append(send(l, "B", g))

        (m_b, d_b, dcc, c3, dq, dpool, dc1, dzg, g_ps, g_lg, g_lb, g_cb) = _mixer_bwd_rows(
            d, s["z"], s["c1"], s["q"], s["cc"], pw, vec("pool_scale", l), vec("conv_ln_g", l), vec("conv_ln_b", l),
            gathered(l, "D"), lay)
        g["wout"] = _tn_matmul(m_b, d_b, "tn_sq", _tile(D, 512))
        g["wco"] = _tn_matmul(c3, dcc, "tn_sq", _tile(D, 512))
        g_pool = _tn_matmul(s["pooled"], dq, "tn_pool", PG, y_follows=True)
        g["pool"] = g_pool.reshape(4, NDEV, PGS, PG).transpose(1, 0, 2, 3)
        held.append(send(l, "D", g))
        dzm, g_cw = _mixer_bwd_time(dc1, dpool, s["z"], cw_full[l], lay, held)
        held = []
        d, u_b, g_mix = _inproj_bwd(d, s["h1"], vec("mix_norm", l), dzm, dzg, gathered(l, "C"), lay)
        g["win"] = jnp.concatenate([_tn_matmul(dzm, u_b, "tn_in3", _tile(D, 512)),
                                    _tn_matmul(dzg, u_b, "tn_in2", _tile(D, 512))], axis=0)
        held.append(send(l, "C", g))

        d, dg1, du1, a1, n1, dh1, g_n1 = _ffn_bwd(d, s["h0"], vec("ffn1_norm", l), s["g1"], s["u1"],
                                                  gathered(l, "A"), lay, held)
        held = []
        small_parts[l] = [g_n1, g_mix, g_ps, g_cb, g_lg, g_lb, g_n2, g_ple, g_cw]
        if l > 0:
            g["g1"] = _tn_matmul(dg1, n1, "tn_ffn", NDEV * FB // 2)
            g["u1"] = _tn_matmul(du1, n1, "tn_ffn", NDEV * FB // 2)
            g["d1"] = _tn_matmul(a1, dh1, "tn_ffn", NDEV * FB // 2)
            held.append(send(l, "A", g))
        else:
            rows_job = _RowsGatherJob("agS", small_rows())
            g["g1"] = _tn_matmul(dg1, n1, "tn_ffn", NDEV * FB // 2, jobs=[rows_job])
            g["u1"] = _tn_matmul(du1, n1, "tn_ffn", NDEV * FB // 2, jobs=[send(l, "A1", g)])
            g["d1"] = _tn_matmul(a1, dh1, "tn_ffn", NDEV * FB // 2, jobs=[send(l, "A2", g)])
            held.append(send(l, "A3", g))
    _run_jobs("scatter_last", held)
    grad_x = d[None]

    big_out = {}
    for (l, pc), job in scatter.items():
        own = lax.dynamic_index_in_dim(job.ins[0], me, 0, keepdims=False)
        big_out[l, pc] = _adamw(own, job.results[0], _pack_piece(w, l, lay, pc, F32), _pack_piece(mom, l, lay, pc, F32),
                                _pack_piece(var, l, lay, pc, F32), "adamw_" + pc)

    small_sum = _sum_slots(rows_job.results[0])
    per_layer = len(_VECS) + CONV_KP
    g_vecs = {n: jnp.stack([small_sum[l * per_layer + k] for l in range(L)]) for k, n in enumerate(_VECS)}
    g_cw_full = jnp.stack([small_sum[l * per_layer + len(_VECS):(l + 1) * per_layer] for l in range(L)])
    g_cw_mine = lax.dynamic_slice_in_dim(g_cw_full, me * CS, CS, axis=2)
    g_fin = small_sum[L * per_layer]
    loss = (0.5 / D) * jnp.sum(small_sum[L * per_layer + 1])

    def small_slab(src, conv, fin):
        parts = [src[n] for n in _VECS] + [fin.reshape(1, D), conv.reshape(cw_rows, D)]
        sl = jnp.concatenate(parts, axis=0)
        return jnp.pad(sl, ((0, -sl.shape[0] % 8), (0, 0)))

    padk = ((0, 0), (0, CONV_KP - CONV_K), (0, 0))
    small_out = _adamw(
        small_slab(g_vecs, g_cw_mine, g_fin), None,
        small_slab(w, jnp.pad(w["conv_dw_w"], padk), w["final_norm"]),
        small_slab(mom, jnp.pad(mom["conv_dw_w"], padk), mom["final_norm"]),
        small_slab(var, jnp.pad(var["conv_dw_w"], padk), var["final_norm"]), "adamw_small")

    outs = []
    nv = len(_VECS)
    for which, small in enumerate(small_out):
        per = [{} for _ in range(L)]
        for (l, pc), slabs in big_out.items():
            per[l].update(_unpack_piece(slabs[which], lay, pc))
        res = {n: jnp.stack([per[l][n] for l in range(L)]) for n in _BIG}
        for k, n in enumerate(_VECS):
            res[n] = small[k * L:(k + 1) * L]
        res["final_norm"] = small[nv * L]
        res["conv_dw_w"] = small[nv * L + 1:nv * L + 1 + cw_rows].reshape(L, CONV_KP, CS)[:, :CONV_K]
        outs.append(res)
    return loss, grad_x, outs


def kernel(x, p, ffn1_norm, ffn1_w_gate, ffn1_w_up, ffn1_w_down, mix_norm, w_in, pool_w, pool_scale, conv_dw_w, conv_dw_b, conv_ln_g, conv_ln_b, conv_w_out, w_out, ffn2_norm, ffn2_w_gate, ffn2_w_up, ffn2_w_down, ple_norm, ple_w_gate, ple_w_proj, final_norm, loss_target, m_ffn1_norm, m_ffn1_w_gate, m_ffn1_w_up, m_ffn1_w_down, m_mix_norm, m_w_in, m_pool_w, m_pool_scale, m_conv_dw_w, m_conv_dw_b, m_conv_ln_g, m_conv_ln_b, m_conv_w_out, m_w_out, m_ffn2_norm, m_ffn2_w_gate, m_ffn2_w_up, m_ffn2_w_down, m_ple_norm, m_ple_w_gate, m_ple_w_proj, m_final_norm, v_ffn1_norm, v_ffn1_w_gate, v_ffn1_w_up, v_ffn1_w_down, v_mix_norm, v_w_in, v_pool_w, v_pool_scale, v_conv_dw_w, v_conv_dw_b, v_conv_ln_g, v_conv_ln_b, v_conv_w_out, v_w_out, v_ffn2_norm, v_ffn2_w_gate, v_ffn2_w_up, v_ffn2_w_down, v_ple_norm, v_ple_w_gate, v_ple_w_proj, v_final_norm):
    given = dict(locals())
    w = {n: given[n] for n in _WEIGHTS}
    mom = {n: given["m_" + n] for n in _WEIGHTS}
    var = {n: given["v_" + n] for n in _WEIGHTS}
    loss, grad_x, (grads, deltas, new_m, new_v) = _step(x, p, loss_target, w, mom, var)
    out = [loss, grad_x]
    for res in (grads, deltas, new_m, new_v):
        out += [res[n] for n in _WEIGHTS]
    return tuple(out)
```

```python
import functools

import jax
import jax.numpy as jnp
from jax import lax
from jax.experimental import pallas as pl
from jax.experimental.pallas import tpu as pltpu

F32, BF16 = jnp.float32, jnp.bfloat16
NDEV = 8
MESH = pl.DeviceIdType.MESH
HALO = 32
POOL_WINDOWS = (2, 4, 8, 16)
CONV_K = 31
CONV_KP = 32
RMS_EPS, LN_EPS = 1e-6, 1e-5
ADAM_LR, ADAM_B1, ADAM_B2, ADAM_EPS, ADAM_WD, ADAM_STEP = 0.001, 0.9, 0.999, 1e-08, 0.01, 10
LANE = 128
VMEM_LIMIT = 56 * 1024 * 1024
ANY = pl.BlockSpec(memory_space=pl.ANY)


def _nn(a, b):
    return jnp.dot(a, b, preferred_element_type=F32)


def _nt(a, b):
    return lax.dot_general(a, b, (((1,), (1,)), ((), ())), preferred_element_type=F32)


def _tn(a, b):
    return lax.dot_general(a, b, (((0,), (0,)), ((), ())), preferred_element_type=F32)


def _colsum8(v):
    return jnp.sum(v.reshape(v.shape[0] // 8, 8, v.shape[1]), axis=0)


def _sig(v):
    return jax.nn.sigmoid(v)


def _rms(h):
    r = lax.rsqrt(jnp.mean(h * h, axis=-1, keepdims=True) + RMS_EPS)
    return h * r, r


def _rms_bwd(dn, xh, r, gain):
    dxh = dn * gain
    return r * (dxh - xh * jnp.mean(dxh * xh, axis=-1, keepdims=True))


def _ln(c1):
    mu = jnp.mean(c1, axis=-1, keepdims=True)
    cen = c1 - mu
    rstd = lax.rsqrt(jnp.mean(cen * cen, axis=-1, keepdims=True) + LN_EPS)
    return cen * rstd, rstd


def _rows(tm, c):
    return pl.BlockSpec((tm, c), lambda i: (i, 0))


def _const(shape):
    return pl.BlockSpec(shape, lambda i: (0,) * len(shape))


def _params(n_grid=1):
    return pltpu.CompilerParams(dimension_semantics=("arbitrary",) * n_grid, vmem_limit_bytes=VMEM_LIMIT)


BF16_ROWS = 16
MXU_CHUNK = 768


def _tile(n, want):
    if n <= want:
        return n
    return max(t for t in range(BF16_ROWS, want + 1, BF16_ROWS) if n % t == 0)


def _hidden_chunks(width):
    return [(o, min(MXU_CHUNK, width - o)) for o in range(0, width, MXU_CHUNK)]


def _fetch(g_hbm, specs, sems):
    cps = []
    for wi, (off, rows, dst) in enumerate(specs):
        for dev in range(NDEV):
            cps.append(pltpu.make_async_copy(g_hbm.at[dev, pl.ds(off, rows), :],
                                             dst.at[pl.ds(dev * rows, rows), :], sems.at[wi, dev]))
    for cp in cps:
        cp.start()
    for cp in cps:
        cp.wait()


PIECES = ("A", "B", "C", "D")


class _Layout:
    def __init__(self, D, FS, NS, PD, PG, PGS):
        self.D, self.FS, self.NS, self.PD, self.PG, self.PGS = D, FS, NS, PD, PG, PGS
        assert FS % BF16_ROWS == 0 and (NDEV * FS) % (2 * LANE) == 0, "FFN shard rows must tile as bf16 row blocks"
        self.FB = FS
        self.DS = D // NDEV
        self.pieces = {
            "A": [("g1", self.FB), ("u1", self.FB), ("d1", self.FB)],
            "B": [("g2", self.FB), ("u2", self.FB), ("d2", self.FB)],
            "C": [("win", NS)],
            "D": [("wco", self.DS), ("wout", self.DS), ("wpg", self.DS), ("wpp", self.DS * PD // D),
                  ("pool", 4 * PGS * PG // D)],
            "A1": [("g1", self.FB)], "A2": [("u1", self.FB)], "A3": [("d1", self.FB)]}
        self.off, self.rows = {}, {}
        for pc in PIECES:
            o = 0
            for n, r in self.pieces[pc]:
                self.off[n], self.rows[n] = o, r
                o += r


def _pack_piece(w, l, lay, pc, dtype):
    D = lay.D
    make = {"g1": lambda: w["ffn1_w_gate"][l].T, "u1": lambda: w["ffn1_w_up"][l].T,
            "d1": lambda: w["ffn1_w_down"][l], "g2": lambda: w["ffn2_w_gate"][l].T,
            "u2": lambda: w["ffn2_w_up"][l].T, "d2": lambda: w["ffn2_w_down"][l],
            "win": lambda: w["w_in"][l].T, "wco": lambda: w["conv_w_out"][l], "wout": lambda: w["w_out"][l],
            "wpg": lambda: w["ple_w_gate"][l], "wpp": lambda: w["ple_w_proj"][l].T.reshape(-1, D),
            "pool": lambda: w["pool_w"][l].reshape(-1, D)}
    return jnp.concatenate([make[n]() for n, _ in lay.pieces[pc]], axis=0).astype(dtype)


def _unpack_piece(slab, lay, pc):
    PD, PG, PGS, DS = lay.PD, lay.PG, lay.PGS, lay.DS
    undo = {"g1": ("ffn1_w_gate", lambda a: a.T), "u1": ("ffn1_w_up", lambda a: a.T),
            "d1": ("ffn1_w_down", lambda a: a), "g2": ("ffn2_w_gate", lambda a: a.T),
            "u2": ("ffn2_w_up", lambda a: a.T), "d2": ("ffn2_w_down", lambda a: a),
            "win": ("w_in", lambda a: a.T), "wco": ("conv_w_out", lambda a: a), "wout": ("w_out", lambda a: a),
            "wpg": ("ple_w_gate", lambda a: a), "wpp": ("ple_w_proj", lambda a: a.reshape(DS, PD).T),
            "pool": ("pool_w", lambda a: a.reshape(4, PGS, PG))}
    out, o = {}, 0
    for n, r in lay.pieces[pc]:
        name, fn = undo[n]
        out[name] = fn(slab[o:o + r])
        o += r
    return out


def _place():
    return lax.axis_index("x"), lax.axis_index("y"), lax.axis_index("c")


FLIPS = [(a, b, d) for a in (0, 1) for b in (0, 1) for d in (0, 1)][1:]


class _GatherJob:
    def __init__(self, tag, slab):
        self.tag, self.ins = tag, [slab]
        self.outs = [jax.ShapeDtypeStruct((NDEV,) + slab.shape, slab.dtype)]
        self.scratch = [pltpu.SemaphoreType.DMA((7,)), pltpu.SemaphoreType.DMA((7,)), pltpu.SemaphoreType.DMA]
        self.results = None

    def _plan(self, ins, outs, sems):
        (x_ref,), (out_ref,), (send_sems, recv_sems, local_sem) = ins, outs, sems
        x, y, c = _place()
        me, sibling = (x, y, c), (x, y, 1 - c)
        chips = [(1 - x, y), (x, 1 - y), (1 - x, 1 - y)]

        def rows(px, py, pc):
            return out_ref.at[4 * px + 2 * py + pc]

        def copy(k, block, to, src=None):
            return pltpu.make_async_remote_copy(
                src_ref=rows(*block) if src is None else src, dst_ref=rows(*block),
                send_sem=send_sems.at[k], recv_sem=recv_sems.at[k], device_id=to, device_id_type=MESH)

        mine = pltpu.make_async_copy(x_ref, rows(*me), local_sem)
        first = [copy(0, me, sibling, src=x_ref)]
        first += [copy(1 + j, me, (*chip, c), src=x_ref) for j, chip in enumerate(chips)]
        passed = [copy(4 + j, (*chip, c), sibling) for j, chip in enumerate(chips)]
        landed = [copy(1 + j, (*chip, c), me) for j, chip in enumerate(chips)]
        late = [copy(0, sibling, me)] + [copy(4 + j, (*chip, 1 - c), me) for j, chip in enumerate(chips)]
        return mine, first, passed, landed, late

    def start(self, ins, outs, sems):
        mine, first, _, _, _ = self._plan(ins, outs, sems)
        mine.start()
        for cp in first:
            cp.start()

    def middle(self, ins, outs, sems):
        _, _, passed, landed, _ = self._plan(ins, outs, sems)
        for got, cp in zip(landed, passed):
            got.wait_recv()
            cp.start()

    def finish(self, ins, outs, sems):
        mine, first, passed, _, late = self._plan(ins, outs, sems)
        for got in late:
            got.wait_recv()
        for cp in first + passed:
            cp.wait_send()
        mine.wait()


class _ScatterJob:
    def __init__(self, tag, grads):
        self.tag, self.ins = tag, [grads]
        self.outs = [jax.ShapeDtypeStruct((7,) + grads.shape[1:], grads.dtype)]
        self.scratch = [pltpu.SemaphoreType.DMA((7,)), pltpu.SemaphoreType.DMA((7,))]
        self.results = None

    def _plan(self, ins, outs, sems):
        (g_ref,), (out_ref,), (send_sems, recv_sems) = ins, outs, sems
        x, y, c = _place()
        cps = []
        for k, (a, b, d) in enumerate(FLIPS):
            px, py, pc = x ^ a, y ^ b, c ^ d
            cps.append(pltpu.make_async_remote_copy(
                src_ref=g_ref.at[4 * px + 2 * py + pc], dst_ref=out_ref.at[k], send_sem=send_sems.at[k],
                recv_sem=recv_sems.at[k], device_id=(px, py, pc), device_id_type=MESH))
        return cps

    def start(self, ins, outs, sems):
        for cp in self._plan(ins, outs, sems):
            cp.start()

    def middle(self, ins, outs, sems):
        pass

    def finish(self, ins, outs, sems):
        cps = self._plan(ins, outs, sems)
        for cp in cps:
            cp.wait_recv()
        for cp in cps:
            cp.wait_send()


class _RowsGatherJob:
    def __init__(self, tag, rows):
        self.tag, self.ins = tag, [rows]
        self.outs = [jax.ShapeDtypeStruct((NDEV,) + rows.shape, rows.dtype)]
        self.scratch = [pltpu.SemaphoreType.DMA((7,)), pltpu.SemaphoreType.DMA((7,)), pltpu.SemaphoreType.DMA]
        self.results = None

    def _plan(self, ins, outs, sems):
        (x_ref,), (out_ref,), (send_sems, recv_sems, local_sem) = ins, outs, sems
        x, y, c = _place()
        me = 4 * x + 2 * y + c
        mine = pltpu.make_async_copy(x_ref, out_ref.at[me], local_sem)
        sends, lands = [], []
        for k, (a, b, d) in enumerate(FLIPS):
            px, py, pc = x ^ a, y ^ b, c ^ d
            for dst, keep in ((out_ref.at[me], sends), (out_ref.at[4 * px + 2 * py + pc], lands)):
                keep.append(pltpu.make_async_remote_copy(
                    src_ref=x_ref, dst_ref=dst, send_sem=send_sems.at[k], recv_sem=recv_sems.at[k],
                    device_id=(px, py, pc), device_id_type=MESH))
        return mine, sends, lands

    def start(self, ins, outs, sems):
        mine, sends, _ = self._plan(ins, outs, sems)
        mine.start()
        for cp in sends:
            cp.start()

    def middle(self, ins, outs, sems):
        pass

    def finish(self, ins, outs, sems):
        mine, sends, lands = self._plan(ins, outs, sems)
        for cp in lands:
            cp.wait_recv()
        for cp in sends:
            cp.wait_send()
        mine.wait()


def _sum_slots(slots):
    n, S, D = slots.shape

    def body(s_ref, o_ref):
        acc = s_ref[0]
        for j in range(1, n):
            acc = acc + s_ref[j]
        o_ref[...] = acc

    vm = pl.BlockSpec(memory_space=pltpu.VMEM)
    return pl.pallas_call(body, name="sum_slots", out_shape=jax.ShapeDtypeStruct((S, D), slots.dtype),
                          in_specs=[vm], out_specs=vm)(slots)


def _launch(name, body, grid, args, in_specs, out_specs, out_shape, scratch, jobs=()):
    grid = (grid,) if isinstance(grid, int) else tuple(grid)
    steps = grid[0] * (grid[1] if len(grid) == 2 else 1)
    n_in, n_out, n_sc = len(args), len(out_shape), len(scratch)
    j_in = [a for jb in jobs for a in jb.ins]
    j_out = [o for jb in jobs for o in jb.outs]
    j_sc = [s for jb in jobs for s in jb.scratch]
    mid = (17 * steps) // 20

    def wrapped(*refs):
        c_in, refs = refs[:n_in], refs[n_in:]
        m_in, refs = refs[:len(j_in)], refs[len(j_in):]
        c_out, refs = refs[:n_out], refs[n_out:]
        m_out, refs = refs[:len(j_out)], refs[len(j_out):]
        c_sc, m_sc = refs[:n_sc], refs[n_sc:]
        bound, a, b, c = [], 0, 0, 0
        for jb in jobs:
            bound.append((jb, m_in[a:a + len(jb.ins)], m_out[b:b + len(jb.outs)], m_sc[c:c + len(jb.scratch)]))
            a, b, c = a + len(jb.ins), b + len(jb.outs), c + len(jb.scratch)
        i = pl.program_id(0) if len(grid) == 1 else pl.program_id(0) * grid[1] + pl.program_id(1)

        def phase(step, which):
            if jobs:
                @pl.when(i == step)
                def _():
                    for jb, ins, outs, sems in bound:
                        getattr(jb, which)(ins, outs, sems)

        phase(0, "start")
        if body is not None:
            body(*c_in, *c_out, *c_sc)
        phase(mid, "middle")
        phase(steps - 1, "finish")

    outs = pl.pallas_call(
        wrapped, name=name + "".join("_" + jb.tag for jb in jobs), grid=grid,
        in_specs=list(in_specs) + [ANY] * len(j_in), out_specs=list(out_specs) + [ANY] * len(j_out),
        out_shape=list(out_shape) + j_out, scratch_shapes=list(scratch) + j_sc, compiler_params=_params(len(grid)),
    )(*args, *j_in)
    pos = n_out
    for jb in jobs:
        jb.results = list(outs[pos:pos + len(jb.outs)])
        pos += len(jb.outs)
    return list(outs[:n_out])


def _run_jobs(name, jobs):
    _launch(name, None, 1, [], [], [], [], [], jobs)


def _all_gather_small(rows_in):
    S, D = rows_in.shape

    def body(x_ref, out_ref, sum_ref, send_sems, recv_sems):
        x, y, c = _place()
        me = 4 * x + 2 * y + c
        out_ref[me] = x_ref[...]
        flips = [(a, b, d) for a in (0, 1) for b in (0, 1) for d in (0, 1)][1:]

        def copy(k):
            a, b, d = flips[k]
            px, py, pc = x ^ a, y ^ b, c ^ d
            peer = 4 * px + 2 * py + pc
            send = pltpu.make_async_remote_copy(
                src_ref=x_ref, dst_ref=out_ref.at[me], send_sem=send_sems.at[k], recv_sem=recv_sems.at[k],
                device_id=(px, py, pc), device_id_type=MESH)
            recv = pltpu.make_async_remote_copy(
                src_ref=x_ref, dst_ref=out_ref.at[peer], send_sem=send_sems.at[k], recv_sem=recv_sems.at[k],
                device_id=(px, py, pc), device_id_type=MESH)
            return send, recv

        cps = [copy(k) for k in range(7)]
        for send, _ in cps:
            send.start()
        for _, recv in cps:
            recv.wait_recv()
        for send, _ in cps:
            send.wait_send()
        acc = out_ref[0]
        for j in range(1, NDEV):
            acc = acc + out_ref[j]
        sum_ref[...] = acc

    vm = pl.BlockSpec(memory_space=pltpu.VMEM)
    return pl.pallas_call(
        body, name="ag_small",
        out_shape=(jax.ShapeDtypeStruct((NDEV, S, D), F32), jax.ShapeDtypeStruct((S, D), F32)),
        in_specs=[vm], out_specs=(vm, vm),
        scratch_shapes=[pltpu.SemaphoreType.DMA((7,)), pltpu.SemaphoreType.DMA((7,))],
    )(rows_in)


def _sum_parts(own, got, name):
    rows, D = own.shape
    tr = _tile(rows, 512)
    n_got = got.shape[0]

    def body(*refs):
        g = refs[0][...].astype(F32)
        for pr in refs[1:1 + n_got]:
            g = g + pr[...].astype(F32)
        refs[-1][...] = g

    got_specs = [pl.BlockSpec((None, tr, D), functools.partial(lambda k, i: (k, i, 0), k)) for k in range(n_got)]
    return pl.pallas_call(
        body, name=name, grid=(rows // tr,), in_specs=[_rows(tr, D)] + got_specs, out_specs=_rows(tr, D),
        out_shape=jax.ShapeDtypeStruct((rows, D), F32), compiler_params=_params(),
    )(own, *[got] * n_got)


def _adamw(name, entries, by_layer):
    n = len(entries)
    L = entries[0][0].shape[0]

    def body(*refs):
        ins, outs = refs[:4 * n], refs[4 * n:]
        for e in range(n):
            w_ref, g_ref, m_ref, v_ref = ins[4 * e:4 * e + 4]
            d_out, m_out, v_out = outs[3 * e:3 * e + 3]
            g = g_ref[...]
            m_new = ADAM_B1 * m_ref[...] + (1.0 - ADAM_B1) * g
            v_new = ADAM_B2 * v_ref[...] + (1.0 - ADAM_B2) * (g * g)
            m_hat = m_new / (1.0 - ADAM_B1 ** ADAM_STEP)
            v_hat = v_new / (1.0 - ADAM_B2 ** ADAM_STEP)
            d_out[...] = -ADAM_LR * (m_hat / (jnp.sqrt(v_hat) + ADAM_EPS) + ADAM_WD * w_ref[...])
            m_out[...] = m_new
            v_out[...] = v_new

    def spec(a):
        if by_layer:
            return pl.BlockSpec((None,) + a.shape[1:], lambda l: (l, 0, 0))
        return pl.BlockSpec(a.shape, lambda l: (0, 0, 0))

    flat = [a for e in entries for a in e]
    outs = pl.pallas_call(
        body, name=name, grid=(L if by_layer else 1,),
        in_specs=[spec(a) for a in flat], out_specs=[spec(e[0]) for e in entries for _ in range(3)],
        out_shape=[jax.ShapeDtypeStruct(e[0].shape, F32) for e in entries for _ in range(3)],
        compiler_params=_params(),
    )(*flat)
    return [tuple(outs[3 * e:3 * e + 3]) for e in range(n)]


def _tn_matmul(xa, ya, name, tmm, y_follows=False, jobs=()):
    T, M = xa.shape
    tn = tmm if y_follows else ya.shape[1]
    tt = _tile(T, 1024)
    nb = M // tmm

    def body(x_ref, y_ref, o_ref, acc):
        k = pl.program_id(1)

        @pl.when(k == 0)
        def _():
            acc[...] = jnp.zeros_like(acc)

        acc[...] += _tn(x_ref[...], y_ref[...])

        @pl.when(k == pl.num_programs(1) - 1)
        def _():
            o_ref[...] = acc[...].astype(o_ref.dtype)

    y_map = (lambda b, k: (k, b)) if y_follows else (lambda b, k: (k, 0))
    return _launch(
        name, body, (nb, T // tt), [xa, ya],
        [pl.BlockSpec((tt, tmm), lambda b, k: (k, b)), pl.BlockSpec((tt, tn), y_map)],
        [pl.BlockSpec((tmm, tn), lambda b, k: (b, 0))], [jax.ShapeDtypeStruct((M, tn), BF16)],
        [pltpu.VMEM((tmm, tn), F32)], jobs)[0]


def _ffn_fwd(h, gain, G, lay, jobs=()):
    T, D = h.shape
    FB = lay.FB
    FP = NDEV * FB
    offs = (0, FB, 2 * FB)
    tm = _tile(T, 512)

    def body(h_ref, gain_ref, g_hbm, ho_ref, go_ref, uo_ref, wg, wu, wd, sems):
        @pl.when(pl.program_id(0) == 0)
        def _():
            _fetch(g_hbm, [(offs[0], FB, wg), (offs[1], FB, wu), (offs[2], FB, wd)], sems)

        h = h_ref[...]
        xh, _ = _rms(h)
        n = (xh * gain_ref[...]).astype(BF16)
        acc = jnp.zeros((tm, D), F32)
        for start, width in _hidden_chunks(FP):
            sl = pl.ds(start, width)
            g = _nt(n, wg[sl, :])
            u = _nt(n, wu[sl, :])
            go_ref[:, sl] = g.astype(BF16)
            uo_ref[:, sl] = u.astype(BF16)
            a = g * _sig(g) * u
            acc = acc + _nn(a.astype(BF16), wd[sl, :])
        ho_ref[...] = h + 0.5 * acc

    return _launch(
        "ffn_fwd", body, T // tm, [h, gain, G],
        [_rows(tm, D), _const((1, D)), ANY], [_rows(tm, D), _rows(tm, FP), _rows(tm, FP)],
        [jax.ShapeDtypeStruct((T, D), F32), jax.ShapeDtypeStruct((T, FP), BF16), jax.ShapeDtypeStruct((T, FP), BF16)],
        [pltpu.VMEM((FP, D), BF16)] * 3 + [pltpu.SemaphoreType.DMA((3, NDEV))], jobs)


def _ffn_bwd(d, h, gain, ga, ua, G, lay, jobs=()):
    T, D = h.shape
    FB = lay.FB
    FP = NDEV * FB
    offs = (0, FB, 2 * FB)
    tm = _tile(T, 256)

    def body(d_ref, h_ref, gain_ref, ga_ref, ua_ref, g_hbm,
             do_ref, dg_ref, du_ref, a_ref, n_ref, dh_ref, gg_ref, wg, wu, wd, sems):
        @pl.when(pl.program_id(0) == 0)
        def _():
            _fetch(g_hbm, [(offs[0], FB, wg), (offs[1], FB, wu), (offs[2], FB, wd)], sems)
            gg_ref[...] = jnp.zeros_like(gg_ref)

        d = d_ref[...]
        gain_v = gain_ref[...]
        xh, r = _rms(h_ref[...])
        n_ref[...] = (xh * gain_v).astype(BF16)
        dh = (0.5 * d).astype(BF16)
        dh_ref[...] = dh
        dn = jnp.zeros((tm, D), F32)
        for start, width in _hidden_chunks(FP):
            sl = pl.ds(start, width)
            g = ga_ref[:, sl].astype(F32)
            u = ua_ref[:, sl].astype(F32)
            da = _nt(dh, wd[sl, :])
            s = _sig(g)
            silu = g * s
            a_ref[:, sl] = (silu * u).astype(BF16)
            dgv = (da * u * (s * (1.0 + g * (1.0 - s)))).astype(BF16)
            duv = (da * silu).astype(BF16)
            dg_ref[:, sl] = dgv
            du_ref[:, sl] = duv
            dn = dn + _nn(dgv, wg[sl, :]) + _nn(duv, wu[sl, :])
        gg_ref[...] += _colsum8(dn * xh)
        do_ref[...] = d + _rms_bwd(dn, xh, r, gain_v)

    wide = jax.ShapeDtypeStruct((T, FP), BF16)
    return _launch(
        "ffn_bwd", body, T // tm, [d, h, gain, ga, ua, G],
        [_rows(tm, D), _rows(tm, D), _const((1, D)), _rows(tm, FP), _rows(tm, FP), ANY],
        [_rows(tm, D), _rows(tm, FP), _rows(tm, FP), _rows(tm, FP), _rows(tm, D), _rows(tm, D), _const((8, D))],
        [jax.ShapeDtypeStruct((T, D), F32), wide, wide, wide, jax.ShapeDtypeStruct((T, D), BF16),
         jax.ShapeDtypeStruct((T, D), BF16), jax.ShapeDtypeStruct((8, D), F32)],
        [pltpu.VMEM((FP, D), BF16)] * 3 + [pltpu.SemaphoreType.DMA((3, NDEV))], jobs)


def _inproj_fwd(h, gain, G, lay):
    T, D = h.shape
    NS = lay.NS
    NIN, CH = NDEV * NS, 2 * NS
    tm = _tile(T, 512)

    def body(h_ref, gain_ref, g_hbm, z_ref, win, sems):
        @pl.when(pl.program_id(0) == 0)
        def _():
            _fetch(g_hbm, [(lay.off["win"], NS, win)], sems)

        xh, _ = _rms(h_ref[...])
        n = (xh * gain_ref[...]).astype(BF16)
        for j in range(NIN // CH):
            sl = pl.ds(j * CH, CH)
            z_ref[:, sl] = _nt(n, win[sl, :]).astype(BF16)

    return pl.pallas_call(
        body, name="inproj_fwd", grid=(T // tm,),
        in_specs=[_rows(tm, D), _const((1, D)), ANY], out_specs=_rows(tm, NIN),
        out_shape=jax.ShapeDtypeStruct((T, NIN), BF16),
        scratch_shapes=[pltpu.VMEM((NIN, D), BF16), pltpu.SemaphoreType.DMA((1, NDEV))],
        compiler_params=_params(),
    )(h, gain, G)


def _conv_chunks(tm, D):
    rb, lc = min(tm, 64), min(D, 256)
    return [(r0, l0, rb, lc) for r0 in range(0, tm, rb) for l0 in range(0, D, lc)]


SUBLANES = 8


def _preshift(sh, n_rows):
    for r in range(1, SUBLANES):
        sh[r, pl.ds(0, n_rows), :] = sh[0, pl.ds(r, n_rows), :]


def _window(sh, start, rows, lanes):
    r = start % SUBLANES
    return sh[r, pl.ds(start - r, rows), lanes]


def _mixer_fwd(z, h, pool_w, pscale, cw, cb, lng, lnb, G, lay, jobs=()):
    T, D = h.shape
    PG, DS = lay.PG, lay.DS
    tm = _tile(T, 256)
    hb = tm // HALO

    def body(z_ref, zp_ref, h_ref, pw_ref, ps_ref, cw_ref, cb_ref, lg_ref, lb_ref, g_hbm,
             ho_ref, c1_ref, q_ref, cc_ref, pool_ref, ext_p, sh_c, a_s, wco, wout, sems):
        i = pl.program_id(0)

        @pl.when(i == 0)
        def _():
            _fetch(g_hbm, [(lay.off["wco"], DS, wco), (lay.off["wout"], DS, wout)], sems)

        live = jnp.where(i > 0, 1.0, 0.0).astype(F32)
        ext_p[pl.ds(0, HALO), :] = zp_ref[:, pl.ds(0, D)].astype(F32) * live
        ext_p[pl.ds(HALO, tm), :] = z_ref[:, pl.ds(0, D)].astype(F32)
        sh_c[0, pl.ds(0, HALO), :] = (zp_ref[:, pl.ds(D, D)].astype(F32)
                                      * _sig(zp_ref[:, pl.ds(2 * D, D)].astype(F32)) * live)
        sh_c[0, pl.ds(HALO, tm), :] = z_ref[:, pl.ds(D, D)].astype(F32) * _sig(z_ref[:, pl.ds(2 * D, D)].astype(F32))
        _preshift(sh_c, tm + HALO - SUBLANES)

        t = i * tm + lax.broadcasted_iota(jnp.int32, (tm, 1), 0)
        for g, w in enumerate(POOL_WINDOWS):
            sl = pl.ds(g * PG, PG)
            s = ext_p[pl.ds(HALO, tm), sl]
            zc = s
            for j in range(1, w):
                s = s + ext_p[pl.ds(HALO - j, tm), sl]
            inv = 1.0 / jnp.minimum(t + 1, w).astype(F32)
            pooled = (s * inv - zc).astype(BF16)
            pool_ref[:, sl] = pooled
            qv = _nn(pooled, pw_ref[g])
            q_ref[:, sl] = qv.astype(BF16)
            a_s[:, sl] = qv * ps_ref[:, sl]

        for r0, l0, rb, lc in _conv_chunks(tm, D):
            ls = pl.ds(l0, lc)
            acc = jnp.zeros((rb, lc), F32) + cb_ref[:, ls]
            for k in range(CONV_K):
                acc = acc + cw_ref[pl.ds(k, 1), ls] * _window(sh_c, r0 + HALO - (CONV_K - 1) + k, rb, ls)
            c1_ref[pl.ds(r0, rb), ls] = acc

        xhat, _ = _ln(c1_ref[...])
        c2 = xhat * lg_ref[...] + lb_ref[...]
        c3 = (c2 * _sig(c2)).astype(BF16)
        cc = _nn(c3, wco[...])
        cc_ref[...] = cc.astype(BF16)
        gp = z_ref[:, pl.ds(3 * D, D)].astype(F32)
        gc = z_ref[:, pl.ds(4 * D, D)].astype(F32)
        m = (_sig(gp) * a_s[...] + _sig(gc) * cc).astype(BF16)
        ho_ref[...] = h_ref[...] + _nn(m, wout[...])

    act = jax.ShapeDtypeStruct((T, D), BF16)
    return _launch(
        "mixer_fwd", body, T // tm, [z, z, h, pool_w, pscale, cw, cb, lng, lnb, G],
        [_rows(tm, 5 * D), pl.BlockSpec((HALO, 5 * D), lambda i: (jnp.maximum(i * hb - 1, 0), 0)),
         _rows(tm, D), _const((4, PG, PG)), _const((1, D)), _const((CONV_KP, D)), _const((1, D)),
         _const((1, D)), _const((1, D)), ANY],
        [_rows(tm, D)] * 5,
        [jax.ShapeDtypeStruct((T, D), F32), jax.ShapeDtypeStruct((T, D), F32), act, act, act],
        [pltpu.VMEM((HALO + tm, D), F32), pltpu.VMEM((SUBLANES, HALO + tm, D), F32), pltpu.VMEM((tm, D), F32),
         pltpu.VMEM((D, D), BF16), pltpu.VMEM((D, D), BF16), pltpu.SemaphoreType.DMA((2, NDEV))], jobs)


def _mixer_bwd_rows(d, z, c1, qa, cca, pool_w, pscale, lng, lnb, G, lay):
    T, D = d.shape
    PG, DS = lay.PG, lay.DS
    tm = _tile(T, 256)

    def body(d_ref, z_ref, c1_ref, q_ref, cc_ref, pw_ref, ps_ref, lg_ref, lb_ref, g_hbm,
             m_ref, db_ref, dcc_ref, c3_ref, dq_ref, dpool_ref, dc1_ref, dzg_ref, gps_ref, glg_ref, glb_ref, gcb_ref,
             wco, wout, sems):
        @pl.when(pl.program_id(0) == 0)
        def _():
            _fetch(g_hbm, [(lay.off["wco"], DS, wco), (lay.off["wout"], DS, wout)], sems)
            for ref in (gps_ref, glg_ref, glb_ref, gcb_ref):
                ref[...] = jnp.zeros_like(ref)

        db = d_ref[...].astype(BF16)
        db_ref[...] = db
        dm = _nt(db, wout[...])
        q = q_ref[...].astype(F32)
        cc = cc_ref[...].astype(F32)
        ps = ps_ref[...]
        sp = _sig(z_ref[:, pl.ds(3 * D, D)].astype(F32))
        sc = _sig(z_ref[:, pl.ds(4 * D, D)].astype(F32))
        a = q * ps
        m_ref[...] = (sp * a + sc * cc).astype(BF16)
        da = dm * sp
        dzg_ref[:, pl.ds(0, D)] = (dm * a * sp * (1.0 - sp)).astype(BF16)
        dzg_ref[:, pl.ds(D, D)] = (dm * cc * sc * (1.0 - sc)).astype(BF16)
        gps_ref[...] += _colsum8(da * q)
        dq = (da * ps).astype(BF16)
        dq_ref[...] = dq
        for g in range(len(POOL_WINDOWS)):
            sl = pl.ds(g * PG, PG)
            dpool_ref[:, sl] = _nt(dq_ref[:, sl], pw_ref[g])

        dcc = (dm * sc).astype(BF16)
        dcc_ref[...] = dcc
        xhat, rstd = _ln(c1_ref[...])
        lg = lg_ref[...]
        c2 = xhat * lg + lb_ref[...]
        s2 = _sig(c2)
        c3_ref[...] = (c2 * s2).astype(BF16)
        dc2 = _nt(dcc, wco[...]) * (s2 * (1.0 + c2 * (1.0 - s2)))
        glg_ref[...] += _colsum8(dc2 * xhat)
        glb_ref[...] += _colsum8(dc2)
        dxh = dc2 * lg
        dc1 = rstd * (dxh - jnp.mean(dxh, axis=-1, keepdims=True)
                      - xhat * jnp.mean(dxh * xhat, axis=-1, keepdims=True))
        dc1_ref[...] = dc1
        gcb_ref[...] += _colsum8(dc1)

    act = jax.ShapeDtypeStruct((T, D), BF16)
    full = jax.ShapeDtypeStruct((T, D), F32)
    vec = jax.ShapeDtypeStruct((8, D), F32)
    return pl.pallas_call(
        body, name="mixer_bwd_rows", grid=(T // tm,),
        in_specs=[_rows(tm, D), _rows(tm, 5 * D), _rows(tm, D), _rows(tm, D), _rows(tm, D),
                  _const((4, PG, PG)), _const((1, D)), _const((1, D)), _const((1, D)), ANY],
        out_specs=[_rows(tm, D)] * 7 + [_rows(tm, 2 * D)] + [_const((8, D))] * 4,
        out_shape=[act, act, act, act, act, full, full, jax.ShapeDtypeStruct((T, 2 * D), BF16), vec, vec, vec, vec],
        scratch_shapes=[pltpu.VMEM((D, D), BF16), pltpu.VMEM((D, D), BF16), pltpu.SemaphoreType.DMA((2, NDEV))],
        compiler_params=_params(),
    )(d, z, c1, qa, cca, pool_w, pscale, lng, lnb, G)


def _mixer_bwd_time(dc1, dpool, z, cw, lay, jobs=()):
    T, D = dc1.shape
    PG = lay.PG
    tm = _tile(T, 256)
    hb = tm // HALO
    nt = T // tm

    def body(dc_ref, dcn_ref, dp_ref, dpn_ref, z_ref, zp_ref, cw_ref, dz_ref, gcw_ref, sh_d, ext_q, sh_c, dc0_s):
        i = pl.program_id(0)

        @pl.when(i == 0)
        def _():
            gcw_ref[...] = jnp.zeros_like(gcw_ref)

        live_p = jnp.where(i > 0, 1.0, 0.0).astype(F32)
        live_n = jnp.where(i < nt - 1, 1.0, 0.0).astype(F32)
        sh_d[0, pl.ds(0, tm), :] = dc_ref[...]
        sh_d[0, pl.ds(tm, HALO), :] = dcn_ref[...] * live_n
        _preshift(sh_d, tm + HALO - SUBLANES)
        zg = z_ref[:, pl.ds(2 * D, D)].astype(F32)
        za = z_ref[:, pl.ds(D, D)].astype(F32)
        sg = _sig(zg)
        sh_c[0, pl.ds(0, HALO), :] = (zp_ref[:, pl.ds(D, D)].astype(F32)
                                      * _sig(zp_ref[:, pl.ds(2 * D, D)].astype(F32)) * live_p)
        sh_c[0, pl.ds(HALO, tm), :] = za * sg
        _preshift(sh_c, tm + HALO - SUBLANES)

        t = i * tm + lax.broadcasted_iota(jnp.int32, (tm, 1), 0)
        tn = (i + 1) * tm + lax.broadcasted_iota(jnp.int32, (HALO, 1), 0)
        for g, w in enumerate(POOL_WINDOWS):
            sl = pl.ds(g * PG, PG)
            ext_q[pl.ds(0, tm), sl] = dp_ref[:, sl] * (1.0 / jnp.minimum(t + 1, w).astype(F32))
            ext_q[pl.ds(tm, HALO), sl] = dpn_ref[:, sl] * (live_n / jnp.minimum(tn + 1, w).astype(F32))
        for g, w in enumerate(POOL_WINDOWS):
            sl = pl.ds(g * PG, PG)
            s = ext_q[pl.ds(0, tm), sl]
            for j in range(1, w):
                s = s + ext_q[pl.ds(j, tm), sl]
            dz_ref[:, sl] = (s - dp_ref[:, sl]).astype(BF16)

        for r0, l0, rb, lc in _conv_chunks(tm, D):
            ls = pl.ds(l0, lc)
            acc = jnp.zeros((rb, lc), F32)
            for j in range(CONV_K):
                acc = acc + cw_ref[pl.ds(CONV_K - 1 - j, 1), ls] * _window(sh_d, r0 + j, rb, ls)
            dc0_s[pl.ds(r0, rb), ls] = acc
        dc0 = dc0_s[...]
        dz_ref[:, pl.ds(D, D)] = (dc0 * sg).astype(BF16)
        dz_ref[:, pl.ds(2 * D, D)] = (dc0 * za * sg * (1.0 - sg)).astype(BF16)

        lc = min(D, 256)
        for l0 in range(0, D, lc):
            ls = pl.ds(l0, lc)
            dcv = dc_ref[:, ls]
            for k in range(CONV_K):
                gcw_ref[pl.ds(8 * k, 8), ls] += _colsum8(dcv * _window(sh_c, HALO - (CONV_K - 1) + k, tm, ls))

    nxt = lambda i: (jnp.minimum((i + 1) * hb, T // HALO - 1), 0)
    return _launch(
        "mixer_bwd_time", body, nt, [dc1, dc1, dpool, dpool, z, z, cw],
        [_rows(tm, D), pl.BlockSpec((HALO, D), nxt), _rows(tm, D), pl.BlockSpec((HALO, D), nxt),
         _rows(tm, 5 * D), pl.BlockSpec((HALO, 5 * D), lambda i: (jnp.maximum(i * hb - 1, 0), 0)),
         _const((CONV_KP, D))],
        [_rows(tm, 3 * D), _const((CONV_KP * 8, D))],
        [jax.ShapeDtypeStruct((T, 3 * D), BF16), jax.ShapeDtypeStruct((CONV_KP * 8, D), F32)],
        [pltpu.VMEM((SUBLANES, tm + HALO, D), F32), pltpu.VMEM((tm + HALO, D), F32),
         pltpu.VMEM((SUBLANES, HALO + tm, D), F32), pltpu.VMEM((tm, D), F32)], jobs)


def _inproj_bwd(d, h, gain, dzm, dzg, G, lay):
    T, D = h.shape
    NS = lay.NS
    NIN = NDEV * NS
    tm = _tile(T, 512)

    def body(d_ref, h_ref, gain_ref, dzm_ref, dzg_ref, g_hbm, do_ref, u_ref, gg_ref, win, sems):
        @pl.when(pl.program_id(0) == 0)
        def _():
            _fetch(g_hbm, [(lay.off["win"], NS, win)], sems)
            gg_ref[...] = jnp.zeros_like(gg_ref)

        gain_v = gain_ref[...]
        xh, r = _rms(h_ref[...])
        u_ref[...] = (xh * gain_v).astype(BF16)
        dn = jnp.zeros((tm, D), F32)
        for j in range(3):
            dn = dn + _nn(dzm_ref[:, pl.ds(j * D, D)], win[pl.ds(j * D, D), :])
        for j in range(2):
            dn = dn + _nn(dzg_ref[:, pl.ds(j * D, D)], win[pl.ds((3 + j) * D, D), :])
        gg_ref[...] += _colsum8(dn * xh)
        do_ref[...] = d_ref[...] + _rms_bwd(dn, xh, r, gain_v)

    return pl.pallas_call(
        body, name="inproj_bwd", grid=(T // tm,),
        in_specs=[_rows(tm, D), _rows(tm, D), _const((1, D)), _rows(tm, 3 * D), _rows(tm, 2 * D), ANY],
        out_specs=[_rows(tm, D), _rows(tm, D), _const((8, D))],
        out_shape=[jax.ShapeDtypeStruct((T, D), F32), jax.ShapeDtypeStruct((T, D), BF16),
                   jax.ShapeDtypeStruct((8, D), F32)],
        scratch_shapes=[pltpu.VMEM((NIN, D), BF16), pltpu.SemaphoreType.DMA((1, NDEV))],
        compiler_params=_params(),
    )(d, h, gain, dzm, dzg, G)


def _ple_fwd(h, pe, gain, wppt, G, lay):
    T, D = h.shape
    PD, DS = lay.PD, lay.DS
    tm = _tile(T, 512)

    def body(h_ref, p_ref, gain_ref, wpp_ref, g_hbm, ho_ref, gate_ref, wpg, sems):
        @pl.when(pl.program_id(0) == 0)
        def _():
            _fetch(g_hbm, [(lay.off["wpg"], DS, wpg)], sems)

        h = h_ref[...]
        xh, _ = _rms(h)
        n = (xh * gain_ref[...]).astype(BF16)
        gate = _sig(_nn(n, wpg[...]))
        gate_ref[...] = gate.astype(BF16)
        e = _nt(p_ref[...].astype(BF16), wpp_ref[...])
        ho_ref[...] = h + gate * e

    return pl.pallas_call(
        body, name="ple_fwd", grid=(T // tm,),
        in_specs=[_rows(tm, D), _rows(tm, PD), _const((1, D)), _const((D, PD)), ANY],
        out_specs=[_rows(tm, D), _rows(tm, D)],
        out_shape=[jax.ShapeDtypeStruct((T, D), F32), jax.ShapeDtypeStruct((T, D), BF16)],
        scratch_shapes=[pltpu.VMEM((D, D), BF16), pltpu.SemaphoreType.DMA((1, NDEV))],
        compiler_params=_params(),
    )(h, pe, gain, wppt, G)


def _ple_bwd(d, h, pe, gate_a, gain, wppt, G, lay):
    T, D = h.shape
    PD, DS = lay.PD, lay.DS
    tm = _tile(T, 512)

    def body(d_ref, h_ref, p_ref, gate_ref, gain_ref, wpp_ref, g_hbm,
             do_ref, dpre_ref, de_ref, n_ref, pb_ref, gg_ref, wpg, sems):
        @pl.when(pl.program_id(0) == 0)
        def _():
            _fetch(g_hbm, [(lay.off["wpg"], DS, wpg)], sems)
            gg_ref[...] = jnp.zeros_like(gg_ref)

        d = d_ref[...]
        gain_v = gain_ref[...]
        xh, r = _rms(h_ref[...])
        n_ref[...] = (xh * gain_v).astype(BF16)
        pb = p_ref[...].astype(BF16)
        pb_ref[...] = pb
        e = _nt(pb, wpp_ref[...])
        gate = gate_ref[...].astype(F32)
        de_ref[...] = (d * gate).astype(BF16)
        dpre = (d * e * gate * (1.0 - gate)).astype(BF16)
        dpre_ref[...] = dpre
        dn = _nt(dpre, wpg[...])
        gg_ref[...] += _colsum8(dn * xh)
        do_ref[...] = d + _rms_bwd(dn, xh, r, gain_v)

    act = jax.ShapeDtypeStruct((T, D), BF16)
    return pl.pallas_call(
        body, name="ple_bwd", grid=(T // tm,),
        in_specs=[_rows(tm, D), _rows(tm, D), _rows(tm, PD), _rows(tm, D), _const((1, D)), _const((D, PD)), ANY],
        out_specs=[_rows(tm, D), _rows(tm, D), _rows(tm, D), _rows(tm, D), _rows(tm, PD), _const((8, D))],
        out_shape=[jax.ShapeDtypeStruct((T, D), F32), act, act, act, jax.ShapeDtypeStruct((T, PD), BF16),
                   jax.ShapeDtypeStruct((8, D), F32)],
        scratch_shapes=[pltpu.VMEM((D, D), BF16), pltpu.SemaphoreType.DMA((1, NDEV))],
        compiler_params=_params(),
    )(d, h, pe, gate_a, gain, wppt, G)


def _head(h, target, gain):
    T, D = h.shape
    tm = _tile(T, 512)

    def body(h_ref, t_ref, gain_ref, do_ref, loss_ref, gg_ref):
        @pl.when(pl.program_id(0) == 0)
        def _():
            loss_ref[...] = jnp.zeros_like(loss_ref)
            gg_ref[...] = jnp.zeros_like(gg_ref)

        gain_v = gain_ref[...]
        xh, r = _rms(h_ref[...])
        err = xh * gain_v - t_ref[...]
        loss_ref[...] += _colsum8(err * err)
        dy = err * (1.0 / D)
        gg_ref[...] += _colsum8(dy * xh)
        do_ref[...] = _rms_bwd(dy, xh, r, gain_v)

    vec = jax.ShapeDtypeStruct((8, D), F32)
    return pl.pallas_call(
        body, name="head", grid=(T // tm,),
        in_specs=[_rows(tm, D), _rows(tm, D), _const((1, D))],
        out_specs=[_rows(tm, D), _const((8, D)), _const((8, D))],
        out_shape=[jax.ShapeDtypeStruct((T, D), F32), vec, vec], compiler_params=_params(),
    )(h, target, gain)


_BIG = ["ffn1_w_gate", "ffn1_w_up", "ffn1_w_down", "w_in", "pool_w", "conv_w_out", "w_out", "ffn2_w_gate",
        "ffn2_w_up", "ffn2_w_down", "ple_w_gate", "ple_w_proj"]
_VECS = ["ffn1_norm", "mix_norm", "pool_scale", "conv_dw_b", "conv_ln_g", "conv_ln_b", "ffn2_norm", "ple_norm"]
_WEIGHTS = ["ffn1_norm", "ffn1_w_gate", "ffn1_w_up", "ffn1_w_down", "mix_norm", "w_in", "pool_w", "pool_scale",
            "conv_dw_w", "conv_dw_b", "conv_ln_g", "conv_ln_b", "conv_w_out", "w_out", "ffn2_norm", "ffn2_w_gate",
            "ffn2_w_up", "ffn2_w_down", "ple_norm", "ple_w_gate", "ple_w_proj", "final_norm"]


def _step(x, p, loss_target, w, mom, var):
    T, D = x.shape[1], x.shape[2]
    L = p.shape[0]
    FS, NS = w["ffn1_w_gate"].shape[2], w["w_in"].shape[2]
    PD = p.shape[3]
    PGS, PG = w["pool_w"].shape[2], w["pool_w"].shape[3]
    CS = w["conv_dw_w"].shape[2]
    lay = _Layout(D, FS, NS, PD, PG, PGS)
    FB = lay.FB
    ax, ay, ac = _place()
    me = 4 * ax + 2 * ay + ac

    assert L == 2, "the exchange schedule below is written for two layers"
    gather = {(l, pc): _GatherJob(f"ag{pc}", _pack_piece(w, l, lay, pc, BF16)) for l in range(L) for pc in PIECES}
    fwd_jobs = {("ffn1", 0): [gather[0, "C"], gather[0, "D"]], ("mixer", 0): [gather[0, "B"]],
                ("ffn2", 0): [gather[1, "A"], gather[1, "D"]], ("ffn1", 1): [gather[1, "C"]],
                ("mixer", 1): [gather[1, "B"]]}
    _run_jobs("gather_first", [gather[0, "A"]])
    cw_mine = jnp.pad(w["conv_dw_w"], ((0, 0), (0, CONV_KP - CONV_K), (0, 0)))
    cw_rows = L * CONV_KP * CS // D
    cw_all, _ = _all_gather_small(cw_mine.reshape(cw_rows, D))
    cw_full = cw_all.reshape(NDEV, L, CONV_KP, CS).transpose(1, 2, 0, 3).reshape(L, CONV_KP, D)

    def gathered(l, pc):
        return gather[l, pc].results[0]

    def small_mats(l):
        G = gathered(l, "D")
        wppt = G[:, lay.off["wpp"]:lay.off["wpp"] + lay.rows["wpp"]].reshape(D, PD)
        pw = G[:, lay.off["pool"]:lay.off["pool"] + lay.rows["pool"]].reshape(NDEV, 4, PGS, PG)
        return wppt, pw.transpose(1, 0, 2, 3).reshape(4, PG, PG)

    def vec(name, l):
        return w[name][l].reshape(1, D)

    h = x[0]
    saved = []
    for l in range(L):
        s = {"h0": h}
        h, s["g1"], s["u1"] = _ffn_fwd(h, vec("ffn1_norm", l), gathered(l, "A"), lay, fwd_jobs.get(("ffn1", l), ()))
        wppt, pw = small_mats(l)
        s["wppt"], s["pw"], s["h1"] = wppt, pw, h
        s["z"] = _inproj_fwd(h, vec("mix_norm", l), gathered(l, "C"), lay)
        h, s["c1"], s["q"], s["cc"], s["pooled"] = _mixer_fwd(
            s["z"], h, pw, vec("pool_scale", l), cw_full[l], vec("conv_dw_b", l), vec("conv_ln_g", l),
            vec("conv_ln_b", l), gathered(l, "D"), lay, fwd_jobs.get(("mixer", l), ()))
        s["h2"] = h
        h, s["g2"], s["u2"] = _ffn_fwd(h, vec("ffn2_norm", l), gathered(l, "B"), lay, fwd_jobs.get(("ffn2", l), ()))
        s["h3"] = h
        h, s["gate"] = _ple_fwd(h, p[l, 0], vec("ple_norm", l), wppt, gathered(l, "D"), lay)
        saved.append(s)

    d, loss_part, g_final = _head(h, loss_target[0], w["final_norm"].reshape(1, D))

    scatter = {}
    small_parts = [None] * L

    def send(l, pc, blocks):
        scatter[l, pc] = _ScatterJob(f"rs{pc}", jnp.concatenate(
            [blocks[n].reshape(NDEV, lay.rows[n], D) for n, _ in lay.pieces[pc]], axis=1))
        return scatter[l, pc]

    def small_rows():
        rows = []
        for l in range(L):
            rows += [v.sum(axis=0, keepdims=True) for v in small_parts[l][:-1]]
            rows.append(small_parts[l][-1].reshape(CONV_KP, 8, D).sum(axis=1))
        rows.append(g_final.sum(axis=0, keepdims=True))
        rows.append(loss_part.sum(axis=0, keepdims=True))
        n_rows = sum(r.shape[0] for r in rows)
        return jnp.pad(jnp.concatenate(rows, axis=0), ((0, -n_rows % 8), (0, 0)))

    held = []
    for l in reversed(range(L)):
        s = saved[l]
        wppt, pw = s["wppt"], s["pw"]
        g = {}
        d, dpre, de, n_ple, pb, g_ple = _ple_bwd(d, s["h3"], p[l, 0], s["gate"], vec("ple_norm", l), wppt,
                                                 gathered(l, "D"), lay)
        g["wpg"] = _tn_matmul(n_ple, dpre, "tn_sq", _tile(D, 512))
        g["wpp"] = _tn_matmul(de, pb, "tn_proj", _tile(D, 512))

        d, dg2, du2, a2, n2, dh2, g_n2 = _ffn_bwd(d, s["h2"], vec("ffn2_norm", l), s["g2"], s["u2"],
                                                  gathered(l, "B"), lay, held)
        held = []
        g["g2"] = _tn_matmul(dg2, n2, "tn_ffn", NDEV * FB // 2)
        g["u2"] = _tn_matmul(du2, n2, "tn_ffn", NDEV * FB // 2)
        g["d2"] = _tn_matmul(a2, dh2, "tn_ffn", NDEV * FB // 2)
        held.append(send(l, "B", g))

        (m_b, d_b, dcc, c3, dq, dpool, dc1, dzg, g_ps, g_lg, g_lb, g_cb) = _mixer_bwd_rows(
            d, s["z"], s["c1"], s["q"], s["cc"], pw, vec("pool_scale", l), vec("conv_ln_g", l), vec("conv_ln_b", l),
            gathered(l, "D"), lay)
        g["wout"] = _tn_matmul(m_b, d_b, "tn_sq", _tile(D, 512))
        g["wco"] = _tn_matmul(c3, dcc, "tn_sq", _tile(D, 512))
        g_pool = _tn_matmul(s["pooled"], dq, "tn_pool", PG, y_follows=True)
        g["pool"] = g_pool.reshape(4, NDEV, PGS, PG).transpose(1, 0, 2, 3)
        held.append(send(l, "D", g))
        dzm, g_cw = _mixer_bwd_time(dc1, dpool, s["z"], cw_full[l], lay, held)
        held = []
        d, u_b, g_mix = _inproj_bwd(d, s["h1"], vec("mix_norm", l), dzm, dzg, gathered(l, "C"), lay)
        g["win"] = jnp.concatenate([_tn_matmul(dzm, u_b, "tn_in3", _tile(D, 512)),
                                    _tn_matmul(dzg, u_b, "tn_in2", _tile(D, 512))], axis=0)
        held.append(send(l, "C", g))

        d, dg1, du1, a1, n1, dh1, g_n1 = _ffn_bwd(d, s["h0"], vec("ffn1_norm", l), s["g1"], s["u1"],
                                                  gathered(l, "A"), lay, held)
        held = []
        small_parts[l] = [g_n1, g_mix, g_ps, g_cb, g_lg, g_lb, g_n2, g_ple, g_cw]
        if l > 0:
            g["g1"] = _tn_matmul(dg1, n1, "tn_ffn", NDEV * FB // 2)
            g["u1"] = _tn_matmul(du1, n1, "tn_ffn", NDEV * FB // 2)
            g["d1"] = _tn_matmul(a1, dh1, "tn_ffn", NDEV * FB // 2)
            held.append(send(l, "A", g))
        else:
            rows_job = _RowsGatherJob("agS", small_rows())
            g["g1"] = _tn_matmul(dg1, n1, "tn_ffn", NDEV * FB // 2, jobs=[rows_job])
            g["u1"] = _tn_matmul(du1, n1, "tn_ffn", NDEV * FB // 2, jobs=[send(l, "A1", g)])
            g["d1"] = _tn_matmul(a1, dh1, "tn_ffn", NDEV * FB // 2, jobs=[send(l, "A2", g)])
            held.append(send(l, "A3", g))
    _run_jobs("scatter_last", held)
    grad_x = d[None]

    per = [{} for _ in range(L)]
    for (l, pc), job in scatter.items():
        own = lax.dynamic_index_in_dim(job.ins[0], me, 0, keepdims=False)
        per[l].update(_unpack_piece(_sum_parts(own, job.results[0], "sum_" + pc), lay, pc))
    grads = {n: jnp.stack([per[l][n] for l in range(L)]) for n in _BIG}

    small_sum = _sum_slots(rows_job.results[0])
    per_layer = len(_VECS) + CONV_KP
    for k, n in enumerate(_VECS):
        grads[n] = jnp.stack([small_sum[l * per_layer + k] for l in range(L)])
    g_cw_full = jnp.stack([small_sum[l * per_layer + len(_VECS):l * per_layer + len(_VECS) + CONV_K]
                           for l in range(L)])
    grads["conv_dw_w"] = lax.dynamic_slice_in_dim(g_cw_full, me * CS, CS, axis=2)
    grads["final_norm"] = small_sum[L * per_layer]
    loss = (0.5 / D) * jnp.sum(small_sum[L * per_layer + 1])

    def as3(a):
        return a.reshape((1, 1, -1) if a.ndim == 1 else (a.shape[0], -1, a.shape[-1]))

    deltas, new_m, new_v = {}, {}, {}

    def update(name, names, by_layer):
        res = _adamw(name, [(as3(w[n]), as3(grads[n]), as3(mom[n]), as3(var[n])) for n in names], by_layer)
        for n, (delta, m_new, v_new) in zip(names, res):
            deltas[n], new_m[n], new_v[n] = (a.reshape(w[n].shape) for a in (delta, m_new, v_new))

    for n in ("ffn1_w_gate", "ffn1_w_up", "ffn1_w_down", "ffn2_w_gate", "ffn2_w_up", "ffn2_w_down", "w_in"):
        update("adamw_" + n, [n], True)
    update("adamw_mid", ["conv_w_out", "w_out", "ple_w_gate", "ple_w_proj", "pool_w"], True)
    update("adamw_small", _VECS + ["conv_dw_w", "final_norm"], False)
    return loss, grad_x, (grads, deltas, new_m, new_v)


def kernel(x, p, ffn1_norm, ffn1_w_gate, ffn1_w_up, ffn1_w_down, mix_norm, w_in, pool_w, pool_scale, conv_dw_w, conv_dw_b, conv_ln_g, conv_ln_b, conv_w_out, w_out, ffn2_norm, ffn2_w_gate, ffn2_w_up, ffn2_w_down, ple_norm, ple_w_gate, ple_w_proj, final_norm, loss_target, m_ffn1_norm, m_ffn1_w_gate, m_ffn1_w_up, m_ffn1_w_down, m_mix_norm, m_w_in, m_pool_w, m_pool_scale, m_conv_dw_w, m_conv_dw_b, m_conv_ln_g, m_conv_ln_b, m_conv_w_out, m_w_out, m_ffn2_norm, m_ffn2_w_gate, m_ffn2_w_up, m_ffn2_w_down, m_ple_norm, m_ple_w_gate, m_ple_w_proj, m_final_norm, v_ffn1_norm, v_ffn1_w_gate, v_ffn1_w_up, v_ffn1_w_down, v_mix_norm, v_w_in, v_pool_w, v_pool_scale, v_conv_dw_w, v_conv_dw_b, v_conv_ln_g, v_conv_ln_b, v_conv_w_out, v_w_out, v_ffn2_norm, v_ffn2_w_gate, v_ffn2_w_up, v_ffn2_w_down, v_ple_norm, v_ple_w_gate, v_ple_w_proj, v_final_norm):
    given = dict(locals())
    w = {n: given[n] for n in _WEIGHTS}
    mom = {n: given["m_" + n] for n in _WEIGHTS}
    var = {n: given["v_" + n] for n in _WEIGHTS}
    loss, grad_x, (grads, deltas, new_m, new_v) = _step(x, p, loss_target, w, mom, var)
    out = [loss, grad_x]
    for res in (grads, deltas, new_m, new_v):
        out += [res[n] for n in _WEIGHTS]
    return tuple(out)
```

```python
import functools

import jax
import jax.numpy as jnp
from jax import lax
from jax.experimental import pallas as pl
from jax.experimental.pallas import tpu as pltpu

F32, BF16 = jnp.float32, jnp.bfloat16
NDEV = 8
MESH = pl.DeviceIdType.MESH
HALO = 32
POOL_WINDOWS = (2, 4, 8, 16)
CONV_K = 31
CONV_KP = 32
RMS_EPS, LN_EPS = 1e-6, 1e-5
ADAM_LR, ADAM_B1, ADAM_B2, ADAM_EPS, ADAM_WD, ADAM_STEP = 0.001, 0.9, 0.999, 1e-08, 0.01, 10
LANE = 128
VMEM_LIMIT = 56 * 1024 * 1024
ANY = pl.BlockSpec(memory_space=pl.ANY)


def _nn(a, b):
    return jnp.dot(a, b, preferred_element_type=F32)


def _nt(a, b):
    return lax.dot_general(a, b, (((1,), (1,)), ((), ())), preferred_element_type=F32)


def _tn(a, b):
    return lax.dot_general(a, b, (((0,), (0,)), ((), ())), preferred_element_type=F32)


def _colsum8(v):
    return jnp.sum(v.reshape(v.shape[0] // 8, 8, v.shape[1]), axis=0)


def _sig(v):
    return jax.nn.sigmoid(v)


def _rms(h):
    r = lax.rsqrt(jnp.mean(h * h, axis=-1, keepdims=True) + RMS_EPS)
    return h * r, r


def _rms_bwd(dn, xh, r, gain):
    dxh = dn * gain
    return r * (dxh - xh * jnp.mean(dxh * xh, axis=-1, keepdims=True))


def _ln(c1):
    mu = jnp.mean(c1, axis=-1, keepdims=True)
    cen = c1 - mu
    rstd = lax.rsqrt(jnp.mean(cen * cen, axis=-1, keepdims=True) + LN_EPS)
    return cen * rstd, rstd


def _rows(tm, c):
    return pl.BlockSpec((tm, c), lambda i: (i, 0))


def _const(shape):
    return pl.BlockSpec(shape, lambda i: (0,) * len(shape))


def _params(n_grid=1):
    return pltpu.CompilerParams(dimension_semantics=("arbitrary",) * n_grid, vmem_limit_bytes=VMEM_LIMIT)


BF16_ROWS = 16
MXU_CHUNK = 768


def _tile(n, want):
    if n <= want:
        return n
    return max(t for t in range(BF16_ROWS, want + 1, BF16_ROWS) if n % t == 0)


def _hidden_chunks(width):
    return [(o, min(MXU_CHUNK, width - o)) for o in range(0, width, MXU_CHUNK)]


def _fetch(g_hbm, specs, sems):
    cps = []
    for wi, (off, rows, dst) in enumerate(specs):
        for dev in range(NDEV):
            cps.append(pltpu.make_async_copy(g_hbm.at[dev, pl.ds(off, rows), :],
                                             dst.at[pl.ds(dev * rows, rows), :], sems.at[wi, dev]))
    for cp in cps:
        cp.start()
    for cp in cps:
        cp.wait()


PIECES = ("A", "B", "C", "D")


class _Layout:
    def __init__(self, D, FS, NS, PD, PG, PGS):
        self.D, self.FS, self.NS, self.PD, self.PG, self.PGS = D, FS, NS, PD, PG, PGS
        assert FS % BF16_ROWS == 0 and (NDEV * FS) % (2 * LANE) == 0, "FFN shard rows must tile as bf16 row blocks"
        self.FB = FS
        self.DS = D // NDEV
        self.pieces = {
            "A": [("g1", self.FB), ("u1", self.FB), ("d1", self.FB)],
            "B": [("g2", self.FB), ("u2", self.FB), ("d2", self.FB)],
            "C": [("win", NS)],
            "D": [("wco", self.DS), ("wout", self.DS), ("wpg", self.DS), ("wpp", self.DS * PD // D),
                  ("pool", 4 * PGS * PG // D)],
            "A1": [("g1", self.FB)], "A2": [("u1", self.FB)], "A3": [("d1", self.FB)]}
        self.off, self.rows = {}, {}
        for pc in PIECES:
            o = 0
            for n, r in self.pieces[pc]:
                self.off[n], self.rows[n] = o, r
                o += r


def _pack_piece(w, l, lay, pc, dtype):
    D = lay.D
    make = {"g1": lambda: w["ffn1_w_gate"][l].T, "u1": lambda: w["ffn1_w_up"][l].T,
            "d1": lambda: w["ffn1_w_down"][l], "g2": lambda: w["ffn2_w_gate"][l].T,
            "u2": lambda: w["ffn2_w_up"][l].T, "d2": lambda: w["ffn2_w_down"][l],
            "win": lambda: w["w_in"][l].T, "wco": lambda: w["conv_w_out"][l], "wout": lambda: w["w_out"][l],
            "wpg": lambda: w["ple_w_gate"][l], "wpp": lambda: w["ple_w_proj"][l].T.reshape(-1, D),
            "pool": lambda: w["pool_w"][l].reshape(-1, D)}
    return jnp.concatenate([make[n]() for n, _ in lay.pieces[pc]], axis=0).astype(dtype)


def _unpack_piece(slab, lay, pc):
    PD, PG, PGS, DS = lay.PD, lay.PG, lay.PGS, lay.DS
    undo = {"g1": ("ffn1_w_gate", lambda a: a.T), "u1": ("ffn1_w_up", lambda a: a.T),
            "d1": ("ffn1_w_down", lambda a: a), "g2": ("ffn2_w_gate", lambda a: a.T),
            "u2": ("ffn2_w_up", lambda a: a.T), "d2": ("ffn2_w_down", lambda a: a),
            "win": ("w_in", lambda a: a.T), "wco": ("conv_w_out", lambda a: a), "wout": ("w_out", lambda a: a),
            "wpg": ("ple_w_gate", lambda a: a), "wpp": ("ple_w_proj", lambda a: a.reshape(DS, PD).T),
            "pool": ("pool_w", lambda a: a.reshape(4, PGS, PG))}
    out, o = {}, 0
    for n, r in lay.pieces[pc]:
        name, fn = undo[n]
        out[name] = fn(slab[o:o + r])
        o += r
    return out


def _place():
    return lax.axis_index("x"), lax.axis_index("y"), lax.axis_index("c")


FLIPS = [(a, b, d) for a in (0, 1) for b in (0, 1) for d in (0, 1)][1:]


class _GatherJob:
    def __init__(self, tag, slab):
        self.tag, self.ins = tag, [slab]
        self.outs = [jax.ShapeDtypeStruct((NDEV,) + slab.shape, slab.dtype)]
        self.scratch = [pltpu.SemaphoreType.DMA((7,)), pltpu.SemaphoreType.DMA((7,)), pltpu.SemaphoreType.DMA]
        self.results = None

    def _plan(self, ins, outs, sems):
        (x_ref,), (out_ref,), (send_sems, recv_sems, local_sem) = ins, outs, sems
        x, y, c = _place()
        me, sibling = (x, y, c), (x, y, 1 - c)
        chips = [(1 - x, y), (x, 1 - y), (1 - x, 1 - y)]

        def rows(px, py, pc):
            return out_ref.at[4 * px + 2 * py + pc]

        def copy(k, block, to, src=None):
            return pltpu.make_async_remote_copy(
                src_ref=rows(*block) if src is None else src, dst_ref=rows(*block),
                send_sem=send_sems.at[k], recv_sem=recv_sems.at[k], device_id=to, device_id_type=MESH)

        mine = pltpu.make_async_copy(x_ref, rows(*me), local_sem)
        first = [copy(0, me, sibling, src=x_ref)]
        first += [copy(1 + j, me, (*chip, c), src=x_ref) for j, chip in enumerate(chips)]
        passed = [copy(4 + j, (*chip, c), sibling) for j, chip in enumerate(chips)]
        landed = [copy(1 + j, (*chip, c), me) for j, chip in enumerate(chips)]
        late = [copy(0, sibling, me)] + [copy(4 + j, (*chip, 1 - c), me) for j, chip in enumerate(chips)]
        return mine, first, passed, landed, late

    def start(self, ins, outs, sems):
        mine, first, _, _, _ = self._plan(ins, outs, sems)
        mine.start()
        for cp in first:
            cp.start()

    def middle(self, ins, outs, sems):
        _, _, passed, landed, _ = self._plan(ins, outs, sems)
        for got, cp in zip(landed, passed):
            got.wait_recv()
            cp.start()

    def finish(self, ins, outs, sems):
        mine, first, passed, _, late = self._plan(ins, outs, sems)
        for got in late:
            got.wait_recv()
        for cp in first + passed:
            cp.wait_send()
        mine.wait()


class _ScatterJob:
    def __init__(self, tag, grads):
        self.tag, self.ins = tag, list(grads)
        self.row_counts = [g.shape[1] for g in grads]
        self.outs = [jax.ShapeDtypeStruct((NDEV, sum(self.row_counts), grads[0].shape[2]), grads[0].dtype)]
        n = len(grads)
        self.scratch = [pltpu.SemaphoreType.DMA((n, 7)), pltpu.SemaphoreType.DMA((n, 7)), pltpu.SemaphoreType.DMA((n,))]
        self.results = None

    def _plan(self, ins, outs, sems):
        (out_ref,), (send_sems, recv_sems, local_sems) = outs, sems
        x, y, c = _place()
        remote, local, off = [], [], 0
        for i, (g_ref, rows) in enumerate(zip(ins, self.row_counts)):
            span = pl.ds(off, rows)
            for k, (a, b, d) in enumerate(FLIPS):
                px, py, pc = x ^ a, y ^ b, c ^ d
                remote.append(pltpu.make_async_remote_copy(
                    src_ref=g_ref.at[4 * px + 2 * py + pc], dst_ref=out_ref.at[k, span, :], send_sem=send_sems.at[i, k],
                    recv_sem=recv_sems.at[i, k], device_id=(px, py, pc), device_id_type=MESH))
            local.append(pltpu.make_async_copy(g_ref.at[4 * x + 2 * y + c], out_ref.at[7, span, :], local_sems.at[i]))
            off += rows
        return remote, local

    def start(self, ins, outs, sems):
        remote, local = self._plan(ins, outs, sems)
        for cp in remote + local:
            cp.start()

    def middle(self, ins, outs, sems):
        pass

    def finish(self, ins, outs, sems):
        remote, local = self._plan(ins, outs, sems)
        for cp in remote:
            cp.wait_recv()
        for cp in remote:
            cp.wait_send()
        for cp in local:
            cp.wait()


class _RowsGatherJob:
    def __init__(self, tag, rows):
        self.tag, self.ins = tag, [rows]
        self.outs = [jax.ShapeDtypeStruct((NDEV,) + rows.shape, rows.dtype)]
        self.scratch = [pltpu.SemaphoreType.DMA((7,)), pltpu.SemaphoreType.DMA((7,)), pltpu.SemaphoreType.DMA]
        self.results = None

    def _plan(self, ins, outs, sems):
        (x_ref,), (out_ref,), (send_sems, recv_sems, local_sem) = ins, outs, sems
        x, y, c = _place()
        me = 4 * x + 2 * y + c
        mine = pltpu.make_async_copy(x_ref, out_ref.at[me], local_sem)
        sends, lands = [], []
        for k, (a, b, d) in enumerate(FLIPS):
            px, py, pc = x ^ a, y ^ b, c ^ d
            for dst, keep in ((out_ref.at[me], sends), (out_ref.at[4 * px + 2 * py + pc], lands)):
                keep.append(pltpu.make_async_remote_copy(
                    src_ref=x_ref, dst_ref=dst, send_sem=send_sems.at[k], recv_sem=recv_sems.at[k],
                    device_id=(px, py, pc), device_id_type=MESH))
        return mine, sends, lands

    def start(self, ins, outs, sems):
        mine, sends, _ = self._plan(ins, outs, sems)
        mine.start()
        for cp in sends:
            cp.start()

    def middle(self, ins, outs, sems):
        pass

    def finish(self, ins, outs, sems):
        mine, sends, lands = self._plan(ins, outs, sems)
        for cp in lands:
            cp.wait_recv()
        for cp in sends:
            cp.wait_send()
        mine.wait()


def _fold_rows(parts):
    D = parts[0].shape[1]
    counts = [a.shape[0] // 8 for a in parts]
    total = -(-sum(counts) // 8) * 8

    def body(*refs):
        out = refs[-1]
        out[...] = jnp.zeros_like(out)
        row = 0
        for ref, k in zip(refs[:-1], counts):
            for j in range(k):
                out[pl.ds(row + j, 1), :] = jnp.sum(ref[pl.ds(8 * j, 8), :], axis=0, keepdims=True)
            row += k

    vm = pl.BlockSpec(memory_space=pltpu.VMEM)
    return pl.pallas_call(body, name="fold_rows", out_shape=jax.ShapeDtypeStruct((total, D), F32),
                          in_specs=[vm] * len(parts), out_specs=vm)(*parts)


def _sum_slots(slots):
    n, S, D = slots.shape

    def body(s_ref, o_ref):
        acc = s_ref[0]
        for j in range(1, n):
            acc = acc + s_ref[j]
        o_ref[...] = acc

    vm = pl.BlockSpec(memory_space=pltpu.VMEM)
    return pl.pallas_call(body, name="sum_slots", out_shape=jax.ShapeDtypeStruct((S, D), slots.dtype),
                          in_specs=[vm], out_specs=vm)(slots)


def _launch(name, body, grid, args, in_specs, out_specs, out_shape, scratch, jobs=()):
    grid = (grid,) if isinstance(grid, int) else tuple(grid)
    steps = grid[0] * (grid[1] if len(grid) == 2 else 1)
    n_in, n_out, n_sc = len(args), len(out_shape), len(scratch)
    j_in = [a for jb in jobs for a in jb.ins]
    j_out = [o for jb in jobs for o in jb.outs]
    j_sc = [s for jb in jobs for s in jb.scratch]
    mid = (17 * steps) // 20

    def wrapped(*refs):
        c_in, refs = refs[:n_in], refs[n_in:]
        m_in, refs = refs[:len(j_in)], refs[len(j_in):]
        c_out, refs = refs[:n_out], refs[n_out:]
        m_out, refs = refs[:len(j_out)], refs[len(j_out):]
        c_sc, m_sc = refs[:n_sc], refs[n_sc:]
        bound, a, b, c = [], 0, 0, 0
        for jb in jobs:
            bound.append((jb, m_in[a:a + len(jb.ins)], m_out[b:b + len(jb.outs)], m_sc[c:c + len(jb.scratch)]))
            a, b, c = a + len(jb.ins), b + len(jb.outs), c + len(jb.scratch)
        i = pl.program_id(0) if len(grid) == 1 else pl.program_id(0) * grid[1] + pl.program_id(1)

        def phase(step, which):
            if jobs:
                @pl.when(i == step)
                def _():
                    for jb, ins, outs, sems in bound:
                        getattr(jb, which)(ins, outs, sems)

        phase(0, "start")
        if body is not None:
            body(*c_in, *c_out, *c_sc)
        phase(mid, "middle")
        phase(steps - 1, "finish")

    outs = pl.pallas_call(
        wrapped, name=name + "".join("_" + jb.tag for jb in jobs), grid=grid,
        in_specs=list(in_specs) + [ANY] * len(j_in), out_specs=list(out_specs) + [ANY] * len(j_out),
        out_shape=list(out_shape) + j_out, scratch_shapes=list(scratch) + j_sc, compiler_params=_params(len(grid)),
    )(*args, *j_in)
    pos = n_out
    for jb in jobs:
        jb.results = list(outs[pos:pos + len(jb.outs)])
        pos += len(jb.outs)
    return list(outs[:n_out])


def _run_jobs(name, jobs):
    _launch(name, None, 1, [], [], [], [], [], jobs)


def _sum_parts(got, name):
    n, rows, D = got.shape
    tr = _tile(rows, 512)
    order = [n - 1] + list(range(n - 1))

    def body(*refs):
        g = refs[0][...].astype(F32)
        for pr in refs[1:n]:
            g = g + pr[...].astype(F32)
        refs[-1][...] = g

    specs = [pl.BlockSpec((None, tr, D), functools.partial(lambda k, i: (k, i, 0), k)) for k in order]
    return pl.pallas_call(
        body, name=name, grid=(rows // tr,), in_specs=specs, out_specs=_rows(tr, D),
        out_shape=jax.ShapeDtypeStruct((rows, D), F32), compiler_params=_params(),
    )(*[got] * n)


def _adamw(name, entries, by_layer):
    n = len(entries)
    L = entries[0][0].shape[0]

    def body(*refs):
        ins, outs = refs[:4 * n], refs[4 * n:]
        for e in range(n):
            w_ref, g_ref, m_ref, v_ref = ins[4 * e:4 * e + 4]
            d_out, m_out, v_out = outs[3 * e:3 * e + 3]
            g = g_ref[...]
            m_new = ADAM_B1 * m_ref[...] + (1.0 - ADAM_B1) * g
            v_new = ADAM_B2 * v_ref[...] + (1.0 - ADAM_B2) * (g * g)
            m_hat = m_new / (1.0 - ADAM_B1 ** ADAM_STEP)
            v_hat = v_new / (1.0 - ADAM_B2 ** ADAM_STEP)
            d_out[...] = -ADAM_LR * (m_hat / (jnp.sqrt(v_hat) + ADAM_EPS) + ADAM_WD * w_ref[...])
            m_out[...] = m_new
            v_out[...] = v_new

    def spec(a):
        rest = (0,) * (a.ndim - 1)
        if by_layer:
            return pl.BlockSpec((None,) + a.shape[1:], lambda l: (l,) + rest)
        return pl.BlockSpec(a.shape, lambda l: (0,) + rest)

    flat = [a for e in entries for a in e]
    outs = pl.pallas_call(
        body, name=name, grid=(L if by_layer else 1,),
        in_specs=[spec(a) for a in flat], out_specs=[spec(e[0]) for e in entries for _ in range(3)],
        out_shape=[jax.ShapeDtypeStruct(e[0].shape, F32) for e in entries for _ in range(3)],
        compiler_params=_params(),
    )(*flat)
    return [tuple(outs[3 * e:3 * e + 3]) for e in range(n)]


def _tn_matmul(xa, ya, name, tmm, y_follows=False, jobs=()):
    T, M = xa.shape
    tn = tmm if y_follows else ya.shape[1]
    tt = _tile(T, 1024)
    nb = M // tmm

    def body(x_ref, y_ref, o_ref, acc):
        k = pl.program_id(1)

        @pl.when(k == 0)
        def _():
            acc[...] = jnp.zeros_like(acc)

        acc[...] += _tn(x_ref[...], y_ref[...])

        @pl.when(k == pl.num_programs(1) - 1)
        def _():
            o_ref[...] = acc[...].astype(o_ref.dtype)

    y_map = (lambda b, k: (k, b)) if y_follows else (lambda b, k: (k, 0))
    return _launch(
        name, body, (nb, T // tt), [xa, ya],
        [pl.BlockSpec((tt, tmm), lambda b, k: (k, b)), pl.BlockSpec((tt, tn), y_map)],
        [pl.BlockSpec((tmm, tn), lambda b, k: (b, 0))], [jax.ShapeDtypeStruct((M, tn), BF16)],
        [pltpu.VMEM((tmm, tn), F32)], jobs)[0]


def _ffn_fwd(h, gain, G, lay, jobs=()):
    T, D = h.shape
    FB = lay.FB
    FP = NDEV * FB
    offs = (0, FB, 2 * FB)
    tm = _tile(T, 512)

    def body(h_ref, gain_ref, g_hbm, ho_ref, go_ref, uo_ref, wg, wu, wd, sems):
        @pl.when(pl.program_id(0) == 0)
        def _():
            _fetch(g_hbm, [(offs[0], FB, wg), (offs[1], FB, wu), (offs[2], FB, wd)], sems)

        h = h_ref[...]
        xh, _ = _rms(h)
        n = (xh * gain_ref[...]).astype(BF16)
        acc = jnp.zeros((tm, D), F32)
        for start, width in _hidden_chunks(FP):
            sl = pl.ds(start, width)
            g = _nt(n, wg[sl, :])
            u = _nt(n, wu[sl, :])
            go_ref[:, sl] = g.astype(BF16)
            uo_ref[:, sl] = u.astype(BF16)
            a = g * _sig(g) * u
            acc = acc + _nn(a.astype(BF16), wd[sl, :])
        ho_ref[...] = h + 0.5 * acc

    return _launch(
        "ffn_fwd", body, T // tm, [h, gain, G],
        [_rows(tm, D), _const((1, D)), ANY], [_rows(tm, D), _rows(tm, FP), _rows(tm, FP)],
        [jax.ShapeDtypeStruct((T, D), F32), jax.ShapeDtypeStruct((T, FP), BF16), jax.ShapeDtypeStruct((T, FP), BF16)],
        [pltpu.VMEM((FP, D), BF16)] * 3 + [pltpu.SemaphoreType.DMA((3, NDEV))], jobs)


def _ffn_bwd(d, h, gain, ga, ua, G, lay, jobs=()):
    T, D = h.shape
    FB = lay.FB
    FP = NDEV * FB
    offs = (0, FB, 2 * FB)
    tm = _tile(T, 256)

    def body(d_ref, h_ref, gain_ref, ga_ref, ua_ref, g_hbm,
             do_ref, dg_ref, du_ref, a_ref, n_ref, dh_ref, gg_ref, wg, wu, wd, sems):
        @pl.when(pl.program_id(0) == 0)
        def _():
            _fetch(g_hbm, [(offs[0], FB, wg), (offs[1], FB, wu), (offs[2], FB, wd)], sems)
            gg_ref[...] = jnp.zeros_like(gg_ref)

        d = d_ref[...]
        gain_v = gain_ref[...]
        xh, r = _rms(h_ref[...])
        n_ref[...] = (xh * gain_v).astype(BF16)
        dh = (0.5 * d).astype(BF16)
        dh_ref[...] = dh
        dn = jnp.zeros((tm, D), F32)
        for start, width in _hidden_chunks(FP):
            sl = pl.ds(start, width)
            g = ga_ref[:, sl].astype(F32)
            u = ua_ref[:, sl].astype(F32)
            da = _nt(dh, wd[sl, :])
            s = _sig(g)
            silu = g * s
            a_ref[:, sl] = (silu * u).astype(BF16)
            dgv = (da * u * (s * (1.0 + g * (1.0 - s)))).astype(BF16)
            duv = (da * silu).astype(BF16)
            dg_ref[:, sl] = dgv
            du_ref[:, sl] = duv
            dn = dn + _nn(dgv, wg[sl, :]) + _nn(duv, wu[sl, :])
        gg_ref[...] += _colsum8(dn * xh)
        do_ref[...] = d + _rms_bwd(dn, xh, r, gain_v)

    wide = jax.ShapeDtypeStruct((T, FP), BF16)
    return _launch(
        "ffn_bwd", body, T // tm, [d, h, gain, ga, ua, G],
        [_rows(tm, D), _rows(tm, D), _const((1, D)), _rows(tm, FP), _rows(tm, FP), ANY],
        [_rows(tm, D), _rows(tm, FP), _rows(tm, FP), _rows(tm, FP), _rows(tm, D), _rows(tm, D), _const((8, D))],
        [jax.ShapeDtypeStruct((T, D), F32), wide, wide, wide, jax.ShapeDtypeStruct((T, D), BF16),
         jax.ShapeDtypeStruct((T, D), BF16), jax.ShapeDtypeStruct((8, D), F32)],
        [pltpu.VMEM((FP, D), BF16)] * 3 + [pltpu.SemaphoreType.DMA((3, NDEV))], jobs)


def _inproj_fwd(h, gain, G, lay):
    T, D = h.shape
    NS = lay.NS
    NIN, CH = NDEV * NS, 2 * NS
    tm = _tile(T, 512)

    def body(h_ref, gain_ref, g_hbm, z_ref, win, sems):
        @pl.when(pl.program_id(0) == 0)
        def _():
            _fetch(g_hbm, [(lay.off["win"], NS, win)], sems)

        xh, _ = _rms(h_ref[...])
        n = (xh * gain_ref[...]).astype(BF16)
        for j in range(NIN // CH):
            sl = pl.ds(j * CH, CH)
            z_ref[:, sl] = _nt(n, win[sl, :]).astype(BF16)

    return pl.pallas_call(
        body, name="inproj_fwd", grid=(T // tm,),
        in_specs=[_rows(tm, D), _const((1, D)), ANY], out_specs=_rows(tm, NIN),
        out_shape=jax.ShapeDtypeStruct((T, NIN), BF16),
        scratch_shapes=[pltpu.VMEM((NIN, D), BF16), pltpu.SemaphoreType.DMA((1, NDEV))],
        compiler_params=_params(),
    )(h, gain, G)


def _conv_chunks(tm, D):
    rb, lc = min(tm, 64), min(D, 256)
    return [(r0, l0, rb, lc) for r0 in range(0, tm, rb) for l0 in range(0, D, lc)]


SUBLANES = 8


def _preshift(sh, n_rows):
    for r in range(1, SUBLANES):
        sh[r, pl.ds(0, n_rows), :] = sh[0, pl.ds(r, n_rows), :]


def _window(sh, start, rows, lanes):
    r = start % SUBLANES
    return sh[r, pl.ds(start - r, rows), lanes]


def _mixer_fwd(z, h, pool_w, pscale, cw, cb, lng, lnb, G, lay, jobs=()):
    T, D = h.shape
    PG, DS = lay.PG, lay.DS
    tm = _tile(T, 256)
    hb = tm // HALO

    def body(z_ref, zp_ref, h_ref, pw_ref, ps_ref, cw_ref, cb_ref, lg_ref, lb_ref, g_hbm,
             ho_ref, c1_ref, q_ref, cc_ref, pool_ref, ext_p, sh_c, a_s, wco, wout, sems):
        i = pl.program_id(0)

        @pl.when(i == 0)
        def _():
            _fetch(g_hbm, [(lay.off["wco"], DS, wco), (lay.off["wout"], DS, wout)], sems)

        live = jnp.where(i > 0, 1.0, 0.0).astype(F32)
        ext_p[pl.ds(0, HALO), :] = zp_ref[:, pl.ds(0, D)].astype(F32) * live
        ext_p[pl.ds(HALO, tm), :] = z_ref[:, pl.ds(0, D)].astype(F32)
        sh_c[0, pl.ds(0, HALO), :] = (zp_ref[:, pl.ds(D, D)].astype(F32)
                                      * _sig(zp_ref[:, pl.ds(2 * D, D)].astype(F32)) * live)
        sh_c[0, pl.ds(HALO, tm), :] = z_ref[:, pl.ds(D, D)].astype(F32) * _sig(z_ref[:, pl.ds(2 * D, D)].astype(F32))
        _preshift(sh_c, tm + HALO - SUBLANES)

        t = i * tm + lax.broadcasted_iota(jnp.int32, (tm, 1), 0)
        for g, w in enumerate(POOL_WINDOWS):
            sl = pl.ds(g * PG, PG)
            s = ext_p[pl.ds(HALO, tm), sl]
            zc = s
            for j in range(1, w):
                s = s + ext_p[pl.ds(HALO - j, tm), sl]
            inv = 1.0 / jnp.minimum(t + 1, w).astype(F32)
            pooled = (s * inv - zc).astype(BF16)
            pool_ref[:, sl] = pooled
            qv = _nn(pooled, pw_ref[g])
            q_ref[:, sl] = qv.astype(BF16)
            a_s[:, sl] = qv * ps_ref[:, sl]

        for r0, l0, rb, lc in _conv_chunks(tm, D):
            ls = pl.ds(l0, lc)
            acc = jnp.zeros((rb, lc), F32) + cb_ref[:, ls]
            for k in range(CONV_K):
                acc = acc + cw_ref[pl.ds(k, 1), ls] * _window(sh_c, r0 + HALO - (CONV_K - 1) + k, rb, ls)
            c1_ref[pl.ds(r0, rb), ls] = acc

        xhat, _ = _ln(c1_ref[...])
        c2 = xhat * lg_ref[...] + lb_ref[...]
        c3 = (c2 * _sig(c2)).astype(BF16)
        cc = _nn(c3, wco[...])
        cc_ref[...] = cc.astype(BF16)
        gp = z_ref[:, pl.ds(3 * D, D)].astype(F32)
        gc = z_ref[:, pl.ds(4 * D, D)].astype(F32)
        m = (_sig(gp) * a_s[...] + _sig(gc) * cc).astype(BF16)
        ho_ref[...] = h_ref[...] + _nn(m, wout[...])

    act = jax.ShapeDtypeStruct((T, D), BF16)
    return _launch(
        "mixer_fwd", body, T // tm, [z, z, h, pool_w, pscale, cw, cb, lng, lnb, G],
        [_rows(tm, 5 * D), pl.BlockSpec((HALO, 5 * D), lambda i: (jnp.maximum(i * hb - 1, 0), 0)),
         _rows(tm, D), _const((4, PG, PG)), _const((1, D)), _const((CONV_KP, D)), _const((1, D)),
         _const((1, D)), _const((1, D)), ANY],
        [_rows(tm, D)] * 5,
        [jax.ShapeDtypeStruct((T, D), F32), jax.ShapeDtypeStruct((T, D), F32), act, act, act],
        [pltpu.VMEM((HALO + tm, D), F32), pltpu.VMEM((SUBLANES, HALO + tm, D), F32), pltpu.VMEM((tm, D), F32),
         pltpu.VMEM((D, D), BF16), pltpu.VMEM((D, D), BF16), pltpu.SemaphoreType.DMA((2, NDEV))], jobs)


def _mixer_bwd_rows(d, z, c1, qa, cca, pool_w, pscale, lng, lnb, G, lay):
    T, D = d.shape
    PG, DS = lay.PG, lay.DS
    tm = _tile(T, 256)

    def body(d_ref, z_ref, c1_ref, q_ref, cc_ref, pw_ref, ps_ref, lg_ref, lb_ref, g_hbm,
             m_ref, db_ref, dcc_ref, c3_ref, dq_ref, dpool_ref, dc1_ref, dzg_ref, gps_ref, glg_ref, glb_ref, gcb_ref,
             wco, wout, sems):
        @pl.when(pl.program_id(0) == 0)
        def _():
            _fetch(g_hbm, [(lay.off["wco"], DS, wco), (lay.off["wout"], DS, wout)], sems)
            for ref in (gps_ref, glg_ref, glb_ref, gcb_ref):
                ref[...] = jnp.zeros_like(ref)

        db = d_ref[...].astype(BF16)
        db_ref[...] = db
        dm = _nt(db, wout[...])
        q = q_ref[...].astype(F32)
        cc = cc_ref[...].astype(F32)
        ps = ps_ref[...]
        sp = _sig(z_ref[:, pl.ds(3 * D, D)].astype(F32))
        sc = _sig(z_ref[:, pl.ds(4 * D, D)].astype(F32))
        a = q * ps
        m_ref[...] = (sp * a + sc * cc).astype(BF16)
        da = dm * sp
        dzg_ref[:, pl.ds(0, D)] = (dm * a * sp * (1.0 - sp)).astype(BF16)
        dzg_ref[:, pl.ds(D, D)] = (dm * cc * sc * (1.0 - sc)).astype(BF16)
        gps_ref[...] += _colsum8(da * q)
        dq = (da * ps).astype(BF16)
        dq_ref[...] = dq
        for g in range(len(POOL_WINDOWS)):
            sl = pl.ds(g * PG, PG)
            dpool_ref[:, sl] = _nt(dq_ref[:, sl], pw_ref[g])

        dcc = (dm * sc).astype(BF16)
        dcc_ref[...] = dcc
        xhat, rstd = _ln(c1_ref[...])
        lg = lg_ref[...]
        c2 = xhat * lg + lb_ref[...]
        s2 = _sig(c2)
        c3_ref[...] = (c2 * s2).astype(BF16)
        dc2 = _nt(dcc, wco[...]) * (s2 * (1.0 + c2 * (1.0 - s2)))
        glg_ref[...] += _colsum8(dc2 * xhat)
        glb_ref[...] += _colsum8(dc2)
        dxh = dc2 * lg
        dc1 = rstd * (dxh - jnp.mean(dxh, axis=-1, keepdims=True)
                      - xhat * jnp.mean(dxh * xhat, axis=-1, keepdims=True))
        dc1_ref[...] = dc1
        gcb_ref[...] += _colsum8(dc1)

    act = jax.ShapeDtypeStruct((T, D), BF16)
    full = jax.ShapeDtypeStruct((T, D), F32)
    vec = jax.ShapeDtypeStruct((8, D), F32)
    return pl.pallas_call(
        body, name="mixer_bwd_rows", grid=(T // tm,),
        in_specs=[_rows(tm, D), _rows(tm, 5 * D), _rows(tm, D), _rows(tm, D), _rows(tm, D),
                  _const((4, PG, PG)), _const((1, D)), _const((1, D)), _const((1, D)), ANY],
        out_specs=[_rows(tm, D)] * 7 + [_rows(tm, 2 * D)] + [_const((8, D))] * 4,
        out_shape=[act, act, act, act, act, full, full, jax.ShapeDtypeStruct((T, 2 * D), BF16), vec, vec, vec, vec],
        scratch_shapes=[pltpu.VMEM((D, D), BF16), pltpu.VMEM((D, D), BF16), pltpu.SemaphoreType.DMA((2, NDEV))],
        compiler_params=_params(),
    )(d, z, c1, qa, cca, pool_w, pscale, lng, lnb, G)


def _mixer_bwd_time(dc1, dpool, z, cw, lay, jobs=()):
    T, D = dc1.shape
    PG = lay.PG
    tm = _tile(T, 256)
    hb = tm // HALO
    nt = T // tm

    def body(dc_ref, dcn_ref, dp_ref, dpn_ref, z_ref, zp_ref, cw_ref, dz_ref, gcw_ref, sh_d, ext_q, sh_c, dc0_s):
        i = pl.program_id(0)

        @pl.when(i == 0)
        def _():
            gcw_ref[...] = jnp.zeros_like(gcw_ref)

        live_p = jnp.where(i > 0, 1.0, 0.0).astype(F32)
        live_n = jnp.where(i < nt - 1, 1.0, 0.0).astype(F32)
        sh_d[0, pl.ds(0, tm), :] = dc_ref[...]
        sh_d[0, pl.ds(tm, HALO), :] = dcn_ref[...] * live_n
        _preshift(sh_d, tm + HALO - SUBLANES)
        zg = z_ref[:, pl.ds(2 * D, D)].astype(F32)
        za = z_ref[:, pl.ds(D, D)].astype(F32)
        sg = _sig(zg)
        sh_c[0, pl.ds(0, HALO), :] = (zp_ref[:, pl.ds(D, D)].astype(F32)
                                      * _sig(zp_ref[:, pl.ds(2 * D, D)].astype(F32)) * live_p)
        sh_c[0, pl.ds(HALO, tm), :] = za * sg
        _preshift(sh_c, tm + HALO - SUBLANES)

        t = i * tm + lax.broadcasted_iota(jnp.int32, (tm, 1), 0)
        tn = (i + 1) * tm + lax.broadcasted_iota(jnp.int32, (HALO, 1), 0)
        for g, w in enumerate(POOL_WINDOWS):
            sl = pl.ds(g * PG, PG)
            ext_q[pl.ds(0, tm), sl] = dp_ref[:, sl] * (1.0 / jnp.minimum(t + 1, w).astype(F32))
            ext_q[pl.ds(tm, HALO), sl] = dpn_ref[:, sl] * (live_n / jnp.minimum(tn + 1, w).astype(F32))
        for g, w in enumerate(POOL_WINDOWS):
            sl = pl.ds(g * PG, PG)
            s = ext_q[pl.ds(0, tm), sl]
            for j in range(1, w):
                s = s + ext_q[pl.ds(j, tm), sl]
            dz_ref[:, sl] = (s - dp_ref[:, sl]).astype(BF16)

        for r0, l0, rb, lc in _conv_chunks(tm, D):
            ls = pl.ds(l0, lc)
            acc = jnp.zeros((rb, lc), F32)
            for j in range(CONV_K):
                acc = acc + cw_ref[pl.ds(CONV_K - 1 - j, 1), ls] * _window(sh_d, r0 + j, rb, ls)
            dc0_s[pl.ds(r0, rb), ls] = acc
        dc0 = dc0_s[...]
        dz_ref[:, pl.ds(D, D)] = (dc0 * sg).astype(BF16)
        dz_ref[:, pl.ds(2 * D, D)] = (dc0 * za * sg * (1.0 - sg)).astype(BF16)

        lc = min(D, 256)
        for l0 in range(0, D, lc):
            ls = pl.ds(l0, lc)
            dcv = dc_ref[:, ls]
            for k in range(CONV_K):
                gcw_ref[pl.ds(8 * k, 8), ls] += _colsum8(dcv * _window(sh_c, HALO - (CONV_K - 1) + k, tm, ls))

    nxt = lambda i: (jnp.minimum((i + 1) * hb, T // HALO - 1), 0)
    return _launch(
        "mixer_bwd_time", body, nt, [dc1, dc1, dpool, dpool, z, z, cw],
        [_rows(tm, D), pl.BlockSpec((HALO, D), nxt), _rows(tm, D), pl.BlockSpec((HALO, D), nxt),
         _rows(tm, 5 * D), pl.BlockSpec((HALO, 5 * D), lambda i: (jnp.maximum(i * hb - 1, 0), 0)),
         _const((CONV_KP, D))],
        [_rows(tm, 3 * D), _const((CONV_KP * 8, D))],
        [jax.ShapeDtypeStruct((T, 3 * D), BF16), jax.ShapeDtypeStruct((CONV_KP * 8, D), F32)],
        [pltpu.VMEM((SUBLANES, tm + HALO, D), F32), pltpu.VMEM((tm + HALO, D), F32),
         pltpu.VMEM((SUBLANES, HALO + tm, D), F32), pltpu.VMEM((tm, D), F32)], jobs)


def _inproj_bwd(d, h, gain, dzm, dzg, G, lay):
    T, D = h.shape
    NS = lay.NS
    NIN = NDEV * NS
    tm = _tile(T, 512)

    def body(d_ref, h_ref, gain_ref, dzm_ref, dzg_ref, g_hbm, do_ref, u_ref, gg_ref, win, sems):
        @pl.when(pl.program_id(0) == 0)
        def _():
            _fetch(g_hbm, [(lay.off["win"], NS, win)], sems)
            gg_ref[...] = jnp.zeros_like(gg_ref)

        gain_v = gain_ref[...]
        xh, r = _rms(h_ref[...])
        u_ref[...] = (xh * gain_v).astype(BF16)
        dn = jnp.zeros((tm, D), F32)
        for j in range(3):
            dn = dn + _nn(dzm_ref[:, pl.ds(j * D, D)], win[pl.ds(j * D, D), :])
        for j in range(2):
            dn = dn + _nn(dzg_ref[:, pl.ds(j * D, D)], win[pl.ds((3 + j) * D, D), :])
        gg_ref[...] += _colsum8(dn * xh)
        do_ref[...] = d_ref[...] + _rms_bwd(dn, xh, r, gain_v)

    return pl.pallas_call(
        body, name="inproj_bwd", grid=(T // tm,),
        in_specs=[_rows(tm, D), _rows(tm, D), _const((1, D)), _rows(tm, 3 * D), _rows(tm, 2 * D), ANY],
        out_specs=[_rows(tm, D), _rows(tm, D), _const((8, D))],
        out_shape=[jax.ShapeDtypeStruct((T, D), F32), jax.ShapeDtypeStruct((T, D), BF16),
                   jax.ShapeDtypeStruct((8, D), F32)],
        scratch_shapes=[pltpu.VMEM((NIN, D), BF16), pltpu.SemaphoreType.DMA((1, NDEV))],
        compiler_params=_params(),
    )(d, h, gain, dzm, dzg, G)


def _ple_fwd(h, pe, gain, wppt, G, lay):
    T, D = h.shape
    PD, DS = lay.PD, lay.DS
    tm = _tile(T, 512)

    def body(h_ref, p_ref, gain_ref, wpp_ref, g_hbm, ho_ref, gate_ref, wpg, sems):
        @pl.when(pl.program_id(0) == 0)
        def _():
            _fetch(g_hbm, [(lay.off["wpg"], DS, wpg)], sems)

        h = h_ref[...]
        xh, _ = _rms(h)
        n = (xh * gain_ref[...]).astype(BF16)
        gate = _sig(_nn(n, wpg[...]))
        gate_ref[...] = gate.astype(BF16)
        e = _nt(p_ref[...].astype(BF16), wpp_ref[...])
        ho_ref[...] = h + gate * e

    return pl.pallas_call(
        body, name="ple_fwd", grid=(T // tm,),
        in_specs=[_rows(tm, D), _rows(tm, PD), _const((1, D)), _const((D, PD)), ANY],
        out_specs=[_rows(tm, D), _rows(tm, D)],
        out_shape=[jax.ShapeDtypeStruct((T, D), F32), jax.ShapeDtypeStruct((T, D), BF16)],
        scratch_shapes=[pltpu.VMEM((D, D), BF16), pltpu.SemaphoreType.DMA((1, NDEV))],
        compiler_params=_params(),
    )(h, pe, gain, wppt, G)


def _ple_bwd(d, h, pe, gate_a, gain, wppt, G, lay):
    T, D = h.shape
    PD, DS = lay.PD, lay.DS
    tm = _tile(T, 512)

    def body(d_ref, h_ref, p_ref, gate_ref, gain_ref, wpp_ref, g_hbm,
             do_ref, dpre_ref, de_ref, n_ref, pb_ref, gg_ref, wpg, sems):
        @pl.when(pl.program_id(0) == 0)
        def _():
            _fetch(g_hbm, [(lay.off["wpg"], DS, wpg)], sems)
            gg_ref[...] = jnp.zeros_like(gg_ref)

        d = d_ref[...]
        gain_v = gain_ref[...]
        xh, r = _rms(h_ref[...])
        n_ref[...] = (xh * gain_v).astype(BF16)
        pb = p_ref[...].astype(BF16)
        pb_ref[...] = pb
        e = _nt(pb, wpp_ref[...])
        gate = gate_ref[...].astype(F32)
        de_ref[...] = (d * gate).astype(BF16)
        dpre = (d * e * gate * (1.0 - gate)).astype(BF16)
        dpre_ref[...] = dpre
        dn = _nt(dpre, wpg[...])
        gg_ref[...] += _colsum8(dn * xh)
        do_ref[...] = d + _rms_bwd(dn, xh, r, gain_v)

    act = jax.ShapeDtypeStruct((T, D), BF16)
    return pl.pallas_call(
        body, name="ple_bwd", grid=(T // tm,),
        in_specs=[_rows(tm, D), _rows(tm, D), _rows(tm, PD), _rows(tm, D), _const((1, D)), _const((D, PD)), ANY],
        out_specs=[_rows(tm, D), _rows(tm, D), _rows(tm, D), _rows(tm, D), _rows(tm, PD), _const((8, D))],
        out_shape=[jax.ShapeDtypeStruct((T, D), F32), act, act, act, jax.ShapeDtypeStruct((T, PD), BF16),
                   jax.ShapeDtypeStruct((8, D), F32)],
        scratch_shapes=[pltpu.VMEM((D, D), BF16), pltpu.SemaphoreType.DMA((1, NDEV))],
        compiler_params=_params(),
    )(d, h, pe, gate_a, gain, wppt, G)


def _head(h, target, gain):
    T, D = h.shape
    tm = _tile(T, 512)

    def body(h_ref, t_ref, gain_ref, do_ref, loss_ref, gg_ref):
        @pl.when(pl.program_id(0) == 0)
        def _():
            loss_ref[...] = jnp.zeros_like(loss_ref)
            gg_ref[...] = jnp.zeros_like(gg_ref)

        gain_v = gain_ref[...]
        xh, r = _rms(h_ref[...])
        err = xh * gain_v - t_ref[...]
        loss_ref[...] += _colsum8(err * err)
        dy = err * (1.0 / D)
        gg_ref[...] += _colsum8(dy * xh)
        do_ref[...] = _rms_bwd(dy, xh, r, gain_v)

    vec = jax.ShapeDtypeStruct((8, D), F32)
    return pl.pallas_call(
        body, name="head", grid=(T // tm,),
        in_specs=[_rows(tm, D), _rows(tm, D), _const((1, D))],
        out_specs=[_rows(tm, D), _const((8, D)), _const((8, D))],
        out_shape=[jax.ShapeDtypeStruct((T, D), F32), vec, vec], compiler_params=_params(),
    )(h, target, gain)


_BIG = ["ffn1_w_gate", "ffn1_w_up", "ffn1_w_down", "w_in", "pool_w", "conv_w_out", "w_out", "ffn2_w_gate",
        "ffn2_w_up", "ffn2_w_down", "ple_w_gate", "ple_w_proj"]
_VECS = ["ffn1_norm", "mix_norm", "pool_scale", "conv_dw_b", "conv_ln_g", "conv_ln_b", "ffn2_norm", "ple_norm"]
_WEIGHTS = ["ffn1_norm", "ffn1_w_gate", "ffn1_w_up", "ffn1_w_down", "mix_norm", "w_in", "pool_w", "pool_scale",
            "conv_dw_w", "conv_dw_b", "conv_ln_g", "conv_ln_b", "conv_w_out", "w_out", "ffn2_norm", "ffn2_w_gate",
            "ffn2_w_up", "ffn2_w_down", "ple_norm", "ple_w_gate", "ple_w_proj", "final_norm"]


def _step(x, p, loss_target, w, mom, var):
    T, D = x.shape[1], x.shape[2]
    L = p.shape[0]
    FS, NS = w["ffn1_w_gate"].shape[2], w["w_in"].shape[2]
    PD = p.shape[3]
    PGS, PG = w["pool_w"].shape[2], w["pool_w"].shape[3]
    CS = w["conv_dw_w"].shape[2]
    lay = _Layout(D, FS, NS, PD, PG, PGS)
    FB = lay.FB
    ax, ay, ac = _place()
    me = 4 * ax + 2 * ay + ac

    assert L == 2, "the exchange schedule below is written for two layers"
    gather = {(l, pc): _GatherJob(f"ag{pc}", _pack_piece(w, l, lay, pc, BF16)) for l in range(L) for pc in PIECES}
    fwd_jobs = {("ffn1", 0): [gather[0, "C"], gather[0, "D"]], ("mixer", 0): [gather[0, "B"]],
                ("ffn2", 0): [gather[1, "A"], gather[1, "D"]], ("ffn1", 1): [gather[1, "C"]],
                ("mixer", 1): [gather[1, "B"]]}
    cw_mine = jnp.pad(w["conv_dw_w"], ((0, 0), (0, CONV_KP - CONV_K), (0, 0)))
    taps_job = _RowsGatherJob("agW", cw_mine.reshape(L * CONV_KP * CS // D, D))
    _run_jobs("gather_first", [gather[0, "A"], taps_job])
    cw_full = taps_job.results[0].reshape(NDEV, L, CONV_KP, CS).transpose(1, 2, 0, 3).reshape(L, CONV_KP, D)

    def gathered(l, pc):
        return gather[l, pc].results[0]

    def small_mats(l):
        G = gathered(l, "D")
        wppt = G[:, lay.off["wpp"]:lay.off["wpp"] + lay.rows["wpp"]].reshape(D, PD)
        pw = G[:, lay.off["pool"]:lay.off["pool"] + lay.rows["pool"]].reshape(NDEV, 4, PGS, PG)
        return wppt, pw.transpose(1, 0, 2, 3).reshape(4, PG, PG)

    def vec(name, l):
        return w[name][l].reshape(1, D)

    h = x[0]
    saved = []
    for l in range(L):
        s = {"h0": h}
        h, s["g1"], s["u1"] = _ffn_fwd(h, vec("ffn1_norm", l), gathered(l, "A"), lay, fwd_jobs.get(("ffn1", l), ()))
        wppt, pw = small_mats(l)
        s["wppt"], s["pw"], s["h1"] = wppt, pw, h
        s["z"] = _inproj_fwd(h, vec("mix_norm", l), gathered(l, "C"), lay)
        h, s["c1"], s["q"], s["cc"], s["pooled"] = _mixer_fwd(
            s["z"], h, pw, vec("pool_scale", l), cw_full[l], vec("conv_dw_b", l), vec("conv_ln_g", l),
            vec("conv_ln_b", l), gathered(l, "D"), lay, fwd_jobs.get(("mixer", l), ()))
        s["h2"] = h
        h, s["g2"], s["u2"] = _ffn_fwd(h, vec("ffn2_norm", l), gathered(l, "B"), lay, fwd_jobs.get(("ffn2", l), ()))
        s["h3"] = h
        h, s["gate"] = _ple_fwd(h, p[l, 0], vec("ple_norm", l), wppt, gathered(l, "D"), lay)
        saved.append(s)

    d, loss_part, g_final = _head(h, loss_target[0], w["final_norm"].reshape(1, D))

    scatter = {}
    small_parts = [None] * L

    def send(l, pc, blocks):
        scatter[l, pc] = _ScatterJob(f"rs{pc}", [blocks[n].reshape(NDEV, lay.rows[n], D) for n, _ in lay.pieces[pc]])
        return scatter[l, pc]

    def small_rows():
        return _fold_rows([v for l in range(L) for v in small_parts[l]] + [g_final, loss_part])

    held = []
    for l in reversed(range(L)):
        s = saved[l]
        wppt, pw = s["wppt"], s["pw"]
        g = {}
        d, dpre, de, n_ple, pb, g_ple = _ple_bwd(d, s["h3"], p[l, 0], s["gate"], vec("ple_norm", l), wppt,
                                                 gathered(l, "D"), lay)
        g["wpg"] = _tn_matmul(n_ple, dpre, "tn_sq", _tile(D, 512))
        g["wpp"] = _tn_matmul(de, pb, "tn_proj", _tile(D, 512))

        d, dg2, du2, a2, n2, dh2, g_n2 = _ffn_bwd(d, s["h2"], vec("ffn2_norm", l), s["g2"], s["u2"],
                                                  gathered(l, "B"), lay, held)
        held = []
        g["g2"] = _tn_matmul(dg2, n2, "tn_ffn", NDEV * FB // 2)
        g["u2"] = _tn_matmul(du2, n2, "tn_ffn", NDEV * FB // 2)
        g["d2"] = _tn_matmul(a2, dh2, "tn_ffn", NDEV * FB // 2)
        held.append(send(l, "B", g))

        (m_b, d_b, dcc, c3, dq, dpool, dc1, dzg, g_ps, g_lg, g_lb, g_cb) = _mixer_bwd_rows(
            d, s["z"], s["c1"], s["q"], s["cc"], pw, vec("pool_scale", l), vec("conv_ln_g", l), vec("conv_ln_b", l),
            gathered(l, "D"), lay)
        g["wout"] = _tn_matmul(m_b, d_b, "tn_sq", _tile(D, 512))
        g["wco"] = _tn_matmul(c3, dcc, "tn_sq", _tile(D, 512))
        g_pool = _tn_matmul(s["pooled"], dq, "tn_pool", PG, y_follows=True)
        g["pool"] = g_pool.reshape(4, NDEV, PGS, PG).transpose(1, 0, 2, 3)
        held.append(send(l, "D", g))
        dzm, g_cw = _mixer_bwd_time(dc1, dpool, s["z"], cw_full[l], lay, held)
        held = []
        d, u_b, g_mix = _inproj_bwd(d, s["h1"], vec("mix_norm", l), dzm, dzg, gathered(l, "C"), lay)
        g["win"] = jnp.concatenate([_tn_matmul(dzm, u_b, "tn_in3", _tile(D, 512)),
                                    _tn_matmul(dzg, u_b, "tn_in2", _tile(D, 512))], axis=0)
        held.append(send(l, "C", g))

        d, dg1, du1, a1, n1, dh1, g_n1 = _ffn_bwd(d, s["h0"], vec("ffn1_norm", l), s["g1"], s["u1"],
                                                  gathered(l, "A"), lay, held)
        held = []
        small_parts[l] = [g_n1, g_mix, g_ps, g_cb, g_lg, g_lb, g_n2, g_ple, g_cw]
        if l > 0:
            g["g1"] = _tn_matmul(dg1, n1, "tn_ffn", NDEV * FB // 2)
            g["u1"] = _tn_matmul(du1, n1, "tn_ffn", NDEV * FB // 2)
            g["d1"] = _tn_matmul(a1, dh1, "tn_ffn", NDEV * FB // 2)
            held.append(send(l, "A", g))
        else:
            rows_job = _RowsGatherJob("agS", small_rows())
            g["g1"] = _tn_matmul(dg1, n1, "tn_ffn", NDEV * FB // 2, jobs=[rows_job])
            g["u1"] = _tn_matmul(du1, n1, "tn_ffn", NDEV * FB // 2, jobs=[send(l, "A1", g)])
            g["d1"] = _tn_matmul(a1, dh1, "tn_ffn", NDEV * FB // 2, jobs=[send(l, "A2", g)])
            held.append(send(l, "A3", g))
    _run_jobs("scatter_last", held)
    grad_x = d[None]

    per = [{} for _ in range(L)]
    for (l, pc), job in scatter.items():
        per[l].update(_unpack_piece(_sum_parts(job.results[0], "sum_" + pc), lay, pc))
    grads = {n: jnp.stack([per[l][n] for l in range(L)]) for n in _BIG}

    small_sum = _sum_slots(rows_job.results[0])
    per_layer = len(_VECS) + CONV_KP
    for k, n in enumerate(_VECS):
        grads[n] = jnp.stack([small_sum[l * per_layer + k] for l in range(L)])
    g_cw_full = jnp.stack([small_sum[l * per_layer + len(_VECS):l * per_layer + len(_VECS) + CONV_K]
                           for l in range(L)])
    grads["conv_dw_w"] = lax.dynamic_slice_in_dim(g_cw_full, me * CS, CS, axis=2)
    grads["final_norm"] = small_sum[L * per_layer]
    loss = (0.5 / D) * jnp.sum(small_sum[L * per_layer + 1])

    def as2(a):
        return a.reshape(1, -1) if a.ndim == 1 else a

    deltas, new_m, new_v = {}, {}, {}

    def update(name, names, by_layer):
        res = _adamw(name, [(as2(w[n]), as2(grads[n]), as2(mom[n]), as2(var[n])) for n in names], by_layer)
        for n, (delta, m_new, v_new) in zip(names, res):
            deltas[n], new_m[n], new_v[n] = (a.reshape(w[n].shape) for a in (delta, m_new, v_new))

    for n in ("ffn1_w_gate", "ffn1_w_up", "ffn1_w_down", "ffn2_w_gate", "ffn2_w_up", "ffn2_w_down", "w_in"):
        update("adamw_" + n, [n], True)
    update("adamw_mid", ["conv_w_out", "w_out", "ple_w_gate", "ple_w_proj", "pool_w"], True)
    update("adamw_small", _VECS + ["conv_dw_w", "final_norm"], False)
    return loss, grad_x, (grads, deltas, new_m, new_v)


def kernel(x, p, ffn1_norm, ffn1_w_gate, ffn1_w_up, ffn1_w_down, mix_norm, w_in, pool_w, pool_scale, conv_dw_w, conv_dw_b, conv_ln_g, conv_ln_b, conv_w_out, w_out, ffn2_norm, ffn2_w_gate, ffn2_w_up, ffn2_w_down, ple_norm, ple_w_gate, ple_w_proj, final_norm, loss_target, m_ffn1_norm, m_ffn1_w_gate, m_ffn1_w_up, m_ffn1_w_down, m_mix_norm, m_w_in, m_pool_w, m_pool_scale, m_conv_dw_w, m_conv_dw_b, m_conv_ln_g, m_conv_ln_b, m_conv_w_out, m_w_out, m_ffn2_norm, m_ffn2_w_gate, m_ffn2_w_up, m_ffn2_w_down, m_ple_norm, m_ple_w_gate, m_ple_w_proj, m_final_norm, v_ffn1_norm, v_ffn1_w_gate, v_ffn1_w_up, v_ffn1_w_down, v_mix_norm, v_w_in, v_pool_w, v_pool_scale, v_conv_dw_w, v_conv_dw_b, v_conv_ln_g, v_conv_ln_b, v_conv_w_out, v_w_out, v_ffn2_norm, v_ffn2_w_gate, v_ffn2_w_up, v_ffn2_w_down, v_ple_norm, v_ple_w_gate, v_ple_w_proj, v_final_norm):
    given = dict(locals())
    w = {n: given[n] for n in _WEIGHTS}
    mom = {n: given["m_" + n] for n in _WEIGHTS}
    var = {n: given["v_" + n] for n in _WEIGHTS}
    loss, grad_x, (grads, deltas, new_m, new_v) = _step(x, p, loss_target, w, mom, var)
    out = [loss, grad_x]
    for res in (grads, deltas, new_m, new_v):
        out += [res[n] for n in _WEIGHTS]
    return tuple(out)
```

```python
import functools

import jax
import jax.numpy as jnp
from jax import lax
from jax.experimental import pallas as pl
from jax.experimental.pallas import tpu as pltpu

F32, BF16 = jnp.float32, jnp.bfloat16
NDEV = 8
MESH = pl.DeviceIdType.MESH
HALO = 32
POOL_WINDOWS = (2, 4, 8, 16)
CONV_K = 31
CONV_KP = 32
RMS_EPS, LN_EPS = 1e-6, 1e-5
ADAM_LR, ADAM_B1, ADAM_B2, ADAM_EPS, ADAM_WD, ADAM_STEP = 0.001, 0.9, 0.999, 1e-08, 0.01, 10
LANE = 128
VMEM_LIMIT = 56 * 1024 * 1024
ANY = pl.BlockSpec(memory_space=pl.ANY)


def _nn(a, b):
    return jnp.dot(a, b, preferred_element_type=F32)


def _nt(a, b):
    return lax.dot_general(a, b, (((1,), (1,)), ((), ())), preferred_element_type=F32)


def _tn(a, b):
    return lax.dot_general(a, b, (((0,), (0,)), ((), ())), preferred_element_type=F32)


def _colsum8(v):
    return jnp.sum(v.reshape(v.shape[0] // 8, 8, v.shape[1]), axis=0)


def _sig(v):
    return jax.nn.sigmoid(v)


def _rms(h):
    r = lax.rsqrt(jnp.mean(h * h, axis=-1, keepdims=True) + RMS_EPS)
    return h * r, r


def _rms_bwd(dn, xh, r, gain):
    dxh = dn * gain
    return r * (dxh - xh * jnp.mean(dxh * xh, axis=-1, keepdims=True))


def _ln(c1):
    mu = jnp.mean(c1, axis=-1, keepdims=True)
    cen = c1 - mu
    rstd = lax.rsqrt(jnp.mean(cen * cen, axis=-1, keepdims=True) + LN_EPS)
    return cen * rstd, rstd


def _rows(tm, c):
    return pl.BlockSpec((tm, c), lambda i: (i, 0))


def _const(shape):
    return pl.BlockSpec(shape, lambda i: (0,) * len(shape))


def _params(n_grid=1):
    return pltpu.CompilerParams(dimension_semantics=("arbitrary",) * n_grid, vmem_limit_bytes=VMEM_LIMIT)


BF16_ROWS = 16
MXU_CHUNK = 768


def _tile(n, want):
    if n <= want:
        return n
    return max(t for t in range(BF16_ROWS, want + 1, BF16_ROWS) if n % t == 0)


def _hidden_chunks(width):
    return [(o, min(MXU_CHUNK, width - o)) for o in range(0, width, MXU_CHUNK)]


def _staggered(chunks, project, activate, contract, acc):
    spans = [pl.ds(start, width) for start, width in chunks]
    ahead = project(spans[0])
    for c, span in enumerate(spans):
        projected = ahead
        if c + 1 < len(spans):
            ahead = project(spans[c + 1])
        acc = contract(span, activate(span, projected), acc)
    return acc


def _fetch(g_hbm, specs, sems):
    cps = []
    for wi, (off, rows, dst) in enumerate(specs):
        for dev in range(NDEV):
            cps.append(pltpu.make_async_copy(g_hbm.at[dev, pl.ds(off, rows), :],
                                             dst.at[pl.ds(dev * rows, rows), :], sems.at[wi, dev]))
    for cp in cps:
        cp.start()
    for cp in cps:
        cp.wait()


PIECES = ("A", "B", "C", "D")


class _Layout:
    def __init__(self, D, FS, NS, PD, PG, PGS):
        self.D, self.FS, self.NS, self.PD, self.PG, self.PGS = D, FS, NS, PD, PG, PGS
        assert FS % BF16_ROWS == 0 and (NDEV * FS) % (2 * LANE) == 0, "FFN shard rows must tile as bf16 row blocks"
        self.FB = FS
        self.DS = D // NDEV
        self.pieces = {
            "A": [("g1", self.FB), ("u1", self.FB), ("d1", self.FB)],
            "B": [("g2", self.FB), ("u2", self.FB), ("d2", self.FB)],
            "C": [("win", NS)],
            "D": [("wco", self.DS), ("wout", self.DS), ("wpg", self.DS), ("wpp", self.DS * PD // D),
                  ("pool", 4 * PGS * PG // D)],
            "A1": [("g1", self.FB)], "A2": [("u1", self.FB)], "A3": [("d1", self.FB)]}
        self.off, self.rows = {}, {}
        for pc in PIECES:
            o = 0
            for n, r in self.pieces[pc]:
                self.off[n], self.rows[n] = o, r
                o += r


def _pack_piece(w, l, lay, pc, dtype):
    D = lay.D
    make = {"g1": lambda: w["ffn1_w_gate"][l].T, "u1": lambda: w["ffn1_w_up"][l].T,
            "d1": lambda: w["ffn1_w_down"][l], "g2": lambda: w["ffn2_w_gate"][l].T,
            "u2": lambda: w["ffn2_w_up"][l].T, "d2": lambda: w["ffn2_w_down"][l],
            "win": lambda: w["w_in"][l].T, "wco": lambda: w["conv_w_out"][l], "wout": lambda: w["w_out"][l],
            "wpg": lambda: w["ple_w_gate"][l], "wpp": lambda: w["ple_w_proj"][l].T.reshape(-1, D),
            "pool": lambda: w["pool_w"][l].reshape(-1, D)}
    return jnp.concatenate([make[n]() for n, _ in lay.pieces[pc]], axis=0).astype(dtype)


def _unpack_piece(slab, lay, pc):
    PD, PG, PGS, DS = lay.PD, lay.PG, lay.PGS, lay.DS
    undo = {"g1": ("ffn1_w_gate", lambda a: a.T), "u1": ("ffn1_w_up", lambda a: a.T),
            "d1": ("ffn1_w_down", lambda a: a), "g2": ("ffn2_w_gate", lambda a: a.T),
            "u2": ("ffn2_w_up", lambda a: a.T), "d2": ("ffn2_w_down", lambda a: a),
            "win": ("w_in", lambda a: a.T), "wco": ("conv_w_out", lambda a: a), "wout": ("w_out", lambda a: a),
            "wpg": ("ple_w_gate", lambda a: a), "wpp": ("ple_w_proj", lambda a: a.reshape(DS, PD).T),
            "pool": ("pool_w", lambda a: a.reshape(4, PGS, PG))}
    out, o = {}, 0
    for n, r in lay.pieces[pc]:
        name, fn = undo[n]
        out[name] = fn(slab[o:o + r])
        o += r
    return out


def _place():
    return lax.axis_index("x"), lax.axis_index("y"), lax.axis_index("c")


FLIPS = [(a, b, d) for a in (0, 1) for b in (0, 1) for d in (0, 1)][1:]


class _GatherJob:
    def __init__(self, tag, slab):
        self.tag, self.ins = tag, [slab]
        self.outs = [jax.ShapeDtypeStruct((NDEV,) + slab.shape, slab.dtype)]
        self.scratch = [pltpu.SemaphoreType.DMA((7,)), pltpu.SemaphoreType.DMA((7,)), pltpu.SemaphoreType.DMA]
        self.results = None

    def _plan(self, ins, outs, sems):
        (x_ref,), (out_ref,), (send_sems, recv_sems, local_sem) = ins, outs, sems
        x, y, c = _place()
        me, sibling = (x, y, c), (x, y, 1 - c)
        chips = [(1 - x, y), (x, 1 - y), (1 - x, 1 - y)]

        def rows(px, py, pc):
            return out_ref.at[4 * px + 2 * py + pc]

        def copy(k, block, to, src=None):
            return pltpu.make_async_remote_copy(
                src_ref=rows(*block) if src is None else src, dst_ref=rows(*block),
                send_sem=send_sems.at[k], recv_sem=recv_sems.at[k], device_id=to, device_id_type=MESH)

        mine = pltpu.make_async_copy(x_ref, rows(*me), local_sem)
        first = [copy(0, me, sibling, src=x_ref)]
        first += [copy(1 + j, me, (*chip, c), src=x_ref) for j, chip in enumerate(chips)]
        passed = [copy(4 + j, (*chip, c), sibling) for j, chip in enumerate(chips)]
        landed = [copy(1 + j, (*chip, c), me) for j, chip in enumerate(chips)]
        late = [copy(0, sibling, me)] + [copy(4 + j, (*chip, 1 - c), me) for j, chip in enumerate(chips)]
        return mine, first, passed, landed, late

    def start(self, ins, outs, sems):
        mine, first, _, _, _ = self._plan(ins, outs, sems)
        mine.start()
        for cp in first:
            cp.start()

    def middle(self, ins, outs, sems):
        _, _, passed, landed, _ = self._plan(ins, outs, sems)
        for got, cp in zip(landed, passed):
            got.wait_recv()
            cp.start()

    def finish(self, ins, outs, sems):
        mine, first, passed, _, late = self._plan(ins, outs, sems)
        for got in late:
            got.wait_recv()
        for cp in first + passed:
            cp.wait_send()
        mine.wait()


class _ScatterJob:
    def __init__(self, tag, grads):
        self.tag, self.ins = tag, list(grads)
        self.row_counts = [g.shape[1] for g in grads]
        self.outs = [jax.ShapeDtypeStruct((NDEV, sum(self.row_counts), grads[0].shape[2]), grads[0].dtype)]
        n = len(grads)
        self.scratch = [pltpu.SemaphoreType.DMA((n, 7)), pltpu.SemaphoreType.DMA((n, 7)), pltpu.SemaphoreType.DMA((n,))]
        self.results = None

    def _plan(self, ins, outs, sems):
        (out_ref,), (send_sems, recv_sems, local_sems) = outs, sems
        x, y, c = _place()
        remote, local, off = [], [], 0
        for i, (g_ref, rows) in enumerate(zip(ins, self.row_counts)):
            span = pl.ds(off, rows)
            for k, (a, b, d) in enumerate(FLIPS):
                px, py, pc = x ^ a, y ^ b, c ^ d
                remote.append(pltpu.make_async_remote_copy(
                    src_ref=g_ref.at[4 * px + 2 * py + pc], dst_ref=out_ref.at[k, span, :], send_sem=send_sems.at[i, k],
                    recv_sem=recv_sems.at[i, k], device_id=(px, py, pc), device_id_type=MESH))
            local.append(pltpu.make_async_copy(g_ref.at[4 * x + 2 * y + c], out_ref.at[7, span, :], local_sems.at[i]))
            off += rows
        return remote, local

    def start(self, ins, outs, sems):
        remote, local = self._plan(ins, outs, sems)
        for cp in remote + local:
            cp.start()

    def middle(self, ins, outs, sems):
        pass

    def finish(self, ins, outs, sems):
        remote, local = self._plan(ins, outs, sems)
        for cp in remote:
            cp.wait_recv()
        for cp in remote:
            cp.wait_send()
        for cp in local:
            cp.wait()


class _RowsGatherJob:
    def __init__(self, tag, rows):
        self.tag, self.ins = tag, [rows]
        self.outs = [jax.ShapeDtypeStruct((NDEV,) + rows.shape, rows.dtype)]
        self.scratch = [pltpu.SemaphoreType.DMA((7,)), pltpu.SemaphoreType.DMA((7,)), pltpu.SemaphoreType.DMA]
        self.results = None

    def _plan(self, ins, outs, sems):
        (x_ref,), (out_ref,), (send_sems, recv_sems, local_sem) = ins, outs, sems
        x, y, c = _place()
        me = 4 * x + 2 * y + c
        mine = pltpu.make_async_copy(x_ref, out_ref.at[me], local_sem)
        sends, lands = [], []
        for k, (a, b, d) in enumerate(FLIPS):
            px, py, pc = x ^ a, y ^ b, c ^ d
            for dst, keep in ((out_ref.at[me], sends), (out_ref.at[4 * px + 2 * py + pc], lands)):
                keep.append(pltpu.make_async_remote_copy(
                    src_ref=x_ref, dst_ref=dst, send_sem=send_sems.at[k], recv_sem=recv_sems.at[k],
                    device_id=(px, py, pc), device_id_type=MESH))
        return mine, sends, lands

    def start(self, ins, outs, sems):
        mine, sends, _ = self._plan(ins, outs, sems)
        mine.start()
        for cp in sends:
            cp.start()

    def middle(self, ins, outs, sems):
        pass

    def finish(self, ins, outs, sems):
        mine, sends, lands = self._plan(ins, outs, sems)
        for cp in lands:
            cp.wait_recv()
        for cp in sends:
            cp.wait_send()
        mine.wait()


def _fold_rows(parts):
    D = parts[0].shape[1]
    counts = [a.shape[0] // 8 for a in parts]
    total = -(-sum(counts) // 8) * 8

    def body(*refs):
        out = refs[-1]
        out[...] = jnp.zeros_like(out)
        row = 0
        for ref, k in zip(refs[:-1], counts):
            for j in range(k):
                out[pl.ds(row + j, 1), :] = jnp.sum(ref[pl.ds(8 * j, 8), :], axis=0, keepdims=True)
            row += k

    vm = pl.BlockSpec(memory_space=pltpu.VMEM)
    return pl.pallas_call(body, name="fold_rows", out_shape=jax.ShapeDtypeStruct((total, D), F32),
                          in_specs=[vm] * len(parts), out_specs=vm)(*parts)


def _sum_slots(slots):
    n, S, D = slots.shape

    def body(s_ref, o_ref):
        acc = s_ref[0]
        for j in range(1, n):
            acc = acc + s_ref[j]
        o_ref[...] = acc

    vm = pl.BlockSpec(memory_space=pltpu.VMEM)
    return pl.pallas_call(body, name="sum_slots", out_shape=jax.ShapeDtypeStruct((S, D), slots.dtype),
                          in_specs=[vm], out_specs=vm)(slots)


def _launch(name, body, grid, args, in_specs, out_specs, out_shape, scratch, jobs=()):
    grid = (grid,) if isinstance(grid, int) else tuple(grid)
    steps = grid[0] * (grid[1] if len(grid) == 2 else 1)
    n_in, n_out, n_sc = len(args), len(out_shape), len(scratch)
    j_in = [a for jb in jobs for a in jb.ins]
    j_out = [o for jb in jobs for o in jb.outs]
    j_sc = [s for jb in jobs for s in jb.scratch]
    mid = (17 * steps) // 20

    def wrapped(*refs):
        c_in, refs = refs[:n_in], refs[n_in:]
        m_in, refs = refs[:len(j_in)], refs[len(j_in):]
        c_out, refs = refs[:n_out], refs[n_out:]
        m_out, refs = refs[:len(j_out)], refs[len(j_out):]
        c_sc, m_sc = refs[:n_sc], refs[n_sc:]
        bound, a, b, c = [], 0, 0, 0
        for jb in jobs:
            bound.append((jb, m_in[a:a + len(jb.ins)], m_out[b:b + len(jb.outs)], m_sc[c:c + len(jb.scratch)]))
            a, b, c = a + len(jb.ins), b + len(jb.outs), c + len(jb.scratch)
        i = pl.program_id(0) if len(grid) == 1 else pl.program_id(0) * grid[1] + pl.program_id(1)

        def phase(step, which):
            if jobs:
                @pl.when(i == step)
                def _():
                    for jb, ins, outs, sems in bound:
                        getattr(jb, which)(ins, outs, sems)

        phase(0, "start")
        if body is not None:
            body(*c_in, *c_out, *c_sc)
        phase(mid, "middle")
        phase(steps - 1, "finish")

    outs = pl.pallas_call(
        wrapped, name=name + "".join("_" + jb.tag for jb in jobs), grid=grid,
        in_specs=list(in_specs) + [ANY] * len(j_in), out_specs=list(out_specs) + [ANY] * len(j_out),
        out_shape=list(out_shape) + j_out, scratch_shapes=list(scratch) + j_sc, compiler_params=_params(len(grid)),
    )(*args, *j_in)
    pos = n_out
    for jb in jobs:
        jb.results = list(outs[pos:pos + len(jb.outs)])
        pos += len(jb.outs)
    return list(outs[:n_out])


def _run_jobs(name, jobs):
    _launch(name, None, 1, [], [], [], [], [], jobs)


def _sum_parts(got, name):
    n, rows, D = got.shape
    tr = _tile(rows, 512)
    order = [n - 1] + list(range(n - 1))

    def body(*refs):
        g = refs[0][...].astype(F32)
        for pr in refs[1:n]:
            g = g + pr[...].astype(F32)
        refs[-1][...] = g

    specs = [pl.BlockSpec((None, tr, D), functools.partial(lambda k, i: (k, i, 0), k)) for k in order]
    return pl.pallas_call(
        body, name=name, grid=(rows // tr,), in_specs=specs, out_specs=_rows(tr, D),
        out_shape=jax.ShapeDtypeStruct((rows, D), F32), compiler_params=_params(),
    )(*[got] * n)


def _adamw(name, entries, by_layer):
    n = len(entries)
    L = entries[0][0].shape[0]

    def body(*refs):
        ins, outs = refs[:4 * n], refs[4 * n:]
        for e in range(n):
            w_ref, g_ref, m_ref, v_ref = ins[4 * e:4 * e + 4]
            d_out, m_out, v_out = outs[3 * e:3 * e + 3]
            g = g_ref[...]
            m_new = ADAM_B1 * m_ref[...] + (1.0 - ADAM_B1) * g
            v_new = ADAM_B2 * v_ref[...] + (1.0 - ADAM_B2) * (g * g)
            m_hat = m_new / (1.0 - ADAM_B1 ** ADAM_STEP)
            v_hat = v_new / (1.0 - ADAM_B2 ** ADAM_STEP)
            d_out[...] = -ADAM_LR * (m_hat / (jnp.sqrt(v_hat) + ADAM_EPS) + ADAM_WD * w_ref[...])
            m_out[...] = m_new
            v_out[...] = v_new

    def spec(a):
        rest = (0,) * (a.ndim - 1)
        if by_layer:
            return pl.BlockSpec((None,) + a.shape[1:], lambda l: (l,) + rest)
        return pl.BlockSpec(a.shape, lambda l: (0,) + rest)

    flat = [a for e in entries for a in e]
    outs = pl.pallas_call(
        body, name=name, grid=(L if by_layer else 1,),
        in_specs=[spec(a) for a in flat], out_specs=[spec(e[0]) for e in entries for _ in range(3)],
        out_shape=[jax.ShapeDtypeStruct(e[0].shape, F32) for e in entries for _ in range(3)],
        compiler_params=_params(),
    )(*flat)
    return [tuple(outs[3 * e:3 * e + 3]) for e in range(n)]


def _tn_matmul(xa, ya, name, tmm, y_follows=False, jobs=()):
    T, M = xa.shape
    tn = tmm if y_follows else ya.shape[1]
    tt = _tile(T, 1024)
    nb = M // tmm

    def body(x_ref, y_ref, o_ref, acc):
        k = pl.program_id(1)

        @pl.when(k == 0)
        def _():
            acc[...] = jnp.zeros_like(acc)

        acc[...] += _tn(x_ref[...], y_ref[...])

        @pl.when(k == pl.num_programs(1) - 1)
        def _():
            o_ref[...] = acc[...].astype(o_ref.dtype)

    y_map = (lambda b, k: (k, b)) if y_follows else (lambda b, k: (k, 0))
    return _launch(
        name, body, (nb, T // tt), [xa, ya],
        [pl.BlockSpec((tt, tmm), lambda b, k: (k, b)), pl.BlockSpec((tt, tn), y_map)],
        [pl.BlockSpec((tmm, tn), lambda b, k: (b, 0))], [jax.ShapeDtypeStruct((M, tn), BF16)],
        [pltpu.VMEM((tmm, tn), F32)], jobs)[0]


def _ffn_fwd(h, gain, G, lay, jobs=()):
    T, D = h.shape
    FB = lay.FB
    FP = NDEV * FB
    offs = (0, FB, 2 * FB)
    tm = _tile(T, 512)

    def body(h_ref, gain_ref, g_hbm, ho_ref, go_ref, uo_ref, wg, wu, wd, sems):
        @pl.when(pl.program_id(0) == 0)
        def _():
            _fetch(g_hbm, [(offs[0], FB, wg), (offs[1], FB, wu), (offs[2], FB, wd)], sems)

        h = h_ref[...]
        xh, _ = _rms(h)
        n = (xh * gain_ref[...]).astype(BF16)
        def project(sl):
            return _nt(n, wg[sl, :]), _nt(n, wu[sl, :])

        def activate(sl, gu):
            g, u = gu
            go_ref[:, sl] = g.astype(BF16)
            uo_ref[:, sl] = u.astype(BF16)
            return (g * _sig(g) * u).astype(BF16)

        def contract(sl, a, acc):
            return acc + _nn(a, wd[sl, :])

        acc = _staggered(_hidden_chunks(FP), project, activate, contract, jnp.zeros((tm, D), F32))
        ho_ref[...] = h + 0.5 * acc

    return _launch(
        "ffn_fwd", body, T // tm, [h, gain, G],
        [_rows(tm, D), _const((1, D)), ANY], [_rows(tm, D), _rows(tm, FP), _rows(tm, FP)],
        [jax.ShapeDtypeStruct((T, D), F32), jax.ShapeDtypeStruct((T, FP), BF16), jax.ShapeDtypeStruct((T, FP), BF16)],
        [pltpu.VMEM((FP, D), BF16)] * 3 + [pltpu.SemaphoreType.DMA((3, NDEV))], jobs)


def _ffn_bwd(d, h, gain, ga, ua, G, lay, jobs=()):
    T, D = h.shape
    FB = lay.FB
    FP = NDEV * FB
    offs = (0, FB, 2 * FB)
    tm = _tile(T, 256)

    def body(d_ref, h_ref, gain_ref, ga_ref, ua_ref, g_hbm,
             do_ref, dg_ref, du_ref, a_ref, n_ref, dh_ref, gg_ref, wg, wu, wd, sems):
        @pl.when(pl.program_id(0) == 0)
        def _():
            _fetch(g_hbm, [(offs[0], FB, wg), (offs[1], FB, wu), (offs[2], FB, wd)], sems)
            gg_ref[...] = jnp.zeros_like(gg_ref)

        d = d_ref[...]
        gain_v = gain_ref[...]
        xh, r = _rms(h_ref[...])
        n_ref[...] = (xh * gain_v).astype(BF16)
        dh = (0.5 * d).astype(BF16)
        dh_ref[...] = dh
        def project(sl):
            return _nt(dh, wd[sl, :])

        def activate(sl, da):
            g = ga_ref[:, sl].astype(F32)
            u = ua_ref[:, sl].astype(F32)
            s = _sig(g)
            silu = g * s
            a_ref[:, sl] = (silu * u).astype(BF16)
            dgv = (da * u * (s * (1.0 + g * (1.0 - s)))).astype(BF16)
            duv = (da * silu).astype(BF16)
            dg_ref[:, sl] = dgv
            du_ref[:, sl] = duv
            return dgv, duv

        def contract(sl, grads, acc):
            return acc + _nn(grads[0], wg[sl, :]) + _nn(grads[1], wu[sl, :])

        dn = _staggered(_hidden_chunks(FP), project, activate, contract, jnp.zeros((tm, D), F32))
        gg_ref[...] += _colsum8(dn * xh)
        do_ref[...] = d + _rms_bwd(dn, xh, r, gain_v)

    wide = jax.ShapeDtypeStruct((T, FP), BF16)
    return _launch(
        "ffn_bwd", body, T // tm, [d, h, gain, ga, ua, G],
        [_rows(tm, D), _rows(tm, D), _const((1, D)), _rows(tm, FP), _rows(tm, FP), ANY],
        [_rows(tm, D), _rows(tm, FP), _rows(tm, FP), _rows(tm, FP), _rows(tm, D), _rows(tm, D), _const((8, D))],
        [jax.ShapeDtypeStruct((T, D), F32), wide, wide, wide, jax.ShapeDtypeStruct((T, D), BF16),
         jax.ShapeDtypeStruct((T, D), BF16), jax.ShapeDtypeStruct((8, D), F32)],
        [pltpu.VMEM((FP, D), BF16)] * 3 + [pltpu.SemaphoreType.DMA((3, NDEV))], jobs)


def _inproj_fwd(h, gain, G, lay):
    T, D = h.shape
    NS = lay.NS
    NIN, CH = NDEV * NS, 2 * NS
    tm = _tile(T, 512)

    def body(h_ref, gain_ref, g_hbm, z_ref, win, sems):
        @pl.when(pl.program_id(0) == 0)
        def _():
            _fetch(g_hbm, [(lay.off["win"], NS, win)], sems)

        xh, _ = _rms(h_ref[...])
        n = (xh * gain_ref[...]).astype(BF16)
        for j in range(NIN // CH):
            sl = pl.ds(j * CH, CH)
            z_ref[:, sl] = _nt(n, win[sl, :]).astype(BF16)

    return pl.pallas_call(
        body, name="inproj_fwd", grid=(T // tm,),
        in_specs=[_rows(tm, D), _const((1, D)), ANY], out_specs=_rows(tm, NIN),
        out_shape=jax.ShapeDtypeStruct((T, NIN), BF16),
        scratch_shapes=[pltpu.VMEM((NIN, D), BF16), pltpu.SemaphoreType.DMA((1, NDEV))],
        compiler_params=_params(),
    )(h, gain, G)


def _conv_chunks(tm, D):
    rb, lc = min(tm, 64), min(D, 256)
    return [(r0, l0, rb, lc) for r0 in range(0, tm, rb) for l0 in range(0, D, lc)]


SUBLANES = 8


def _preshift(sh, n_rows):
    for r in range(1, SUBLANES):
        sh[r, pl.ds(0, n_rows), :] = sh[0, pl.ds(r, n_rows), :]


def _window(sh, start, rows, lanes):
    r = start % SUBLANES
    return sh[r, pl.ds(start - r, rows), lanes]


def _mixer_fwd(z, h, pool_w, pscale, cw, cb, lng, lnb, G, lay, jobs=()):
    T, D = h.shape
    PG, DS = lay.PG, lay.DS
    tm = _tile(T, 256)
    hb = tm // HALO

    def body(z_ref, zp_ref, h_ref, pw_ref, ps_ref, cw_ref, cb_ref, lg_ref, lb_ref, g_hbm,
             ho_ref, c1_ref, q_ref, cc_ref, pool_ref, ext_p, sh_c, a_s, wco, wout, sems):
        i = pl.program_id(0)

        @pl.when(i == 0)
        def _():
            _fetch(g_hbm, [(lay.off["wco"], DS, wco), (lay.off["wout"], DS, wout)], sems)

        live = jnp.where(i > 0, 1.0, 0.0).astype(F32)
        ext_p[pl.ds(0, HALO), :] = zp_ref[:, pl.ds(0, D)].astype(F32) * live
        ext_p[pl.ds(HALO, tm), :] = z_ref[:, pl.ds(0, D)].astype(F32)
        sh_c[0, pl.ds(0, HALO), :] = (zp_ref[:, pl.ds(D, D)].astype(F32)
                                      * _sig(zp_ref[:, pl.ds(2 * D, D)].astype(F32)) * live)
        sh_c[0, pl.ds(HALO, tm), :] = z_ref[:, pl.ds(D, D)].astype(F32) * _sig(z_ref[:, pl.ds(2 * D, D)].astype(F32))
        _preshift(sh_c, tm + HALO - SUBLANES)

        t = i * tm + lax.broadcasted_iota(jnp.int32, (tm, 1), 0)
        for g, w in enumerate(POOL_WINDOWS):
            sl = pl.ds(g * PG, PG)
            s = ext_p[pl.ds(HALO, tm), sl]
            zc = s
            for j in range(1, w):
                s = s + ext_p[pl.ds(HALO - j, tm), sl]
            inv = 1.0 / jnp.minimum(t + 1, w).astype(F32)
            pooled = (s * inv - zc).astype(BF16)
            pool_ref[:, sl] = pooled
            qv = _nn(pooled, pw_ref[g])
            q_ref[:, sl] = qv.astype(BF16)
            a_s[:, sl] = qv * ps_ref[:, sl]

        for r0, l0, rb, lc in _conv_chunks(tm, D):
            ls = pl.ds(l0, lc)
            acc = jnp.zeros((rb, lc), F32) + cb_ref[:, ls]
            for k in range(CONV_K):
                acc = acc + cw_ref[pl.ds(k, 1), ls] * _window(sh_c, r0 + HALO - (CONV_K - 1) + k, rb, ls)
            c1_ref[pl.ds(r0, rb), ls] = acc

        xhat, _ = _ln(c1_ref[...])
        c2 = xhat * lg_ref[...] + lb_ref[...]
        c3 = (c2 * _sig(c2)).astype(BF16)
        cc = _nn(c3, wco[...])
        cc_ref[...] = cc.astype(BF16)
        gp = z_ref[:, pl.ds(3 * D, D)].astype(F32)
        gc = z_ref[:, pl.ds(4 * D, D)].astype(F32)
        m = (_sig(gp) * a_s[...] + _sig(gc) * cc).astype(BF16)
        ho_ref[...] = h_ref[...] + _nn(m, wout[...])

    act = jax.ShapeDtypeStruct((T, D), BF16)
    return _launch(
        "mixer_fwd", body, T // tm, [z, z, h, pool_w, pscale, cw, cb, lng, lnb, G],
        [_rows(tm, 5 * D), pl.BlockSpec((HALO, 5 * D), lambda i: (jnp.maximum(i * hb - 1, 0), 0)),
         _rows(tm, D), _const((4, PG, PG)), _const((1, D)), _const((CONV_KP, D)), _const((1, D)),
         _const((1, D)), _const((1, D)), ANY],
        [_rows(tm, D)] * 5,
        [jax.ShapeDtypeStruct((T, D), F32), jax.ShapeDtypeStruct((T, D), F32), act, act, act],
        [pltpu.VMEM((HALO + tm, D), F32), pltpu.VMEM((SUBLANES, HALO + tm, D), F32), pltpu.VMEM((tm, D), F32),
         pltpu.VMEM((D, D), BF16), pltpu.VMEM((D, D), BF16), pltpu.SemaphoreType.DMA((2, NDEV))], jobs)


def _mixer_bwd_rows(d, z, c1, qa, cca, pool_w, pscale, lng, lnb, G, lay):
    T, D = d.shape
    PG, DS = lay.PG, lay.DS
    tm = _tile(T, 256)

    def body(d_ref, z_ref, c1_ref, q_ref, cc_ref, pw_ref, ps_ref, lg_ref, lb_ref, g_hbm,
             m_ref, db_ref, dcc_ref, c3_ref, dq_ref, dpool_ref, dc1_ref, dzg_ref, gps_ref, glg_ref, glb_ref, gcb_ref,
             wco, wout, sems):
        @pl.when(pl.program_id(0) == 0)
        def _():
            _fetch(g_hbm, [(lay.off["wco"], DS, wco), (lay.off["wout"], DS, wout)], sems)
            for ref in (gps_ref, glg_ref, glb_ref, gcb_ref):
                ref[...] = jnp.zeros_like(ref)

        db = d_ref[...].astype(BF16)
        db_ref[...] = db
        dm = _nt(db, wout[...])
        q = q_ref[...].astype(F32)
        cc = cc_ref[...].astype(F32)
        ps = ps_ref[...]
        sp = _sig(z_ref[:, pl.ds(3 * D, D)].astype(F32))
        sc = _sig(z_ref[:, pl.ds(4 * D, D)].astype(F32))
        a = q * ps
        m_ref[...] = (sp * a + sc * cc).astype(BF16)
        da = dm * sp
        dzg_ref[:, pl.ds(0, D)] = (dm * a * sp * (1.0 - sp)).astype(BF16)
        dzg_ref[:, pl.ds(D, D)] = (dm * cc * sc * (1.0 - sc)).astype(BF16)
        gps_ref[...] += _colsum8(da * q)
        dq = (da * ps).astype(BF16)
        dq_ref[...] = dq
        for g in range(len(POOL_WINDOWS)):
            sl = pl.ds(g * PG, PG)
            dpool_ref[:, sl] = _nt(dq_ref[:, sl], pw_ref[g])

        dcc = (dm * sc).astype(BF16)
        dcc_ref[...] = dcc
        xhat, rstd = _ln(c1_ref[...])
        lg = lg_ref[...]
        c2 = xhat * lg + lb_ref[...]
        s2 = _sig(c2)
        c3_ref[...] = (c2 * s2).astype(BF16)
        dc2 = _nt(dcc, wco[...]) * (s2 * (1.0 + c2 * (1.0 - s2)))
        glg_ref[...] += _colsum8(dc2 * xhat)
        glb_ref[...] += _colsum8(dc2)
        dxh = dc2 * lg
        dc1 = rstd * (dxh - jnp.mean(dxh, axis=-1, keepdims=True)
                      - xhat * jnp.mean(dxh * xhat, axis=-1, keepdims=True))
        dc1_ref[...] = dc1
        gcb_ref[...] += _colsum8(dc1)

    act = jax.ShapeDtypeStruct((T, D), BF16)
    full = jax.ShapeDtypeStruct((T, D), F32)
    vec = jax.ShapeDtypeStruct((8, D), F32)
    return pl.pallas_call(
        body, name="mixer_bwd_rows", grid=(T // tm,),
        in_specs=[_rows(tm, D), _rows(tm, 5 * D), _rows(tm, D), _rows(tm, D), _rows(tm, D),
                  _const((4, PG, PG)), _const((1, D)), _const((1, D)), _const((1, D)), ANY],
        out_specs=[_rows(tm, D)] * 7 + [_rows(tm, 2 * D)] + [_const((8, D))] * 4,
        out_shape=[act, act, act, act, act, full, full, jax.ShapeDtypeStruct((T, 2 * D), BF16), vec, vec, vec, vec],
        scratch_shapes=[pltpu.VMEM((D, D), BF16), pltpu.VMEM((D, D), BF16), pltpu.SemaphoreType.DMA((2, NDEV))],
        compiler_params=_params(),
    )(d, z, c1, qa, cca, pool_w, pscale, lng, lnb, G)


def _mixer_bwd_time(dc1, dpool, z, cw, lay, jobs=()):
    T, D = dc1.shape
    PG = lay.PG
    tm = _tile(T, 256)
    hb = tm // HALO
    nt = T // tm

    def body(dc_ref, dcn_ref, dp_ref, dpn_ref, z_ref, zp_ref, cw_ref, dz_ref, gcw_ref, sh_d, ext_q, sh_c, dc0_s):
        i = pl.program_id(0)

        @pl.when(i == 0)
        def _():
            gcw_ref[...] = jnp.zeros_like(gcw_ref)

        live_p = jnp.where(i > 0, 1.0, 0.0).astype(F32)
        live_n = jnp.where(i < nt - 1, 1.0, 0.0).astype(F32)
        sh_d[0, pl.ds(0, tm), :] = dc_ref[...]
        sh_d[0, pl.ds(tm, HALO), :] = dcn_ref[...] * live_n
        _preshift(sh_d, tm + HALO - SUBLANES)
        zg = z_ref[:, pl.ds(2 * D, D)].astype(F32)
        za = z_ref[:, pl.ds(D, D)].astype(F32)
        sg = _sig(zg)
        sh_c[0, pl.ds(0, HALO), :] = (zp_ref[:, pl.ds(D, D)].astype(F32)
                                      * _sig(zp_ref[:, pl.ds(2 * D, D)].astype(F32)) * live_p)
        sh_c[0, pl.ds(HALO, tm), :] = za * sg
        _preshift(sh_c, tm + HALO - SUBLANES)

        t = i * tm + lax.broadcasted_iota(jnp.int32, (tm, 1), 0)
        tn = (i + 1) * tm + lax.broadcasted_iota(jnp.int32, (HALO, 1), 0)
        for g, w in enumerate(POOL_WINDOWS):
            sl = pl.ds(g * PG, PG)
            ext_q[pl.ds(0, tm), sl] = dp_ref[:, sl] * (1.0 / jnp.minimum(t + 1, w).astype(F32))
            ext_q[pl.ds(tm, HALO), sl] = dpn_ref[:, sl] * (live_n / jnp.minimum(tn + 1, w).astype(F32))
        for g, w in enumerate(POOL_WINDOWS):
            sl = pl.ds(g * PG, PG)
            s = ext_q[pl.ds(0, tm), sl]
            for j in range(1, w):
                s = s + ext_q[pl.ds(j, tm), sl]
            dz_ref[:, sl] = (s - dp_ref[:, sl]).astype(BF16)

        for r0, l0, rb, lc in _conv_chunks(tm, D):
            ls = pl.ds(l0, lc)
            acc = jnp.zeros((rb, lc), F32)
            for j in range(CONV_K):
                acc = acc + cw_ref[pl.ds(CONV_K - 1 - j, 1), ls] * _window(sh_d, r0 + j, rb, ls)
            dc0_s[pl.ds(r0, rb), ls] = acc
        dc0 = dc0_s[...]
        dz_ref[:, pl.ds(D, D)] = (dc0 * sg).astype(BF16)
        dz_ref[:, pl.ds(2 * D, D)] = (dc0 * za * sg * (1.0 - sg)).astype(BF16)

        lc = min(D, 256)
        for l0 in range(0, D, lc):
            ls = pl.ds(l0, lc)
            dcv = dc_ref[:, ls]
            for k in range(CONV_K):
                gcw_ref[pl.ds(8 * k, 8), ls] += _colsum8(dcv * _window(sh_c, HALO - (CONV_K - 1) + k, tm, ls))

    nxt = lambda i: (jnp.minimum((i + 1) * hb, T // HALO - 1), 0)
    return _launch(
        "mixer_bwd_time", body, nt, [dc1, dc1, dpool, dpool, z, z, cw],
        [_rows(tm, D), pl.BlockSpec((HALO, D), nxt), _rows(tm, D), pl.BlockSpec((HALO, D), nxt),
         _rows(tm, 5 * D), pl.BlockSpec((HALO, 5 * D), lambda i: (jnp.maximum(i * hb - 1, 0), 0)),
         _const((CONV_KP, D))],
        [_rows(tm, 3 * D), _const((CONV_KP * 8, D))],
        [jax.ShapeDtypeStruct((T, 3 * D), BF16), jax.ShapeDtypeStruct((CONV_KP * 8, D), F32)],
        [pltpu.VMEM((SUBLANES, tm + HALO, D), F32), pltpu.VMEM((tm + HALO, D), F32),
         pltpu.VMEM((SUBLANES, HALO + tm, D), F32), pltpu.VMEM((tm, D), F32)], jobs)


def _inproj_bwd(d, h, gain, dzm, dzg, G, lay):
    T, D = h.shape
    NS = lay.NS
    NIN = NDEV * NS
    tm = _tile(T, 512)

    def body(d_ref, h_ref, gain_ref, dzm_ref, dzg_ref, g_hbm, do_ref, u_ref, gg_ref, win, sems):
        @pl.when(pl.program_id(0) == 0)
        def _():
            _fetch(g_hbm, [(lay.off["win"], NS, win)], sems)
            gg_ref[...] = jnp.zeros_like(gg_ref)

        gain_v = gain_ref[...]
        xh, r = _rms(h_ref[...])
        u_ref[...] = (xh * gain_v).astype(BF16)
        dn = jnp.zeros((tm, D), F32)
        for j in range(3):
            dn = dn + _nn(dzm_ref[:, pl.ds(j * D, D)], win[pl.ds(j * D, D), :])
        for j in range(2):
            dn = dn + _nn(dzg_ref[:, pl.ds(j * D, D)], win[pl.ds((3 + j) * D, D), :])
        gg_ref[...] += _colsum8(dn * xh)
        do_ref[...] = d_ref[...] + _rms_bwd(dn, xh, r, gain_v)

    return pl.pallas_call(
        body, name="inproj_bwd", grid=(T // tm,),
        in_specs=[_rows(tm, D), _rows(tm, D), _const((1, D)), _rows(tm, 3 * D), _rows(tm, 2 * D), ANY],
        out_specs=[_rows(tm, D), _rows(tm, D), _const((8, D))],
        out_shape=[jax.ShapeDtypeStruct((T, D), F32), jax.ShapeDtypeStruct((T, D), BF16),
                   jax.ShapeDtypeStruct((8, D), F32)],
        scratch_shapes=[pltpu.VMEM((NIN, D), BF16), pltpu.SemaphoreType.DMA((1, NDEV))],
        compiler_params=_params(),
    )(d, h, gain, dzm, dzg, G)


def _ple_fwd(h, pe, gain, wppt, G, lay):
    T, D = h.shape
    PD, DS = lay.PD, lay.DS
    tm = _tile(T, 512)

    def body(h_ref, p_ref, gain_ref, wpp_ref, g_hbm, ho_ref, gate_ref, wpg, sems):
        @pl.when(pl.program_id(0) == 0)
        def _():
            _fetch(g_hbm, [(lay.off["wpg"], DS, wpg)], sems)

        h = h_ref[...]
        xh, _ = _rms(h)
        n = (xh * gain_ref[...]).astype(BF16)
        gate = _sig(_nn(n, wpg[...]))
        gate_ref[...] = gate.astype(BF16)
        e = _nt(p_ref[...].astype(BF16), wpp_ref[...])
        ho_ref[...] = h + gate * e

    return pl.pallas_call(
        body, name="ple_fwd", grid=(T // tm,),
        in_specs=[_rows(tm, D), _rows(tm, PD), _const((1, D)), _const((D, PD)), ANY],
        out_specs=[_rows(tm, D), _rows(tm, D)],
        out_shape=[jax.ShapeDtypeStruct((T, D), F32), jax.ShapeDtypeStruct((T, D), BF16)],
        scratch_shapes=[pltpu.VMEM((D, D), BF16), pltpu.SemaphoreType.DMA((1, NDEV))],
        compiler_params=_params(),
    )(h, pe, gain, wppt, G)


def _ple_bwd(d, h, pe, gate_a, gain, wppt, G, lay):
    T, D = h.shape
    PD, DS = lay.PD, lay.DS
    tm = _tile(T, 512)

    def body(d_ref, h_ref, p_ref, gate_ref, gain_ref, wpp_ref, g_hbm,
             do_ref, dpre_ref, de_ref, n_ref, pb_ref, gg_ref, wpg, sems):
        @pl.when(pl.program_id(0) == 0)
        def _():
            _fetch(g_hbm, [(lay.off["wpg"], DS, wpg)], sems)
            gg_ref[...] = jnp.zeros_like(gg_ref)

        d = d_ref[...]
        gain_v = gain_ref[...]
        xh, r = _rms(h_ref[...])
        n_ref[...] = (xh * gain_v).astype(BF16)
        pb = p_ref[...].astype(BF16)
        pb_ref[...] = pb
        e = _nt(pb, wpp_ref[...])
        gate = gate_ref[...].astype(F32)
        de_ref[...] = (d * gate).astype(BF16)
        dpre = (d * e * gate * (1.0 - gate)).astype(BF16)
        dpre_ref[...] = dpre
        dn = _nt(dpre, wpg[...])
        gg_ref[...] += _colsum8(dn * xh)
        do_ref[...] = d + _rms_bwd(dn, xh, r, gain_v)

    act = jax.ShapeDtypeStruct((T, D), BF16)
    return pl.pallas_call(
        body, name="ple_bwd", grid=(T // tm,),
        in_specs=[_rows(tm, D), _rows(tm, D), _rows(tm, PD), _rows(tm, D), _const((1, D)), _const((D, PD)), ANY],
        out_specs=[_rows(tm, D), _rows(tm, D), _rows(tm, D), _rows(tm, D), _rows(tm, PD), _const((8, D))],
        out_shape=[jax.ShapeDtypeStruct((T, D), F32), act, act, act, jax.ShapeDtypeStruct((T, PD), BF16),
                   jax.ShapeDtypeStruct((8, D), F32)],
        scratch_shapes=[pltpu.VMEM((D, D), BF16), pltpu.SemaphoreType.DMA((1, NDEV))],
        compiler_params=_params(),
    )(d, h, pe, gate_a, gain, wppt, G)


def _head(h, target, gain):
    T, D = h.shape
    tm = _tile(T, 512)

    def body(h_ref, t_ref, gain_ref, do_ref, loss_ref, gg_ref):
        @pl.when(pl.program_id(0) == 0)
        def _():
            loss_ref[...] = jnp.zeros_like(loss_ref)
            gg_ref[...] = jnp.zeros_like(gg_ref)

        gain_v = gain_ref[...]
        xh, r = _rms(h_ref[...])
        err = xh * gain_v - t_ref[...]
        loss_ref[...] += _colsum8(err * err)
        dy = err * (1.0 / D)
        gg_ref[...] += _colsum8(dy * xh)
        do_ref[...] = _rms_bwd(dy, xh, r, gain_v)

    vec = jax.ShapeDtypeStruct((8, D), F32)
    return pl.pallas_call(
        body, name="head", grid=(T // tm,),
        in_specs=[_rows(tm, D), _rows(tm, D), _const((1, D))],
        out_specs=[_rows(tm, D), _const((8, D)), _const((8, D))],
        out_shape=[jax.ShapeDtypeStruct((T, D), F32), vec, vec], compiler_params=_params(),
    )(h, target, gain)


_BIG = ["ffn1_w_gate", "ffn1_w_up", "ffn1_w_down", "w_in", "pool_w", "conv_w_out", "w_out", "ffn2_w_gate",
        "ffn2_w_up", "ffn2_w_down", "ple_w_gate", "ple_w_proj"]
_VECS = ["ffn1_norm", "mix_norm", "pool_scale", "conv_dw_b", "conv_ln_g", "conv_ln_b", "ffn2_norm", "ple_norm"]
_WEIGHTS = ["ffn1_norm", "ffn1_w_gate", "ffn1_w_up", "ffn1_w_down", "mix_norm", "w_in", "pool_w", "pool_scale",
            "conv_dw_w", "conv_dw_b", "conv_ln_g", "conv_ln_b", "conv_w_out", "w_out", "ffn2_norm", "ffn2_w_gate",
            "ffn2_w_up", "ffn2_w_down", "ple_norm", "ple_w_gate", "ple_w_proj", "final_norm"]


def _step(x, p, loss_target, w, mom, var):
    T, D = x.shape[1], x.shape[2]
    L = p.shape[0]
    FS, NS = w["ffn1_w_gate"].shape[2], w["w_in"].shape[2]
    PD = p.shape[3]
    PGS, PG = w["pool_w"].shape[2], w["pool_w"].shape[3]
    CS = w["conv_dw_w"].shape[2]
    lay = _Layout(D, FS, NS, PD, PG, PGS)
    FB = lay.FB
    ax, ay, ac = _place()
    me = 4 * ax + 2 * ay + ac

    assert L == 2, "the exchange schedule below is written for two layers"
    gather = {(l, pc): _GatherJob(f"ag{pc}", _pack_piece(w, l, lay, pc, BF16)) for l in range(L) for pc in PIECES}
    fwd_jobs = {("ffn1", 0): [gather[0, "C"], gather[0, "D"]], ("mixer", 0): [gather[0, "B"]],
                ("ffn2", 0): [gather[1, "A"], gather[1, "D"]], ("ffn1", 1): [gather[1, "C"]],
                ("mixer", 1): [gather[1, "B"]]}
    cw_mine = jnp.pad(w["conv_dw_w"], ((0, 0), (0, CONV_KP - CONV_K), (0, 0)))
    taps_job = _RowsGatherJob("agW", cw_mine.reshape(L * CONV_KP * CS // D, D))
    _run_jobs("gather_first", [gather[0, "A"], taps_job])
    cw_full = taps_job.results[0].reshape(NDEV, L, CONV_KP, CS).transpose(1, 2, 0, 3).reshape(L, CONV_KP, D)

    def gathered(l, pc):
        return gather[l, pc].results[0]

    def small_mats(l):
        G = gathered(l, "D")
        wppt = G[:, lay.off["wpp"]:lay.off["wpp"] + lay.rows["wpp"]].reshape(D, PD)
        pw = G[:, lay.off["pool"]:lay.off["pool"] + lay.rows["pool"]].reshape(NDEV, 4, PGS, PG)
        return wppt, pw.transpose(1, 0, 2, 3).reshape(4, PG, PG)

    def vec(name, l):
        return w[name][l].reshape(1, D)

    h = x[0]
    saved = []
    for l in range(L):
        s = {"h0": h}
        h, s["g1"], s["u1"] = _ffn_fwd(h, vec("ffn1_norm", l), gathered(l, "A"), lay, fwd_jobs.get(("ffn1", l), ()))
        wppt, pw = small_mats(l)
        s["wppt"], s["pw"], s["h1"] = wppt, pw, h
        s["z"] = _inproj_fwd(h, vec("mix_norm", l), gathered(l, "C"), lay)
        h, s["c1"], s["q"], s["cc"], s["pooled"] = _mixer_fwd(
            s["z"], h, pw, vec("pool_scale", l), cw_full[l], vec("conv_dw_b", l), vec("conv_ln_g", l),
            vec("conv_ln_b", l), gathered(l, "D"), lay, fwd_jobs.get(("mixer", l), ()))
        s["h2"] = h
        h, s["g2"], s["u2"] = _ffn_fwd(h, vec("ffn2_norm", l), gathered(l, "B"), lay, fwd_jobs.get(("ffn2", l), ()))
        s["h3"] = h
        h, s["gate"] = _ple_fwd(h, p[l, 0], vec("ple_norm", l), wppt, gathered(l, "D"), lay)
        saved.append(s)

    d, loss_part, g_final = _head(h, loss_target[0], w["final_norm"].reshape(1, D))

    scatter = {}
    small_parts = [None] * L

    def send(l, pc, blocks):
        scatter[l, pc] = _ScatterJob(f"rs{pc}", [blocks[n].reshape(NDEV, lay.rows[n], D) for n, _ in lay.pieces[pc]])
        return scatter[l, pc]

    def small_rows():
        return _fold_rows([v for l in range(L) for v in small_parts[l]] + [g_final, loss_part])

    held = []
    for l in reversed(range(L)):
        s = saved[l]
        wppt, pw = s["wppt"], s["pw"]
        g = {}
        d, dpre, de, n_ple, pb, g_ple = _ple_bwd(d, s["h3"], p[l, 0], s["gate"], vec("ple_norm", l), wppt,
                                                 gathered(l, "D"), lay)
        g["wpg"] = _tn_matmul(n_ple, dpre, "tn_sq", D)
        g["wpp"] = _tn_matmul(de, pb, "tn_proj", D)

        d, dg2, du2, a2, n2, dh2, g_n2 = _ffn_bwd(d, s["h2"], vec("ffn2_norm", l), s["g2"], s["u2"],
                                                  gathered(l, "B"), lay, held)
        held = []
        g["g2"] = _tn_matmul(dg2, n2, "tn_ffn", NDEV * FB)
        g["u2"] = _tn_matmul(du2, n2, "tn_ffn", NDEV * FB)
        g["d2"] = _tn_matmul(a2, dh2, "tn_ffn", NDEV * FB)
        held.append(send(l, "B", g))

        (m_b, d_b, dcc, c3, dq, dpool, dc1, dzg, g_ps, g_lg, g_lb, g_cb) = _mixer_bwd_rows(
            d, s["z"], s["c1"], s["q"], s["cc"], pw, vec("pool_scale", l), vec("conv_ln_g", l), vec("conv_ln_b", l),
            gathered(l, "D"), lay)
        g["wout"] = _tn_matmul(m_b, d_b, "tn_sq", D)
        g["wco"] = _tn_matmul(c3, dcc, "tn_sq", D)
        g_pool = _tn_matmul(s["pooled"], dq, "tn_pool", PG, y_follows=True)
        g["pool"] = g_pool.reshape(4, NDEV, PGS, PG).transpose(1, 0, 2, 3)
        held.append(send(l, "D", g))
        dzm, g_cw = _mixer_bwd_time(dc1, dpool, s["z"], cw_full[l], lay, held)
        held = []
        d, u_b, g_mix = _inproj_bwd(d, s["h1"], vec("mix_norm", l), dzm, dzg, gathered(l, "C"), lay)
        g["win"] = jnp.concatenate([_tn_matmul(dzm, u_b, "tn_in3", 3 * D // 2),
                                    _tn_matmul(dzg, u_b, "tn_in2", D)], axis=0)
        held.append(send(l, "C", g))

        d, dg1, du1, a1, n1, dh1, g_n1 = _ffn_bwd(d, s["h0"], vec("ffn1_norm", l), s["g1"], s["u1"],
                                                  gathered(l, "A"), lay, held)
        held = []
        small_parts[l] = [g_n1, g_mix, g_ps, g_cb, g_lg, g_lb, g_n2, g_ple, g_cw]
        if l > 0:
            g["g1"] = _tn_matmul(dg1, n1, "tn_ffn", NDEV * FB)
            g["u1"] = _tn_matmul(du1, n1, "tn_ffn", NDEV * FB)
            g["d1"] = _tn_matmul(a1, dh1, "tn_ffn", NDEV * FB)
            held.append(send(l, "A", g))
        else:
            rows_job = _RowsGatherJob("agS", small_rows())
            g["g1"] = _tn_matmul(dg1, n1, "tn_ffn", NDEV * FB, jobs=[rows_job])
            g["u1"] = _tn_matmul(du1, n1, "tn_ffn", NDEV * FB, jobs=[send(l, "A1", g)])
            g["d1"] = _tn_matmul(a1, dh1, "tn_ffn", NDEV * FB, jobs=[send(l, "A2", g)])
            held.append(send(l, "A3", g))
    _run_jobs("scatter_last", held)
    grad_x = d[None]

    per = [{} for _ in range(L)]
    for (l, pc), job in scatter.items():
        per[l].update(_unpack_piece(_sum_parts(job.results[0], "sum_" + pc), lay, pc))
    grads = {n: jnp.stack([per[l][n] for l in range(L)]) for n in _BIG}

    small_sum = _sum_slots(rows_job.results[0])
    per_layer = len(_VECS) + CONV_KP
    for k, n in enumerate(_VECS):
        grads[n] = jnp.stack([small_sum[l * per_layer + k] for l in range(L)])
    g_cw_full = jnp.stack([small_sum[l * per_layer + len(_VECS):l * per_layer + len(_VECS) + CONV_K]
                           for l in range(L)])
    grads["conv_dw_w"] = lax.dynamic_slice_in_dim(g_cw_full, me * CS, CS, axis=2)
    grads["final_norm"] = small_sum[L * per_layer]
    loss = (0.5 / D) * jnp.sum(small_sum[L * per_layer + 1])

    def as2(a):
        return a.reshape(1, -1) if a.ndim == 1 else a

    deltas, new_m, new_v = {}, {}, {}

    def update(name, names, by_layer):
        res = _adamw(name, [(as2(w[n]), as2(grads[n]), as2(mom[n]), as2(var[n])) for n in names], by_layer)
        for n, (delta, m_new, v_new) in zip(names, res):
            deltas[n], new_m[n], new_v[n] = (a.reshape(w[n].shape) for a in (delta, m_new, v_new))

    for n in ("ffn1_w_gate", "ffn1_w_up", "ffn1_w_down", "ffn2_w_gate", "ffn2_w_up", "ffn2_w_down", "w_in"):
        update("adamw_" + n, [n], True)
    update("adamw_mid", ["conv_w_out", "w_out", "ple_w_gate", "ple_w_proj", "pool_w"], True)
    update("adamw_small", _VECS + ["conv_dw_w", "final_norm"], False)
    return loss, grad_x, (grads, deltas, new_m, new_v)


def kernel(x, p, ffn1_norm, ffn1_w_gate, ffn1_w_up, ffn1_w_down, mix_norm, w_in, pool_w, pool_scale, conv_dw_w, conv_dw_b, conv_ln_g, conv_ln_b, conv_w_out, w_out, ffn2_norm, ffn2_w_gate, ffn2_w_up, ffn2_w_down, ple_norm, ple_w_gate, ple_w_proj, final_norm, loss_target, m_ffn1_norm, m_ffn1_w_gate, m_ffn1_w_up, m_ffn1_w_down, m_mix_norm, m_w_in, m_pool_w, m_pool_scale, m_conv_dw_w, m_conv_dw_b, m_conv_ln_g, m_conv_ln_b, m_conv_w_out, m_w_out, m_ffn2_norm, m_ffn2_w_gate, m_ffn2_w_up, m_ffn2_w_down, m_ple_norm, m_ple_w_gate, m_ple_w_proj, m_final_norm, v_ffn1_norm, v_ffn1_w_gate, v_ffn1_w_up, v_ffn1_w_down, v_mix_norm, v_w_in, v_pool_w, v_pool_scale, v_conv_dw_w, v_conv_dw_b, v_conv_ln_g, v_conv_ln_b, v_conv_w_out, v_w_out, v_ffn2_norm, v_ffn2_w_gate, v_ffn2_w_up, v_ffn2_w_down, v_ple_norm, v_ple_w_gate, v_ple_w_proj, v_final_norm):
    given = dict(locals())
    w = {n: given[n] for n in _WEIGHTS}
    mom = {n: given["m_" + n] for n in _WEIGHTS}
    var = {n: given["v_" + n] for n in _WEIGHTS}
    loss, grad_x, (grads, deltas, new_m, new_v) = _step(x, p, loss_target, w, mom, var)
    out = [loss, grad_x]
    for res in (grads, deltas, new_m, new_v):
        out += [res[n] for n in _WEIGHTS]
    return tuple(out)
```

```python
import functools

import jax
import jax.numpy as jnp
from jax import lax
from jax.experimental import pallas as pl
from jax.experimental.pallas import tpu as pltpu

F32, BF16 = jnp.float32, jnp.bfloat16
NDEV = 8
MESH = pl.DeviceIdType.MESH
HALO = 32
POOL_WINDOWS = (2, 4, 8, 16)
CONV_K = 31
CONV_KP = 32
RMS_EPS, LN_EPS = 1e-6, 1e-5
ADAM_LR, ADAM_B1, ADAM_B2, ADAM_EPS, ADAM_WD, ADAM_STEP = 0.001, 0.9, 0.999, 1e-08, 0.01, 10
LANE = 128
VMEM_LIMIT = 56 * 1024 * 1024
ANY = pl.BlockSpec(memory_space=pl.ANY)


def _nn(a, b):
    return jnp.dot(a, b, preferred_element_type=F32)


def _nt(a, b):
    return lax.dot_general(a, b, (((1,), (1,)), ((), ())), preferred_element_type=F32)


def _tn(a, b):
    return lax.dot_general(a, b, (((0,), (0,)), ((), ())), preferred_element_type=F32)


def _colsum8(v):
    return jnp.sum(v.reshape(v.shape[0] // 8, 8, v.shape[1]), axis=0)


def _sig(v):
    return jax.nn.sigmoid(v)


def _rms(h):
    r = lax.rsqrt(jnp.mean(h * h, axis=-1, keepdims=True) + RMS_EPS)
    return h * r, r


def _rms_bwd(dn, xh, r, gain):
    dxh = dn * gain
    return r * (dxh - xh * jnp.mean(dxh * xh, axis=-1, keepdims=True))


def _ln(c1):
    mu = jnp.mean(c1, axis=-1, keepdims=True)
    cen = c1 - mu
    rstd = lax.rsqrt(jnp.mean(cen * cen, axis=-1, keepdims=True) + LN_EPS)
    return cen * rstd, rstd


def _rows(tm, c):
    return pl.BlockSpec((tm, c), lambda i: (i, 0))


def _const(shape):
    return pl.BlockSpec(shape, lambda i: (0,) * len(shape))


def _params(n_grid=1):
    return pltpu.CompilerParams(dimension_semantics=("arbitrary",) * n_grid, vmem_limit_bytes=VMEM_LIMIT)


BF16_ROWS = 16
MXU_CHUNK = 768


def _tile(n, want):
    if n <= want:
        return n
    return max(t for t in range(BF16_ROWS, want + 1, BF16_ROWS) if n % t == 0)


def _hidden_chunks(width):
    return [(o, min(MXU_CHUNK, width - o)) for o in range(0, width, MXU_CHUNK)]


def _staggered(chunks, project, activate, contract, acc):
    spans = [pl.ds(start, width) for start, width in chunks]
    ahead = project(spans[0])
    for c, span in enumerate(spans):
        projected = ahead
        if c + 1 < len(spans):
            ahead = project(spans[c + 1])
        acc = contract(span, activate(span, projected), acc)
    return acc


def _fetch(g_hbm, specs, sems):
    cps = []
    for wi, (off, rows, dst) in enumerate(specs):
        for dev in range(NDEV):
            cps.append(pltpu.make_async_copy(g_hbm.at[dev, pl.ds(off, rows), :],
                                             dst.at[pl.ds(dev * rows, rows), :], sems.at[wi, dev]))
    for cp in cps:
        cp.start()
    for cp in cps:
        cp.wait()


PIECES = ("A", "B", "C", "D")


class _Layout:
    def __init__(self, D, FS, NS, PD, PG, PGS):
        self.D, self.FS, self.NS, self.PD, self.PG, self.PGS = D, FS, NS, PD, PG, PGS
        assert FS % BF16_ROWS == 0 and (NDEV * FS) % (2 * LANE) == 0, "FFN shard rows must tile as bf16 row blocks"
        self.FB = FS
        self.DS = D // NDEV
        self.pieces = {
            "A": [("g1", self.FB), ("u1", self.FB), ("d1", self.FB)],
            "B": [("g2", self.FB), ("u2", self.FB), ("d2", self.FB)],
            "C": [("win", NS)],
            "D": [("wco", self.DS), ("wout", self.DS), ("wpg", self.DS), ("wpp", self.DS * PD // D),
                  ("pool", 4 * PGS * PG // D)],
            "A1": [("g1", self.FB)], "A2": [("u1", self.FB)], "A3": [("d1", self.FB)]}
        self.off, self.rows = {}, {}
        for pc in PIECES:
            o = 0
            for n, r in self.pieces[pc]:
                self.off[n], self.rows[n] = o, r
                o += r


def _pack_piece(w, l, lay, pc, dtype):
    D = lay.D
    make = {"g1": lambda: w["ffn1_w_gate"][l].T, "u1": lambda: w["ffn1_w_up"][l].T,
            "d1": lambda: w["ffn1_w_down"][l], "g2": lambda: w["ffn2_w_gate"][l].T,
            "u2": lambda: w["ffn2_w_up"][l].T, "d2": lambda: w["ffn2_w_down"][l],
            "win": lambda: w["w_in"][l].T, "wco": lambda: w["conv_w_out"][l], "wout": lambda: w["w_out"][l],
            "wpg": lambda: w["ple_w_gate"][l], "wpp": lambda: w["ple_w_proj"][l].T.reshape(-1, D),
            "pool": lambda: w["pool_w"][l].reshape(-1, D)}
    return jnp.concatenate([make[n]() for n, _ in lay.pieces[pc]], axis=0).astype(dtype)


def _unpack_piece(slab, lay, pc):
    PD, PG, PGS, DS = lay.PD, lay.PG, lay.PGS, lay.DS
    undo = {"g1": ("ffn1_w_gate", lambda a: a.T), "u1": ("ffn1_w_up", lambda a: a.T),
            "d1": ("ffn1_w_down", lambda a: a), "g2": ("ffn2_w_gate", lambda a: a.T),
            "u2": ("ffn2_w_up", lambda a: a.T), "d2": ("ffn2_w_down", lambda a: a),
            "win": ("w_in", lambda a: a.T), "wco": ("conv_w_out", lambda a: a), "wout": ("w_out", lambda a: a),
            "wpg": ("ple_w_gate", lambda a: a), "wpp": ("ple_w_proj", lambda a: a.reshape(DS, PD).T),
            "pool": ("pool_w", lambda a: a.reshape(4, PGS, PG))}
    out, o = {}, 0
    for n, r in lay.pieces[pc]:
        name, fn = undo[n]
        out[name] = fn(slab[o:o + r])
        o += r
    return out


def _place():
    return lax.axis_index("x"), lax.axis_index("y"), lax.axis_index("c")


FLIPS = [(a, b, d) for a in (0, 1) for b in (0, 1) for d in (0, 1)][1:]


class _GatherJob:
    def __init__(self, tag, slab):
        self.tag, self.ins = tag, [slab]
        self.outs = [jax.ShapeDtypeStruct((NDEV,) + slab.shape, slab.dtype)]
        self.scratch = [pltpu.SemaphoreType.DMA((7,)), pltpu.SemaphoreType.DMA((7,)), pltpu.SemaphoreType.DMA]
        self.results = None

    def _plan(self, ins, outs, sems):
        (x_ref,), (out_ref,), (send_sems, recv_sems, local_sem) = ins, outs, sems
        x, y, c = _place()
        me, sibling = (x, y, c), (x, y, 1 - c)
        chips = [(1 - x, y), (x, 1 - y), (1 - x, 1 - y)]

        def rows(px, py, pc):
            return out_ref.at[4 * px + 2 * py + pc]

        def copy(k, block, to, src=None):
            return pltpu.make_async_remote_copy(
                src_ref=rows(*block) if src is None else src, dst_ref=rows(*block),
                send_sem=send_sems.at[k], recv_sem=recv_sems.at[k], device_id=to, device_id_type=MESH)

        mine = pltpu.make_async_copy(x_ref, rows(*me), local_sem)
        first = [copy(0, me, sibling, src=x_ref)]
        first += [copy(1 + j, me, (*chip, c), src=x_ref) for j, chip in enumerate(chips)]
        passed = [copy(4 + j, (*chip, c), sibling) for j, chip in enumerate(chips)]
        landed = [copy(1 + j, (*chip, c), me) for j, chip in enumerate(chips)]
        late = [copy(0, sibling, me)] + [copy(4 + j, (*chip, 1 - c), me) for j, chip in enumerate(chips)]
        return mine, first, passed, landed, late

    def start(self, ins, outs, sems):
        mine, first, _, _, _ = self._plan(ins, outs, sems)
        mine.start()
        for cp in first:
            cp.start()

    def middle(self, ins, outs, sems):
        _, _, passed, landed, _ = self._plan(ins, outs, sems)
        for got, cp in zip(landed, passed):
            got.wait_recv()
            cp.start()

    def finish(self, ins, outs, sems):
        mine, first, passed, _, late = self._plan(ins, outs, sems)
        for got in late:
            got.wait_recv()
        for cp in first + passed:
            cp.wait_send()
        mine.wait()


class _ScatterJob:
    def __init__(self, tag, grads):
        self.tag, self.ins = tag, list(grads)
        self.row_counts = [g.shape[1] for g in grads]
        self.outs = [jax.ShapeDtypeStruct((NDEV, sum(self.row_counts), grads[0].shape[2]), grads[0].dtype)]
        n = len(grads)
        self.scratch = [pltpu.SemaphoreType.DMA((n, 7)), pltpu.SemaphoreType.DMA((n, 7)), pltpu.SemaphoreType.DMA((n,))]
        self.results = None

    def _plan(self, ins, outs, sems):
        (out_ref,), (send_sems, recv_sems, local_sems) = outs, sems
        x, y, c = _place()
        remote, local, off = [], [], 0
        for i, (g_ref, rows) in enumerate(zip(ins, self.row_counts)):
            span = pl.ds(off, rows)
            for k, (a, b, d) in enumerate(FLIPS):
                px, py, pc = x ^ a, y ^ b, c ^ d
                remote.append(pltpu.make_async_remote_copy(
                    src_ref=g_ref.at[4 * px + 2 * py + pc], dst_ref=out_ref.at[k, span, :], send_sem=send_sems.at[i, k],
                    recv_sem=recv_sems.at[i, k], device_id=(px, py, pc), device_id_type=MESH))
            local.append(pltpu.make_async_copy(g_ref.at[4 * x + 2 * y + c], out_ref.at[7, span, :], local_sems.at[i]))
            off += rows
        return remote, local

    def start(self, ins, outs, sems):
        remote, local = self._plan(ins, outs, sems)
        for cp in remote + local:
            cp.start()

    def middle(self, ins, outs, sems):
        pass

    def finish(self, ins, outs, sems):
        remote, local = self._plan(ins, outs, sems)
        for cp in remote:
            cp.wait_recv()
        for cp in remote:
            cp.wait_send()
        for cp in local:
            cp.wait()


class _RowsGatherJob:
    def __init__(self, tag, rows):
        self.tag, self.ins = tag, [rows]
        self.outs = [jax.ShapeDtypeStruct((NDEV,) + rows.shape, rows.dtype)]
        self.scratch = [pltpu.SemaphoreType.DMA((7,)), pltpu.SemaphoreType.DMA((7,)), pltpu.SemaphoreType.DMA]
        self.results = None

    def _plan(self, ins, outs, sems):
        (x_ref,), (out_ref,), (send_sems, recv_sems, local_sem) = ins, outs, sems
        x, y, c = _place()
        me = 4 * x + 2 * y + c
        mine = pltpu.make_async_copy(x_ref, out_ref.at[me], local_sem)
        sends, lands = [], []
        for k, (a, b, d) in enumerate(FLIPS):
            px, py, pc = x ^ a, y ^ b, c ^ d
            for dst, keep in ((out_ref.at[me], sends), (out_ref.at[4 * px + 2 * py + pc], lands)):
                keep.append(pltpu.make_async_remote_copy(
                    src_ref=x_ref, dst_ref=dst, send_sem=send_sems.at[k], recv_sem=recv_sems.at[k],
                    device_id=(px, py, pc), device_id_type=MESH))
        return mine, sends, lands

    def start(self, ins, outs, sems):
        mine, sends, _ = self._plan(ins, outs, sems)
        mine.start()
        for cp in sends:
            cp.start()

    def middle(self, ins, outs, sems):
        pass

    def finish(self, ins, outs, sems):
        mine, sends, lands = self._plan(ins, outs, sems)
        for cp in lands:
            cp.wait_recv()
        for cp in sends:
            cp.wait_send()
        mine.wait()


def _fold_rows(parts):
    D = parts[0].shape[1]
    counts = [a.shape[0] // 8 for a in parts]
    total = -(-sum(counts) // 8) * 8

    def body(*refs):
        out = refs[-1]
        out[...] = jnp.zeros_like(out)
        row = 0
        for ref, k in zip(refs[:-1], counts):
            for j in range(k):
                out[pl.ds(row + j, 1), :] = jnp.sum(ref[pl.ds(8 * j, 8), :], axis=0, keepdims=True)
            row += k

    vm = pl.BlockSpec(memory_space=pltpu.VMEM)
    return pl.pallas_call(body, name="fold_rows", out_shape=jax.ShapeDtypeStruct((total, D), F32),
                          in_specs=[vm] * len(parts), out_specs=vm)(*parts)


def _sum_slots(slots):
    n, S, D = slots.shape

    def body(s_ref, o_ref):
        acc = s_ref[0]
        for j in range(1, n):
            acc = acc + s_ref[j]
        o_ref[...] = acc

    vm = pl.BlockSpec(memory_space=pltpu.VMEM)
    return pl.pallas_call(body, name="sum_slots", out_shape=jax.ShapeDtypeStruct((S, D), slots.dtype),
                          in_specs=[vm], out_specs=vm)(slots)


def _launch(name, body, grid, args, in_specs, out_specs, out_shape, scratch, jobs=()):
    grid = (grid,) if isinstance(grid, int) else tuple(grid)
    steps = grid[0] * (grid[1] if len(grid) == 2 else 1)
    n_in, n_out, n_sc = len(args), len(out_shape), len(scratch)
    j_in = [a for jb in jobs for a in jb.ins]
    j_out = [o for jb in jobs for o in jb.outs]
    j_sc = [s for jb in jobs for s in jb.scratch]
    mid = (17 * steps) // 20

    def wrapped(*refs):
        c_in, refs = refs[:n_in], refs[n_in:]
        m_in, refs = refs[:len(j_in)], refs[len(j_in):]
        c_out, refs = refs[:n_out], refs[n_out:]
        m_out, refs = refs[:len(j_out)], refs[len(j_out):]
        c_sc, m_sc = refs[:n_sc], refs[n_sc:]
        bound, a, b, c = [], 0, 0, 0
        for jb in jobs:
            bound.append((jb, m_in[a:a + len(jb.ins)], m_out[b:b + len(jb.outs)], m_sc[c:c + len(jb.scratch)]))
            a, b, c = a + len(jb.ins), b + len(jb.outs), c + len(jb.scratch)
        i = pl.program_id(0) if len(grid) == 1 else pl.program_id(0) * grid[1] + pl.program_id(1)

        def phase(step, which):
            if jobs:
                @pl.when(i == step)
                def _():
                    for jb, ins, outs, sems in bound:
                        getattr(jb, which)(ins, outs, sems)

        phase(0, "start")
        if body is not None:
            body(*c_in, *c_out, *c_sc)
        phase(mid, "middle")
        phase(steps - 1, "finish")

    outs = pl.pallas_call(
        wrapped, name=name + "".join("_" + jb.tag for jb in jobs), grid=grid,
        in_specs=list(in_specs) + [ANY] * len(j_in), out_specs=list(out_specs) + [ANY] * len(j_out),
        out_shape=list(out_shape) + j_out, scratch_shapes=list(scratch) + j_sc, compiler_params=_params(len(grid)),
    )(*args, *j_in)
    pos = n_out
    for jb in jobs:
        jb.results = list(outs[pos:pos + len(jb.outs)])
        pos += len(jb.outs)
    return list(outs[:n_out])


def _run_jobs(name, jobs):
    _launch(name, None, 1, [], [], [], [], [], jobs)


def _sum_parts(got, name):
    n, rows, D = got.shape
    tr = _tile(rows, 512)
    order = [n - 1] + list(range(n - 1))

    def body(*refs):
        g = refs[0][...].astype(F32)
        for pr in refs[1:n]:
            g = g + pr[...].astype(F32)
        refs[-1][...] = g

    specs = [pl.BlockSpec((None, tr, D), functools.partial(lambda k, i: (k, i, 0), k)) for k in order]
    return pl.pallas_call(
        body, name=name, grid=(rows // tr,), in_specs=specs, out_specs=_rows(tr, D),
        out_shape=jax.ShapeDtypeStruct((rows, D), F32), compiler_params=_params(),
    )(*[got] * n)


def _adamw(name, entries, by_layer):
    n = len(entries)
    L = entries[0][0].shape[0]

    def body(*refs):
        ins, outs = refs[:4 * n], refs[4 * n:]
        for e in range(n):
            w_ref, g_ref, m_ref, v_ref = ins[4 * e:4 * e + 4]
            d_out, m_out, v_out = outs[3 * e:3 * e + 3]
            g = g_ref[...]
            m_new = ADAM_B1 * m_ref[...] + (1.0 - ADAM_B1) * g
            v_new = ADAM_B2 * v_ref[...] + (1.0 - ADAM_B2) * (g * g)
            m_hat = m_new / (1.0 - ADAM_B1 ** ADAM_STEP)
            v_hat = v_new / (1.0 - ADAM_B2 ** ADAM_STEP)
            d_out[...] = -ADAM_LR * (m_hat / (jnp.sqrt(v_hat) + ADAM_EPS) + ADAM_WD * w_ref[...])
            m_out[...] = m_new
            v_out[...] = v_new

    def spec(a):
        rest = (0,) * (a.ndim - 1)
        if by_layer:
            return pl.BlockSpec((None,) + a.shape[1:], lambda l: (l,) + rest)
        return pl.BlockSpec(a.shape, lambda l: (0,) + rest)

    flat = [a for e in entries for a in e]
    outs = pl.pallas_call(
        body, name=name, grid=(L if by_layer else 1,),
        in_specs=[spec(a) for a in flat], out_specs=[spec(e[0]) for e in entries for _ in range(3)],
        out_shape=[jax.ShapeDtypeStruct(e[0].shape, F32) for e in entries for _ in range(3)],
        compiler_params=_params(),
    )(*flat)
    return [tuple(outs[3 * e:3 * e + 3]) for e in range(n)]


def _tn_matmul(xa, ya, name, tmm, y_follows=False, jobs=()):
    T, M = xa.shape
    tn = tmm if y_follows else ya.shape[1]
    tt = _tile(T, 1024)
    nb = M // tmm

    def body(x_ref, y_ref, o_ref, acc):
        k = pl.program_id(1)

        @pl.when(k == 0)
        def _():
            acc[...] = jnp.zeros_like(acc)

        acc[...] += _tn(x_ref[...], y_ref[...])

        @pl.when(k == pl.num_programs(1) - 1)
        def _():
            o_ref[...] = acc[...].astype(o_ref.dtype)

    y_map = (lambda b, k: (k, b)) if y_follows else (lambda b, k: (k, 0))
    return _launch(
        name, body, (nb, T // tt), [xa, ya],
        [pl.BlockSpec((tt, tmm), lambda b, k: (k, b)), pl.BlockSpec((tt, tn), y_map)],
        [pl.BlockSpec((tmm, tn), lambda b, k: (b, 0))], [jax.ShapeDtypeStruct((M, tn), BF16)],
        [pltpu.VMEM((tmm, tn), F32)], jobs)[0]


def _ffn_fwd(h, gain, G, lay, jobs=()):
    T, D = h.shape
    FB = lay.FB
    FP = NDEV * FB
    offs = (0, FB, 2 * FB)
    tm = _tile(T, 512)

    def body(h_ref, gain_ref, g_hbm, ho_ref, go_ref, uo_ref, n_ref, wg, wu, wd, sems):
        @pl.when(pl.program_id(0) == 0)
        def _():
            _fetch(g_hbm, [(offs[0], FB, wg), (offs[1], FB, wu), (offs[2], FB, wd)], sems)

        h = h_ref[...]
        xh, _ = _rms(h)
        n = (xh * gain_ref[...]).astype(BF16)
        n_ref[...] = n

        def project(sl):
            return _nt(n, wg[sl, :]), _nt(n, wu[sl, :])

        def activate(sl, gu):
            g, u = gu
            go_ref[:, sl] = g.astype(BF16)
            uo_ref[:, sl] = u.astype(BF16)
            return (g * _sig(g) * u).astype(BF16)

        def contract(sl, a, acc):
            return acc + _nn(a, wd[sl, :])

        acc = _staggered(_hidden_chunks(FP), project, activate, contract, jnp.zeros((tm, D), F32))
        ho_ref[...] = h + 0.5 * acc

    return _launch(
        "ffn_fwd", body, T // tm, [h, gain, G],
        [_rows(tm, D), _const((1, D)), ANY], [_rows(tm, D), _rows(tm, FP), _rows(tm, FP), _rows(tm, D)],
        [jax.ShapeDtypeStruct((T, D), F32), jax.ShapeDtypeStruct((T, FP), BF16), jax.ShapeDtypeStruct((T, FP), BF16),
         jax.ShapeDtypeStruct((T, D), BF16)],
        [pltpu.VMEM((FP, D), BF16)] * 3 + [pltpu.SemaphoreType.DMA((3, NDEV))], jobs)


def _ffn_bwd(d, h, gain, ga, ua, G, lay, jobs=()):
    T, D = h.shape
    FB = lay.FB
    FP = NDEV * FB
    offs = (0, FB, 2 * FB)
    tm = _tile(T, 256)

    def body(d_ref, h_ref, gain_ref, ga_ref, ua_ref, g_hbm,
             do_ref, dg_ref, du_ref, a_ref, dh_ref, gg_ref, wg, wu, wd, sems):
        @pl.when(pl.program_id(0) == 0)
        def _():
            _fetch(g_hbm, [(offs[0], FB, wg), (offs[1], FB, wu), (offs[2], FB, wd)], sems)
            gg_ref[...] = jnp.zeros_like(gg_ref)

        d = d_ref[...]
        gain_v = gain_ref[...]
        xh, r = _rms(h_ref[...])
        dh = (0.5 * d).astype(BF16)
        dh_ref[...] = dh

        def project(sl):
            return _nt(dh, wd[sl, :])

        def activate(sl, da):
            g = ga_ref[:, sl].astype(F32)
            u = ua_ref[:, sl].astype(F32)
            s = _sig(g)
            silu = g * s
            a_ref[:, sl] = (silu * u).astype(BF16)
            dgv = (da * u * (s * (1.0 + g * (1.0 - s)))).astype(BF16)
            duv = (da * silu).astype(BF16)
            dg_ref[:, sl] = dgv
            du_ref[:, sl] = duv
            return dgv, duv

        def contract(sl, grads, acc):
            return acc + _nn(grads[0], wg[sl, :]) + _nn(grads[1], wu[sl, :])

        dn = _staggered(_hidden_chunks(FP), project, activate, contract, jnp.zeros((tm, D), F32))
        gg_ref[...] += _colsum8(dn * xh)
        do_ref[...] = d + _rms_bwd(dn, xh, r, gain_v)

    wide = jax.ShapeDtypeStruct((T, FP), BF16)
    return _launch(
        "ffn_bwd", body, T // tm, [d, h, gain, ga, ua, G],
        [_rows(tm, D), _rows(tm, D), _const((1, D)), _rows(tm, FP), _rows(tm, FP), ANY],
        [_rows(tm, D), _rows(tm, FP), _rows(tm, FP), _rows(tm, FP), _rows(tm, D), _const((8, D))],
        [jax.ShapeDtypeStruct((T, D), F32), wide, wide, wide, jax.ShapeDtypeStruct((T, D), BF16),
         jax.ShapeDtypeStruct((8, D), F32)],
        [pltpu.VMEM((FP, D), BF16)] * 3 + [pltpu.SemaphoreType.DMA((3, NDEV))], jobs)


def _inproj_fwd(h, gain, G, lay):
    T, D = h.shape
    NS = lay.NS
    NIN, CH = NDEV * NS, 2 * NS
    tm = _tile(T, 512)

    def body(h_ref, gain_ref, g_hbm, z_ref, win, sems):
        @pl.when(pl.program_id(0) == 0)
        def _():
            _fetch(g_hbm, [(lay.off["win"], NS, win)], sems)

        xh, _ = _rms(h_ref[...])
        n = (xh * gain_ref[...]).astype(BF16)
        for j in range(NIN // CH):
            sl = pl.ds(j * CH, CH)
            z_ref[:, sl] = _nt(n, win[sl, :]).astype(BF16)

    return pl.pallas_call(
        body, name="inproj_fwd", grid=(T // tm,),
        in_specs=[_rows(tm, D), _const((1, D)), ANY], out_specs=_rows(tm, NIN),
        out_shape=jax.ShapeDtypeStruct((T, NIN), BF16),
        scratch_shapes=[pltpu.VMEM((NIN, D), BF16), pltpu.SemaphoreType.DMA((1, NDEV))],
        compiler_params=_params(),
    )(h, gain, G)


def _conv_chunks(tm, D):
    rb, lc = min(tm, 64), min(D, 256)
    return [(r0, l0, rb, lc) for r0 in range(0, tm, rb) for l0 in range(0, D, lc)]


SUBLANES = 8


def _preshift(sh, n_rows):
    for r in range(1, SUBLANES):
        sh[r, pl.ds(0, n_rows), :] = sh[0, pl.ds(r, n_rows), :]


def _window(sh, start, rows, lanes):
    r = start % SUBLANES
    return sh[r, pl.ds(start - r, rows), lanes]


def _mixer_fwd(z, h, pool_w, pscale, cw, cb, lng, lnb, G, lay, jobs=()):
    T, D = h.shape
    PG, DS = lay.PG, lay.DS
    tm = _tile(T, 256)
    hb = tm // HALO

    def body(z_ref, zp_ref, h_ref, pw_ref, ps_ref, cw_ref, cb_ref, lg_ref, lb_ref, g_hbm,
             ho_ref, c1_ref, q_ref, cc_ref, pool_ref, ext_p, sh_c, a_s, wco, wout, sems):
        i = pl.program_id(0)

        @pl.when(i == 0)
        def _():
            _fetch(g_hbm, [(lay.off["wco"], DS, wco), (lay.off["wout"], DS, wout)], sems)

        live = jnp.where(i > 0, 1.0, 0.0).astype(F32)
        ext_p[pl.ds(0, HALO), :] = zp_ref[:, pl.ds(0, D)].astype(F32) * live
        ext_p[pl.ds(HALO, tm), :] = z_ref[:, pl.ds(0, D)].astype(F32)
        sh_c[0, pl.ds(0, HALO), :] = (zp_ref[:, pl.ds(D, D)].astype(F32)
                                      * _sig(zp_ref[:, pl.ds(2 * D, D)].astype(F32)) * live)
        sh_c[0, pl.ds(HALO, tm), :] = z_ref[:, pl.ds(D, D)].astype(F32) * _sig(z_ref[:, pl.ds(2 * D, D)].astype(F32))
        _preshift(sh_c, tm + HALO - SUBLANES)

        t = i * tm + lax.broadcasted_iota(jnp.int32, (tm, 1), 0)
        for g, w in enumerate(POOL_WINDOWS):
            sl = pl.ds(g * PG, PG)
            s = ext_p[pl.ds(HALO, tm), sl]
            zc = s
            for j in range(1, w):
                s = s + ext_p[pl.ds(HALO - j, tm), sl]
            inv = 1.0 / jnp.minimum(t + 1, w).astype(F32)
            pooled = (s * inv - zc).astype(BF16)
            pool_ref[:, sl] = pooled
            qv = _nn(pooled, pw_ref[g])
            q_ref[:, sl] = qv.astype(BF16)
            a_s[:, sl] = qv * ps_ref[:, sl]

        for r0, l0, rb, lc in _conv_chunks(tm, D):
            ls = pl.ds(l0, lc)
            acc = jnp.zeros((rb, lc), F32) + cb_ref[:, ls]
            for k in range(CONV_K):
                acc = acc + cw_ref[pl.ds(k, 1), ls] * _window(sh_c, r0 + HALO - (CONV_K - 1) + k, rb, ls)
            c1_ref[pl.ds(r0, rb), ls] = acc

        xhat, _ = _ln(c1_ref[...])
        c2 = xhat * lg_ref[...] + lb_ref[...]
        c3 = (c2 * _sig(c2)).astype(BF16)
        cc = _nn(c3, wco[...])
        cc_ref[...] = cc.astype(BF16)
        gp = z_ref[:, pl.ds(3 * D, D)].astype(F32)
        gc = z_ref[:, pl.ds(4 * D, D)].astype(F32)
        m = (_sig(gp) * a_s[...] + _sig(gc) * cc).astype(BF16)
        ho_ref[...] = h_ref[...] + _nn(m, wout[...])

    act = jax.ShapeDtypeStruct((T, D), BF16)
    return _launch(
        "mixer_fwd", body, T // tm, [z, z, h, pool_w, pscale, cw, cb, lng, lnb, G],
        [_rows(tm, 5 * D), pl.BlockSpec((HALO, 5 * D), lambda i: (jnp.maximum(i * hb - 1, 0), 0)),
         _rows(tm, D), _const((4, PG, PG)), _const((1, D)), _const((CONV_KP, D)), _const((1, D)),
         _const((1, D)), _const((1, D)), ANY],
        [_rows(tm, D)] * 5,
        [jax.ShapeDtypeStruct((T, D), F32), jax.ShapeDtypeStruct((T, D), F32), act, act, act],
        [pltpu.VMEM((HALO + tm, D), F32), pltpu.VMEM((SUBLANES, HALO + tm, D), F32), pltpu.VMEM((tm, D), F32),
         pltpu.VMEM((D, D), BF16), pltpu.VMEM((D, D), BF16), pltpu.SemaphoreType.DMA((2, NDEV))], jobs)


def _mixer_bwd_rows(d, z, c1, qa, cca, pool_w, pscale, lng, lnb, G, lay):
    T, D = d.shape
    PG, DS = lay.PG, lay.DS
    tm = _tile(T, 256)

    def body(d_ref, zgp_ref, zgc_ref, c1_ref, q_ref, cc_ref, pw_ref, ps_ref, lg_ref, lb_ref, g_hbm,
             m_ref, db_ref, dcc_ref, c3_ref, dq_ref, dpool_ref, dc1_ref, dzg_ref, gps_ref, glg_ref, glb_ref, gcb_ref,
             wco, wout, sems):
        @pl.when(pl.program_id(0) == 0)
        def _():
            _fetch(g_hbm, [(lay.off["wco"], DS, wco), (lay.off["wout"], DS, wout)], sems)
            for ref in (gps_ref, glg_ref, glb_ref, gcb_ref):
                ref[...] = jnp.zeros_like(ref)

        db = d_ref[...].astype(BF16)
        db_ref[...] = db
        dm = _nt(db, wout[...])
        q = q_ref[...].astype(F32)
        cc = cc_ref[...].astype(F32)
        ps = ps_ref[...]
        sp = _sig(zgp_ref[...].astype(F32))
        sc = _sig(zgc_ref[...].astype(F32))
        a = q * ps
        m_ref[...] = (sp * a + sc * cc).astype(BF16)
        da = dm * sp
        dzg_ref[:, pl.ds(0, D)] = (dm * a * sp * (1.0 - sp)).astype(BF16)
        dzg_ref[:, pl.ds(D, D)] = (dm * cc * sc * (1.0 - sc)).astype(BF16)
        gps_ref[...] += _colsum8(da * q)
        dq = (da * ps).astype(BF16)
        dq_ref[...] = dq
        for g in range(len(POOL_WINDOWS)):
            sl = pl.ds(g * PG, PG)
            dpool_ref[:, sl] = _nt(dq_ref[:, sl], pw_ref[g])

        dcc = (dm * sc).astype(BF16)
        dcc_ref[...] = dcc
        xhat, rstd = _ln(c1_ref[...])
        lg = lg_ref[...]
        c2 = xhat * lg + lb_ref[...]
        s2 = _sig(c2)
        c3_ref[...] = (c2 * s2).astype(BF16)
        dc2 = _nt(dcc, wco[...]) * (s2 * (1.0 + c2 * (1.0 - s2)))
        glg_ref[...] += _colsum8(dc2 * xhat)
        glb_ref[...] += _colsum8(dc2)
        dxh = dc2 * lg
        dc1 = rstd * (dxh - jnp.mean(dxh, axis=-1, keepdims=True)
                      - xhat * jnp.mean(dxh * xhat, axis=-1, keepdims=True))
        dc1_ref[...] = dc1
        gcb_ref[...] += _colsum8(dc1)

    act = jax.ShapeDtypeStruct((T, D), BF16)
    full = jax.ShapeDtypeStruct((T, D), F32)
    vec = jax.ShapeDtypeStruct((8, D), F32)
    return pl.pallas_call(
        body, name="mixer_bwd_rows", grid=(T // tm,),
        in_specs=[_rows(tm, D), pl.BlockSpec((tm, D), lambda i: (i, 3)), pl.BlockSpec((tm, D), lambda i: (i, 4)),
                  _rows(tm, D), _rows(tm, D), _rows(tm, D),
                  _const((4, PG, PG)), _const((1, D)), _const((1, D)), _const((1, D)), ANY],
        out_specs=[_rows(tm, D)] * 7 + [_rows(tm, 2 * D)] + [_const((8, D))] * 4,
        out_shape=[act, act, act, act, act, full, full, jax.ShapeDtypeStruct((T, 2 * D), BF16), vec, vec, vec, vec],
        scratch_shapes=[pltpu.VMEM((D, D), BF16), pltpu.VMEM((D, D), BF16), pltpu.SemaphoreType.DMA((2, NDEV))],
        compiler_params=_params(),
    )(d, z, z, c1, qa, cca, pool_w, pscale, lng, lnb, G)


def _mixer_bwd_time(dc1, dpool, z, cw, lay, jobs=()):
    T, D = dc1.shape
    PG = lay.PG
    tm = _tile(T, 256)
    hb = tm // HALO
    nt = T // tm

    def body(dc_ref, dcn_ref, dp_ref, dpn_ref, z_ref, zp_ref, cw_ref, dz_ref, gcw_ref, sh_d, ext_q, sh_c, dc0_s):
        i = pl.program_id(0)

        @pl.when(i == 0)
        def _():
            gcw_ref[...] = jnp.zeros_like(gcw_ref)

        live_p = jnp.where(i > 0, 1.0, 0.0).astype(F32)
        live_n = jnp.where(i < nt - 1, 1.0, 0.0).astype(F32)
        sh_d[0, pl.ds(0, tm), :] = dc_ref[...]
        sh_d[0, pl.ds(tm, HALO), :] = dcn_ref[...] * live_n
        _preshift(sh_d, tm + HALO - SUBLANES)
        zg = z_ref[:, pl.ds(2 * D, D)].astype(F32)
        za = z_ref[:, pl.ds(D, D)].astype(F32)
        sg = _sig(zg)
        sh_c[0, pl.ds(0, HALO), :] = (zp_ref[:, pl.ds(D, D)].astype(F32)
                                      * _sig(zp_ref[:, pl.ds(2 * D, D)].astype(F32)) * live_p)
        sh_c[0, pl.ds(HALO, tm), :] = za * sg
        _preshift(sh_c, tm + HALO - SUBLANES)

        t = i * tm + lax.broadcasted_iota(jnp.int32, (tm, 1), 0)
        tn = (i + 1) * tm + lax.broadcasted_iota(jnp.int32, (HALO, 1), 0)
        for g, w in enumerate(POOL_WINDOWS):
            sl = pl.ds(g * PG, PG)
            ext_q[pl.ds(0, tm), sl] = dp_ref[:, sl] * (1.0 / jnp.minimum(t + 1, w).astype(F32))
            ext_q[pl.ds(tm, HALO), sl] = dpn_ref[:, sl] * (live_n / jnp.minimum(tn + 1, w).astype(F32))
        for g, w in enumerate(POOL_WINDOWS):
            sl = pl.ds(g * PG, PG)
            s = ext_q[pl.ds(0, tm), sl]
            for j in range(1, w):
                s = s + ext_q[pl.ds(j, tm), sl]
            dz_ref[:, sl] = (s - dp_ref[:, sl]).astype(BF16)

        for r0, l0, rb, lc in _conv_chunks(tm, D):
            ls = pl.ds(l0, lc)
            acc = jnp.zeros((rb, lc), F32)
            for j in range(CONV_K):
                acc = acc + cw_ref[pl.ds(CONV_K - 1 - j, 1), ls] * _window(sh_d, r0 + j, rb, ls)
            dc0_s[pl.ds(r0, rb), ls] = acc
        dc0 = dc0_s[...]
        dz_ref[:, pl.ds(D, D)] = (dc0 * sg).astype(BF16)
        dz_ref[:, pl.ds(2 * D, D)] = (dc0 * za * sg * (1.0 - sg)).astype(BF16)

        for r0, l0, rb, lc in _conv_chunks(tm, D):
            ls = pl.ds(l0, lc)
            dcv = dc_ref[pl.ds(r0, rb), ls]
            for k in range(CONV_K):
                gcw_ref[pl.ds(8 * k, 8), ls] += _colsum8(dcv * _window(sh_c, r0 + HALO - (CONV_K - 1) + k, rb, ls))

    nxt = lambda i: (jnp.minimum((i + 1) * hb, T // HALO - 1), 0)
    return _launch(
        "mixer_bwd_time", body, nt, [dc1, dc1, dpool, dpool, z, z, cw],
        [_rows(tm, D), pl.BlockSpec((HALO, D), nxt), _rows(tm, D), pl.BlockSpec((HALO, D), nxt),
         _rows(tm, 5 * D), pl.BlockSpec((HALO, 5 * D), lambda i: (jnp.maximum(i * hb - 1, 0), 0)),
         _const((CONV_KP, D))],
        [_rows(tm, 3 * D), _const((CONV_KP * 8, D))],
        [jax.ShapeDtypeStruct((T, 3 * D), BF16), jax.ShapeDtypeStruct((CONV_KP * 8, D), F32)],
        [pltpu.VMEM((SUBLANES, tm + HALO, D), F32), pltpu.VMEM((tm + HALO, D), F32),
         pltpu.VMEM((SUBLANES, HALO + tm, D), F32), pltpu.VMEM((tm, D), F32)], jobs)


def _inproj_bwd(d, h, gain, dzm, dzg, G, lay):
    T, D = h.shape
    NS = lay.NS
    NIN = NDEV * NS
    tm = _tile(T, 512)

    def body(d_ref, h_ref, gain_ref, dzm_ref, dzg_ref, g_hbm, do_ref, u_ref, gg_ref, win, sems):
        @pl.when(pl.program_id(0) == 0)
        def _():
            _fetch(g_hbm, [(lay.off["win"], NS, win)], sems)
            gg_ref[...] = jnp.zeros_like(gg_ref)

        gain_v = gain_ref[...]
        xh, r = _rms(h_ref[...])
        u_ref[...] = (xh * gain_v).astype(BF16)
        dn = jnp.zeros((tm, D), F32)
        for j in range(3):
            dn = dn + _nn(dzm_ref[:, pl.ds(j * D, D)], win[pl.ds(j * D, D), :])
        for j in range(2):
            dn = dn + _nn(dzg_ref[:, pl.ds(j * D, D)], win[pl.ds((3 + j) * D, D), :])
        gg_ref[...] += _colsum8(dn * xh)
        do_ref[...] = d_ref[...] + _rms_bwd(dn, xh, r, gain_v)

    return pl.pallas_call(
        body, name="inproj_bwd", grid=(T // tm,),
        in_specs=[_rows(tm, D), _rows(tm, D), _const((1, D)), _rows(tm, 3 * D), _rows(tm, 2 * D), ANY],
        out_specs=[_rows(tm, D), _rows(tm, D), _const((8, D))],
        out_shape=[jax.ShapeDtypeStruct((T, D), F32), jax.ShapeDtypeStruct((T, D), BF16),
                   jax.ShapeDtypeStruct((8, D), F32)],
        scratch_shapes=[pltpu.VMEM((NIN, D), BF16), pltpu.SemaphoreType.DMA((1, NDEV))],
        compiler_params=_params(),
    )(d, h, gain, dzm, dzg, G)


def _ple_fwd(h, pe, gain, wppt, G, lay, head=None):
    T, D = h.shape
    PD, DS = lay.PD, lay.DS
    tm = _tile(T, 512)

    def body(*refs):
        if head is None:
            h_ref, p_ref, gain_ref, wpp_ref, g_hbm, ho_ref, gate_ref, wpg, sems = refs
        else:
            (h_ref, p_ref, gain_ref, wpp_ref, g_hbm, t_ref, fgain_ref,
             do_ref, gate_ref, loss_ref, fg_ref, wpg, sems) = refs

        @pl.when(pl.program_id(0) == 0)
        def _():
            _fetch(g_hbm, [(lay.off["wpg"], DS, wpg)], sems)
            if head is not None:
                loss_ref[...] = jnp.zeros_like(loss_ref)
                fg_ref[...] = jnp.zeros_like(fg_ref)

        h = h_ref[...]
        xh, _ = _rms(h)
        n = (xh * gain_ref[...]).astype(BF16)
        gate = _sig(_nn(n, wpg[...]))
        gate_ref[...] = gate.astype(BF16)
        e = _nt(p_ref[...].astype(BF16), wpp_ref[...])
        out = h + gate * e
        if head is None:
            ho_ref[...] = out
        else:
            fgain = fgain_ref[...]
            yh, r = _rms(out)
            err = yh * fgain - t_ref[...]
            loss_ref[...] += _colsum8(err * err)
            dy = err * (1.0 / D)
            fg_ref[...] += _colsum8(dy * yh)
            do_ref[...] = _rms_bwd(dy, yh, r, fgain)

    args = [h, pe, gain, wppt, G]
    in_specs = [_rows(tm, D), _rows(tm, PD), _const((1, D)), _const((D, PD)), ANY]
    out_specs = [_rows(tm, D), _rows(tm, D)]
    out_shape = [jax.ShapeDtypeStruct((T, D), F32), jax.ShapeDtypeStruct((T, D), BF16)]
    if head is not None:
        args += list(head)
        in_specs += [_rows(tm, D), _const((1, D))]
        out_specs += [_const((8, D))] * 2
        out_shape += [jax.ShapeDtypeStruct((8, D), F32)] * 2
    return pl.pallas_call(
        body, name="ple_fwd" if head is None else "ple_fwd_head", grid=(T // tm,),
        in_specs=in_specs, out_specs=out_specs, out_shape=out_shape,
        scratch_shapes=[pltpu.VMEM((D, D), BF16), pltpu.SemaphoreType.DMA((1, NDEV))],
        compiler_params=_params(),
    )(*args)


def _ple_bwd(d, h, pe, gate_a, gain, wppt, G, lay):
    T, D = h.shape
    PD, DS = lay.PD, lay.DS
    tm = _tile(T, 512)

    def body(d_ref, h_ref, p_ref, gate_ref, gain_ref, wpp_ref, g_hbm,
             do_ref, dpre_ref, de_ref, n_ref, pb_ref, gg_ref, wpg, sems):
        @pl.when(pl.program_id(0) == 0)
        def _():
            _fetch(g_hbm, [(lay.off["wpg"], DS, wpg)], sems)
            gg_ref[...] = jnp.zeros_like(gg_ref)

        d = d_ref[...]
        gain_v = gain_ref[...]
        xh, r = _rms(h_ref[...])
        n_ref[...] = (xh * gain_v).astype(BF16)
        pb = p_ref[...].astype(BF16)
        pb_ref[...] = pb
        e = _nt(pb, wpp_ref[...])
        gate = gate_ref[...].astype(F32)
        de_ref[...] = (d * gate).astype(BF16)
        dpre = (d * e * gate * (1.0 - gate)).astype(BF16)
        dpre_ref[...] = dpre
        dn = _nt(dpre, wpg[...])
        gg_ref[...] += _colsum8(dn * xh)
        do_ref[...] = d + _rms_bwd(dn, xh, r, gain_v)

    act = jax.ShapeDtypeStruct((T, D), BF16)
    return pl.pallas_call(
        body, name="ple_bwd", grid=(T // tm,),
        in_specs=[_rows(tm, D), _rows(tm, D), _rows(tm, PD), _rows(tm, D), _const((1, D)), _const((D, PD)), ANY],
        out_specs=[_rows(tm, D), _rows(tm, D), _rows(tm, D), _rows(tm, D), _rows(tm, PD), _const((8, D))],
        out_shape=[jax.ShapeDtypeStruct((T, D), F32), act, act, act, jax.ShapeDtypeStruct((T, PD), BF16),
                   jax.ShapeDtypeStruct((8, D), F32)],
        scratch_shapes=[pltpu.VMEM((D, D), BF16), pltpu.SemaphoreType.DMA((1, NDEV))],
        compiler_params=_params(),
    )(d, h, pe, gate_a, gain, wppt, G)


_BIG = ["ffn1_w_gate", "ffn1_w_up", "ffn1_w_down", "w_in", "pool_w", "conv_w_out", "w_out", "ffn2_w_gate",
        "ffn2_w_up", "ffn2_w_down", "ple_w_gate", "ple_w_proj"]
_VECS = ["ffn1_norm", "mix_norm", "pool_scale", "conv_dw_b", "conv_ln_g", "conv_ln_b", "ffn2_norm", "ple_norm"]
_WEIGHTS = ["ffn1_norm", "ffn1_w_gate", "ffn1_w_up", "ffn1_w_down", "mix_norm", "w_in", "pool_w", "pool_scale",
            "conv_dw_w", "conv_dw_b", "conv_ln_g", "conv_ln_b", "conv_w_out", "w_out", "ffn2_norm", "ffn2_w_gate",
            "ffn2_w_up", "ffn2_w_down", "ple_norm", "ple_w_gate", "ple_w_proj", "final_norm"]


def _step(x, p, loss_target, w, mom, var):
    T, D = x.shape[1], x.shape[2]
    L = p.shape[0]
    FS, NS = w["ffn1_w_gate"].shape[2], w["w_in"].shape[2]
    PD = p.shape[3]
    PGS, PG = w["pool_w"].shape[2], w["pool_w"].shape[3]
    CS = w["conv_dw_w"].shape[2]
    lay = _Layout(D, FS, NS, PD, PG, PGS)
    FB = lay.FB
    ax, ay, ac = _place()
    me = 4 * ax + 2 * ay + ac

    assert L == 2, "the exchange schedule below is written for two layers"
    gather = {(l, pc): _GatherJob(f"ag{pc}", _pack_piece(w, l, lay, pc, BF16)) for l in range(L) for pc in PIECES}
    fwd_jobs = {("ffn1", 0): [gather[0, "C"], gather[0, "D"]], ("mixer", 0): [gather[0, "B"]],
                ("ffn2", 0): [gather[1, "A"], gather[1, "D"]], ("ffn1", 1): [gather[1, "C"]],
                ("mixer", 1): [gather[1, "B"]]}
    cw_mine = jnp.pad(w["conv_dw_w"], ((0, 0), (0, CONV_KP - CONV_K), (0, 0)))
    taps_job = _RowsGatherJob("agW", cw_mine.reshape(L * CONV_KP * CS // D, D))
    _run_jobs("gather_first", [gather[0, "A"], taps_job])
    cw_full = taps_job.results[0].reshape(NDEV, L, CONV_KP, CS).transpose(1, 2, 0, 3).reshape(L, CONV_KP, D)

    def gathered(l, pc):
        return gather[l, pc].results[0]

    def small_mats(l):
        G = gathered(l, "D")
        wppt = G[:, lay.off["wpp"]:lay.off["wpp"] + lay.rows["wpp"]].reshape(D, PD)
        pw = G[:, lay.off["pool"]:lay.off["pool"] + lay.rows["pool"]].reshape(NDEV, 4, PGS, PG)
        return wppt, pw.transpose(1, 0, 2, 3).reshape(4, PG, PG)

    def vec(name, l):
        return w[name][l].reshape(1, D)

    h = x[0]
    saved = []
    for l in range(L):
        s = {"h0": h}
        h, s["g1"], s["u1"], s["n1"] = _ffn_fwd(h, vec("ffn1_norm", l), gathered(l, "A"), lay,
                                                fwd_jobs.get(("ffn1", l), ()))
        wppt, pw = small_mats(l)
        s["wppt"], s["pw"], s["h1"] = wppt, pw, h
        s["z"] = _inproj_fwd(h, vec("mix_norm", l), gathered(l, "C"), lay)
        h, s["c1"], s["q"], s["cc"], s["pooled"] = _mixer_fwd(
            s["z"], h, pw, vec("pool_scale", l), cw_full[l], vec("conv_dw_b", l), vec("conv_ln_g", l),
            vec("conv_ln_b", l), gathered(l, "D"), lay, fwd_jobs.get(("mixer", l), ()))
        s["h2"] = h
        h, s["g2"], s["u2"], s["n2"] = _ffn_fwd(h, vec("ffn2_norm", l), gathered(l, "B"), lay,
                                                fwd_jobs.get(("ffn2", l), ()))
        s["h3"] = h
        if l + 1 < L:
            h, s["gate"] = _ple_fwd(h, p[l, 0], vec("ple_norm", l), wppt, gathered(l, "D"), lay)
        else:
            d, s["gate"], loss_part, g_final = _ple_fwd(h, p[l, 0], vec("ple_norm", l), wppt, gathered(l, "D"), lay,
                                                         head=(loss_target[0], w["final_norm"].reshape(1, D)))
        saved.append(s)

    scatter = {}
    small_parts = [None] * L

    def send(l, pc, blocks):
        scatter[l, pc] = _ScatterJob(f"rs{pc}", [blocks[n].reshape(NDEV, lay.rows[n], D) for n, _ in lay.pieces[pc]])
        return scatter[l, pc]

    def small_rows():
        return _fold_rows([v for l in range(L) for v in small_parts[l]] + [g_final, loss_part])

    held = []
    for l in reversed(range(L)):
        s = saved[l]
        wppt, pw = s["wppt"], s["pw"]
        g = {}
        d, dpre, de, n_ple, pb, g_ple = _ple_bwd(d, s["h3"], p[l, 0], s["gate"], vec("ple_norm", l), wppt,
                                                 gathered(l, "D"), lay)
        g["wpg"] = _tn_matmul(n_ple, dpre, "tn_sq", D)
        g["wpp"] = _tn_matmul(de, pb, "tn_proj", D)

        n1, n2 = s["n1"], s["n2"]
        d, dg2, du2, a2, dh2, g_n2 = _ffn_bwd(d, s["h2"], vec("ffn2_norm", l), s["g2"], s["u2"],
                                                  gathered(l, "B"), lay, held)
        held = []
        g["g2"] = _tn_matmul(dg2, n2, "tn_ffn", NDEV * FB)
        g["u2"] = _tn_matmul(du2, n2, "tn_ffn", NDEV * FB)
        g["d2"] = _tn_matmul(a2, dh2, "tn_ffn", NDEV * FB)
        held.append(send(l, "B", g))

        (m_b, d_b, dcc, c3, dq, dpool, dc1, dzg, g_ps, g_lg, g_lb, g_cb) = _mixer_bwd_rows(
            d, s["z"], s["c1"], s["q"], s["cc"], pw, vec("pool_scale", l), vec("conv_ln_g", l), vec("conv_ln_b", l),
            gathered(l, "D"), lay)
        g["wout"] = _tn_matmul(m_b, d_b, "tn_sq", D)
        g["wco"] = _tn_matmul(c3, dcc, "tn_sq", D)
        g_pool = _tn_matmul(s["pooled"], dq, "tn_pool", PG, y_follows=True)
        g["pool"] = g_pool.reshape(4, NDEV, PGS, PG).transpose(1, 0, 2, 3)
        held.append(send(l, "D", g))
        dzm, g_cw = _mixer_bwd_time(dc1, dpool, s["z"], cw_full[l], lay, held)
        held = []
        d, u_b, g_mix = _inproj_bwd(d, s["h1"], vec("mix_norm", l), dzm, dzg, gathered(l, "C"), lay)
        g["win"] = jnp.concatenate([_tn_matmul(dzm, u_b, "tn_in3", 3 * D // 2),
                                    _tn_matmul(dzg, u_b, "tn_in2", D)], axis=0)
        held.append(send(l, "C", g))

        d, dg1, du1, a1, dh1, g_n1 = _ffn_bwd(d, s["h0"], vec("ffn1_norm", l), s["g1"], s["u1"],
                                                  gathered(l, "A"), lay, held)
        held = []
        small_parts[l] = [g_n1, g_mix, g_ps, g_cb, g_lg, g_lb, g_n2, g_ple, g_cw]
        if l > 0:
            g["g1"] = _tn_matmul(dg1, n1, "tn_ffn", NDEV * FB)
            g["u1"] = _tn_matmul(du1, n1, "tn_ffn", NDEV * FB)
            g["d1"] = _tn_matmul(a1, dh1, "tn_ffn", NDEV * FB)
            held.append(send(l, "A", g))
        else:
            rows_job = _RowsGatherJob("agS", small_rows())
            g["g1"] = _tn_matmul(dg1, n1, "tn_ffn", NDEV * FB, jobs=[rows_job])
            g["u1"] = _tn_matmul(du1, n1, "tn_ffn", NDEV * FB, jobs=[send(l, "A1", g)])
            g["d1"] = _tn_matmul(a1, dh1, "tn_ffn", NDEV * FB, jobs=[send(l, "A2", g)])
            held.append(send(l, "A3", g))
    _run_jobs("scatter_last", held)
    grad_x = d[None]

    per = [{} for _ in range(L)]
    for (l, pc), job in scatter.items():
        per[l].update(_unpack_piece(_sum_parts(job.results[0], "sum_" + pc), lay, pc))
    grads = {n: jnp.stack([per[l][n] for l in range(L)]) for n in _BIG}

    small_sum = _sum_slots(rows_job.results[0])
    per_layer = len(_VECS) + CONV_KP
    for k, n in enumerate(_VECS):
        grads[n] = jnp.stack([small_sum[l * per_layer + k] for l in range(L)])
    g_cw_full = jnp.stack([small_sum[l * per_layer + len(_VECS):l * per_layer + len(_VECS) + CONV_K]
                           for l in range(L)])
    grads["conv_dw_w"] = lax.dynamic_slice_in_dim(g_cw_full, me * CS, CS, axis=2)
    grads["final_norm"] = small_sum[L * per_layer]
    loss = (0.5 / D) * jnp.sum(small_sum[L * per_layer + 1])

    def as2(a):
        return a.reshape(1, -1) if a.ndim == 1 else a

    deltas, new_m, new_v = {}, {}, {}

    def update(name, names, by_layer):
        res = _adamw(name, [(as2(w[n]), as2(grads[n]), as2(mom[n]), as2(var[n])) for n in names], by_layer)
        for n, (delta, m_new, v_new) in zip(names, res):
            deltas[n], new_m[n], new_v[n] = (a.reshape(w[n].shape) for a in (delta, m_new, v_new))

    for n in ("ffn1_w_gate", "ffn1_w_up", "ffn1_w_down", "ffn2_w_gate", "ffn2_w_up", "ffn2_w_down", "w_in"):
        update("adamw_" + n, [n], True)
    update("adamw_mid", ["conv_w_out", "w_out", "ple_w_gate", "ple_w_proj", "pool_w"], True)
    update("adamw_small", _VECS + ["conv_dw_w", "final_norm"], False)
    return loss, grad_x, (grads, deltas, new_m, new_v)


def kernel(x, p, ffn1_norm, ffn1_w_gate, ffn1_w_up, ffn1_w_down, mix_norm, w_in, pool_w, pool_scale, conv_dw_w, conv_dw_b, conv_ln_g, conv_ln_b, conv_w_out, w_out, ffn2_norm, ffn2_w_gate, ffn2_w_up, ffn2_w_down, ple_norm, ple_w_gate, ple_w_proj, final_norm, loss_target, m_ffn1_norm, m_ffn1_w_gate, m_ffn1_w_up, m_ffn1_w_down, m_mix_norm, m_w_in, m_pool_w, m_pool_scale, m_conv_dw_w, m_conv_dw_b, m_conv_ln_g, m_conv_ln_b, m_conv_w_out, m_w_out, m_ffn2_norm, m_ffn2_w_gate, m_ffn2_w_up, m_ffn2_w_down, m_ple_norm, m_ple_w_gate, m_ple_w_proj, m_final_norm, v_ffn1_norm, v_ffn1_w_gate, v_ffn1_w_up, v_ffn1_w_down, v_mix_norm, v_w_in, v_pool_w, v_pool_scale, v_conv_dw_w, v_conv_dw_b, v_conv_ln_g, v_conv_ln_b, v_conv_w_out, v_w_out, v_ffn2_norm, v_ffn2_w_gate, v_ffn2_w_up, v_ffn2_w_down, v_ple_norm, v_ple_w_gate, v_ple_w_proj, v_final_norm):
    given = dict(locals())
    w = {n: given[n] for n in _WEIGHTS}
    mom = {n: given["m_" + n] for n in _WEIGHTS}
    var = {n: given["v_" + n] for n in _WEIGHTS}
    loss, grad_x, (grads, deltas, new_m, new_v) = _step(x, p, loss_target, w, mom, var)
    out = [loss, grad_x]
    for res in (grads, deltas, new_m, new_v):
        out += [res[n] for n in _WEIGHTS]
    return tuple(out)
```

```python
import functools

import jax
import jax.numpy as jnp
from jax import lax
from jax.experimental import pallas as pl
from jax.experimental.pallas import tpu as pltpu

F32, BF16 = jnp.float32, jnp.bfloat16
NDEV = 8
MESH = pl.DeviceIdType.MESH
HALO = 32
POOL_WINDOWS = (2, 4, 8, 16)
CONV_K = 31
CONV_KP = 32
RMS_EPS, LN_EPS = 1e-6, 1e-5
ADAM_LR, ADAM_B1, ADAM_B2, ADAM_EPS, ADAM_WD, ADAM_STEP = 0.001, 0.9, 0.999, 1e-08, 0.01, 10
LANE = 128
VMEM_LIMIT = 56 * 1024 * 1024
ANY = pl.BlockSpec(memory_space=pl.ANY)


def _nn(a, b):
    return jnp.dot(a, b, preferred_element_type=F32)


def _nt(a, b):
    return lax.dot_general(a, b, (((1,), (1,)), ((), ())), preferred_element_type=F32)


def _tn(a, b):
    return lax.dot_general(a, b, (((0,), (0,)), ((), ())), preferred_element_type=F32)


def _colsum8(v):
    return jnp.sum(v.reshape(v.shape[0] // 8, 8, v.shape[1]), axis=0)


def _sig(v):
    return jax.nn.sigmoid(v)


def _rms(h):
    r = lax.rsqrt(jnp.mean(h * h, axis=-1, keepdims=True) + RMS_EPS)
    return h * r, r


def _rms_bwd(dn, xh, r, gain):
    dxh = dn * gain
    return r * (dxh - xh * jnp.mean(dxh * xh, axis=-1, keepdims=True))


def _ln(c1):
    mu = jnp.mean(c1, axis=-1, keepdims=True)
    cen = c1 - mu
    rstd = lax.rsqrt(jnp.mean(cen * cen, axis=-1, keepdims=True) + LN_EPS)
    return cen * rstd, rstd


def _rows(tm, c):
    return pl.BlockSpec((tm, c), lambda i: (i, 0))


def _const(shape):
    return pl.BlockSpec(shape, lambda i: (0,) * len(shape))


def _params(n_grid=1):
    return pltpu.CompilerParams(dimension_semantics=("arbitrary",) * n_grid, vmem_limit_bytes=VMEM_LIMIT)


BF16_ROWS = 16
MXU_CHUNK = 768


def _tile(n, want):
    if n <= want:
        return n
    return max(t for t in range(BF16_ROWS, want + 1, BF16_ROWS) if n % t == 0)


def _hidden_chunks(width):
    return [(o, min(MXU_CHUNK, width - o)) for o in range(0, width, MXU_CHUNK)]


def _staggered(chunks, project, activate, contract, acc):
    spans = [pl.ds(start, width) for start, width in chunks]
    ahead = project(spans[0])
    for c, span in enumerate(spans):
        projected = ahead
        if c + 1 < len(spans):
            ahead = project(spans[c + 1])
        acc = contract(span, activate(span, projected), acc)
    return acc


def _fetch(g_hbm, specs, sems):
    cps = []
    for wi, (off, rows, dst) in enumerate(specs):
        for dev in range(NDEV):
            cps.append(pltpu.make_async_copy(g_hbm.at[dev, pl.ds(off, rows), :],
                                             dst.at[pl.ds(dev * rows, rows), :], sems.at[wi, dev]))
    for cp in cps:
        cp.start()
    for cp in cps:
        cp.wait()


PIECES = ("A", "B", "C", "D")


class _Layout:
    def __init__(self, D, FS, NS, PD, PG, PGS):
        self.D, self.FS, self.NS, self.PD, self.PG, self.PGS = D, FS, NS, PD, PG, PGS
        assert FS % BF16_ROWS == 0 and (NDEV * FS) % (2 * LANE) == 0, "FFN shard rows must tile as bf16 row blocks"
        self.FB = FS
        self.DS = D // NDEV
        self.pieces = {
            "A": [("g1", self.FB), ("u1", self.FB), ("d1", self.FB)],
            "B": [("g2", self.FB), ("u2", self.FB), ("d2", self.FB)],
            "C": [("win", NS)],
            "D": [("wco", self.DS), ("wout", self.DS), ("wpg", self.DS), ("wpp", self.DS * PD // D),
                  ("pool", 4 * PGS * PG // D)],
            "A1": [("g1", self.FB)], "A2": [("u1", self.FB)], "A3": [("d1", self.FB)]}
        self.off, self.rows = {}, {}
        for pc in PIECES:
            o = 0
            for n, r in self.pieces[pc]:
                self.off[n], self.rows[n] = o, r
                o += r


def _pack_piece(w, l, lay, pc, dtype):
    D = lay.D
    make = {"g1": lambda: w["ffn1_w_gate"][l].T, "u1": lambda: w["ffn1_w_up"][l].T,
            "d1": lambda: w["ffn1_w_down"][l], "g2": lambda: w["ffn2_w_gate"][l].T,
            "u2": lambda: w["ffn2_w_up"][l].T, "d2": lambda: w["ffn2_w_down"][l],
            "win": lambda: w["w_in"][l].T, "wco": lambda: w["conv_w_out"][l], "wout": lambda: w["w_out"][l],
            "wpg": lambda: w["ple_w_gate"][l], "wpp": lambda: w["ple_w_proj"][l].T.reshape(-1, D),
            "pool": lambda: w["pool_w"][l].reshape(-1, D)}
    return jnp.concatenate([make[n]() for n, _ in lay.pieces[pc]], axis=0).astype(dtype)


def _unpack_piece(slab, lay, pc):
    PD, PG, PGS, DS = lay.PD, lay.PG, lay.PGS, lay.DS
    undo = {"g1": ("ffn1_w_gate", lambda a: a.T), "u1": ("ffn1_w_up", lambda a: a.T),
            "d1": ("ffn1_w_down", lambda a: a), "g2": ("ffn2_w_gate", lambda a: a.T),
            "u2": ("ffn2_w_up", lambda a: a.T), "d2": ("ffn2_w_down", lambda a: a),
            "win": ("w_in", lambda a: a.T), "wco": ("conv_w_out", lambda a: a), "wout": ("w_out", lambda a: a),
            "wpg": ("ple_w_gate", lambda a: a), "wpp": ("ple_w_proj", lambda a: a.reshape(DS, PD).T),
            "pool": ("pool_w", lambda a: a.reshape(4, PGS, PG))}
    out, o = {}, 0
    for n, r in lay.pieces[pc]:
        name, fn = undo[n]
        out[name] = fn(slab[o:o + r])
        o += r
    return out


def _place():
    return lax.axis_index("x"), lax.axis_index("y"), lax.axis_index("c")


FLIPS = [(a, b, d) for a in (0, 1) for b in (0, 1) for d in (0, 1)][1:]


class _GatherJob:
    def __init__(self, tag, slab):
        self.tag, self.ins = tag, [slab]
        self.outs = [jax.ShapeDtypeStruct((NDEV,) + slab.shape, slab.dtype)]
        self.scratch = [pltpu.SemaphoreType.DMA((7,)), pltpu.SemaphoreType.DMA((7,)), pltpu.SemaphoreType.DMA]
        self.results = None

    def _plan(self, ins, outs, sems):
        (x_ref,), (out_ref,), (send_sems, recv_sems, local_sem) = ins, outs, sems
        x, y, c = _place()
        me, sibling = (x, y, c), (x, y, 1 - c)
        chips = [(1 - x, y), (x, 1 - y), (1 - x, 1 - y)]

        def rows(px, py, pc):
            return out_ref.at[4 * px + 2 * py + pc]

        def copy(k, block, to, src=None):
            return pltpu.make_async_remote_copy(
                src_ref=rows(*block) if src is None else src, dst_ref=rows(*block),
                send_sem=send_sems.at[k], recv_sem=recv_sems.at[k], device_id=to, device_id_type=MESH)

        mine = pltpu.make_async_copy(x_ref, rows(*me), local_sem)
        first = [copy(0, me, sibling, src=x_ref)]
        first += [copy(1 + j, me, (*chip, c), src=x_ref) for j, chip in enumerate(chips)]
        passed = [copy(4 + j, (*chip, c), sibling) for j, chip in enumerate(chips)]
        landed = [copy(1 + j, (*chip, c), me) for j, chip in enumerate(chips)]
        late = [copy(0, sibling, me)] + [copy(4 + j, (*chip, 1 - c), me) for j, chip in enumerate(chips)]
        return mine, first, passed, landed, late

    def start(self, ins, outs, sems):
        mine, first, _, _, _ = self._plan(ins, outs, sems)
        mine.start()
        for cp in first:
            cp.start()

    def middle(self, ins, outs, sems):
        _, _, passed, landed, _ = self._plan(ins, outs, sems)
        for got, cp in zip(landed, passed):
            got.wait_recv()
            cp.start()

    def finish(self, ins, outs, sems):
        mine, first, passed, _, late = self._plan(ins, outs, sems)
        for got in late:
            got.wait_recv()
        for cp in first + passed:
            cp.wait_send()
        mine.wait()


class _ScatterJob:
    def __init__(self, tag, grads):
        self.tag, self.ins = tag, list(grads)
        self.row_counts = [g.shape[1] for g in grads]
        self.outs = [jax.ShapeDtypeStruct((NDEV, sum(self.row_counts), grads[0].shape[2]), grads[0].dtype)]
        n = len(grads)
        self.scratch = [pltpu.SemaphoreType.DMA((n, 7)), pltpu.SemaphoreType.DMA((n, 7)), pltpu.SemaphoreType.DMA((n,))]
        self.results = None

    def _plan(self, ins, outs, sems):
        (out_ref,), (send_sems, recv_sems, local_sems) = outs, sems
        x, y, c = _place()
        remote, local, off = [], [], 0
        for i, (g_ref, rows) in enumerate(zip(ins, self.row_counts)):
            span = pl.ds(off, rows)
            for k, (a, b, d) in enumerate(FLIPS):
                px, py, pc = x ^ a, y ^ b, c ^ d
                remote.append(pltpu.make_async_remote_copy(
                    src_ref=g_ref.at[4 * px + 2 * py + pc], dst_ref=out_ref.at[k, span, :], send_sem=send_sems.at[i, k],
                    recv_sem=recv_sems.at[i, k], device_id=(px, py, pc), device_id_type=MESH))
            local.append(pltpu.make_async_copy(g_ref.at[4 * x + 2 * y + c], out_ref.at[7, span, :], local_sems.at[i]))
            off += rows
        return remote, local

    def start(self, ins, outs, sems):
        remote, local = self._plan(ins, outs, sems)
        for cp in remote + local:
            cp.start()

    def middle(self, ins, outs, sems):
        pass

    def finish(self, ins, outs, sems):
        remote, local = self._plan(ins, outs, sems)
        for cp in remote:
            cp.wait_recv()
        for cp in remote:
            cp.wait_send()
        for cp in local:
            cp.wait()


class _RowsGatherJob:
    def __init__(self, tag, rows):
        self.tag, self.ins = tag, [rows]
        self.outs = [jax.ShapeDtypeStruct((NDEV,) + rows.shape, rows.dtype)]
        self.scratch = [pltpu.SemaphoreType.DMA((7,)), pltpu.SemaphoreType.DMA((7,)), pltpu.SemaphoreType.DMA]
        self.results = None

    def _plan(self, ins, outs, sems):
        (x_ref,), (out_ref,), (send_sems, recv_sems, local_sem) = ins, outs, sems
        x, y, c = _place()
        me = 4 * x + 2 * y + c
        mine = pltpu.make_async_copy(x_ref, out_ref.at[me], local_sem)
        sends, lands = [], []
        for k, (a, b, d) in enumerate(FLIPS):
            px, py, pc = x ^ a, y ^ b, c ^ d
            for dst, keep in ((out_ref.at[me], sends), (out_ref.at[4 * px + 2 * py + pc], lands)):
                keep.append(pltpu.make_async_remote_copy(
                    src_ref=x_ref, dst_ref=dst, send_sem=send_sems.at[k], recv_sem=recv_sems.at[k],
                    device_id=(px, py, pc), device_id_type=MESH))
        return mine, sends, lands

    def start(self, ins, outs, sems):
        mine, sends, _ = self._plan(ins, outs, sems)
        mine.start()
        for cp in sends:
            cp.start()

    def middle(self, ins, outs, sems):
        pass

    def finish(self, ins, outs, sems):
        mine, sends, lands = self._plan(ins, outs, sems)
        for cp in lands:
            cp.wait_recv()
        for cp in sends:
            cp.wait_send()
        mine.wait()


def _fold_rows(parts):
    D = parts[0].shape[1]
    counts = [a.shape[0] // 8 for a in parts]
    total = -(-sum(counts) // 8) * 8

    def body(*refs):
        out = refs[-1]
        out[...] = jnp.zeros_like(out)
        row = 0
        for ref, k in zip(refs[:-1], counts):
            for j in range(k):
                out[pl.ds(row + j, 1), :] = jnp.sum(ref[pl.ds(8 * j, 8), :], axis=0, keepdims=True)
            row += k

    vm = pl.BlockSpec(memory_space=pltpu.VMEM)
    return pl.pallas_call(body, name="fold_rows", out_shape=jax.ShapeDtypeStruct((total, D), F32),
                          in_specs=[vm] * len(parts), out_specs=vm)(*parts)


def _sum_slots(slots):
    n, S, D = slots.shape

    def body(s_ref, o_ref):
        acc = s_ref[0]
        for j in range(1, n):
            acc = acc + s_ref[j]
        o_ref[...] = acc

    vm = pl.BlockSpec(memory_space=pltpu.VMEM)
    return pl.pallas_call(body, name="sum_slots", out_shape=jax.ShapeDtypeStruct((S, D), slots.dtype),
                          in_specs=[vm], out_specs=vm)(slots)


def _launch(name, body, grid, args, in_specs, out_specs, out_shape, scratch, jobs=()):
    grid = (grid,) if isinstance(grid, int) else tuple(grid)
    steps = grid[0] * (grid[1] if len(grid) == 2 else 1)
    n_in, n_out, n_sc = len(args), len(out_shape), len(scratch)
    j_in = [a for jb in jobs for a in jb.ins]
    j_out = [o for jb in jobs for o in jb.outs]
    j_sc = [s for jb in jobs for s in jb.scratch]
    mid = (17 * steps) // 20

    def wrapped(*refs):
        c_in, refs = refs[:n_in], refs[n_in:]
        m_in, refs = refs[:len(j_in)], refs[len(j_in):]
        c_out, refs = refs[:n_out], refs[n_out:]
        m_out, refs = refs[:len(j_out)], refs[len(j_out):]
        c_sc, m_sc = refs[:n_sc], refs[n_sc:]
        bound, a, b, c = [], 0, 0, 0
        for jb in jobs:
            bound.append((jb, m_in[a:a + len(jb.ins)], m_out[b:b + len(jb.outs)], m_sc[c:c + len(jb.scratch)]))
            a, b, c = a + len(jb.ins), b + len(jb.outs), c + len(jb.scratch)
        i = pl.program_id(0) if len(grid) == 1 else pl.program_id(0) * grid[1] + pl.program_id(1)

        def phase(step, which):
            if jobs:
                @pl.when(i == step)
                def _():
                    for jb, ins, outs, sems in bound:
                        getattr(jb, which)(ins, outs, sems)

        phase(0, "start")
        if body is not None:
            body(*c_in, *c_out, *c_sc)
        phase(mid, "middle")
        phase(steps - 1, "finish")

    outs = pl.pallas_call(
        wrapped, name=name + "".join("_" + jb.tag for jb in jobs), grid=grid,
        in_specs=list(in_specs) + [ANY] * len(j_in), out_specs=list(out_specs) + [ANY] * len(j_out),
        out_shape=list(out_shape) + j_out, scratch_shapes=list(scratch) + j_sc, compiler_params=_params(len(grid)),
    )(*args, *j_in)
    pos = n_out
    for jb in jobs:
        jb.results = list(outs[pos:pos + len(jb.outs)])
        pos += len(jb.outs)
    return list(outs[:n_out])


def _run_jobs(name, jobs):
    _launch(name, None, 1, [], [], [], [], [], jobs)


def _sum_parts(got, name):
    n, rows, D = got.shape
    tr = _tile(rows, 512)
    order = [n - 1] + list(range(n - 1))

    def body(*refs):
        g = refs[0][...].astype(F32)
        for pr in refs[1:n]:
            g = g + pr[...].astype(F32)
        refs[-1][...] = g

    specs = [pl.BlockSpec((None, tr, D), functools.partial(lambda k, i: (k, i, 0), k)) for k in order]
    return pl.pallas_call(
        body, name=name, grid=(rows // tr,), in_specs=specs, out_specs=_rows(tr, D),
        out_shape=jax.ShapeDtypeStruct((rows, D), F32), compiler_params=_params(),
    )(*[got] * n)


def _adamw(name, entries, by_layer):
    n = len(entries)
    L = entries[0][0].shape[0]

    def body(*refs):
        ins, outs = refs[:4 * n], refs[4 * n:]
        for e in range(n):
            w_ref, g_ref, m_ref, v_ref = ins[4 * e:4 * e + 4]
            d_out, m_out, v_out = outs[3 * e:3 * e + 3]
            g = g_ref[...]
            m_new = ADAM_B1 * m_ref[...] + (1.0 - ADAM_B1) * g
            v_new = ADAM_B2 * v_ref[...] + (1.0 - ADAM_B2) * (g * g)
            m_hat = m_new / (1.0 - ADAM_B1 ** ADAM_STEP)
            v_hat = v_new / (1.0 - ADAM_B2 ** ADAM_STEP)
            d_out[...] = -ADAM_LR * (m_hat / (jnp.sqrt(v_hat) + ADAM_EPS) + ADAM_WD * w_ref[...])
            m_out[...] = m_new
            v_out[...] = v_new

    def spec(a):
        rest = (0,) * (a.ndim - 1)
        if by_layer:
            return pl.BlockSpec((None,) + a.shape[1:], lambda l: (l,) + rest)
        return pl.BlockSpec(a.shape, lambda l: (0,) + rest)

    flat = [a for e in entries for a in e]
    outs = pl.pallas_call(
        body, name=name, grid=(L if by_layer else 1,),
        in_specs=[spec(a) for a in flat], out_specs=[spec(e[0]) for e in entries for _ in range(3)],
        out_shape=[jax.ShapeDtypeStruct(e[0].shape, F32) for e in entries for _ in range(3)],
        compiler_params=_params(),
    )(*flat)
    return [tuple(outs[3 * e:3 * e + 3]) for e in range(n)]


def _tn_matmul(xa, ya, name, tmm, jobs=()):
    T, M = xa.shape
    tn = ya.shape[1]
    tt = _tile(T, 1024)
    nb = M // tmm

    def body(x_ref, y_ref, o_ref, acc):
        k = pl.program_id(1)

        @pl.when(k == 0)
        def _():
            acc[...] = jnp.zeros_like(acc)

        acc[...] += _tn(x_ref[...], y_ref[...])

        @pl.when(k == pl.num_programs(1) - 1)
        def _():
            o_ref[...] = acc[...].astype(o_ref.dtype)

    return _launch(
        name, body, (nb, T // tt), [xa, ya],
        [pl.BlockSpec((tt, tmm), lambda b, k: (k, b)), pl.BlockSpec((tt, tn), lambda b, k: (k, 0))],
        [pl.BlockSpec((tmm, tn), lambda b, k: (b, 0))], [jax.ShapeDtypeStruct((M, tn), BF16)],
        [pltpu.VMEM((tmm, tn), F32)], jobs)[0]


def _tn_groups(xa, ya, groups, name):
    T, M = xa.shape
    w = M // groups
    tt = _tile(T, 1024)
    steps = T // tt

    def body(x_ref, y_ref, o_ref, acc):
        k = pl.program_id(0)

        @pl.when(k == 0)
        def _():
            acc[...] = jnp.zeros_like(acc)

        for g in range(groups):
            cols = pl.ds(g * w, w)
            acc[cols, :] += _tn(x_ref[:, cols], y_ref[:, cols])

        @pl.when(k == steps - 1)
        def _():
            o_ref[...] = acc[...].astype(o_ref.dtype)

    return pl.pallas_call(
        body, name=name, grid=(steps,), in_specs=[_rows(tt, M), _rows(tt, M)], out_specs=_const((M, w)),
        out_shape=jax.ShapeDtypeStruct((M, w), BF16), scratch_shapes=[pltpu.VMEM((M, w), F32)],
        compiler_params=_params(),
    )(xa, ya)


def _ffn_fwd(h, gain, G, lay, jobs=()):
    T, D = h.shape
    FB = lay.FB
    FP = NDEV * FB
    offs = (0, FB, 2 * FB)
    tm = _tile(T, 512)

    def body(h_ref, gain_ref, g_hbm, ho_ref, go_ref, uo_ref, n_ref, wg, wu, wd, sems):
        @pl.when(pl.program_id(0) == 0)
        def _():
            _fetch(g_hbm, [(offs[0], FB, wg), (offs[1], FB, wu), (offs[2], FB, wd)], sems)

        h = h_ref[...]
        xh, _ = _rms(h)
        n = (xh * gain_ref[...]).astype(BF16)
        n_ref[...] = n

        def project(sl):
            return _nt(n, wg[sl, :]), _nt(n, wu[sl, :])

        def activate(sl, gu):
            g, u = gu
            go_ref[:, sl] = g.astype(BF16)
            uo_ref[:, sl] = u.astype(BF16)
            return (g * _sig(g) * u).astype(BF16)

        def contract(sl, a, acc):
            return acc + _nn(a, wd[sl, :])

        acc = _staggered(_hidden_chunks(FP), project, activate, contract, jnp.zeros((tm, D), F32))
        ho_ref[...] = h + 0.5 * acc

    return _launch(
        "ffn_fwd", body, T // tm, [h, gain, G],
        [_rows(tm, D), _const((1, D)), ANY], [_rows(tm, D), _rows(tm, FP), _rows(tm, FP), _rows(tm, D)],
        [jax.ShapeDtypeStruct((T, D), F32), jax.ShapeDtypeStruct((T, FP), BF16), jax.ShapeDtypeStruct((T, FP), BF16),
         jax.ShapeDtypeStruct((T, D), BF16)],
        [pltpu.VMEM((FP, D), BF16)] * 3 + [pltpu.SemaphoreType.DMA((3, NDEV))], jobs)


def _ffn_bwd(d, h, gain, ga, ua, G, lay, jobs=()):
    T, D = h.shape
    FB = lay.FB
    FP = NDEV * FB
    offs = (0, FB, 2 * FB)
    tm = _tile(T, 256)

    def body(d_ref, h_ref, gain_ref, ga_ref, ua_ref, g_hbm,
             do_ref, dg_ref, du_ref, a_ref, dh_ref, gg_ref, wg, wu, wd, sems):
        @pl.when(pl.program_id(0) == 0)
        def _():
            _fetch(g_hbm, [(offs[0], FB, wg), (offs[1], FB, wu), (offs[2], FB, wd)], sems)
            gg_ref[...] = jnp.zeros_like(gg_ref)

        d = d_ref[...]
        gain_v = gain_ref[...]
        xh, r = _rms(h_ref[...])
        dh = (0.5 * d).astype(BF16)
        dh_ref[...] = dh

        def project(sl):
            return _nt(dh, wd[sl, :])

        def activate(sl, da):
            g = ga_ref[:, sl].astype(F32)
            u = ua_ref[:, sl].astype(F32)
            s = _sig(g)
            silu = g * s
            a_ref[:, sl] = (silu * u).astype(BF16)
            dgv = (da * u * (s * (1.0 + g * (1.0 - s)))).astype(BF16)
            duv = (da * silu).astype(BF16)
            dg_ref[:, sl] = dgv
            du_ref[:, sl] = duv
            return dgv, duv

        def contract(sl, grads, acc):
            return acc + _nn(grads[0], wg[sl, :]) + _nn(grads[1], wu[sl, :])

        dn = _staggered(_hidden_chunks(FP), project, activate, contract, jnp.zeros((tm, D), F32))
        gg_ref[...] += _colsum8(dn * xh)
        do_ref[...] = d + _rms_bwd(dn, xh, r, gain_v)

    wide = jax.ShapeDtypeStruct((T, FP), BF16)
    return _launch(
        "ffn_bwd", body, T // tm, [d, h, gain, ga, ua, G],
        [_rows(tm, D), _rows(tm, D), _const((1, D)), _rows(tm, FP), _rows(tm, FP), ANY],
        [_rows(tm, D), _rows(tm, FP), _rows(tm, FP), _rows(tm, FP), _rows(tm, D), _const((8, D))],
        [jax.ShapeDtypeStruct((T, D), F32), wide, wide, wide, jax.ShapeDtypeStruct((T, D), BF16),
         jax.ShapeDtypeStruct((8, D), F32)],
        [pltpu.VMEM((FP, D), BF16)] * 3 + [pltpu.SemaphoreType.DMA((3, NDEV))], jobs)


def _inproj_fwd(h, gain, G, lay):
    T, D = h.shape
    NS = lay.NS
    NIN, CH = NDEV * NS, 2 * NS
    tm = _tile(T, 512)

    def body(h_ref, gain_ref, g_hbm, z_ref, win, sems):
        @pl.when(pl.program_id(0) == 0)
        def _():
            _fetch(g_hbm, [(lay.off["win"], NS, win)], sems)

        xh, _ = _rms(h_ref[...])
        n = (xh * gain_ref[...]).astype(BF16)
        for j in range(NIN // CH):
            sl = pl.ds(j * CH, CH)
            z_ref[:, sl] = _nt(n, win[sl, :]).astype(BF16)

    return pl.pallas_call(
        body, name="inproj_fwd", grid=(T // tm,),
        in_specs=[_rows(tm, D), _const((1, D)), ANY], out_specs=_rows(tm, NIN),
        out_shape=jax.ShapeDtypeStruct((T, NIN), BF16),
        scratch_shapes=[pltpu.VMEM((NIN, D), BF16), pltpu.SemaphoreType.DMA((1, NDEV))],
        compiler_params=_params(),
    )(h, gain, G)


def _conv_chunks(tm, D):
    rb, lc = min(tm, 64), min(D, 256)
    return [(r0, l0, rb, lc) for r0 in range(0, tm, rb) for l0 in range(0, D, lc)]


SUBLANES = 8


def _preshift(sh, n_rows):
    for r in range(1, SUBLANES):
        sh[r, pl.ds(0, n_rows), :] = sh[0, pl.ds(r, n_rows), :]


def _window(sh, start, rows, lanes):
    r = start % SUBLANES
    return sh[r, pl.ds(start - r, rows), lanes]


def _mixer_fwd(z, h, pool_w, pscale, cw, cb, lng, lnb, G, lay, jobs=()):
    T, D = h.shape
    PG, DS = lay.PG, lay.DS
    tm = _tile(T, 256)
    hb = tm // HALO

    def body(z_ref, zp_ref, h_ref, pw_ref, ps_ref, cw_ref, cb_ref, lg_ref, lb_ref, g_hbm,
             ho_ref, c1_ref, q_ref, cc_ref, pool_ref, ext_p, sh_c, a_s, wco, wout, sems):
        i = pl.program_id(0)

        @pl.when(i == 0)
        def _():
            _fetch(g_hbm, [(lay.off["wco"], DS, wco), (lay.off["wout"], DS, wout)], sems)

        live = jnp.where(i > 0, 1.0, 0.0).astype(F32)
        ext_p[pl.ds(0, HALO), :] = zp_ref[:, pl.ds(0, D)].astype(F32) * live
        ext_p[pl.ds(HALO, tm), :] = z_ref[:, pl.ds(0, D)].astype(F32)
        sh_c[0, pl.ds(0, HALO), :] = (zp_ref[:, pl.ds(D, D)].astype(F32)
                                      * _sig(zp_ref[:, pl.ds(2 * D, D)].astype(F32)) * live)
        sh_c[0, pl.ds(HALO, tm), :] = z_ref[:, pl.ds(D, D)].astype(F32) * _sig(z_ref[:, pl.ds(2 * D, D)].astype(F32))
        _preshift(sh_c, tm + HALO - SUBLANES)

        t = i * tm + lax.broadcasted_iota(jnp.int32, (tm, 1), 0)
        for g, w in enumerate(POOL_WINDOWS):
            sl = pl.ds(g * PG, PG)
            s = ext_p[pl.ds(HALO, tm), sl]
            zc = s
            for j in range(1, w):
                s = s + ext_p[pl.ds(HALO - j, tm), sl]
            inv = 1.0 / jnp.minimum(t + 1, w).astype(F32)
            pooled = (s * inv - zc).astype(BF16)
            pool_ref[:, sl] = pooled
            qv = _nn(pooled, pw_ref[g])
            q_ref[:, sl] = qv.astype(BF16)
            a_s[:, sl] = qv * ps_ref[:, sl]

        for r0, l0, rb, lc in _conv_chunks(tm, D):
            ls = pl.ds(l0, lc)
            acc = jnp.zeros((rb, lc), F32) + cb_ref[:, ls]
            for k in range(CONV_K):
                acc = acc + cw_ref[pl.ds(k, 1), ls] * _window(sh_c, r0 + HALO - (CONV_K - 1) + k, rb, ls)
            c1_ref[pl.ds(r0, rb), ls] = acc

        xhat, _ = _ln(c1_ref[...])
        c2 = xhat * lg_ref[...] + lb_ref[...]
        c3 = (c2 * _sig(c2)).astype(BF16)
        cc = _nn(c3, wco[...])
        cc_ref[...] = cc.astype(BF16)
        gp = z_ref[:, pl.ds(3 * D, D)].astype(F32)
        gc = z_ref[:, pl.ds(4 * D, D)].astype(F32)
        m = (_sig(gp) * a_s[...] + _sig(gc) * cc).astype(BF16)
        ho_ref[...] = h_ref[...] + _nn(m, wout[...])

    act = jax.ShapeDtypeStruct((T, D), BF16)
    return _launch(
        "mixer_fwd", body, T // tm, [z, z, h, pool_w, pscale, cw, cb, lng, lnb, G],
        [_rows(tm, 5 * D), pl.BlockSpec((HALO, 5 * D), lambda i: (jnp.maximum(i * hb - 1, 0), 0)),
         _rows(tm, D), _const((4, PG, PG)), _const((1, D)), _const((CONV_KP, D)), _const((1, D)),
         _const((1, D)), _const((1, D)), ANY],
        [_rows(tm, D)] * 5,
        [jax.ShapeDtypeStruct((T, D), F32), jax.ShapeDtypeStruct((T, D), F32), act, act, act],
        [pltpu.VMEM((HALO + tm, D), F32), pltpu.VMEM((SUBLANES, HALO + tm, D), F32), pltpu.VMEM((tm, D), F32),
         pltpu.VMEM((D, D), BF16), pltpu.VMEM((D, D), BF16), pltpu.SemaphoreType.DMA((2, NDEV))], jobs)


def _mixer_bwd_rows(d, z, c1, qa, cca, pool_w, pscale, lng, lnb, G, lay):
    T, D = d.shape
    PG, DS = lay.PG, lay.DS
    tm = _tile(T, 256)

    def body(d_ref, zgp_ref, zgc_ref, c1_ref, q_ref, cc_ref, pw_ref, ps_ref, lg_ref, lb_ref, g_hbm,
             m_ref, db_ref, dcc_ref, c3_ref, dq_ref, dpool_ref, dc1_ref, dzg_ref, gps_ref, glg_ref, glb_ref, gcb_ref,
             wco, wout, sems):
        @pl.when(pl.program_id(0) == 0)
        def _():
            _fetch(g_hbm, [(lay.off["wco"], DS, wco), (lay.off["wout"], DS, wout)], sems)
            for ref in (gps_ref, glg_ref, glb_ref, gcb_ref):
                ref[...] = jnp.zeros_like(ref)

        db = d_ref[...].astype(BF16)
        db_ref[...] = db
        dm = _nt(db, wout[...])
        q = q_ref[...].astype(F32)
        cc = cc_ref[...].astype(F32)
        ps = ps_ref[...]
        sp = _sig(zgp_ref[...].astype(F32))
        sc = _sig(zgc_ref[...].astype(F32))
        a = q * ps
        m_ref[...] = (sp * a + sc * cc).astype(BF16)
        da = dm * sp
        dzg_ref[:, pl.ds(0, D)] = (dm * a * sp * (1.0 - sp)).astype(BF16)
        dzg_ref[:, pl.ds(D, D)] = (dm * cc * sc * (1.0 - sc)).astype(BF16)
        gps_ref[...] += _colsum8(da * q)
        dq = (da * ps).astype(BF16)
        dq_ref[...] = dq
        for g in range(len(POOL_WINDOWS)):
            sl = pl.ds(g * PG, PG)
            dpool_ref[:, sl] = _nt(dq_ref[:, sl], pw_ref[g])

        dcc = (dm * sc).astype(BF16)
        dcc_ref[...] = dcc
        xhat, rstd = _ln(c1_ref[...])
        lg = lg_ref[...]
        c2 = xhat * lg + lb_ref[...]
        s2 = _sig(c2)
        c3_ref[...] = (c2 * s2).astype(BF16)
        dc2 = _nt(dcc, wco[...]) * (s2 * (1.0 + c2 * (1.0 - s2)))
        glg_ref[...] += _colsum8(dc2 * xhat)
        glb_ref[...] += _colsum8(dc2)
        dxh = dc2 * lg
        dc1 = rstd * (dxh - jnp.mean(dxh, axis=-1, keepdims=True)
                      - xhat * jnp.mean(dxh * xhat, axis=-1, keepdims=True))
        dc1_ref[...] = dc1
        gcb_ref[...] += _colsum8(dc1)

    act = jax.ShapeDtypeStruct((T, D), BF16)
    full = jax.ShapeDtypeStruct((T, D), F32)
    vec = jax.ShapeDtypeStruct((8, D), F32)
    return pl.pallas_call(
        body, name="mixer_bwd_rows", grid=(T // tm,),
        in_specs=[_rows(tm, D), pl.BlockSpec((tm, D), lambda i: (i, 3)), pl.BlockSpec((tm, D), lambda i: (i, 4)),
                  _rows(tm, D), _rows(tm, D), _rows(tm, D),
                  _const((4, PG, PG)), _const((1, D)), _const((1, D)), _const((1, D)), ANY],
        out_specs=[_rows(tm, D)] * 7 + [_rows(tm, 2 * D)] + [_const((8, D))] * 4,
        out_shape=[act, act, act, act, act, full, full, jax.ShapeDtypeStruct((T, 2 * D), BF16), vec, vec, vec, vec],
        scratch_shapes=[pltpu.VMEM((D, D), BF16), pltpu.VMEM((D, D), BF16), pltpu.SemaphoreType.DMA((2, NDEV))],
        compiler_params=_params(),
    )(d, z, z, c1, qa, cca, pool_w, pscale, lng, lnb, G)


def _mixer_bwd_time(dc1, dpool, z, cw, lay, jobs=()):
    T, D = dc1.shape
    PG = lay.PG
    tm = _tile(T, 256)
    hb = tm // HALO
    nt = T // tm

    def body(dc_ref, dcn_ref, dp_ref, dpn_ref, z_ref, zp_ref, cw_ref, dz_ref, gcw_ref, sh_d, ext_q, sh_c, dc0_s):
        i = pl.program_id(0)

        @pl.when(i == 0)
        def _():
            gcw_ref[...] = jnp.zeros_like(gcw_ref)

        live_p = jnp.where(i > 0, 1.0, 0.0).astype(F32)
        live_n = jnp.where(i < nt - 1, 1.0, 0.0).astype(F32)
        sh_d[0, pl.ds(0, tm), :] = dc_ref[...]
        sh_d[0, pl.ds(tm, HALO), :] = dcn_ref[...] * live_n
        _preshift(sh_d, tm + HALO - SUBLANES)
        zg = z_ref[:, pl.ds(2 * D, D)].astype(F32)
        za = z_ref[:, pl.ds(D, D)].astype(F32)
        sg = _sig(zg)
        sh_c[0, pl.ds(0, HALO), :] = (zp_ref[:, pl.ds(D, D)].astype(F32)
                                      * _sig(zp_ref[:, pl.ds(2 * D, D)].astype(F32)) * live_p)
        sh_c[0, pl.ds(HALO, tm), :] = za * sg
        _preshift(sh_c, tm + HALO - SUBLANES)

        t = i * tm + lax.broadcasted_iota(jnp.int32, (tm, 1), 0)
        tn = (i + 1) * tm + lax.broadcasted_iota(jnp.int32, (HALO, 1), 0)
        for g, w in enumerate(POOL_WINDOWS):
            sl = pl.ds(g * PG, PG)
            ext_q[pl.ds(0, tm), sl] = dp_ref[:, sl] * (1.0 / jnp.minimum(t + 1, w).astype(F32))
            ext_q[pl.ds(tm, HALO), sl] = dpn_ref[:, sl] * (live_n / jnp.minimum(tn + 1, w).astype(F32))
        for g, w in enumerate(POOL_WINDOWS):
            sl = pl.ds(g * PG, PG)
            s = ext_q[pl.ds(0, tm), sl]
            for j in range(1, w):
                s = s + ext_q[pl.ds(j, tm), sl]
            dz_ref[:, sl] = (s - dp_ref[:, sl]).astype(BF16)

        for r0, l0, rb, lc in _conv_chunks(tm, D):
            ls = pl.ds(l0, lc)
            acc = jnp.zeros((rb, lc), F32)
            for j in range(CONV_K):
                acc = acc + cw_ref[pl.ds(CONV_K - 1 - j, 1), ls] * _window(sh_d, r0 + j, rb, ls)
            dc0_s[pl.ds(r0, rb), ls] = acc
        dc0 = dc0_s[...]
        dz_ref[:, pl.ds(D, D)] = (dc0 * sg).astype(BF16)
        dz_ref[:, pl.ds(2 * D, D)] = (dc0 * za * sg * (1.0 - sg)).astype(BF16)

        for r0, l0, rb, lc in _conv_chunks(tm, D):
            ls = pl.ds(l0, lc)
            dcv = dc_ref[pl.ds(r0, rb), ls]
            for k in range(CONV_K):
                gcw_ref[pl.ds(8 * k, 8), ls] += _colsum8(dcv * _window(sh_c, r0 + HALO - (CONV_K - 1) + k, rb, ls))

    nxt = lambda i: (jnp.minimum((i + 1) * hb, T // HALO - 1), 0)
    return _launch(
        "mixer_bwd_time", body, nt, [dc1, dc1, dpool, dpool, z, z, cw],
        [_rows(tm, D), pl.BlockSpec((HALO, D), nxt), _rows(tm, D), pl.BlockSpec((HALO, D), nxt),
         _rows(tm, 5 * D), pl.BlockSpec((HALO, 5 * D), lambda i: (jnp.maximum(i * hb - 1, 0), 0)),
         _const((CONV_KP, D))],
        [_rows(tm, 3 * D), _const((CONV_KP * 8, D))],
        [jax.ShapeDtypeStruct((T, 3 * D), BF16), jax.ShapeDtypeStruct((CONV_KP * 8, D), F32)],
        [pltpu.VMEM((SUBLANES, tm + HALO, D), F32), pltpu.VMEM((tm + HALO, D), F32),
         pltpu.VMEM((SUBLANES, HALO + tm, D), F32), pltpu.VMEM((tm, D), F32)], jobs)


def _inproj_bwd(d, h, gain, dzm, dzg, G, lay):
    T, D = h.shape
    NS = lay.NS
    NIN = NDEV * NS
    tm = _tile(T, 512)

    def body(d_ref, h_ref, gain_ref, dzm_ref, dzg_ref, g_hbm, do_ref, u_ref, gg_ref, win, sems):
        @pl.when(pl.program_id(0) == 0)
        def _():
            _fetch(g_hbm, [(lay.off["win"], NS, win)], sems)
            gg_ref[...] = jnp.zeros_like(gg_ref)

        gain_v = gain_ref[...]
        xh, r = _rms(h_ref[...])
        u_ref[...] = (xh * gain_v).astype(BF16)
        dn = jnp.zeros((tm, D), F32)
        for j in range(3):
            dn = dn + _nn(dzm_ref[:, pl.ds(j * D, D)], win[pl.ds(j * D, D), :])
        for j in range(2):
            dn = dn + _nn(dzg_ref[:, pl.ds(j * D, D)], win[pl.ds((3 + j) * D, D), :])
        gg_ref[...] += _colsum8(dn * xh)
        do_ref[...] = d_ref[...] + _rms_bwd(dn, xh, r, gain_v)

    return pl.pallas_call(
        body, name="inproj_bwd", grid=(T // tm,),
        in_specs=[_rows(tm, D), _rows(tm, D), _const((1, D)), _rows(tm, 3 * D), _rows(tm, 2 * D), ANY],
        out_specs=[_rows(tm, D), _rows(tm, D), _const((8, D))],
        out_shape=[jax.ShapeDtypeStruct((T, D), F32), jax.ShapeDtypeStruct((T, D), BF16),
                   jax.ShapeDtypeStruct((8, D), F32)],
        scratch_shapes=[pltpu.VMEM((NIN, D), BF16), pltpu.SemaphoreType.DMA((1, NDEV))],
        compiler_params=_params(),
    )(d, h, gain, dzm, dzg, G)


def _ple_fwd(h, pe, gain, wppt, G, lay, head=None):
    T, D = h.shape
    PD, DS = lay.PD, lay.DS
    tm = _tile(T, 512)

    def body(*refs):
        if head is None:
            h_ref, p_ref, gain_ref, wpp_ref, g_hbm, ho_ref, gate_ref, wpg, sems = refs
        else:
            (h_ref, p_ref, gain_ref, wpp_ref, g_hbm, t_ref, fgain_ref,
             do_ref, dpre_ref, de_ref, n_ref, pb_ref, gg_ref, loss_ref, fg_ref, wpg, sems) = refs

        @pl.when(pl.program_id(0) == 0)
        def _():
            _fetch(g_hbm, [(lay.off["wpg"], DS, wpg)], sems)
            if head is not None:
                for ref in (gg_ref, loss_ref, fg_ref):
                    ref[...] = jnp.zeros_like(ref)

        h = h_ref[...]
        gain_v = gain_ref[...]
        xh, r = _rms(h)
        n = (xh * gain_v).astype(BF16)
        gate = _sig(_nn(n, wpg[...]))
        pb = p_ref[...].astype(BF16)
        e = _nt(pb, wpp_ref[...])
        out = h + gate * e
        if head is None:
            gate_ref[...] = gate.astype(BF16)
            ho_ref[...] = out
            return
        fgain = fgain_ref[...]
        yh, ry = _rms(out)
        err = yh * fgain - t_ref[...]
        loss_ref[...] += _colsum8(err * err)
        dy = err * (1.0 / D)
        fg_ref[...] += _colsum8(dy * yh)
        d = _rms_bwd(dy, yh, ry, fgain)
        n_ref[...] = n
        pb_ref[...] = pb
        de_ref[...] = (d * gate).astype(BF16)
        dpre = (d * e * gate * (1.0 - gate)).astype(BF16)
        dpre_ref[...] = dpre
        dn = _nt(dpre, wpg[...])
        gg_ref[...] += _colsum8(dn * xh)
        do_ref[...] = d + _rms_bwd(dn, xh, r, gain_v)

    args = [h, pe, gain, wppt, G]
    in_specs = [_rows(tm, D), _rows(tm, PD), _const((1, D)), _const((D, PD)), ANY]
    act = jax.ShapeDtypeStruct((T, D), BF16)
    if head is None:
        out_specs = [_rows(tm, D), _rows(tm, D)]
        out_shape = [jax.ShapeDtypeStruct((T, D), F32), act]
    else:
        args += list(head)
        in_specs += [_rows(tm, D), _const((1, D))]
        out_specs = [_rows(tm, D)] * 4 + [_rows(tm, PD)] + [_const((8, D))] * 3
        out_shape = [jax.ShapeDtypeStruct((T, D), F32), act, act, act, jax.ShapeDtypeStruct((T, PD), BF16)]
        out_shape += [jax.ShapeDtypeStruct((8, D), F32)] * 3
    return pl.pallas_call(
        body, name="ple_fwd" if head is None else "ple_head", grid=(T // tm,),
        in_specs=in_specs, out_specs=out_specs, out_shape=out_shape,
        scratch_shapes=[pltpu.VMEM((D, D), BF16), pltpu.SemaphoreType.DMA((1, NDEV))],
        compiler_params=_params(),
    )(*args)


def _ple_bwd(d, h, pe, gate_a, gain, wppt, G, lay):
    T, D = h.shape
    PD, DS = lay.PD, lay.DS
    tm = _tile(T, 512)

    def body(d_ref, h_ref, p_ref, gate_ref, gain_ref, wpp_ref, g_hbm,
             do_ref, dpre_ref, de_ref, n_ref, pb_ref, gg_ref, wpg, sems):
        @pl.when(pl.program_id(0) == 0)
        def _():
            _fetch(g_hbm, [(lay.off["wpg"], DS, wpg)], sems)
            gg_ref[...] = jnp.zeros_like(gg_ref)

        d = d_ref[...]
        gain_v = gain_ref[...]
        xh, r = _rms(h_ref[...])
        n_ref[...] = (xh * gain_v).astype(BF16)
        pb = p_ref[...].astype(BF16)
        pb_ref[...] = pb
        e = _nt(pb, wpp_ref[...])
        gate = gate_ref[...].astype(F32)
        de_ref[...] = (d * gate).astype(BF16)
        dpre = (d * e * gate * (1.0 - gate)).astype(BF16)
        dpre_ref[...] = dpre
        dn = _nt(dpre, wpg[...])
        gg_ref[...] += _colsum8(dn * xh)
        do_ref[...] = d + _rms_bwd(dn, xh, r, gain_v)

    act = jax.ShapeDtypeStruct((T, D), BF16)
    return pl.pallas_call(
        body, name="ple_bwd", grid=(T // tm,),
        in_specs=[_rows(tm, D), _rows(tm, D), _rows(tm, PD), _rows(tm, D), _const((1, D)), _const((D, PD)), ANY],
        out_specs=[_rows(tm, D), _rows(tm, D), _rows(tm, D), _rows(tm, D), _rows(tm, PD), _const((8, D))],
        out_shape=[jax.ShapeDtypeStruct((T, D), F32), act, act, act, jax.ShapeDtypeStruct((T, PD), BF16),
                   jax.ShapeDtypeStruct((8, D), F32)],
        scratch_shapes=[pltpu.VMEM((D, D), BF16), pltpu.SemaphoreType.DMA((1, NDEV))],
        compiler_params=_params(),
    )(d, h, pe, gate_a, gain, wppt, G)


_BIG = ["ffn1_w_gate", "ffn1_w_up", "ffn1_w_down", "w_in", "pool_w", "conv_w_out", "w_out", "ffn2_w_gate",
        "ffn2_w_up", "ffn2_w_down", "ple_w_gate", "ple_w_proj"]
_VECS = ["ffn1_norm", "mix_norm", "pool_scale", "conv_dw_b", "conv_ln_g", "conv_ln_b", "ffn2_norm", "ple_norm"]
_WEIGHTS = ["ffn1_norm", "ffn1_w_gate", "ffn1_w_up", "ffn1_w_down", "mix_norm", "w_in", "pool_w", "pool_scale",
            "conv_dw_w", "conv_dw_b", "conv_ln_g", "conv_ln_b", "conv_w_out", "w_out", "ffn2_norm", "ffn2_w_gate",
            "ffn2_w_up", "ffn2_w_down", "ple_norm", "ple_w_gate", "ple_w_proj", "final_norm"]


def _step(x, p, loss_target, w, mom, var):
    T, D = x.shape[1], x.shape[2]
    L = p.shape[0]
    FS, NS = w["ffn1_w_gate"].shape[2], w["w_in"].shape[2]
    PD = p.shape[3]
    PGS, PG = w["pool_w"].shape[2], w["pool_w"].shape[3]
    CS = w["conv_dw_w"].shape[2]
    lay = _Layout(D, FS, NS, PD, PG, PGS)
    FB = lay.FB
    ax, ay, ac = _place()
    me = 4 * ax + 2 * ay + ac

    assert L == 2, "the exchange schedule below is written for two layers"
    gather = {(l, pc): _GatherJob(f"ag{pc}", _pack_piece(w, l, lay, pc, BF16)) for l in range(L) for pc in PIECES}
    fwd_jobs = {("ffn1", 0): [gather[0, "C"], gather[0, "D"]], ("mixer", 0): [gather[0, "B"]],
                ("ffn2", 0): [gather[1, "A"], gather[1, "D"]], ("ffn1", 1): [gather[1, "C"]],
                ("mixer", 1): [gather[1, "B"]]}
    cw_mine = jnp.pad(w["conv_dw_w"], ((0, 0), (0, CONV_KP - CONV_K), (0, 0)))
    taps_job = _RowsGatherJob("agW", cw_mine.reshape(L * CONV_KP * CS // D, D))
    _run_jobs("gather_first", [gather[0, "A"], taps_job])
    cw_full = taps_job.results[0].reshape(NDEV, L, CONV_KP, CS).transpose(1, 2, 0, 3).reshape(L, CONV_KP, D)

    def gathered(l, pc):
        return gather[l, pc].results[0]

    def small_mats(l):
        G = gathered(l, "D")
        wppt = G[:, lay.off["wpp"]:lay.off["wpp"] + lay.rows["wpp"]].reshape(D, PD)
        pw = G[:, lay.off["pool"]:lay.off["pool"] + lay.rows["pool"]].reshape(NDEV, 4, PGS, PG)
        return wppt, pw.transpose(1, 0, 2, 3).reshape(4, PG, PG)

    def vec(name, l):
        return w[name][l].reshape(1, D)

    h = x[0]
    saved = []
    for l in range(L):
        s = {"h0": h}
        h, s["g1"], s["u1"], s["n1"] = _ffn_fwd(h, vec("ffn1_norm", l), gathered(l, "A"), lay,
                                                fwd_jobs.get(("ffn1", l), ()))
        wppt, pw = small_mats(l)
        s["wppt"], s["pw"], s["h1"] = wppt, pw, h
        s["z"] = _inproj_fwd(h, vec("mix_norm", l), gathered(l, "C"), lay)
        h, s["c1"], s["q"], s["cc"], s["pooled"] = _mixer_fwd(
            s["z"], h, pw, vec("pool_scale", l), cw_full[l], vec("conv_dw_b", l), vec("conv_ln_g", l),
            vec("conv_ln_b", l), gathered(l, "D"), lay, fwd_jobs.get(("mixer", l), ()))
        s["h2"] = h
        h, s["g2"], s["u2"], s["n2"] = _ffn_fwd(h, vec("ffn2_norm", l), gathered(l, "B"), lay,
                                                fwd_jobs.get(("ffn2", l), ()))
        s["h3"] = h
        if l + 1 < L:
            h, s["gate"] = _ple_fwd(h, p[l, 0], vec("ple_norm", l), wppt, gathered(l, "D"), lay)
        else:
            *turned, loss_part, g_final = _ple_fwd(h, p[l, 0], vec("ple_norm", l), wppt, gathered(l, "D"), lay,
                                                    head=(loss_target[0], w["final_norm"].reshape(1, D)))
        saved.append(s)

    scatter = {}
    small_parts = [None] * L

    def send(l, pc, blocks):
        scatter[l, pc] = _ScatterJob(f"rs{pc}", [blocks[n].reshape(NDEV, lay.rows[n], D) for n, _ in lay.pieces[pc]])
        return scatter[l, pc]

    def small_rows():
        return _fold_rows([v for l in range(L) for v in small_parts[l]] + [g_final, loss_part])

    held = []
    for l in reversed(range(L)):
        s = saved[l]
        wppt, pw = s["wppt"], s["pw"]
        g = {}
        if l == L - 1:
            d, dpre, de, n_ple, pb, g_ple = turned
        else:
            d, dpre, de, n_ple, pb, g_ple = _ple_bwd(d, s["h3"], p[l, 0], s["gate"], vec("ple_norm", l), wppt,
                                                     gathered(l, "D"), lay)
        g["wpg"] = _tn_matmul(n_ple, dpre, "tn_sq", D)
        g["wpp"] = _tn_matmul(de, pb, "tn_proj", D)

        n1, n2 = s["n1"], s["n2"]
        d, dg2, du2, a2, dh2, g_n2 = _ffn_bwd(d, s["h2"], vec("ffn2_norm", l), s["g2"], s["u2"],
                                                  gathered(l, "B"), lay, held)
        held = []
        g["g2"] = _tn_matmul(dg2, n2, "tn_ffn", NDEV * FB)
        g["u2"] = _tn_matmul(du2, n2, "tn_ffn", NDEV * FB)
        g["d2"] = _tn_matmul(a2, dh2, "tn_ffn", NDEV * FB)
        held.append(send(l, "B", g))

        (m_b, d_b, dcc, c3, dq, dpool, dc1, dzg, g_ps, g_lg, g_lb, g_cb) = _mixer_bwd_rows(
            d, s["z"], s["c1"], s["q"], s["cc"], pw, vec("pool_scale", l), vec("conv_ln_g", l), vec("conv_ln_b", l),
            gathered(l, "D"), lay)
        g["wout"] = _tn_matmul(m_b, d_b, "tn_sq", D)
        g["wco"] = _tn_matmul(c3, dcc, "tn_sq", D)
        g_pool = _tn_groups(s["pooled"], dq, len(POOL_WINDOWS), "tn_pool")
        g["pool"] = g_pool.reshape(4, NDEV, PGS, PG).transpose(1, 0, 2, 3)
        held.append(send(l, "D", g))
        dzm, g_cw = _mixer_bwd_time(dc1, dpool, s["z"], cw_full[l], lay, held)
        held = []
        d, u_b, g_mix = _inproj_bwd(d, s["h1"], vec("mix_norm", l), dzm, dzg, gathered(l, "C"), lay)
        g["win"] = jnp.concatenate([_tn_matmul(dzm, u_b, "tn_in3", 3 * D // 2),
                                    _tn_matmul(dzg, u_b, "tn_in2", D)], axis=0)
        held.append(send(l, "C", g))

        d, dg1, du1, a1, dh1, g_n1 = _ffn_bwd(d, s["h0"], vec("ffn1_norm", l), s["g1"], s["u1"],
                                                  gathered(l, "A"), lay, held)
        held = []
        small_parts[l] = [g_n1, g_mix, g_ps, g_cb, g_lg, g_lb, g_n2, g_ple, g_cw]
        if l > 0:
            g["g1"] = _tn_matmul(dg1, n1, "tn_ffn", NDEV * FB)
            g["u1"] = _tn_matmul(du1, n1, "tn_ffn", NDEV * FB)
            g["d1"] = _tn_matmul(a1, dh1, "tn_ffn", NDEV * FB)
            held.append(send(l, "A", g))
        else:
            rows_job = _RowsGatherJob("agS", small_rows())
            g["g1"] = _tn_matmul(dg1, n1, "tn_ffn", NDEV * FB, jobs=[rows_job])
            g["u1"] = _tn_matmul(du1, n1, "tn_ffn", NDEV * FB, jobs=[send(l, "A1", g)])
            g["d1"] = _tn_matmul(a1, dh1, "tn_ffn", NDEV * FB, jobs=[send(l, "A2", g)])
            held.append(send(l, "A3", g))
    _run_jobs("scatter_last", held)
    grad_x = d[None]

    per = [{} for _ in range(L)]
    for (l, pc), job in scatter.items():
        per[l].update(_unpack_piece(_sum_parts(job.results[0], "sum_" + pc), lay, pc))
    grads = {n: jnp.stack([per[l][n] for l in range(L)]) for n in _BIG}

    small_sum = _sum_slots(rows_job.results[0])
    per_layer = len(_VECS) + CONV_KP
    for k, n in enumerate(_VECS):
        grads[n] = jnp.stack([small_sum[l * per_layer + k] for l in range(L)])
    g_cw_full = jnp.stack([small_sum[l * per_layer + len(_VECS):l * per_layer + len(_VECS) + CONV_K]
                           for l in range(L)])
    grads["conv_dw_w"] = lax.dynamic_slice_in_dim(g_cw_full, me * CS, CS, axis=2)
    grads["final_norm"] = small_sum[L * per_layer]
    loss = (0.5 / D) * jnp.sum(small_sum[L * per_layer + 1])

    def as2(a):
        return a.reshape(1, -1) if a.ndim == 1 else a

    deltas, new_m, new_v = {}, {}, {}

    def update(name, names, by_layer):
        res = _adamw(name, [(as2(w[n]), as2(grads[n]), as2(mom[n]), as2(var[n])) for n in names], by_layer)
        for n, (delta, m_new, v_new) in zip(names, res):
            deltas[n], new_m[n], new_v[n] = (a.reshape(w[n].shape) for a in (delta, m_new, v_new))

    for n in ("ffn1_w_gate", "ffn1_w_up", "ffn1_w_down", "ffn2_w_gate", "ffn2_w_up", "ffn2_w_down", "w_in"):
        update("adamw_" + n, [n], True)
    update("adamw_mid", ["conv_w_out", "w_out", "ple_w_gate", "ple_w_proj", "pool_w"], True)
    update("adamw_small", _VECS + ["conv_dw_w", "final_norm"], False)
    return loss, grad_x, (grads, deltas, new_m, new_v)


def kernel(x, p, ffn1_norm, ffn1_w_gate, ffn1_w_up, ffn1_w_down, mix_norm, w_in, pool_w, pool_scale, conv_dw_w, conv_dw_b, conv_ln_g, conv_ln_b, conv_w_out, w_out, ffn2_norm, ffn2_w_gate, ffn2_w_up, ffn2_w_down, ple_norm, ple_w_gate, ple_w_proj, final_norm, loss_target, m_ffn1_norm, m_ffn1_w_gate, m_ffn1_w_up, m_ffn1_w_down, m_mix_norm, m_w_in, m_pool_w, m_pool_scale, m_conv_dw_w, m_conv_dw_b, m_conv_ln_g, m_conv_ln_b, m_conv_w_out, m_w_out, m_ffn2_norm, m_ffn2_w_gate, m_ffn2_w_up, m_ffn2_w_down, m_ple_norm, m_ple_w_gate, m_ple_w_proj, m_final_norm, v_ffn1_norm, v_ffn1_w_gate, v_ffn1_w_up, v_ffn1_w_down, v_mix_norm, v_w_in, v_pool_w, v_pool_scale, v_conv_dw_w, v_conv_dw_b, v_conv_ln_g, v_conv_ln_b, v_conv_w_out, v_w_out, v_ffn2_norm, v_ffn2_w_gate, v_ffn2_w_up, v_ffn2_w_down, v_ple_norm, v_ple_w_gate, v_ple_w_proj, v_final_norm):
    given = dict(locals())
    w = {n: given[n] for n in _WEIGHTS}
    mom = {n: given["m_" + n] for n in _WEIGHTS}
    var = {n: given["v_" + n] for n in _WEIGHTS}
    loss, grad_x, (grads, deltas, new_m, new_v) = _step(x, p, loss_target, w, mom, var)
    out = [loss, grad_x]
    for res in (grads, deltas, new_m, new_v):
        out += [res[n] for n in _WEIGHTS]
    return tuple(out)
```

```python
import functools

import jax
import jax.numpy as jnp
from jax import lax
from jax.experimental import pallas as pl
from jax.experimental.pallas import tpu as pltpu

F32, BF16 = jnp.float32, jnp.bfloat16
NDEV = 8
MESH = pl.DeviceIdType.MESH
HALO = 32
POOL_WINDOWS = (2, 4, 8, 16)
CONV_K = 31
CONV_KP = 32
RMS_EPS, LN_EPS = 1e-6, 1e-5
ADAM_LR, ADAM_B1, ADAM_B2, ADAM_EPS, ADAM_WD, ADAM_STEP = 0.001, 0.9, 0.999, 1e-08, 0.01, 10
LANE = 128
VMEM_LIMIT = 56 * 1024 * 1024
ANY = pl.BlockSpec(memory_space=pl.ANY)


def _nn(a, b):
    return jnp.dot(a, b, preferred_element_type=F32)


def _nt(a, b):
    return lax.dot_general(a, b, (((1,), (1,)), ((), ())), preferred_element_type=F32)


def _tn(a, b):
    return lax.dot_general(a, b, (((0,), (0,)), ((), ())), preferred_element_type=F32)


def _colsum8(v):
    return jnp.sum(v.reshape(v.shape[0] // 8, 8, v.shape[1]), axis=0)


def _sig(v):
    return jax.nn.sigmoid(v)


def _rms(h):
    r = lax.rsqrt(jnp.mean(h * h, axis=-1, keepdims=True) + RMS_EPS)
    return h * r, r


def _rms_bwd(dn, xh, r, gain):
    dxh = dn * gain
    return r * (dxh - xh * jnp.mean(dxh * xh, axis=-1, keepdims=True))


def _ln(c1):
    mu = jnp.mean(c1, axis=-1, keepdims=True)
    cen = c1 - mu
    rstd = lax.rsqrt(jnp.mean(cen * cen, axis=-1, keepdims=True) + LN_EPS)
    return cen * rstd, rstd


def _rows(tm, c):
    return pl.BlockSpec((tm, c), lambda i: (i, 0))


def _const(shape):
    return pl.BlockSpec(shape, lambda i: (0,) * len(shape))


def _params(n_grid=1):
    return pltpu.CompilerParams(dimension_semantics=("arbitrary",) * n_grid, vmem_limit_bytes=VMEM_LIMIT)


BF16_ROWS = 16
MXU_CHUNK = 768


def _tile(n, want):
    if n <= want:
        return n
    return max(t for t in range(BF16_ROWS, want + 1, BF16_ROWS) if n % t == 0)


def _hidden_chunks(width):
    return [(o, min(MXU_CHUNK, width - o)) for o in range(0, width, MXU_CHUNK)]


def _staggered(chunks, project, activate, contract, acc):
    spans = [pl.ds(start, width) for start, width in chunks]
    ahead = project(spans[0])
    for c, span in enumerate(spans):
        projected = ahead
        if c + 1 < len(spans):
            ahead = project(spans[c + 1])
        acc = contract(span, activate(span, projected), acc)
    return acc


def _fetch(g_hbm, specs, sems):
    cps = []
    for wi, (off, rows, dst) in enumerate(specs):
        for dev in range(NDEV):
            cps.append(pltpu.make_async_copy(g_hbm.at[dev, pl.ds(off, rows), :],
                                             dst.at[pl.ds(dev * rows, rows), :], sems.at[wi, dev]))
    for cp in cps:
        cp.start()
    for cp in cps:
        cp.wait()


PIECES = ("A", "B", "C", "D")


class _Layout:
    def __init__(self, D, FS, NS, PD, PG, PGS):
        self.D, self.FS, self.NS, self.PD, self.PG, self.PGS = D, FS, NS, PD, PG, PGS
        assert FS % BF16_ROWS == 0 and (NDEV * FS) % (2 * LANE) == 0, "FFN shard rows must tile as bf16 row blocks"
        self.FB = FS
        self.DS = D // NDEV
        self.pieces = {
            "A": [("g1", self.FB), ("u1", self.FB), ("d1", self.FB)],
            "B": [("g2", self.FB), ("u2", self.FB), ("d2", self.FB)],
            "C": [("win", NS)],
            "D": [("wco", self.DS), ("wout", self.DS), ("wpg", self.DS), ("wpp", self.DS * PD // D),
                  ("pool", 4 * PGS * PG // D)],
            "A1": [("g1", self.FB)], "A2": [("u1", self.FB)], "A3": [("d1", self.FB)]}
        self.off, self.rows = {}, {}
        for pc in PIECES:
            o = 0
            for n, r in self.pieces[pc]:
                self.off[n], self.rows[n] = o, r
                o += r


def _pack_piece(w, l, lay, pc, dtype):
    D = lay.D
    make = {"g1": lambda: w["ffn1_w_gate"][l].T, "u1": lambda: w["ffn1_w_up"][l].T,
            "d1": lambda: w["ffn1_w_down"][l], "g2": lambda: w["ffn2_w_gate"][l].T,
            "u2": lambda: w["ffn2_w_up"][l].T, "d2": lambda: w["ffn2_w_down"][l],
            "win": lambda: w["w_in"][l].T, "wco": lambda: w["conv_w_out"][l], "wout": lambda: w["w_out"][l],
            "wpg": lambda: w["ple_w_gate"][l], "wpp": lambda: w["ple_w_proj"][l].T.reshape(-1, D),
            "pool": lambda: w["pool_w"][l].reshape(-1, D)}
    return jnp.concatenate([make[n]() for n, _ in lay.pieces[pc]], axis=0).astype(dtype)


def _unpack_piece(slab, lay, pc):
    PD, PG, PGS, DS = lay.PD, lay.PG, lay.PGS, lay.DS
    undo = {"g1": ("ffn1_w_gate", lambda a: a.T), "u1": ("ffn1_w_up", lambda a: a.T),
            "d1": ("ffn1_w_down", lambda a: a), "g2": ("ffn2_w_gate", lambda a: a.T),
            "u2": ("ffn2_w_up", lambda a: a.T), "d2": ("ffn2_w_down", lambda a: a),
            "win": ("w_in", lambda a: a.T), "wco": ("conv_w_out", lambda a: a), "wout": ("w_out", lambda a: a),
            "wpg": ("ple_w_gate", lambda a: a), "wpp": ("ple_w_proj", lambda a: a.reshape(DS, PD).T),
            "pool": ("pool_w", lambda a: a.reshape(4, PGS, PG))}
    out, o = {}, 0
    for n, r in lay.pieces[pc]:
        name, fn = undo[n]
        out[name] = fn(slab[o:o + r])
        o += r
    return out


def _place():
    return lax.axis_index("x"), lax.axis_index("y"), lax.axis_index("c")


FLIPS = [(a, b, d) for a in (0, 1) for b in (0, 1) for d in (0, 1)][1:]


class _GatherJob:
    def __init__(self, tag, slab):
        self.tag, self.ins = tag, [slab]
        self.outs = [jax.ShapeDtypeStruct((NDEV,) + slab.shape, slab.dtype)]
        self.scratch = [pltpu.SemaphoreType.DMA((7,)), pltpu.SemaphoreType.DMA((7,)), pltpu.SemaphoreType.DMA]
        self.results = None

    def _plan(self, ins, outs, sems):
        (x_ref,), (out_ref,), (send_sems, recv_sems, local_sem) = ins, outs, sems
        x, y, c = _place()
        me, sibling = (x, y, c), (x, y, 1 - c)
        chips = [(1 - x, y), (x, 1 - y), (1 - x, 1 - y)]

        def rows(px, py, pc):
            return out_ref.at[4 * px + 2 * py + pc]

        def copy(k, block, to, src=None):
            return pltpu.make_async_remote_copy(
                src_ref=rows(*block) if src is None else src, dst_ref=rows(*block),
                send_sem=send_sems.at[k], recv_sem=recv_sems.at[k], device_id=to, device_id_type=MESH)

        mine = pltpu.make_async_copy(x_ref, rows(*me), local_sem)
        first = [copy(0, me, sibling, src=x_ref)]
        first += [copy(1 + j, me, (*chip, c), src=x_ref) for j, chip in enumerate(chips)]
        passed = [copy(4 + j, (*chip, c), sibling) for j, chip in enumerate(chips)]
        landed = [copy(1 + j, (*chip, c), me) for j, chip in enumerate(chips)]
        late = [copy(0, sibling, me)] + [copy(4 + j, (*chip, 1 - c), me) for j, chip in enumerate(chips)]
        return mine, first, passed, landed, late

    def start(self, ins, outs, sems):
        mine, first, _, _, _ = self._plan(ins, outs, sems)
        mine.start()
        for cp in first:
            cp.start()

    def middle(self, ins, outs, sems):
        _, _, passed, landed, _ = self._plan(ins, outs, sems)
        for got, cp in zip(landed, passed):
            got.wait_recv()
            cp.start()

    def finish(self, ins, outs, sems):
        mine, first, passed, _, late = self._plan(ins, outs, sems)
        for got in late:
            got.wait_recv()
        for cp in first + passed:
            cp.wait_send()
        mine.wait()


class _ScatterJob:
    def __init__(self, tag, grads):
        self.tag, self.ins = tag, list(grads)
        self.row_counts = [g.shape[1] for g in grads]
        self.outs = [jax.ShapeDtypeStruct((NDEV, sum(self.row_counts), grads[0].shape[2]), grads[0].dtype)]
        n = len(grads)
        self.scratch = [pltpu.SemaphoreType.DMA((n, 7)), pltpu.SemaphoreType.DMA((n, 7)), pltpu.SemaphoreType.DMA((n,))]
        self.results = None

    def _plan(self, ins, outs, sems):
        (out_ref,), (send_sems, recv_sems, local_sems) = outs, sems
        x, y, c = _place()
        remote, local, off = [], [], 0
        for i, (g_ref, rows) in enumerate(zip(ins, self.row_counts)):
            span = pl.ds(off, rows)
            for k, (a, b, d) in enumerate(FLIPS):
                px, py, pc = x ^ a, y ^ b, c ^ d
                remote.append(pltpu.make_async_remote_copy(
                    src_ref=g_ref.at[4 * px + 2 * py + pc], dst_ref=out_ref.at[k, span, :], send_sem=send_sems.at[i, k],
                    recv_sem=recv_sems.at[i, k], device_id=(px, py, pc), device_id_type=MESH))
            local.append(pltpu.make_async_copy(g_ref.at[4 * x + 2 * y + c], out_ref.at[7, span, :], local_sems.at[i]))
            off += rows
        return remote, local

    def start(self, ins, outs, sems):
        remote, local = self._plan(ins, outs, sems)
        for cp in remote + local:
            cp.start()

    def middle(self, ins, outs, sems):
        pass

    def finish(self, ins, outs, sems):
        remote, local = self._plan(ins, outs, sems)
        for cp in remote:
            cp.wait_recv()
        for cp in remote:
            cp.wait_send()
        for cp in local:
            cp.wait()


class _RowsGatherJob:
    def __init__(self, tag, rows):
        self.tag, self.ins = tag, [rows]
        self.outs = [jax.ShapeDtypeStruct((NDEV,) + rows.shape, rows.dtype)]
        self.scratch = [pltpu.SemaphoreType.DMA((7,)), pltpu.SemaphoreType.DMA((7,)), pltpu.SemaphoreType.DMA]
        self.results = None

    def _plan(self, ins, outs, sems):
        (x_ref,), (out_ref,), (send_sems, recv_sems, local_sem) = ins, outs, sems
        x, y, c = _place()
        me = 4 * x + 2 * y + c
        mine = pltpu.make_async_copy(x_ref, out_ref.at[me], local_sem)
        sends, lands = [], []
        for k, (a, b, d) in enumerate(FLIPS):
            px, py, pc = x ^ a, y ^ b, c ^ d
            for dst, keep in ((out_ref.at[me], sends), (out_ref.at[4 * px + 2 * py + pc], lands)):
                keep.append(pltpu.make_async_remote_copy(
                    src_ref=x_ref, dst_ref=dst, send_sem=send_sems.at[k], recv_sem=recv_sems.at[k],
                    device_id=(px, py, pc), device_id_type=MESH))
        return mine, sends, lands

    def start(self, ins, outs, sems):
        mine, sends, _ = self._plan(ins, outs, sems)
        mine.start()
        for cp in sends:
            cp.start()

    def middle(self, ins, outs, sems):
        pass

    def finish(self, ins, outs, sems):
        mine, sends, lands = self._plan(ins, outs, sems)
        for cp in lands:
            cp.wait_recv()
        for cp in sends:
            cp.wait_send()
        mine.wait()


def _fold_rows(parts):
    D = parts[0].shape[1]
    counts = [a.shape[0] // 8 for a in parts]
    total = -(-sum(counts) // 8) * 8

    def body(*refs):
        out = refs[-1]
        out[...] = jnp.zeros_like(out)
        row = 0
        for ref, k in zip(refs[:-1], counts):
            for j in range(k):
                out[pl.ds(row + j, 1), :] = jnp.sum(ref[pl.ds(8 * j, 8), :], axis=0, keepdims=True)
            row += k

    vm = pl.BlockSpec(memory_space=pltpu.VMEM)
    return pl.pallas_call(body, name="fold_rows", out_shape=jax.ShapeDtypeStruct((total, D), F32),
                          in_specs=[vm] * len(parts), out_specs=vm)(*parts)


def _sum_slots(slots):
    n, S, D = slots.shape

    def body(s_ref, o_ref):
        acc = s_ref[0]
        for j in range(1, n):
            acc = acc + s_ref[j]
        o_ref[...] = acc

    vm = pl.BlockSpec(memory_space=pltpu.VMEM)
    return pl.pallas_call(body, name="sum_slots", out_shape=jax.ShapeDtypeStruct((S, D), slots.dtype),
                          in_specs=[vm], out_specs=vm)(slots)


def _launch(name, body, grid, args, in_specs, out_specs, out_shape, scratch, jobs=()):
    grid = (grid,) if isinstance(grid, int) else tuple(grid)
    steps = grid[0] * (grid[1] if len(grid) == 2 else 1)
    n_in, n_out, n_sc = len(args), len(out_shape), len(scratch)
    j_in = [a for jb in jobs for a in jb.ins]
    j_out = [o for jb in jobs for o in jb.outs]
    j_sc = [s for jb in jobs for s in jb.scratch]
    mid = (17 * steps) // 20

    def wrapped(*refs):
        c_in, refs = refs[:n_in], refs[n_in:]
        m_in, refs = refs[:len(j_in)], refs[len(j_in):]
        c_out, refs = refs[:n_out], refs[n_out:]
        m_out, refs = refs[:len(j_out)], refs[len(j_out):]
        c_sc, m_sc = refs[:n_sc], refs[n_sc:]
        bound, a, b, c = [], 0, 0, 0
        for jb in jobs:
            bound.append((jb, m_in[a:a + len(jb.ins)], m_out[b:b + len(jb.outs)], m_sc[c:c + len(jb.scratch)]))
            a, b, c = a + len(jb.ins), b + len(jb.outs), c + len(jb.scratch)
        i = pl.program_id(0) if len(grid) == 1 else pl.program_id(0) * grid[1] + pl.program_id(1)

        def phase(step, which):
            if jobs:
                @pl.when(i == step)
                def _():
                    for jb, ins, outs, sems in bound:
                        getattr(jb, which)(ins, outs, sems)

        phase(0, "start")
        if body is not None:
            body(*c_in, *c_out, *c_sc)
        phase(mid, "middle")
        phase(steps - 1, "finish")

    outs = pl.pallas_call(
        wrapped, name=name + "".join("_" + jb.tag for jb in jobs), grid=grid,
        in_specs=list(in_specs) + [ANY] * len(j_in), out_specs=list(out_specs) + [ANY] * len(j_out),
        out_shape=list(out_shape) + j_out, scratch_shapes=list(scratch) + j_sc, compiler_params=_params(len(grid)),
    )(*args, *j_in)
    pos = n_out
    for jb in jobs:
        jb.results = list(outs[pos:pos + len(jb.outs)])
        pos += len(jb.outs)
    return list(outs[:n_out])


def _run_jobs(name, jobs):
    _launch(name, None, 1, [], [], [], [], [], jobs)


def _sum_parts(got, name):
    n, rows, D = got.shape
    tr = _tile(rows, 512)
    order = [n - 1] + list(range(n - 1))

    def body(*refs):
        g = refs[0][...].astype(F32)
        for pr in refs[1:n]:
            g = g + pr[...].astype(F32)
        refs[-1][...] = g

    specs = [pl.BlockSpec((None, tr, D), functools.partial(lambda k, i: (k, i, 0), k)) for k in order]
    return pl.pallas_call(
        body, name=name, grid=(rows // tr,), in_specs=specs, out_specs=_rows(tr, D),
        out_shape=jax.ShapeDtypeStruct((rows, D), F32), compiler_params=_params(),
    )(*[got] * n)


def _adamw(name, entries, by_layer):
    n = len(entries)
    L = entries[0][0].shape[0]

    def body(*refs):
        ins, outs = refs[:4 * n], refs[4 * n:]
        for e in range(n):
            w_ref, g_ref, m_ref, v_ref = ins[4 * e:4 * e + 4]
            d_out, m_out, v_out = outs[3 * e:3 * e + 3]
            g = g_ref[...]
            m_new = ADAM_B1 * m_ref[...] + (1.0 - ADAM_B1) * g
            v_new = ADAM_B2 * v_ref[...] + (1.0 - ADAM_B2) * (g * g)
            m_hat = m_new / (1.0 - ADAM_B1 ** ADAM_STEP)
            v_hat = v_new / (1.0 - ADAM_B2 ** ADAM_STEP)
            d_out[...] = -ADAM_LR * (m_hat / (jnp.sqrt(v_hat) + ADAM_EPS) + ADAM_WD * w_ref[...])
            m_out[...] = m_new
            v_out[...] = v_new

    def spec(a):
        rest = (0,) * (a.ndim - 1)
        if by_layer:
            return pl.BlockSpec((None,) + a.shape[1:], lambda l: (l,) + rest)
        return pl.BlockSpec(a.shape, lambda l: (0,) + rest)

    flat = [a for e in entries for a in e]
    outs = pl.pallas_call(
        body, name=name, grid=(L if by_layer else 1,),
        in_specs=[spec(a) for a in flat], out_specs=[spec(e[0]) for e in entries for _ in range(3)],
        out_shape=[jax.ShapeDtypeStruct(e[0].shape, F32) for e in entries for _ in range(3)],
        compiler_params=_params(),
    )(*flat)
    return [tuple(outs[3 * e:3 * e + 3]) for e in range(n)]


def _tn_matmul(xa, ya, name, tmm, y_scale=1.0, jobs=()):
    T, M = xa.shape
    tn = ya.shape[1]
    tt = _tile(T, 1024)
    nb = M // tmm

    def body(x_ref, y_ref, o_ref, acc):
        k = pl.program_id(1)

        @pl.when(k == 0)
        def _():
            acc[...] = jnp.zeros_like(acc)

        y = y_ref[...]
        if y_scale != 1.0 or y.dtype != BF16:
            y = (y_scale * y).astype(BF16)
        acc[...] += _tn(x_ref[...], y)

        @pl.when(k == pl.num_programs(1) - 1)
        def _():
            o_ref[...] = acc[...].astype(o_ref.dtype)

    return _launch(
        name, body, (nb, T // tt), [xa, ya],
        [pl.BlockSpec((tt, tmm), lambda b, k: (k, b)), pl.BlockSpec((tt, tn), lambda b, k: (k, 0))],
        [pl.BlockSpec((tmm, tn), lambda b, k: (b, 0))], [jax.ShapeDtypeStruct((M, tn), BF16)],
        [pltpu.VMEM((tmm, tn), F32)], jobs)[0]


def _tn_groups(xa, ya, groups, name):
    T, M = xa.shape
    w = M // groups
    tt = _tile(T, 1024)
    steps = T // tt

    def body(x_ref, y_ref, o_ref, acc):
        k = pl.program_id(0)

        @pl.when(k == 0)
        def _():
            acc[...] = jnp.zeros_like(acc)

        for g in range(groups):
            cols = pl.ds(g * w, w)
            acc[cols, :] += _tn(x_ref[:, cols], y_ref[:, cols])

        @pl.when(k == steps - 1)
        def _():
            o_ref[...] = acc[...].astype(o_ref.dtype)

    return pl.pallas_call(
        body, name=name, grid=(steps,), in_specs=[_rows(tt, M), _rows(tt, M)], out_specs=_const((M, w)),
        out_shape=jax.ShapeDtypeStruct((M, w), BF16), scratch_shapes=[pltpu.VMEM((M, w), F32)],
        compiler_params=_params(),
    )(xa, ya)


def _ffn_fwd(h, gain, G, lay, jobs=()):
    T, D = h.shape
    FB = lay.FB
    FP = NDEV * FB
    offs = (0, FB, 2 * FB)
    tm = _tile(T, 512)

    def body(h_ref, gain_ref, g_hbm, ho_ref, go_ref, uo_ref, n_ref, wg, wu, wd, sems):
        @pl.when(pl.program_id(0) == 0)
        def _():
            _fetch(g_hbm, [(offs[0], FB, wg), (offs[1], FB, wu), (offs[2], FB, wd)], sems)

        h = h_ref[...]
        xh, _ = _rms(h)
        n = (xh * gain_ref[...]).astype(BF16)
        n_ref[...] = n

        def project(sl):
            return _nt(n, wg[sl, :]), _nt(n, wu[sl, :])

        def activate(sl, gu):
            g, u = gu
            go_ref[:, sl] = g.astype(BF16)
            uo_ref[:, sl] = u.astype(BF16)
            return (g * _sig(g) * u).astype(BF16)

        def contract(sl, a, acc):
            return acc + _nn(a, wd[sl, :])

        acc = _staggered(_hidden_chunks(FP), project, activate, contract, jnp.zeros((tm, D), F32))
        ho_ref[...] = h + 0.5 * acc

    return _launch(
        "ffn_fwd", body, T // tm, [h, gain, G],
        [_rows(tm, D), _const((1, D)), ANY], [_rows(tm, D), _rows(tm, FP), _rows(tm, FP), _rows(tm, D)],
        [jax.ShapeDtypeStruct((T, D), F32), jax.ShapeDtypeStruct((T, FP), BF16), jax.ShapeDtypeStruct((T, FP), BF16),
         jax.ShapeDtypeStruct((T, D), BF16)],
        [pltpu.VMEM((FP, D), BF16)] * 3 + [pltpu.SemaphoreType.DMA((3, NDEV))], jobs)


def _ffn_bwd(d, h, gain, ga, ua, G, lay, jobs=()):
    T, D = h.shape
    FB = lay.FB
    FP = NDEV * FB
    offs = (0, FB, 2 * FB)
    tm = _tile(T, 256)

    def body(d_ref, h_ref, gain_ref, ga_ref, ua_ref, g_hbm,
             do_ref, dg_ref, du_ref, a_ref, gg_ref, wg, wu, wd, sems):
        @pl.when(pl.program_id(0) == 0)
        def _():
            _fetch(g_hbm, [(offs[0], FB, wg), (offs[1], FB, wu), (offs[2], FB, wd)], sems)
            gg_ref[...] = jnp.zeros_like(gg_ref)

        d = d_ref[...]
        gain_v = gain_ref[...]
        xh, r = _rms(h_ref[...])
        dh = (0.5 * d).astype(BF16)

        def project(sl):
            return _nt(dh, wd[sl, :])

        def activate(sl, da):
            g = ga_ref[:, sl].astype(F32)
            u = ua_ref[:, sl].astype(F32)
            s = _sig(g)
            silu = g * s
            a_ref[:, sl] = (silu * u).astype(BF16)
            dgv = (da * u * (s * (1.0 + g * (1.0 - s)))).astype(BF16)
            duv = (da * silu).astype(BF16)
            dg_ref[:, sl] = dgv
            du_ref[:, sl] = duv
            return dgv, duv

        def contract(sl, grads, acc):
            return acc + _nn(grads[0], wg[sl, :]) + _nn(grads[1], wu[sl, :])

        dn = _staggered(_hidden_chunks(FP), project, activate, contract, jnp.zeros((tm, D), F32))
        gg_ref[...] += _colsum8(dn * xh)
        do_ref[...] = d + _rms_bwd(dn, xh, r, gain_v)

    wide = jax.ShapeDtypeStruct((T, FP), BF16)
    return _launch(
        "ffn_bwd", body, T // tm, [d, h, gain, ga, ua, G],
        [_rows(tm, D), _rows(tm, D), _const((1, D)), _rows(tm, FP), _rows(tm, FP), ANY],
        [_rows(tm, D), _rows(tm, FP), _rows(tm, FP), _rows(tm, FP), _const((8, D))],
        [jax.ShapeDtypeStruct((T, D), F32), wide, wide, wide, jax.ShapeDtypeStruct((8, D), F32)],
        [pltpu.VMEM((FP, D), BF16)] * 3 + [pltpu.SemaphoreType.DMA((3, NDEV))], jobs)


def _inproj_fwd(h, gain, G, lay):
    T, D = h.shape
    NS = lay.NS
    NIN, CH = NDEV * NS, 2 * NS
    tm = _tile(T, 512)

    def body(h_ref, gain_ref, g_hbm, z_ref, win, sems):
        @pl.when(pl.program_id(0) == 0)
        def _():
            _fetch(g_hbm, [(lay.off["win"], NS, win)], sems)

        xh, _ = _rms(h_ref[...])
        n = (xh * gain_ref[...]).astype(BF16)
        for j in range(NIN // CH):
            sl = pl.ds(j * CH, CH)
            z_ref[:, sl] = _nt(n, win[sl, :]).astype(BF16)

    return pl.pallas_call(
        body, name="inproj_fwd", grid=(T // tm,),
        in_specs=[_rows(tm, D), _const((1, D)), ANY], out_specs=_rows(tm, NIN),
        out_shape=jax.ShapeDtypeStruct((T, NIN), BF16),
        scratch_shapes=[pltpu.VMEM((NIN, D), BF16), pltpu.SemaphoreType.DMA((1, NDEV))],
        compiler_params=_params(),
    )(h, gain, G)


def _conv_chunks(tm, D):
    rb, lc = min(tm, 64), min(D, 256)
    return [(r0, l0, rb, lc) for r0 in range(0, tm, rb) for l0 in range(0, D, lc)]


SUBLANES = 8


def _preshift(sh, n_rows):
    for r in range(1, SUBLANES):
        sh[r, pl.ds(0, n_rows), :] = sh[0, pl.ds(r, n_rows), :]


def _window(sh, start, rows, lanes):
    r = start % SUBLANES
    return sh[r, pl.ds(start - r, rows), lanes]


def _mixer_fwd(z, h, pool_w, pscale, cw, cb, lng, lnb, G, lay, jobs=()):
    T, D = h.shape
    PG, DS = lay.PG, lay.DS
    tm = _tile(T, 256)
    hb = tm // HALO

    def body(z_ref, zp_ref, h_ref, pw_ref, ps_ref, cw_ref, cb_ref, lg_ref, lb_ref, g_hbm,
             ho_ref, c1_ref, q_ref, cc_ref, pool_ref, ext_p, sh_c, a_s, wco, wout, sems):
        i = pl.program_id(0)

        @pl.when(i == 0)
        def _():
            _fetch(g_hbm, [(lay.off["wco"], DS, wco), (lay.off["wout"], DS, wout)], sems)

        live = jnp.where(i > 0, 1.0, 0.0).astype(F32)
        ext_p[pl.ds(0, HALO), :] = zp_ref[:, pl.ds(0, D)].astype(F32) * live
        ext_p[pl.ds(HALO, tm), :] = z_ref[:, pl.ds(0, D)].astype(F32)
        sh_c[0, pl.ds(0, HALO), :] = (zp_ref[:, pl.ds(D, D)].astype(F32)
                                      * _sig(zp_ref[:, pl.ds(2 * D, D)].astype(F32)) * live)
        sh_c[0, pl.ds(HALO, tm), :] = z_ref[:, pl.ds(D, D)].astype(F32) * _sig(z_ref[:, pl.ds(2 * D, D)].astype(F32))
        _preshift(sh_c, tm + HALO - SUBLANES)

        t = i * tm + lax.broadcasted_iota(jnp.int32, (tm, 1), 0)
        for g, w in enumerate(POOL_WINDOWS):
            sl = pl.ds(g * PG, PG)
            s = ext_p[pl.ds(HALO, tm), sl]
            zc = s
            for j in range(1, w):
                s = s + ext_p[pl.ds(HALO - j, tm), sl]
            inv = 1.0 / jnp.minimum(t + 1, w).astype(F32)
            pooled = (s * inv - zc).astype(BF16)
            pool_ref[:, sl] = pooled
            qv = _nn(pooled, pw_ref[g])
            q_ref[:, sl] = qv.astype(BF16)
            a_s[:, sl] = qv * ps_ref[:, sl]

        for r0, l0, rb, lc in _conv_chunks(tm, D):
            ls = pl.ds(l0, lc)
            acc = jnp.zeros((rb, lc), F32) + cb_ref[:, ls]
            for k in range(CONV_K):
                acc = acc + cw_ref[pl.ds(k, 1), ls] * _window(sh_c, r0 + HALO - (CONV_K - 1) + k, rb, ls)
            c1_ref[pl.ds(r0, rb), ls] = acc

        xhat, _ = _ln(c1_ref[...])
        c2 = xhat * lg_ref[...] + lb_ref[...]
        c3 = (c2 * _sig(c2)).astype(BF16)
        cc = _nn(c3, wco[...])
        cc_ref[...] = cc.astype(BF16)
        gp = z_ref[:, pl.ds(3 * D, D)].astype(F32)
        gc = z_ref[:, pl.ds(4 * D, D)].astype(F32)
        m = (_sig(gp) * a_s[...] + _sig(gc) * cc).astype(BF16)
        ho_ref[...] = h_ref[...] + _nn(m, wout[...])

    act = jax.ShapeDtypeStruct((T, D), BF16)
    return _launch(
        "mixer_fwd", body, T // tm, [z, z, h, pool_w, pscale, cw, cb, lng, lnb, G],
        [_rows(tm, 5 * D), pl.BlockSpec((HALO, 5 * D), lambda i: (jnp.maximum(i * hb - 1, 0), 0)),
         _rows(tm, D), _const((4, PG, PG)), _const((1, D)), _const((CONV_KP, D)), _const((1, D)),
         _const((1, D)), _const((1, D)), ANY],
        [_rows(tm, D)] * 5,
        [jax.ShapeDtypeStruct((T, D), F32), jax.ShapeDtypeStruct((T, D), F32), act, act, act],
        [pltpu.VMEM((HALO + tm, D), F32), pltpu.VMEM((SUBLANES, HALO + tm, D), F32), pltpu.VMEM((tm, D), F32),
         pltpu.VMEM((D, D), BF16), pltpu.VMEM((D, D), BF16), pltpu.SemaphoreType.DMA((2, NDEV))], jobs)


def _mixer_bwd_rows(d, z, c1, qa, cca, pool_w, pscale, lng, lnb, G, lay):
    T, D = d.shape
    PG, DS = lay.PG, lay.DS
    tm = _tile(T, 256)

    def body(d_ref, zgp_ref, zgc_ref, c1_ref, q_ref, cc_ref, pw_ref, ps_ref, lg_ref, lb_ref, g_hbm,
             m_ref, dcc_ref, c3_ref, dq_ref, dpool_ref, dc1_ref, dzg_ref, gps_ref, glg_ref, glb_ref, gcb_ref,
             wco, wout, sems):
        @pl.when(pl.program_id(0) == 0)
        def _():
            _fetch(g_hbm, [(lay.off["wco"], DS, wco), (lay.off["wout"], DS, wout)], sems)
            for ref in (gps_ref, glg_ref, glb_ref, gcb_ref):
                ref[...] = jnp.zeros_like(ref)

        dm = _nt(d_ref[...].astype(BF16), wout[...])
        q = q_ref[...].astype(F32)
        cc = cc_ref[...].astype(F32)
        ps = ps_ref[...]
        sp = _sig(zgp_ref[...].astype(F32))
        sc = _sig(zgc_ref[...].astype(F32))
        a = q * ps
        m_ref[...] = (sp * a + sc * cc).astype(BF16)
        da = dm * sp
        dzg_ref[:, pl.ds(0, D)] = (dm * a * sp * (1.0 - sp)).astype(BF16)
        dzg_ref[:, pl.ds(D, D)] = (dm * cc * sc * (1.0 - sc)).astype(BF16)
        gps_ref[...] += _colsum8(da * q)
        dq = (da * ps).astype(BF16)
        dq_ref[...] = dq
        for g in range(len(POOL_WINDOWS)):
            sl = pl.ds(g * PG, PG)
            dpool_ref[:, sl] = _nt(dq_ref[:, sl], pw_ref[g])

        dcc = (dm * sc).astype(BF16)
        dcc_ref[...] = dcc
        xhat, rstd = _ln(c1_ref[...])
        lg = lg_ref[...]
        c2 = xhat * lg + lb_ref[...]
        s2 = _sig(c2)
        c3_ref[...] = (c2 * s2).astype(BF16)
        dc2 = _nt(dcc, wco[...]) * (s2 * (1.0 + c2 * (1.0 - s2)))
        glg_ref[...] += _colsum8(dc2 * xhat)
        glb_ref[...] += _colsum8(dc2)
        dxh = dc2 * lg
        dc1 = rstd * (dxh - jnp.mean(dxh, axis=-1, keepdims=True)
                      - xhat * jnp.mean(dxh * xhat, axis=-1, keepdims=True))
        dc1_ref[...] = dc1
        gcb_ref[...] += _colsum8(dc1)

    act = jax.ShapeDtypeStruct((T, D), BF16)
    full = jax.ShapeDtypeStruct((T, D), F32)
    vec = jax.ShapeDtypeStruct((8, D), F32)
    return pl.pallas_call(
        body, name="mixer_bwd_rows", grid=(T // tm,),
        in_specs=[_rows(tm, D), pl.BlockSpec((tm, D), lambda i: (i, 3)), pl.BlockSpec((tm, D), lambda i: (i, 4)),
                  _rows(tm, D), _rows(tm, D), _rows(tm, D),
                  _const((4, PG, PG)), _const((1, D)), _const((1, D)), _const((1, D)), ANY],
        out_specs=[_rows(tm, D)] * 6 + [_rows(tm, 2 * D)] + [_const((8, D))] * 4,
        out_shape=[act, act, act, act, full, full, jax.ShapeDtypeStruct((T, 2 * D), BF16), vec, vec, vec, vec],
        scratch_shapes=[pltpu.VMEM((D, D), BF16), pltpu.VMEM((D, D), BF16), pltpu.SemaphoreType.DMA((2, NDEV))],
        compiler_params=_params(),
    )(d, z, z, c1, qa, cca, pool_w, pscale, lng, lnb, G)


def _mixer_bwd_time(dc1, dpool, z, cw, lay, jobs=()):
    T, D = dc1.shape
    PG = lay.PG
    tm = _tile(T, 256)
    hb = tm // HALO
    nt = T // tm

    def body(dc_ref, dcn_ref, dp_ref, dpn_ref, z_ref, zp_ref, cw_ref, dz_ref, gcw_ref, sh_d, ext_q, sh_c, dc0_s):
        i = pl.program_id(0)

        @pl.when(i == 0)
        def _():
            gcw_ref[...] = jnp.zeros_like(gcw_ref)

        live_p = jnp.where(i > 0, 1.0, 0.0).astype(F32)
        live_n = jnp.where(i < nt - 1, 1.0, 0.0).astype(F32)
        sh_d[0, pl.ds(0, tm), :] = dc_ref[...]
        sh_d[0, pl.ds(tm, HALO), :] = dcn_ref[...] * live_n
        _preshift(sh_d, tm + HALO - SUBLANES)
        zg = z_ref[:, pl.ds(2 * D, D)].astype(F32)
        za = z_ref[:, pl.ds(D, D)].astype(F32)
        sg = _sig(zg)
        sh_c[0, pl.ds(0, HALO), :] = (zp_ref[:, pl.ds(D, D)].astype(F32)
                                      * _sig(zp_ref[:, pl.ds(2 * D, D)].astype(F32)) * live_p)
        sh_c[0, pl.ds(HALO, tm), :] = za * sg
        _preshift(sh_c, tm + HALO - SUBLANES)

        t = i * tm + lax.broadcasted_iota(jnp.int32, (tm, 1), 0)
        tn = (i + 1) * tm + lax.broadcasted_iota(jnp.int32, (HALO, 1), 0)
        for g, w in enumerate(POOL_WINDOWS):
            sl = pl.ds(g * PG, PG)
            ext_q[pl.ds(0, tm), sl] = dp_ref[:, sl] * (1.0 / jnp.minimum(t + 1, w).astype(F32))
            ext_q[pl.ds(tm, HALO), sl] = dpn_ref[:, sl] * (live_n / jnp.minimum(tn + 1, w).astype(F32))
        for g, w in enumerate(POOL_WINDOWS):
            sl = pl.ds(g * PG, PG)
            s = ext_q[pl.ds(0, tm), sl]
            for j in range(1, w):
                s = s + ext_q[pl.ds(j, tm), sl]
            dz_ref[:, sl] = (s - dp_ref[:, sl]).astype(BF16)

        for r0, l0, rb, lc in _conv_chunks(tm, D):
            ls = pl.ds(l0, lc)
            acc = jnp.zeros((rb, lc), F32)
            for j in range(CONV_K):
                acc = acc + cw_ref[pl.ds(CONV_K - 1 - j, 1), ls] * _window(sh_d, r0 + j, rb, ls)
            dc0_s[pl.ds(r0, rb), ls] = acc
        dc0 = dc0_s[...]
        dz_ref[:, pl.ds(D, D)] = (dc0 * sg).astype(BF16)
        dz_ref[:, pl.ds(2 * D, D)] = (dc0 * za * sg * (1.0 - sg)).astype(BF16)

        for r0, l0, rb, lc in _conv_chunks(tm, D):
            ls = pl.ds(l0, lc)
            dcv = dc_ref[pl.ds(r0, rb), ls]
            for k in range(CONV_K):
                gcw_ref[pl.ds(8 * k, 8), ls] += _colsum8(dcv * _window(sh_c, r0 + HALO - (CONV_K - 1) + k, rb, ls))

    nxt = lambda i: (jnp.minimum((i + 1) * hb, T // HALO - 1), 0)
    return _launch(
        "mixer_bwd_time", body, nt, [dc1, dc1, dpool, dpool, z, z, cw],
        [_rows(tm, D), pl.BlockSpec((HALO, D), nxt), _rows(tm, D), pl.BlockSpec((HALO, D), nxt),
         _rows(tm, 5 * D), pl.BlockSpec((HALO, 5 * D), lambda i: (jnp.maximum(i * hb - 1, 0), 0)),
         _const((CONV_KP, D))],
        [_rows(tm, 3 * D), _const((CONV_KP * 8, D))],
        [jax.ShapeDtypeStruct((T, 3 * D), BF16), jax.ShapeDtypeStruct((CONV_KP * 8, D), F32)],
        [pltpu.VMEM((SUBLANES, tm + HALO, D), F32), pltpu.VMEM((tm + HALO, D), F32),
         pltpu.VMEM((SUBLANES, HALO + tm, D), F32), pltpu.VMEM((tm, D), F32)], jobs)


def _inproj_bwd(d, h, gain, dzm, dzg, G, lay):
    T, D = h.shape
    NS = lay.NS
    NIN = NDEV * NS
    tm = _tile(T, 512)

    def body(d_ref, h_ref, gain_ref, dzm_ref, dzg_ref, g_hbm, do_ref, u_ref, gg_ref, win, sems):
        @pl.when(pl.program_id(0) == 0)
        def _():
            _fetch(g_hbm, [(lay.off["win"], NS, win)], sems)
            gg_ref[...] = jnp.zeros_like(gg_ref)

        gain_v = gain_ref[...]
        xh, r = _rms(h_ref[...])
        u_ref[...] = (xh * gain_v).astype(BF16)
        dn = jnp.zeros((tm, D), F32)
        for j in range(3):
            dn = dn + _nn(dzm_ref[:, pl.ds(j * D, D)], win[pl.ds(j * D, D), :])
        for j in range(2):
            dn = dn + _nn(dzg_ref[:, pl.ds(j * D, D)], win[pl.ds((3 + j) * D, D), :])
        gg_ref[...] += _colsum8(dn * xh)
        do_ref[...] = d_ref[...] + _rms_bwd(dn, xh, r, gain_v)

    return pl.pallas_call(
        body, name="inproj_bwd", grid=(T // tm,),
        in_specs=[_rows(tm, D), _rows(tm, D), _const((1, D)), _rows(tm, 3 * D), _rows(tm, 2 * D), ANY],
        out_specs=[_rows(tm, D), _rows(tm, D), _const((8, D))],
        out_shape=[jax.ShapeDtypeStruct((T, D), F32), jax.ShapeDtypeStruct((T, D), BF16),
                   jax.ShapeDtypeStruct((8, D), F32)],
        scratch_shapes=[pltpu.VMEM((NIN, D), BF16), pltpu.SemaphoreType.DMA((1, NDEV))],
        compiler_params=_params(),
    )(d, h, gain, dzm, dzg, G)


def _ple_fwd(h, pe, gain, wppt, G, lay, head=None):
    T, D = h.shape
    PD, DS = lay.PD, lay.DS
    tm = _tile(T, 512)

    def body(*refs):
        if head is None:
            h_ref, p_ref, gain_ref, wpp_ref, g_hbm, ho_ref, gate_ref, wpg, sems = refs
        else:
            (h_ref, p_ref, gain_ref, wpp_ref, g_hbm, t_ref, fgain_ref,
             do_ref, dpre_ref, de_ref, n_ref, pb_ref, gg_ref, loss_ref, fg_ref, wpg, sems) = refs

        @pl.when(pl.program_id(0) == 0)
        def _():
            _fetch(g_hbm, [(lay.off["wpg"], DS, wpg)], sems)
            if head is not None:
                for ref in (gg_ref, loss_ref, fg_ref):
                    ref[...] = jnp.zeros_like(ref)

        h = h_ref[...]
        gain_v = gain_ref[...]
        xh, r = _rms(h)
        n = (xh * gain_v).astype(BF16)
        gate = _sig(_nn(n, wpg[...]))
        pb = p_ref[...].astype(BF16)
        e = _nt(pb, wpp_ref[...])
        out = h + gate * e
        if head is None:
            gate_ref[...] = gate.astype(BF16)
            ho_ref[...] = out
            return
        fgain = fgain_ref[...]
        yh, ry = _rms(out)
        err = yh * fgain - t_ref[...]
        loss_ref[...] += _colsum8(err * err)
        dy = err * (1.0 / D)
        fg_ref[...] += _colsum8(dy * yh)
        d = _rms_bwd(dy, yh, ry, fgain)
        n_ref[...] = n
        pb_ref[...] = pb
        de_ref[...] = (d * gate).astype(BF16)
        dpre = (d * e * gate * (1.0 - gate)).astype(BF16)
        dpre_ref[...] = dpre
        dn = _nt(dpre, wpg[...])
        gg_ref[...] += _colsum8(dn * xh)
        do_ref[...] = d + _rms_bwd(dn, xh, r, gain_v)

    args = [h, pe, gain, wppt, G]
    in_specs = [_rows(tm, D), _rows(tm, PD), _const((1, D)), _const((D, PD)), ANY]
    act = jax.ShapeDtypeStruct((T, D), BF16)
    if head is None:
        out_specs = [_rows(tm, D), _rows(tm, D)]
        out_shape = [jax.ShapeDtypeStruct((T, D), F32), act]
    else:
        args += list(head)
        in_specs += [_rows(tm, D), _const((1, D))]
        out_specs = [_rows(tm, D)] * 4 + [_rows(tm, PD)] + [_const((8, D))] * 3
        out_shape = [jax.ShapeDtypeStruct((T, D), F32), act, act, act, jax.ShapeDtypeStruct((T, PD), BF16)]
        out_shape += [jax.ShapeDtypeStruct((8, D), F32)] * 3
    return pl.pallas_call(
        body, name="ple_fwd" if head is None else "ple_head", grid=(T // tm,),
        in_specs=in_specs, out_specs=out_specs, out_shape=out_shape,
        scratch_shapes=[pltpu.VMEM((D, D), BF16), pltpu.SemaphoreType.DMA((1, NDEV))],
        compiler_params=_params(),
    )(*args)


def _ple_bwd(d, h, pe, gate_a, gain, wppt, G, lay):
    T, D = h.shape
    PD, DS = lay.PD, lay.DS
    tm = _tile(T, 512)

    def body(d_ref, h_ref, p_ref, gate_ref, gain_ref, wpp_ref, g_hbm,
             do_ref, dpre_ref, de_ref, n_ref, pb_ref, gg_ref, wpg, sems):
        @pl.when(pl.program_id(0) == 0)
        def _():
            _fetch(g_hbm, [(lay.off["wpg"], DS, wpg)], sems)
            gg_ref[...] = jnp.zeros_like(gg_ref)

        d = d_ref[...]
        gain_v = gain_ref[...]
        xh, r = _rms(h_ref[...])
        n_ref[...] = (xh * gain_v).astype(BF16)
        pb = p_ref[...].astype(BF16)
        pb_ref[...] = pb
        e = _nt(pb, wpp_ref[...])
        gate = gate_ref[...].astype(F32)
        de_ref[...] = (d * gate).astype(BF16)
        dpre = (d * e * gate * (1.0 - gate)).astype(BF16)
        dpre_ref[...] = dpre
        dn = _nt(dpre, wpg[...])
        gg_ref[...] += _colsum8(dn * xh)
        do_ref[...] = d + _rms_bwd(dn, xh, r, gain_v)

    act = jax.ShapeDtypeStruct((T, D), BF16)
    return pl.pallas_call(
        body, name="ple_bwd", grid=(T // tm,),
        in_specs=[_rows(tm, D), _rows(tm, D), _rows(tm, PD), _rows(tm, D), _const((1, D)), _const((D, PD)), ANY],
        out_specs=[_rows(tm, D), _rows(tm, D), _rows(tm, D), _rows(tm, D), _rows(tm, PD), _const((8, D))],
        out_shape=[jax.ShapeDtypeStruct((T, D), F32), act, act, act, jax.ShapeDtypeStruct((T, PD), BF16),
                   jax.ShapeDtypeStruct((8, D), F32)],
        scratch_shapes=[pltpu.VMEM((D, D), BF16), pltpu.SemaphoreType.DMA((1, NDEV))],
        compiler_params=_params(),
    )(d, h, pe, gate_a, gain, wppt, G)


_BIG = ["ffn1_w_gate", "ffn1_w_up", "ffn1_w_down", "w_in", "pool_w", "conv_w_out", "w_out", "ffn2_w_gate",
        "ffn2_w_up", "ffn2_w_down", "ple_w_gate", "ple_w_proj"]
_VECS = ["ffn1_norm", "mix_norm", "pool_scale", "conv_dw_b", "conv_ln_g", "conv_ln_b", "ffn2_norm", "ple_norm"]
_WEIGHTS = ["ffn1_norm", "ffn1_w_gate", "ffn1_w_up", "ffn1_w_down", "mix_norm", "w_in", "pool_w", "pool_scale",
            "conv_dw_w", "conv_dw_b", "conv_ln_g", "conv_ln_b", "conv_w_out", "w_out", "ffn2_norm", "ffn2_w_gate",
            "ffn2_w_up", "ffn2_w_down", "ple_norm", "ple_w_gate", "ple_w_proj", "final_norm"]


def _step(x, p, loss_target, w, mom, var):
    T, D = x.shape[1], x.shape[2]
    L = p.shape[0]
    FS, NS = w["ffn1_w_gate"].shape[2], w["w_in"].shape[2]
    PD = p.shape[3]
    PGS, PG = w["pool_w"].shape[2], w["pool_w"].shape[3]
    CS = w["conv_dw_w"].shape[2]
    lay = _Layout(D, FS, NS, PD, PG, PGS)
    FB = lay.FB
    ax, ay, ac = _place()
    me = 4 * ax + 2 * ay + ac

    assert L == 2, "the exchange schedule below is written for two layers"
    gather = {(l, pc): _GatherJob(f"ag{pc}", _pack_piece(w, l, lay, pc, BF16)) for l in range(L) for pc in PIECES}
    fwd_jobs = {("ffn1", 0): [gather[0, "C"], gather[0, "D"]], ("mixer", 0): [gather[0, "B"]],
                ("ffn2", 0): [gather[1, "A"], gather[1, "D"]], ("ffn1", 1): [gather[1, "C"]],
                ("mixer", 1): [gather[1, "B"]]}
    cw_mine = jnp.pad(w["conv_dw_w"], ((0, 0), (0, CONV_KP - CONV_K), (0, 0)))
    taps_job = _RowsGatherJob("agW", cw_mine.reshape(L * CONV_KP * CS // D, D))
    _run_jobs("gather_first", [gather[0, "A"], taps_job])
    cw_full = taps_job.results[0].reshape(NDEV, L, CONV_KP, CS).transpose(1, 2, 0, 3).reshape(L, CONV_KP, D)

    def gathered(l, pc):
        return gather[l, pc].results[0]

    def small_mats(l):
        G = gathered(l, "D")
        wppt = G[:, lay.off["wpp"]:lay.off["wpp"] + lay.rows["wpp"]].reshape(D, PD)
        pw = G[:, lay.off["pool"]:lay.off["pool"] + lay.rows["pool"]].reshape(NDEV, 4, PGS, PG)
        return wppt, pw.transpose(1, 0, 2, 3).reshape(4, PG, PG)

    def vec(name, l):
        return w[name][l].reshape(1, D)

    h = x[0]
    saved = []
    for l in range(L):
        s = {"h0": h}
        h, s["g1"], s["u1"], s["n1"] = _ffn_fwd(h, vec("ffn1_norm", l), gathered(l, "A"), lay,
                                                fwd_jobs.get(("ffn1", l), ()))
        wppt, pw = small_mats(l)
        s["wppt"], s["pw"], s["h1"] = wppt, pw, h
        s["z"] = _inproj_fwd(h, vec("mix_norm", l), gathered(l, "C"), lay)
        h, s["c1"], s["q"], s["cc"], s["pooled"] = _mixer_fwd(
            s["z"], h, pw, vec("pool_scale", l), cw_full[l], vec("conv_dw_b", l), vec("conv_ln_g", l),
            vec("conv_ln_b", l), gathered(l, "D"), lay, fwd_jobs.get(("mixer", l), ()))
        s["h2"] = h
        h, s["g2"], s["u2"], s["n2"] = _ffn_fwd(h, vec("ffn2_norm", l), gathered(l, "B"), lay,
                                                fwd_jobs.get(("ffn2", l), ()))
        s["h3"] = h
        if l + 1 < L:
            h, s["gate"] = _ple_fwd(h, p[l, 0], vec("ple_norm", l), wppt, gathered(l, "D"), lay)
        else:
            *turned, loss_part, g_final = _ple_fwd(h, p[l, 0], vec("ple_norm", l), wppt, gathered(l, "D"), lay,
                                                    head=(loss_target[0], w["final_norm"].reshape(1, D)))
        saved.append(s)

    scatter = {}
    small_parts = [None] * L

    def send(l, pc, blocks):
        scatter[l, pc] = _ScatterJob(f"rs{pc}", [blocks[n].reshape(NDEV, lay.rows[n], D) for n, _ in lay.pieces[pc]])
        return scatter[l, pc]

    def small_rows():
        return _fold_rows([v for l in range(L) for v in small_parts[l]] + [g_final, loss_part])

    held = []
    for l in reversed(range(L)):
        s = saved[l]
        wppt, pw = s["wppt"], s["pw"]
        g = {}
        if l == L - 1:
            d, dpre, de, n_ple, pb, g_ple = turned
        else:
            d, dpre, de, n_ple, pb, g_ple = _ple_bwd(d, s["h3"], p[l, 0], s["gate"], vec("ple_norm", l), wppt,
                                                     gathered(l, "D"), lay)
        g["wpg"] = _tn_matmul(n_ple, dpre, "tn_sq", D)
        g["wpp"] = _tn_matmul(de, pb, "tn_proj", D)

        n1, n2 = s["n1"], s["n2"]
        d_out = d
        d, dg2, du2, a2, g_n2 = _ffn_bwd(d, s["h2"], vec("ffn2_norm", l), s["g2"], s["u2"], gathered(l, "B"), lay, held)
        held = []
        g["g2"] = _tn_matmul(dg2, n2, "tn_ffn", NDEV * FB)
        g["u2"] = _tn_matmul(du2, n2, "tn_ffn", NDEV * FB)
        g["d2"] = _tn_matmul(a2, d_out, "tn_ffn_d", NDEV * FB, y_scale=0.5)
        held.append(send(l, "B", g))

        (m_b, dcc, c3, dq, dpool, dc1, dzg, g_ps, g_lg, g_lb, g_cb) = _mixer_bwd_rows(
            d, s["z"], s["c1"], s["q"], s["cc"], pw, vec("pool_scale", l), vec("conv_ln_g", l), vec("conv_ln_b", l),
            gathered(l, "D"), lay)
        g["wout"] = _tn_matmul(m_b, d, "tn_sq_d", D)
        g["wco"] = _tn_matmul(c3, dcc, "tn_sq", D)
        g_pool = _tn_groups(s["pooled"], dq, len(POOL_WINDOWS), "tn_pool")
        g["pool"] = g_pool.reshape(4, NDEV, PGS, PG).transpose(1, 0, 2, 3)
        held.append(send(l, "D", g))
        dzm, g_cw = _mixer_bwd_time(dc1, dpool, s["z"], cw_full[l], lay, held)
        held = []
        d, u_b, g_mix = _inproj_bwd(d, s["h1"], vec("mix_norm", l), dzm, dzg, gathered(l, "C"), lay)
        g["win"] = jnp.concatenate([_tn_matmul(dzm, u_b, "tn_in3", 3 * D // 2),
                                    _tn_matmul(dzg, u_b, "tn_in2", D)], axis=0)
        held.append(send(l, "C", g))

        d_out = d
        d, dg1, du1, a1, g_n1 = _ffn_bwd(d, s["h0"], vec("ffn1_norm", l), s["g1"], s["u1"], gathered(l, "A"), lay, held)
        held = []
        small_parts[l] = [g_n1, g_mix, g_ps, g_cb, g_lg, g_lb, g_n2, g_ple, g_cw]
        if l > 0:
            g["g1"] = _tn_matmul(dg1, n1, "tn_ffn", NDEV * FB)
            g["u1"] = _tn_matmul(du1, n1, "tn_ffn", NDEV * FB)
            g["d1"] = _tn_matmul(a1, d_out, "tn_ffn_d", NDEV * FB, y_scale=0.5)
            held.append(send(l, "A", g))
        else:
            rows_job = _RowsGatherJob("agS", small_rows())
            g["g1"] = _tn_matmul(dg1, n1, "tn_ffn", NDEV * FB, jobs=[rows_job])
            g["u1"] = _tn_matmul(du1, n1, "tn_ffn", NDEV * FB, jobs=[send(l, "A1", g)])
            g["d1"] = _tn_matmul(a1, d_out, "tn_ffn_d", NDEV * FB, y_scale=0.5, jobs=[send(l, "A2", g)])
            held.append(send(l, "A3", g))
    _run_jobs("scatter_last", held)
    grad_x = d[None]

    per = [{} for _ in range(L)]
    for (l, pc), job in scatter.items():
        per[l].update(_unpack_piece(_sum_parts(job.results[0], "sum_" + pc), lay, pc))
    grads = {n: jnp.stack([per[l][n] for l in range(L)]) for n in _BIG}

    small_sum = _sum_slots(rows_job.results[0])
    per_layer = len(_VECS) + CONV_KP
    for k, n in enumerate(_VECS):
        grads[n] = jnp.stack([small_sum[l * per_layer + k] for l in range(L)])
    g_cw_full = jnp.stack([small_sum[l * per_layer + len(_VECS):l * per_layer + len(_VECS) + CONV_K]
                           for l in range(L)])
    grads["conv_dw_w"] = lax.dynamic_slice_in_dim(g_cw_full, me * CS, CS, axis=2)
    grads["final_norm"] = small_sum[L * per_layer]
    loss = (0.5 / D) * jnp.sum(small_sum[L * per_layer + 1])

    def as2(a):
        return a.reshape(1, -1) if a.ndim == 1 else a

    deltas, new_m, new_v = {}, {}, {}

    def update(name, names, by_layer):
        res = _adamw(name, [(as2(w[n]), as2(grads[n]), as2(mom[n]), as2(var[n])) for n in names], by_layer)
        for n, (delta, m_new, v_new) in zip(names, res):
            deltas[n], new_m[n], new_v[n] = (a.reshape(w[n].shape) for a in (delta, m_new, v_new))

    for n in ("ffn1_w_gate", "ffn1_w_up", "ffn1_w_down", "ffn2_w_gate", "ffn2_w_up", "ffn2_w_down", "w_in"):
        update("adamw_" + n, [n], True)
    update("adamw_mid", ["conv_w_out", "w_out", "ple_w_gate", "ple_w_proj", "pool_w"], True)
    update("adamw_small", _VECS + ["conv_dw_w", "final_norm"], False)
    return loss, grad_x, (grads, deltas, new_m, new_v)


def kernel(x, p, ffn1_norm, ffn1_w_gate, ffn1_w_up, ffn1_w_down, mix_norm, w_in, pool_w, pool_scale, conv_dw_w, conv_dw_b, conv_ln_g, conv_ln_b, conv_w_out, w_out, ffn2_norm, ffn2_w_gate, ffn2_w_up, ffn2_w_down, ple_norm, ple_w_gate, ple_w_proj, final_norm, loss_target, m_ffn1_norm, m_ffn1_w_gate, m_ffn1_w_up, m_ffn1_w_down, m_mix_norm, m_w_in, m_pool_w, m_pool_scale, m_conv_dw_w, m_conv_dw_b, m_conv_ln_g, m_conv_ln_b, m_conv_w_out, m_w_out, m_ffn2_norm, m_ffn2_w_gate, m_ffn2_w_up, m_ffn2_w_down, m_ple_norm, m_ple_w_gate, m_ple_w_proj, m_final_norm, v_ffn1_norm, v_ffn1_w_gate, v_ffn1_w_up, v_ffn1_w_down, v_mix_norm, v_w_in, v_pool_w, v_pool_scale, v_conv_dw_w, v_conv_dw_b, v_conv_ln_g, v_conv_ln_b, v_conv_w_out, v_w_out, v_ffn2_norm, v_ffn2_w_gate, v_ffn2_w_up, v_ffn2_w_down, v_ple_norm, v_ple_w_gate, v_ple_w_proj, v_final_norm):
    given = dict(locals())
    w = {n: given[n] for n in _WEIGHTS}
    mom = {n: given["m_" + n] for n in _WEIGHTS}
    var = {n: given["v_" + n] for n in _WEIGHTS}
    loss, grad_x, (grads, deltas, new_m, new_v) = _step(x, p, loss_target, w, mom, var)
    out = [loss, grad_x]
    for res in (grads, deltas, new_m, new_v):
        out += [res[n] for n in _WEIGHTS]
    return tuple(out)
```

```python
import functools

import jax
import jax.numpy as jnp
from jax import lax
from jax.experimental import pallas as pl
from jax.experimental.pallas import tpu as pltpu

F32, BF16 = jnp.float32, jnp.bfloat16
NDEV = 8
MESH = pl.DeviceIdType.MESH
HALO = 32
POOL_WINDOWS = (2, 4, 8, 16)
CONV_K = 31
CONV_KP = 32
RMS_EPS, LN_EPS = 1e-6, 1e-5
ADAM_LR, ADAM_B1, ADAM_B2, ADAM_EPS, ADAM_WD, ADAM_STEP = 0.001, 0.9, 0.999, 1e-08, 0.01, 10
LANE = 128
VMEM_LIMIT = 56 * 1024 * 1024
ANY = pl.BlockSpec(memory_space=pl.ANY)


def _nn(a, b):
    return jnp.dot(a, b, preferred_element_type=F32)


def _nt(a, b):
    return lax.dot_general(a, b, (((1,), (1,)), ((), ())), preferred_element_type=F32)


def _tn(a, b):
    return lax.dot_general(a, b, (((0,), (0,)), ((), ())), preferred_element_type=F32)


def _colsum8(v):
    return jnp.sum(v.reshape(v.shape[0] // 8, 8, v.shape[1]), axis=0)


def _sig(v):
    return jax.nn.sigmoid(v)


def _rms(h):
    r = lax.rsqrt(jnp.mean(h * h, axis=-1, keepdims=True) + RMS_EPS)
    return h * r, r


def _rms_bwd(dn, xh, r, gain):
    dxh = dn * gain
    return r * (dxh - xh * jnp.mean(dxh * xh, axis=-1, keepdims=True))


def _ln(c1):
    mu = jnp.mean(c1, axis=-1, keepdims=True)
    cen = c1 - mu
    rstd = lax.rsqrt(jnp.mean(cen * cen, axis=-1, keepdims=True) + LN_EPS)
    return cen * rstd, rstd


def _rows(tm, c):
    return pl.BlockSpec((tm, c), lambda i: (i, 0))


def _const(shape):
    return pl.BlockSpec(shape, lambda i: (0,) * len(shape))


def _params(n_grid=1):
    return pltpu.CompilerParams(dimension_semantics=("arbitrary",) * n_grid, vmem_limit_bytes=VMEM_LIMIT)


BF16_ROWS = 16
MXU_CHUNK = 768
BWD_CHUNK = 512


def _tile(n, want):
    if n <= want:
        return n
    return max(t for t in range(BF16_ROWS, want + 1, BF16_ROWS) if n % t == 0)


def _hidden_chunks(width, chunk=MXU_CHUNK):
    return [(o, min(chunk, width - o)) for o in range(0, width, chunk)]


def _staggered(chunks, project, activate, contract, acc):
    spans = [pl.ds(start, width) for start, width in chunks]
    ahead = project(spans[0])
    for c, span in enumerate(spans):
        projected = ahead
        if c + 1 < len(spans):
            ahead = project(spans[c + 1])
        acc = contract(span, activate(span, projected), acc)
    return acc


def _fetch(g_hbm, specs, sems):
    cps = []
    for wi, (off, rows, dst) in enumerate(specs):
        for dev in range(NDEV):
            cps.append(pltpu.make_async_copy(g_hbm.at[dev, pl.ds(off, rows), :],
                                             dst.at[pl.ds(dev * rows, rows), :], sems.at[wi, dev]))
    for cp in cps:
        cp.start()
    for cp in cps:
        cp.wait()


PIECES = ("A", "B", "C", "D")


class _Layout:
    def __init__(self, D, FS, NS, PD, PG, PGS):
        self.D, self.FS, self.NS, self.PD, self.PG, self.PGS = D, FS, NS, PD, PG, PGS
        assert FS % BF16_ROWS == 0 and (NDEV * FS) % (2 * LANE) == 0, "FFN shard rows must tile as bf16 row blocks"
        self.FB = FS
        self.DS = D // NDEV
        self.pieces = {
            "A": [("g1", self.FB), ("u1", self.FB), ("d1", self.FB)],
            "B": [("g2", self.FB), ("u2", self.FB), ("d2", self.FB)],
            "C": [("win", NS)],
            "D": [("wco", self.DS), ("wout", self.DS), ("wpg", self.DS), ("wpp", self.DS * PD // D),
                  ("pool", 4 * PGS * PG // D)],
            "A1": [("g1", self.FB)], "A2": [("u1", self.FB)], "A3": [("d1", self.FB)]}
        self.off, self.rows = {}, {}
        for pc in PIECES:
            o = 0
            for n, r in self.pieces[pc]:
                self.off[n], self.rows[n] = o, r
                o += r


def _pack_piece(w, l, lay, pc, dtype):
    D = lay.D
    make = {"g1": lambda: w["ffn1_w_gate"][l].T, "u1": lambda: w["ffn1_w_up"][l].T,
            "d1": lambda: w["ffn1_w_down"][l], "g2": lambda: w["ffn2_w_gate"][l].T,
            "u2": lambda: w["ffn2_w_up"][l].T, "d2": lambda: w["ffn2_w_down"][l],
            "win": lambda: w["w_in"][l].T, "wco": lambda: w["conv_w_out"][l], "wout": lambda: w["w_out"][l],
            "wpg": lambda: w["ple_w_gate"][l], "wpp": lambda: w["ple_w_proj"][l].T.reshape(-1, D),
            "pool": lambda: w["pool_w"][l].reshape(-1, D)}
    return jnp.concatenate([make[n]() for n, _ in lay.pieces[pc]], axis=0).astype(dtype)


def _unpack_piece(slab, lay, pc):
    PD, PG, PGS, DS = lay.PD, lay.PG, lay.PGS, lay.DS
    undo = {"g1": ("ffn1_w_gate", lambda a: a.T), "u1": ("ffn1_w_up", lambda a: a.T),
            "d1": ("ffn1_w_down", lambda a: a), "g2": ("ffn2_w_gate", lambda a: a.T),
            "u2": ("ffn2_w_up", lambda a: a.T), "d2": ("ffn2_w_down", lambda a: a),
            "win": ("w_in", lambda a: a.T), "wco": ("conv_w_out", lambda a: a), "wout": ("w_out", lambda a: a),
            "wpg": ("ple_w_gate", lambda a: a), "wpp": ("ple_w_proj", lambda a: a.reshape(DS, PD).T),
            "pool": ("pool_w", lambda a: a.reshape(4, PGS, PG))}
    out, o = {}, 0
    for n, r in lay.pieces[pc]:
        name, fn = undo[n]
        out[name] = fn(slab[o:o + r])
        o += r
    return out


def _place():
    return lax.axis_index("x"), lax.axis_index("y"), lax.axis_index("c")


FLIPS = [(a, b, d) for a in (0, 1) for b in (0, 1) for d in (0, 1)][1:]


class _GatherJob:
    def __init__(self, tag, slab):
        self.tag, self.ins = tag, [slab]
        self.outs = [jax.ShapeDtypeStruct((NDEV,) + slab.shape, slab.dtype)]
        self.scratch = [pltpu.SemaphoreType.DMA((7,)), pltpu.SemaphoreType.DMA((7,)), pltpu.SemaphoreType.DMA]
        self.results = None

    def _plan(self, ins, outs, sems):
        (x_ref,), (out_ref,), (send_sems, recv_sems, local_sem) = ins, outs, sems
        x, y, c = _place()
        me, sibling = (x, y, c), (x, y, 1 - c)
        chips = [(1 - x, y), (x, 1 - y), (1 - x, 1 - y)]

        def rows(px, py, pc):
            return out_ref.at[4 * px + 2 * py + pc]

        def copy(k, block, to, src=None):
            return pltpu.make_async_remote_copy(
                src_ref=rows(*block) if src is None else src, dst_ref=rows(*block),
                send_sem=send_sems.at[k], recv_sem=recv_sems.at[k], device_id=to, device_id_type=MESH)

        mine = pltpu.make_async_copy(x_ref, rows(*me), local_sem)
        first = [copy(0, me, sibling, src=x_ref)]
        first += [copy(1 + j, me, (*chip, c), src=x_ref) for j, chip in enumerate(chips)]
        passed = [copy(4 + j, (*chip, c), sibling) for j, chip in enumerate(chips)]
        landed = [copy(1 + j, (*chip, c), me) for j, chip in enumerate(chips)]
        late = [copy(0, sibling, me)] + [copy(4 + j, (*chip, 1 - c), me) for j, chip in enumerate(chips)]
        return mine, first, passed, landed, late

    def start(self, ins, outs, sems):
        mine, first, _, _, _ = self._plan(ins, outs, sems)
        mine.start()
        for cp in first:
            cp.start()

    def middle(self, ins, outs, sems):
        _, _, passed, landed, _ = self._plan(ins, outs, sems)
        for got, cp in zip(landed, passed):
            got.wait_recv()
            cp.start()

    def finish(self, ins, outs, sems):
        mine, first, passed, _, late = self._plan(ins, outs, sems)
        for got in late:
            got.wait_recv()
        for cp in first + passed:
            cp.wait_send()
        mine.wait()


class _ScatterJob:
    def __init__(self, tag, grads):
        self.tag, self.ins = tag, list(grads)
        self.row_counts = [g.shape[1] for g in grads]
        self.outs = [jax.ShapeDtypeStruct((NDEV, sum(self.row_counts), grads[0].shape[2]), grads[0].dtype)]
        n = len(grads)
        self.scratch = [pltpu.SemaphoreType.DMA((n, 7)), pltpu.SemaphoreType.DMA((n, 7)), pltpu.SemaphoreType.DMA((n,))]
        self.results = None

    def _plan(self, ins, outs, sems):
        (out_ref,), (send_sems, recv_sems, local_sems) = outs, sems
        x, y, c = _place()
        remote, local, off = [], [], 0
        for i, (g_ref, rows) in enumerate(zip(ins, self.row_counts)):
            span = pl.ds(off, rows)
            for k, (a, b, d) in enumerate(FLIPS):
                px, py, pc = x ^ a, y ^ b, c ^ d
                remote.append(pltpu.make_async_remote_copy(
                    src_ref=g_ref.at[4 * px + 2 * py + pc], dst_ref=out_ref.at[k, span, :], send_sem=send_sems.at[i, k],
                    recv_sem=recv_sems.at[i, k], device_id=(px, py, pc), device_id_type=MESH))
            local.append(pltpu.make_async_copy(g_ref.at[4 * x + 2 * y + c], out_ref.at[7, span, :], local_sems.at[i]))
            off += rows
        return remote, local

    def start(self, ins, outs, sems):
        remote, local = self._plan(ins, outs, sems)
        for cp in remote + local:
            cp.start()

    def middle(self, ins, outs, sems):
        pass

    def finish(self, ins, outs, sems):
        remote, local = self._plan(ins, outs, sems)
        for cp in remote:
            cp.wait_recv()
        for cp in remote:
            cp.wait_send()
        for cp in local:
            cp.wait()


class _RowsGatherJob:
    def __init__(self, tag, rows):
        self.tag, self.ins = tag, [rows]
        self.outs = [jax.ShapeDtypeStruct((NDEV,) + rows.shape, rows.dtype)]
        self.scratch = [pltpu.SemaphoreType.DMA((7,)), pltpu.SemaphoreType.DMA((7,)), pltpu.SemaphoreType.DMA]
        self.results = None

    def _plan(self, ins, outs, sems):
        (x_ref,), (out_ref,), (send_sems, recv_sems, local_sem) = ins, outs, sems
        x, y, c = _place()
        me = 4 * x + 2 * y + c
        mine = pltpu.make_async_copy(x_ref, out_ref.at[me], local_sem)
        sends, lands = [], []
        for k, (a, b, d) in enumerate(FLIPS):
            px, py, pc = x ^ a, y ^ b, c ^ d
            for dst, keep in ((out_ref.at[me], sends), (out_ref.at[4 * px + 2 * py + pc], lands)):
                keep.append(pltpu.make_async_remote_copy(
                    src_ref=x_ref, dst_ref=dst, send_sem=send_sems.at[k], recv_sem=recv_sems.at[k],
                    device_id=(px, py, pc), device_id_type=MESH))
        return mine, sends, lands

    def start(self, ins, outs, sems):
        mine, sends, _ = self._plan(ins, outs, sems)
        mine.start()
        for cp in sends:
            cp.start()

    def middle(self, ins, outs, sems):
        pass

    def finish(self, ins, outs, sems):
        mine, sends, lands = self._plan(ins, outs, sems)
        for cp in lands:
            cp.wait_recv()
        for cp in sends:
            cp.wait_send()
        mine.wait()


def _fold_rows(parts):
    D = parts[0].shape[1]
    counts = [a.shape[0] // 8 for a in parts]
    total = -(-sum(counts) // 8) * 8

    def body(*refs):
        out = refs[-1]
        out[...] = jnp.zeros_like(out)
        row = 0
        for ref, k in zip(refs[:-1], counts):
            for j in range(k):
                out[pl.ds(row + j, 1), :] = jnp.sum(ref[pl.ds(8 * j, 8), :], axis=0, keepdims=True)
            row += k

    vm = pl.BlockSpec(memory_space=pltpu.VMEM)
    return pl.pallas_call(body, name="fold_rows", out_shape=jax.ShapeDtypeStruct((total, D), F32),
                          in_specs=[vm] * len(parts), out_specs=vm)(*parts)


def _sum_slots(slots):
    n, S, D = slots.shape

    def body(s_ref, o_ref):
        acc = s_ref[0]
        for j in range(1, n):
            acc = acc + s_ref[j]
        o_ref[...] = acc

    vm = pl.BlockSpec(memory_space=pltpu.VMEM)
    return pl.pallas_call(body, name="sum_slots", out_shape=jax.ShapeDtypeStruct((S, D), slots.dtype),
                          in_specs=[vm], out_specs=vm)(slots)


def _launch(name, body, grid, args, in_specs, out_specs, out_shape, scratch, jobs=()):
    grid = (grid,) if isinstance(grid, int) else tuple(grid)
    steps = grid[0] * (grid[1] if len(grid) == 2 else 1)
    n_in, n_out, n_sc = len(args), len(out_shape), len(scratch)
    j_in = [a for jb in jobs for a in jb.ins]
    j_out = [o for jb in jobs for o in jb.outs]
    j_sc = [s for jb in jobs for s in jb.scratch]
    mid = (17 * steps) // 20

    def wrapped(*refs):
        c_in, refs = refs[:n_in], refs[n_in:]
        m_in, refs = refs[:len(j_in)], refs[len(j_in):]
        c_out, refs = refs[:n_out], refs[n_out:]
        m_out, refs = refs[:len(j_out)], refs[len(j_out):]
        c_sc, m_sc = refs[:n_sc], refs[n_sc:]
        bound, a, b, c = [], 0, 0, 0
        for jb in jobs:
            bound.append((jb, m_in[a:a + len(jb.ins)], m_out[b:b + len(jb.outs)], m_sc[c:c + len(jb.scratch)]))
            a, b, c = a + len(jb.ins), b + len(jb.outs), c + len(jb.scratch)
        i = pl.program_id(0) if len(grid) == 1 else pl.program_id(0) * grid[1] + pl.program_id(1)

        def phase(step, which):
            if jobs:
                @pl.when(i == step)
                def _():
                    for jb, ins, outs, sems in bound:
                        getattr(jb, which)(ins, outs, sems)

        phase(0, "start")
        if body is not None:
            body(*c_in, *c_out, *c_sc)
        phase(mid, "middle")
        phase(steps - 1, "finish")

    outs = pl.pallas_call(
        wrapped, name=name + "".join("_" + jb.tag for jb in jobs), grid=grid,
        in_specs=list(in_specs) + [ANY] * len(j_in), out_specs=list(out_specs) + [ANY] * len(j_out),
        out_shape=list(out_shape) + j_out, scratch_shapes=list(scratch) + j_sc, compiler_params=_params(len(grid)),
    )(*args, *j_in)
    pos = n_out
    for jb in jobs:
        jb.results = list(outs[pos:pos + len(jb.outs)])
        pos += len(jb.outs)
    return list(outs[:n_out])


def _run_jobs(name, jobs):
    _launch(name, None, 1, [], [], [], [], [], jobs)


def _sum_parts(got, name):
    n, rows, D = got.shape
    tr = _tile(rows, 512)
    order = [n - 1] + list(range(n - 1))

    def body(*refs):
        g = refs[0][...].astype(F32)
        for pr in refs[1:n]:
            g = g + pr[...].astype(F32)
        refs[-1][...] = g

    specs = [pl.BlockSpec((None, tr, D), functools.partial(lambda k, i: (k, i, 0), k)) for k in order]
    return pl.pallas_call(
        body, name=name, grid=(rows // tr,), in_specs=specs, out_specs=_rows(tr, D),
        out_shape=jax.ShapeDtypeStruct((rows, D), F32), compiler_params=_params(),
    )(*[got] * n)


def _adamw(name, entries, by_layer):
    n = len(entries)
    L = entries[0][0].shape[0]

    def body(*refs):
        ins, outs = refs[:4 * n], refs[4 * n:]
        for e in range(n):
            w_ref, g_ref, m_ref, v_ref = ins[4 * e:4 * e + 4]
            d_out, m_out, v_out = outs[3 * e:3 * e + 3]
            g = g_ref[...]
            m_new = ADAM_B1 * m_ref[...] + (1.0 - ADAM_B1) * g
            v_new = ADAM_B2 * v_ref[...] + (1.0 - ADAM_B2) * (g * g)
            m_hat = m_new / (1.0 - ADAM_B1 ** ADAM_STEP)
            v_hat = v_new / (1.0 - ADAM_B2 ** ADAM_STEP)
            d_out[...] = -ADAM_LR * (m_hat / (jnp.sqrt(v_hat) + ADAM_EPS) + ADAM_WD * w_ref[...])
            m_out[...] = m_new
            v_out[...] = v_new

    def spec(a):
        rest = (0,) * (a.ndim - 1)
        if by_layer:
            return pl.BlockSpec((None,) + a.shape[1:], lambda l: (l,) + rest)
        return pl.BlockSpec(a.shape, lambda l: (0,) + rest)

    flat = [a for e in entries for a in e]
    outs = pl.pallas_call(
        body, name=name, grid=(L if by_layer else 1,),
        in_specs=[spec(a) for a in flat], out_specs=[spec(e[0]) for e in entries for _ in range(3)],
        out_shape=[jax.ShapeDtypeStruct(e[0].shape, F32) for e in entries for _ in range(3)],
        compiler_params=_params(),
    )(*flat)
    return [tuple(outs[3 * e:3 * e + 3]) for e in range(n)]


def _tn_matmul(xa, ya, name, tmm, y_scale=1.0, jobs=()):
    T, M = xa.shape
    tn = ya.shape[1]
    tt = _tile(T, 1024)
    nb = M // tmm

    def body(x_ref, y_ref, o_ref, acc):
        k = pl.program_id(1)

        @pl.when(k == 0)
        def _():
            acc[...] = jnp.zeros_like(acc)

        y = y_ref[...]
        if y_scale != 1.0 or y.dtype != BF16:
            y = (y_scale * y).astype(BF16)
        acc[...] += _tn(x_ref[...], y)

        @pl.when(k == pl.num_programs(1) - 1)
        def _():
            o_ref[...] = acc[...].astype(o_ref.dtype)

    return _launch(
        name, body, (nb, T // tt), [xa, ya],
        [pl.BlockSpec((tt, tmm), lambda b, k: (k, b)), pl.BlockSpec((tt, tn), lambda b, k: (k, 0))],
        [pl.BlockSpec((tmm, tn), lambda b, k: (b, 0))], [jax.ShapeDtypeStruct((M, tn), BF16)],
        [pltpu.VMEM((tmm, tn), F32)], jobs)[0]


def _tn_groups(xa, ya, groups, name):
    T, M = xa.shape
    w = M // groups
    tt = _tile(T, 1024)
    steps = T // tt

    def body(x_ref, y_ref, o_ref, acc):
        k = pl.program_id(0)

        @pl.when(k == 0)
        def _():
            acc[...] = jnp.zeros_like(acc)

        for g in range(groups):
            cols = pl.ds(g * w, w)
            acc[cols, :] += _tn(x_ref[:, cols], y_ref[:, cols])

        @pl.when(k == steps - 1)
        def _():
            o_ref[...] = acc[...].astype(o_ref.dtype)

    return pl.pallas_call(
        body, name=name, grid=(steps,), in_specs=[_rows(tt, M), _rows(tt, M)], out_specs=_const((M, w)),
        out_shape=jax.ShapeDtypeStruct((M, w), BF16), scratch_shapes=[pltpu.VMEM((M, w), F32)],
        compiler_params=_params(),
    )(xa, ya)


def _ffn_fwd(h, gain, G, lay, jobs=()):
    T, D = h.shape
    FB = lay.FB
    FP = NDEV * FB
    offs = (0, FB, 2 * FB)
    tm = _tile(T, 512)

    def body(h_ref, gain_ref, g_hbm, ho_ref, go_ref, uo_ref, n_ref, wg, wu, wd, sems):
        @pl.when(pl.program_id(0) == 0)
        def _():
            _fetch(g_hbm, [(offs[0], FB, wg), (offs[1], FB, wu), (offs[2], FB, wd)], sems)

        h = h_ref[...]
        xh, _ = _rms(h)
        n = (xh * gain_ref[...]).astype(BF16)
        n_ref[...] = n

        def project(sl):
            return _nt(n, wg[sl, :]), _nt(n, wu[sl, :])

        def activate(sl, gu):
            g, u = gu
            go_ref[:, sl] = g.astype(BF16)
            uo_ref[:, sl] = u.astype(BF16)
            return (g * _sig(g) * u).astype(BF16)

        def contract(sl, a, acc):
            return acc + _nn(a, wd[sl, :])

        acc = _staggered(_hidden_chunks(FP), project, activate, contract, jnp.zeros((tm, D), F32))
        ho_ref[...] = h + 0.5 * acc

    return _launch(
        "ffn_fwd", body, T // tm, [h, gain, G],
        [_rows(tm, D), _const((1, D)), ANY], [_rows(tm, D), _rows(tm, FP), _rows(tm, FP), _rows(tm, D)],
        [jax.ShapeDtypeStruct((T, D), F32), jax.ShapeDtypeStruct((T, FP), BF16), jax.ShapeDtypeStruct((T, FP), BF16),
         jax.ShapeDtypeStruct((T, D), BF16)],
        [pltpu.VMEM((FP, D), BF16)] * 3 + [pltpu.SemaphoreType.DMA((3, NDEV))], jobs)


def _ffn_bwd(d, h, gain, ga, ua, G, lay, jobs=()):
    T, D = h.shape
    FB = lay.FB
    FP = NDEV * FB
    offs = (0, FB, 2 * FB)
    tm = _tile(T, 256)

    def body(d_ref, h_ref, gain_ref, ga_ref, ua_ref, g_hbm,
             do_ref, dg_ref, du_ref, a_ref, gg_ref, wg, wu, wd, sems):
        @pl.when(pl.program_id(0) == 0)
        def _():
            _fetch(g_hbm, [(offs[0], FB, wg), (offs[1], FB, wu), (offs[2], FB, wd)], sems)
            gg_ref[...] = jnp.zeros_like(gg_ref)

        d = d_ref[...]
        gain_v = gain_ref[...]
        xh, r = _rms(h_ref[...])
        dh = (0.5 * d).astype(BF16)

        def project(sl):
            return _nt(dh, wd[sl, :])

        def activate(sl, da):
            g = ga_ref[:, sl].astype(F32)
            u = ua_ref[:, sl].astype(F32)
            s = _sig(g)
            silu = g * s
            a_ref[:, sl] = (silu * u).astype(BF16)
            dgv = (da * u * (s * (1.0 + g * (1.0 - s)))).astype(BF16)
            duv = (da * silu).astype(BF16)
            dg_ref[:, sl] = dgv
            du_ref[:, sl] = duv
            return dgv, duv

        def contract(sl, grads, acc):
            return acc + _nn(grads[0], wg[sl, :]) + _nn(grads[1], wu[sl, :])

        dn = _staggered(_hidden_chunks(FP, BWD_CHUNK), project, activate, contract, jnp.zeros((tm, D), F32))
        gg_ref[...] += _colsum8(dn * xh)
        do_ref[...] = d + _rms_bwd(dn, xh, r, gain_v)

    wide = jax.ShapeDtypeStruct((T, FP), BF16)
    return _launch(
        "ffn_bwd", body, T // tm, [d, h, gain, ga, ua, G],
        [_rows(tm, D), _rows(tm, D), _const((1, D)), _rows(tm, FP), _rows(tm, FP), ANY],
        [_rows(tm, D), _rows(tm, FP), _rows(tm, FP), _rows(tm, FP), _const((8, D))],
        [jax.ShapeDtypeStruct((T, D), F32), wide, wide, wide, jax.ShapeDtypeStruct((8, D), F32)],
        [pltpu.VMEM((FP, D), BF16)] * 3 + [pltpu.SemaphoreType.DMA((3, NDEV))], jobs)


def _inproj_fwd(h, gain, G, lay):
    T, D = h.shape
    NS = lay.NS
    NIN, CH = NDEV * NS, 2 * NS
    tm = _tile(T, 512)

    def body(h_ref, gain_ref, g_hbm, z_ref, win, sems):
        @pl.when(pl.program_id(0) == 0)
        def _():
            _fetch(g_hbm, [(lay.off["win"], NS, win)], sems)

        xh, _ = _rms(h_ref[...])
        n = (xh * gain_ref[...]).astype(BF16)
        for j in range(NIN // CH):
            sl = pl.ds(j * CH, CH)
            z_ref[:, sl] = _nt(n, win[sl, :]).astype(BF16)

    return pl.pallas_call(
        body, name="inproj_fwd", grid=(T // tm,),
        in_specs=[_rows(tm, D), _const((1, D)), ANY], out_specs=_rows(tm, NIN),
        out_shape=jax.ShapeDtypeStruct((T, NIN), BF16),
        scratch_shapes=[pltpu.VMEM((NIN, D), BF16), pltpu.SemaphoreType.DMA((1, NDEV))],
        compiler_params=_params(),
    )(h, gain, G)


def _conv_chunks(tm, D):
    rb, lc = min(tm, 64), min(D, 256)
    return [(r0, l0, rb, lc) for r0 in range(0, tm, rb) for l0 in range(0, D, lc)]


SUBLANES = 8


def _preshift(sh, n_rows):
    for r in range(1, SUBLANES):
        sh[r, pl.ds(0, n_rows), :] = sh[0, pl.ds(r, n_rows), :]


def _window(sh, start, rows, lanes):
    r = start % SUBLANES
    return sh[r, pl.ds(start - r, rows), lanes]


def _mixer_fwd(z, h, pool_w, pscale, cw, cb, lng, lnb, G, lay, jobs=()):
    T, D = h.shape
    PG, DS = lay.PG, lay.DS
    tm = _tile(T, 256)
    hb = tm // HALO

    def body(z_ref, zp_ref, h_ref, pw_ref, ps_ref, cw_ref, cb_ref, lg_ref, lb_ref, g_hbm,
             ho_ref, c1_ref, q_ref, cc_ref, pool_ref, ext_p, sh_c, a_s, wco, wout, sems):
        i = pl.program_id(0)

        @pl.when(i == 0)
        def _():
            _fetch(g_hbm, [(lay.off["wco"], DS, wco), (lay.off["wout"], DS, wout)], sems)

        live = jnp.where(i > 0, 1.0, 0.0).astype(F32)
        ext_p[pl.ds(0, HALO), :] = zp_ref[:, pl.ds(0, D)].astype(F32) * live
        ext_p[pl.ds(HALO, tm), :] = z_ref[:, pl.ds(0, D)].astype(F32)
        sh_c[0, pl.ds(0, HALO), :] = (zp_ref[:, pl.ds(D, D)].astype(F32)
                                      * _sig(zp_ref[:, pl.ds(2 * D, D)].astype(F32)) * live)
        sh_c[0, pl.ds(HALO, tm), :] = z_ref[:, pl.ds(D, D)].astype(F32) * _sig(z_ref[:, pl.ds(2 * D, D)].astype(F32))
        _preshift(sh_c, tm + HALO - SUBLANES)

        t = i * tm + lax.broadcasted_iota(jnp.int32, (tm, 1), 0)
        for g, w in enumerate(POOL_WINDOWS):
            sl = pl.ds(g * PG, PG)
            s = ext_p[pl.ds(HALO, tm), sl]
            zc = s
            for j in range(1, w):
                s = s + ext_p[pl.ds(HALO - j, tm), sl]
            inv = 1.0 / jnp.minimum(t + 1, w).astype(F32)
            pooled = (s * inv - zc).astype(BF16)
            pool_ref[:, sl] = pooled
            qv = _nn(pooled, pw_ref[g])
            q_ref[:, sl] = qv.astype(BF16)
            a_s[:, sl] = qv * ps_ref[:, sl]

        for r0, l0, rb, lc in _conv_chunks(tm, D):
            ls = pl.ds(l0, lc)
            acc = jnp.zeros((rb, lc), F32) + cb_ref[:, ls]
            for k in range(CONV_K):
                acc = acc + cw_ref[pl.ds(k, 1), ls] * _window(sh_c, r0 + HALO - (CONV_K - 1) + k, rb, ls)
            c1_ref[pl.ds(r0, rb), ls] = acc

        xhat, _ = _ln(c1_ref[...])
        c2 = xhat * lg_ref[...] + lb_ref[...]
        c3 = (c2 * _sig(c2)).astype(BF16)
        cc = _nn(c3, wco[...])
        cc_ref[...] = cc.astype(BF16)
        gp = z_ref[:, pl.ds(3 * D, D)].astype(F32)
        gc = z_ref[:, pl.ds(4 * D, D)].astype(F32)
        m = (_sig(gp) * a_s[...] + _sig(gc) * cc).astype(BF16)
        ho_ref[...] = h_ref[...] + _nn(m, wout[...])

    act = jax.ShapeDtypeStruct((T, D), BF16)
    return _launch(
        "mixer_fwd", body, T // tm, [z, z, h, pool_w, pscale, cw, cb, lng, lnb, G],
        [_rows(tm, 5 * D), pl.BlockSpec((HALO, 5 * D), lambda i: (jnp.maximum(i * hb - 1, 0), 0)),
         _rows(tm, D), _const((4, PG, PG)), _const((1, D)), _const((CONV_KP, D)), _const((1, D)),
         _const((1, D)), _const((1, D)), ANY],
        [_rows(tm, D)] * 5,
        [jax.ShapeDtypeStruct((T, D), F32), jax.ShapeDtypeStruct((T, D), F32), act, act, act],
        [pltpu.VMEM((HALO + tm, D), F32), pltpu.VMEM((SUBLANES, HALO + tm, D), F32), pltpu.VMEM((tm, D), F32),
         pltpu.VMEM((D, D), BF16), pltpu.VMEM((D, D), BF16), pltpu.SemaphoreType.DMA((2, NDEV))], jobs)


def _mixer_bwd_rows(d, z, c1, qa, cca, pool_w, pscale, lng, lnb, G, lay):
    T, D = d.shape
    PG, DS = lay.PG, lay.DS
    tm = _tile(T, 256)

    def body(d_ref, zgp_ref, zgc_ref, c1_ref, q_ref, cc_ref, pw_ref, ps_ref, lg_ref, lb_ref, g_hbm,
             m_ref, dcc_ref, c3_ref, dq_ref, dpool_ref, dc1_ref, dzg_ref, gps_ref, glg_ref, glb_ref, gcb_ref,
             wco, wout, sems):
        @pl.when(pl.program_id(0) == 0)
        def _():
            _fetch(g_hbm, [(lay.off["wco"], DS, wco), (lay.off["wout"], DS, wout)], sems)
            for ref in (gps_ref, glg_ref, glb_ref, gcb_ref):
                ref[...] = jnp.zeros_like(ref)

        dm = _nt(d_ref[...].astype(BF16), wout[...])
        q = q_ref[...].astype(F32)
        cc = cc_ref[...].astype(F32)
        ps = ps_ref[...]
        sp = _sig(zgp_ref[...].astype(F32))
        sc = _sig(zgc_ref[...].astype(F32))
        a = q * ps
        m_ref[...] = (sp * a + sc * cc).astype(BF16)
        da = dm * sp
        dzg_ref[:, pl.ds(0, D)] = (dm * a * sp * (1.0 - sp)).astype(BF16)
        dzg_ref[:, pl.ds(D, D)] = (dm * cc * sc * (1.0 - sc)).astype(BF16)
        gps_ref[...] += _colsum8(da * q)
        dq = (da * ps).astype(BF16)
        dq_ref[...] = dq
        for g in range(len(POOL_WINDOWS)):
            sl = pl.ds(g * PG, PG)
            dpool_ref[:, sl] = _nt(dq_ref[:, sl], pw_ref[g])

        dcc = (dm * sc).astype(BF16)
        dcc_ref[...] = dcc
        xhat, rstd = _ln(c1_ref[...])
        lg = lg_ref[...]
        c2 = xhat * lg + lb_ref[...]
        s2 = _sig(c2)
        c3_ref[...] = (c2 * s2).astype(BF16)
        dc2 = _nt(dcc, wco[...]) * (s2 * (1.0 + c2 * (1.0 - s2)))
        glg_ref[...] += _colsum8(dc2 * xhat)
        glb_ref[...] += _colsum8(dc2)
        dxh = dc2 * lg
        dc1 = rstd * (dxh - jnp.mean(dxh, axis=-1, keepdims=True)
                      - xhat * jnp.mean(dxh * xhat, axis=-1, keepdims=True))
        dc1_ref[...] = dc1
        gcb_ref[...] += _colsum8(dc1)

    act = jax.ShapeDtypeStruct((T, D), BF16)
    full = jax.ShapeDtypeStruct((T, D), F32)
    vec = jax.ShapeDtypeStruct((8, D), F32)
    return pl.pallas_call(
        body, name="mixer_bwd_rows", grid=(T // tm,),
        in_specs=[_rows(tm, D), pl.BlockSpec((tm, D), lambda i: (i, 3)), pl.BlockSpec((tm, D), lambda i: (i, 4)),
                  _rows(tm, D), _rows(tm, D), _rows(tm, D),
                  _const((4, PG, PG)), _const((1, D)), _const((1, D)), _const((1, D)), ANY],
        out_specs=[_rows(tm, D)] * 6 + [_rows(tm, 2 * D)] + [_const((8, D))] * 4,
        out_shape=[act, act, act, act, full, full, jax.ShapeDtypeStruct((T, 2 * D), BF16), vec, vec, vec, vec],
        scratch_shapes=[pltpu.VMEM((D, D), BF16), pltpu.VMEM((D, D), BF16), pltpu.SemaphoreType.DMA((2, NDEV))],
        compiler_params=_params(),
    )(d, z, z, c1, qa, cca, pool_w, pscale, lng, lnb, G)


def _mixer_bwd_time(dc1, dpool, z, cw, lay, jobs=()):
    T, D = dc1.shape
    PG = lay.PG
    tm = _tile(T, 256)
    hb = tm // HALO
    nt = T // tm

    def body(dc_ref, dcn_ref, dp_ref, dpn_ref, z_ref, zp_ref, cw_ref, dz_ref, gcw_ref, sh_d, ext_q, sh_c, dc0_s):
        i = pl.program_id(0)

        @pl.when(i == 0)
        def _():
            gcw_ref[...] = jnp.zeros_like(gcw_ref)

        live_p = jnp.where(i > 0, 1.0, 0.0).astype(F32)
        live_n = jnp.where(i < nt - 1, 1.0, 0.0).astype(F32)
        sh_d[0, pl.ds(0, tm), :] = dc_ref[...]
        sh_d[0, pl.ds(tm, HALO), :] = dcn_ref[...] * live_n
        _preshift(sh_d, tm + HALO - SUBLANES)
        zg = z_ref[:, pl.ds(2 * D, D)].astype(F32)
        za = z_ref[:, pl.ds(D, D)].astype(F32)
        sg = _sig(zg)
        sh_c[0, pl.ds(0, HALO), :] = (zp_ref[:, pl.ds(D, D)].astype(F32)
                                      * _sig(zp_ref[:, pl.ds(2 * D, D)].astype(F32)) * live_p)
        sh_c[0, pl.ds(HALO, tm), :] = za * sg
        _preshift(sh_c, tm + HALO - SUBLANES)

        t = i * tm + lax.broadcasted_iota(jnp.int32, (tm, 1), 0)
        tn = (i + 1) * tm + lax.broadcasted_iota(jnp.int32, (HALO, 1), 0)
        for g, w in enumerate(POOL_WINDOWS):
            sl = pl.ds(g * PG, PG)
            ext_q[pl.ds(0, tm), sl] = dp_ref[:, sl] * (1.0 / jnp.minimum(t + 1, w).astype(F32))
            ext_q[pl.ds(tm, HALO), sl] = dpn_ref[:, sl] * (live_n / jnp.minimum(tn + 1, w).astype(F32))
        for g, w in enumerate(POOL_WINDOWS):
            sl = pl.ds(g * PG, PG)
            s = ext_q[pl.ds(0, tm), sl]
            for j in range(1, w):
                s = s + ext_q[pl.ds(j, tm), sl]
            dz_ref[:, sl] = (s - dp_ref[:, sl]).astype(BF16)

        for r0, l0, rb, lc in _conv_chunks(tm, D):
            ls = pl.ds(l0, lc)
            acc = jnp.zeros((rb, lc), F32)
            for j in range(CONV_K):
                acc = acc + cw_ref[pl.ds(CONV_K - 1 - j, 1), ls] * _window(sh_d, r0 + j, rb, ls)
            dc0_s[pl.ds(r0, rb), ls] = acc
        dc0 = dc0_s[...]
        dz_ref[:, pl.ds(D, D)] = (dc0 * sg).astype(BF16)
        dz_ref[:, pl.ds(2 * D, D)] = (dc0 * za * sg * (1.0 - sg)).astype(BF16)

        for r0, l0, rb, lc in _conv_chunks(tm, D):
            ls = pl.ds(l0, lc)
            dcv = dc_ref[pl.ds(r0, rb), ls]
            for k in range(CONV_K):
                gcw_ref[pl.ds(8 * k, 8), ls] += _colsum8(dcv * _window(sh_c, r0 + HALO - (CONV_K - 1) + k, rb, ls))

    nxt = lambda i: (jnp.minimum((i + 1) * hb, T // HALO - 1), 0)
    return _launch(
        "mixer_bwd_time", body, nt, [dc1, dc1, dpool, dpool, z, z, cw],
        [_rows(tm, D), pl.BlockSpec((HALO, D), nxt), _rows(tm, D), pl.BlockSpec((HALO, D), nxt),
         _rows(tm, 5 * D), pl.BlockSpec((HALO, 5 * D), lambda i: (jnp.maximum(i * hb - 1, 0), 0)),
         _const((CONV_KP, D))],
        [_rows(tm, 3 * D), _const((CONV_KP * 8, D))],
        [jax.ShapeDtypeStruct((T, 3 * D), BF16), jax.ShapeDtypeStruct((CONV_KP * 8, D), F32)],
        [pltpu.VMEM((SUBLANES, tm + HALO, D), F32), pltpu.VMEM((tm + HALO, D), F32),
         pltpu.VMEM((SUBLANES, HALO + tm, D), F32), pltpu.VMEM((tm, D), F32)], jobs)


def _inproj_bwd(d, h, gain, dzm, dzg, G, lay):
    T, D = h.shape
    NS = lay.NS
    NIN = NDEV * NS
    tm = _tile(T, 512)

    def body(d_ref, h_ref, gain_ref, dzm_ref, dzg_ref, g_hbm, do_ref, u_ref, gg_ref, win, sems):
        @pl.when(pl.program_id(0) == 0)
        def _():
            _fetch(g_hbm, [(lay.off["win"], NS, win)], sems)
            gg_ref[...] = jnp.zeros_like(gg_ref)

        gain_v = gain_ref[...]
        xh, r = _rms(h_ref[...])
        u_ref[...] = (xh * gain_v).astype(BF16)
        dn = jnp.zeros((tm, D), F32)
        for j in range(3):
            dn = dn + _nn(dzm_ref[:, pl.ds(j * D, D)], win[pl.ds(j * D, D), :])
        for j in range(2):
            dn = dn + _nn(dzg_ref[:, pl.ds(j * D, D)], win[pl.ds((3 + j) * D, D), :])
        gg_ref[...] += _colsum8(dn * xh)
        do_ref[...] = d_ref[...] + _rms_bwd(dn, xh, r, gain_v)

    return pl.pallas_call(
        body, name="inproj_bwd", grid=(T // tm,),
        in_specs=[_rows(tm, D), _rows(tm, D), _const((1, D)), _rows(tm, 3 * D), _rows(tm, 2 * D), ANY],
        out_specs=[_rows(tm, D), _rows(tm, D), _const((8, D))],
        out_shape=[jax.ShapeDtypeStruct((T, D), F32), jax.ShapeDtypeStruct((T, D), BF16),
                   jax.ShapeDtypeStruct((8, D), F32)],
        scratch_shapes=[pltpu.VMEM((NIN, D), BF16), pltpu.SemaphoreType.DMA((1, NDEV))],
        compiler_params=_params(),
    )(d, h, gain, dzm, dzg, G)


def _ple_fwd(h, pe, gain, wppt, G, lay, head=None):
    T, D = h.shape
    PD, DS = lay.PD, lay.DS
    tm = _tile(T, 512)

    def body(*refs):
        if head is None:
            h_ref, p_ref, gain_ref, wpp_ref, g_hbm, ho_ref, gate_ref, wpg, sems = refs
        else:
            (h_ref, p_ref, gain_ref, wpp_ref, g_hbm, t_ref, fgain_ref,
             do_ref, dpre_ref, de_ref, n_ref, pb_ref, gg_ref, loss_ref, fg_ref, wpg, sems) = refs

        @pl.when(pl.program_id(0) == 0)
        def _():
            _fetch(g_hbm, [(lay.off["wpg"], DS, wpg)], sems)
            if head is not None:
                for ref in (gg_ref, loss_ref, fg_ref):
                    ref[...] = jnp.zeros_like(ref)

        h = h_ref[...]
        gain_v = gain_ref[...]
        xh, r = _rms(h)
        n = (xh * gain_v).astype(BF16)
        gate = _sig(_nn(n, wpg[...]))
        pb = p_ref[...].astype(BF16)
        e = _nt(pb, wpp_ref[...])
        out = h + gate * e
        if head is None:
            gate_ref[...] = gate.astype(BF16)
            ho_ref[...] = out
            return
        fgain = fgain_ref[...]
        yh, ry = _rms(out)
        err = yh * fgain - t_ref[...]
        loss_ref[...] += _colsum8(err * err)
        dy = err * (1.0 / D)
        fg_ref[...] += _colsum8(dy * yh)
        d = _rms_bwd(dy, yh, ry, fgain)
        n_ref[...] = n
        pb_ref[...] = pb
        de_ref[...] = (d * gate).astype(BF16)
        dpre = (d * e * gate * (1.0 - gate)).astype(BF16)
        dpre_ref[...] = dpre
        dn = _nt(dpre, wpg[...])
        gg_ref[...] += _colsum8(dn * xh)
        do_ref[...] = d + _rms_bwd(dn, xh, r, gain_v)

    args = [h, pe, gain, wppt, G]
    in_specs = [_rows(tm, D), _rows(tm, PD), _const((1, D)), _const((D, PD)), ANY]
    act = jax.ShapeDtypeStruct((T, D), BF16)
    if head is None:
        out_specs = [_rows(tm, D), _rows(tm, D)]
        out_shape = [jax.ShapeDtypeStruct((T, D), F32), act]
    else:
        args += list(head)
        in_specs += [_rows(tm, D), _const((1, D))]
        out_specs = [_rows(tm, D)] * 4 + [_rows(tm, PD)] + [_const((8, D))] * 3
        out_shape = [jax.ShapeDtypeStruct((T, D), F32), act, act, act, jax.ShapeDtypeStruct((T, PD), BF16)]
        out_shape += [jax.ShapeDtypeStruct((8, D), F32)] * 3
    return pl.pallas_call(
        body, name="ple_fwd" if head is None else "ple_head", grid=(T // tm,),
        in_specs=in_specs, out_specs=out_specs, out_shape=out_shape,
        scratch_shapes=[pltpu.VMEM((D, D), BF16), pltpu.SemaphoreType.DMA((1, NDEV))],
        compiler_params=_params(),
    )(*args)


def _ple_bwd(d, h, pe, gate_a, gain, wppt, G, lay):
    T, D = h.shape
    PD, DS = lay.PD, lay.DS
    tm = _tile(T, 512)

    def body(d_ref, h_ref, p_ref, gate_ref, gain_ref, wpp_ref, g_hbm,
             do_ref, dpre_ref, de_ref, n_ref, pb_ref, gg_ref, wpg, sems):
        @pl.when(pl.program_id(0) == 0)
        def _():
            _fetch(g_hbm, [(lay.off["wpg"], DS, wpg)], sems)
            gg_ref[...] = jnp.zeros_like(gg_ref)

        d = d_ref[...]
        gain_v = gain_ref[...]
        xh, r = _rms(h_ref[...])
        n_ref[...] = (xh * gain_v).astype(BF16)
        pb = p_ref[...].astype(BF16)
        pb_ref[...] = pb
        e = _nt(pb, wpp_ref[...])
        gate = gate_ref[...].astype(F32)
        de_ref[...] = (d * gate).astype(BF16)
        dpre = (d * e * gate * (1.0 - gate)).astype(BF16)
        dpre_ref[...] = dpre
        dn = _nt(dpre, wpg[...])
        gg_ref[...] += _colsum8(dn * xh)
        do_ref[...] = d + _rms_bwd(dn, xh, r, gain_v)

    act = jax.ShapeDtypeStruct((T, D), BF16)
    return pl.pallas_call(
        body, name="ple_bwd", grid=(T // tm,),
        in_specs=[_rows(tm, D), _rows(tm, D), _rows(tm, PD), _rows(tm, D), _const((1, D)), _const((D, PD)), ANY],
        out_specs=[_rows(tm, D), _rows(tm, D), _rows(tm, D), _rows(tm, D), _rows(tm, PD), _const((8, D))],
        out_shape=[jax.ShapeDtypeStruct((T, D), F32), act, act, act, jax.ShapeDtypeStruct((T, PD), BF16),
                   jax.ShapeDtypeStruct((8, D), F32)],
        scratch_shapes=[pltpu.VMEM((D, D), BF16), pltpu.SemaphoreType.DMA((1, NDEV))],
        compiler_params=_params(),
    )(d, h, pe, gate_a, gain, wppt, G)


_BIG = ["ffn1_w_gate", "ffn1_w_up", "ffn1_w_down", "w_in", "pool_w", "conv_w_out", "w_out", "ffn2_w_gate",
        "ffn2_w_up", "ffn2_w_down", "ple_w_gate", "ple_w_proj"]
_VECS = ["ffn1_norm", "mix_norm", "pool_scale", "conv_dw_b", "conv_ln_g", "conv_ln_b", "ffn2_norm", "ple_norm"]
_WEIGHTS = ["ffn1_norm", "ffn1_w_gate", "ffn1_w_up", "ffn1_w_down", "mix_norm", "w_in", "pool_w", "pool_scale",
            "conv_dw_w", "conv_dw_b", "conv_ln_g", "conv_ln_b", "conv_w_out", "w_out", "ffn2_norm", "ffn2_w_gate",
            "ffn2_w_up", "ffn2_w_down", "ple_norm", "ple_w_gate", "ple_w_proj", "final_norm"]


def _step(x, p, loss_target, w, mom, var):
    T, D = x.shape[1], x.shape[2]
    L = p.shape[0]
    FS, NS = w["ffn1_w_gate"].shape[2], w["w_in"].shape[2]
    PD = p.shape[3]
    PGS, PG = w["pool_w"].shape[2], w["pool_w"].shape[3]
    CS = w["conv_dw_w"].shape[2]
    lay = _Layout(D, FS, NS, PD, PG, PGS)
    FB = lay.FB
    ax, ay, ac = _place()
    me = 4 * ax + 2 * ay + ac

    assert L == 2, "the exchange schedule below is written for two layers"
    gather = {(l, pc): _GatherJob(f"ag{pc}", _pack_piece(w, l, lay, pc, BF16)) for l in range(L) for pc in PIECES}
    fwd_jobs = {("ffn1", 0): [gather[0, "C"], gather[0, "D"]], ("mixer", 0): [gather[0, "B"]],
                ("ffn2", 0): [gather[1, "A"], gather[1, "D"]], ("ffn1", 1): [gather[1, "C"]],
                ("mixer", 1): [gather[1, "B"]]}
    cw_mine = jnp.pad(w["conv_dw_w"], ((0, 0), (0, CONV_KP - CONV_K), (0, 0)))
    taps_job = _RowsGatherJob("agW", cw_mine.reshape(L * CONV_KP * CS // D, D))
    _run_jobs("gather_first", [gather[0, "A"], taps_job])
    cw_full = taps_job.results[0].reshape(NDEV, L, CONV_KP, CS).transpose(1, 2, 0, 3).reshape(L, CONV_KP, D)

    def gathered(l, pc):
        return gather[l, pc].results[0]

    def small_mats(l):
        G = gathered(l, "D")
        wppt = G[:, lay.off["wpp"]:lay.off["wpp"] + lay.rows["wpp"]].reshape(D, PD)
        pw = G[:, lay.off["pool"]:lay.off["pool"] + lay.rows["pool"]].reshape(NDEV, 4, PGS, PG)
        return wppt, pw.transpose(1, 0, 2, 3).reshape(4, PG, PG)

    def vec(name, l):
        return w[name][l].reshape(1, D)

    h = x[0]
    saved = []
    for l in range(L):
        s = {"h0": h}
        h, s["g1"], s["u1"], s["n1"] = _ffn_fwd(h, vec("ffn1_norm", l), gathered(l, "A"), lay,
                                                fwd_jobs.get(("ffn1", l), ()))
        wppt, pw = small_mats(l)
        s["wppt"], s["pw"], s["h1"] = wppt, pw, h
        s["z"] = _inproj_fwd(h, vec("mix_norm", l), gathered(l, "C"), lay)
        h, s["c1"], s["q"], s["cc"], s["pooled"] = _mixer_fwd(
            s["z"], h, pw, vec("pool_scale", l), cw_full[l], vec("conv_dw_b", l), vec("conv_ln_g", l),
            vec("conv_ln_b", l), gathered(l, "D"), lay, fwd_jobs.get(("mixer", l), ()))
        s["h2"] = h
        h, s["g2"], s["u2"], s["n2"] = _ffn_fwd(h, vec("ffn2_norm", l), gathered(l, "B"), lay,
                                                fwd_jobs.get(("ffn2", l), ()))
        s["h3"] = h
        if l + 1 < L:
            h, s["gate"] = _ple_fwd(h, p[l, 0], vec("ple_norm", l), wppt, gathered(l, "D"), lay)
        else:
            *turned, loss_part, g_final = _ple_fwd(h, p[l, 0], vec("ple_norm", l), wppt, gathered(l, "D"), lay,
                                                    head=(loss_target[0], w["final_norm"].reshape(1, D)))
        saved.append(s)

    scatter = {}
    small_parts = [None] * L

    def send(l, pc, blocks):
        scatter[l, pc] = _ScatterJob(f"rs{pc}", [blocks[n].reshape(NDEV, lay.rows[n], D) for n, _ in lay.pieces[pc]])
        return scatter[l, pc]

    def small_rows():
        return _fold_rows([v for l in range(L) for v in small_parts[l]] + [g_final, loss_part])

    held = []
    for l in reversed(range(L)):
        s = saved[l]
        wppt, pw = s["wppt"], s["pw"]
        g = {}
        if l == L - 1:
            d, dpre, de, n_ple, pb, g_ple = turned
        else:
            d, dpre, de, n_ple, pb, g_ple = _ple_bwd(d, s["h3"], p[l, 0], s["gate"], vec("ple_norm", l), wppt,
                                                     gathered(l, "D"), lay)
        g["wpg"] = _tn_matmul(n_ple, dpre, "tn_sq", D)
        g["wpp"] = _tn_matmul(de, pb, "tn_proj", D)

        n1, n2 = s["n1"], s["n2"]
        d_out = d
        d, dg2, du2, a2, g_n2 = _ffn_bwd(d, s["h2"], vec("ffn2_norm", l), s["g2"], s["u2"], gathered(l, "B"), lay, held)
        held = []
        g["g2"] = _tn_matmul(dg2, n2, "tn_ffn", NDEV * FB)
        g["u2"] = _tn_matmul(du2, n2, "tn_ffn", NDEV * FB)
        g["d2"] = _tn_matmul(a2, d_out, "tn_ffn_d", NDEV * FB, y_scale=0.5)
        held.append(send(l, "B", g))

        (m_b, dcc, c3, dq, dpool, dc1, dzg, g_ps, g_lg, g_lb, g_cb) = _mixer_bwd_rows(
            d, s["z"], s["c1"], s["q"], s["cc"], pw, vec("pool_scale", l), vec("conv_ln_g", l), vec("conv_ln_b", l),
            gathered(l, "D"), lay)
        g["wout"] = _tn_matmul(m_b, d, "tn_sq_d", D)
        g["wco"] = _tn_matmul(c3, dcc, "tn_sq", D)
        g_pool = _tn_groups(s["pooled"], dq, len(POOL_WINDOWS), "tn_pool")
        g["pool"] = g_pool.reshape(4, NDEV, PGS, PG).transpose(1, 0, 2, 3)
        held.append(send(l, "D", g))
        dzm, g_cw = _mixer_bwd_time(dc1, dpool, s["z"], cw_full[l], lay, held)
        held = []
        d, u_b, g_mix = _inproj_bwd(d, s["h1"], vec("mix_norm", l), dzm, dzg, gathered(l, "C"), lay)
        g["win"] = jnp.concatenate([_tn_matmul(dzm, u_b, "tn_in3", 3 * D // 2),
                                    _tn_matmul(dzg, u_b, "tn_in2", D)], axis=0)
        held.append(send(l, "C", g))

        d_out = d
        d, dg1, du1, a1, g_n1 = _ffn_bwd(d, s["h0"], vec("ffn1_norm", l), s["g1"], s["u1"], gathered(l, "A"), lay, held)
        held = []
        small_parts[l] = [g_n1, g_mix, g_ps, g_cb, g_lg, g_lb, g_n2, g_ple, g_cw]
        if l > 0:
            g["g1"] = _tn_matmul(dg1, n1, "tn_ffn", NDEV * FB)
            g["u1"] = _tn_matmul(du1, n1, "tn_ffn", NDEV * FB)
            g["d1"] = _tn_matmul(a1, d_out, "tn_ffn_d", NDEV * FB, y_scale=0.5)
            held.append(send(l, "A", g))
        else:
            rows_job = _RowsGatherJob("agS", small_rows())
            g["g1"] = _tn_matmul(dg1, n1, "tn_ffn", NDEV * FB, jobs=[rows_job])
            g["u1"] = _tn_matmul(du1, n1, "tn_ffn", NDEV * FB, jobs=[send(l, "A1", g)])
            g["d1"] = _tn_matmul(a1, d_out, "tn_ffn_d", NDEV * FB, y_scale=0.5, jobs=[send(l, "A2", g)])
            held.append(send(l, "A3", g))
    _run_jobs("scatter_last", held)
    grad_x = d[None]

    per = [{} for _ in range(L)]
    for (l, pc), job in scatter.items():
        per[l].update(_unpack_piece(_sum_parts(job.results[0], "sum_" + pc), lay, pc))
    grads = {n: jnp.stack([per[l][n] for l in range(L)]) for n in _BIG}

    small_sum = _sum_slots(rows_job.results[0])
    per_layer = len(_VECS) + CONV_KP
    for k, n in enumerate(_VECS):
        grads[n] = jnp.stack([small_sum[l * per_layer + k] for l in range(L)])
    g_cw_full = jnp.stack([small_sum[l * per_layer + len(_VECS):l * per_layer + len(_VECS) + CONV_K]
                           for l in range(L)])
    grads["conv_dw_w"] = lax.dynamic_slice_in_dim(g_cw_full, me * CS, CS, axis=2)
    grads["final_norm"] = small_sum[L * per_layer]
    loss = (0.5 / D) * jnp.sum(small_sum[L * per_layer + 1])

    def as2(a):
        return a.reshape(1, -1) if a.ndim == 1 else a

    deltas, new_m, new_v = {}, {}, {}

    def update(name, names, by_layer):
        res = _adamw(name, [(as2(w[n]), as2(grads[n]), as2(mom[n]), as2(var[n])) for n in names], by_layer)
        for n, (delta, m_new, v_new) in zip(names, res):
            deltas[n], new_m[n], new_v[n] = (a.reshape(w[n].shape) for a in (delta, m_new, v_new))

    for n in ("ffn1_w_gate", "ffn1_w_up", "ffn1_w_down", "ffn2_w_gate", "ffn2_w_up", "ffn2_w_down", "w_in"):
        update("adamw_" + n, [n], True)
    update("adamw_mid", ["conv_w_out", "w_out", "ple_w_gate", "ple_w_proj", "pool_w"], True)
    update("adamw_small", _VECS + ["conv_dw_w", "final_norm"], False)
    return loss, grad_x, (grads, deltas, new_m, new_v)


def kernel(x, p, ffn1_norm, ffn1_w_gate, ffn1_w_up, ffn1_w_down, mix_norm, w_in, pool_w, pool_scale, conv_dw_w, conv_dw_b, conv_ln_g, conv_ln_b, conv_w_out, w_out, ffn2_norm, ffn2_w_gate, ffn2_w_up, ffn2_w_down, ple_norm, ple_w_gate, ple_w_proj, final_norm, loss_target, m_ffn1_norm, m_ffn1_w_gate, m_ffn1_w_up, m_ffn1_w_down, m_mix_norm, m_w_in, m_pool_w, m_pool_scale, m_conv_dw_w, m_conv_dw_b, m_conv_ln_g, m_conv_ln_b, m_conv_w_out, m_w_out, m_ffn2_norm, m_ffn2_w_gate, m_ffn2_w_up, m_ffn2_w_down, m_ple_norm, m_ple_w_gate, m_ple_w_proj, m_final_norm, v_ffn1_norm, v_ffn1_w_gate, v_ffn1_w_up, v_ffn1_w_down, v_mix_norm, v_w_in, v_pool_w, v_pool_scale, v_conv_dw_w, v_conv_dw_b, v_conv_ln_g, v_conv_ln_b, v_conv_w_out, v_w_out, v_ffn2_norm, v_ffn2_w_gate, v_ffn2_w_up, v_ffn2_w_down, v_ple_norm, v_ple_w_gate, v_ple_w_proj, v_final_norm):
    given = dict(locals())
    w = {n: given[n] for n in _WEIGHTS}
    mom = {n: given["m_" + n] for n in _WEIGHTS}
    var = {n: given["v_" + n] for n in _WEIGHTS}
    loss, grad_x, (grads, deltas, new_m, new_v) = _step(x, p, loss_target, w, mom, var)
    out = [loss, grad_x]
    for res in (grads, deltas, new_m, new_v):
        out += [res[n] for n in _WEIGHTS]
    return tuple(out)
```

```python
import functools

import jax
import jax.numpy as jnp
from jax import lax
from jax.experimental import pallas as pl
from jax.experimental.pallas import tpu as pltpu

F32, BF16 = jnp.float32, jnp.bfloat16
NDEV = 8
MESH = pl.DeviceIdType.MESH
HALO = 32
POOL_WINDOWS = (2, 4, 8, 16)
CONV_K = 31
CONV_KP = 32
RMS_EPS, LN_EPS = 1e-6, 1e-5
ADAM_LR, ADAM_B1, ADAM_B2, ADAM_EPS, ADAM_WD, ADAM_STEP = 0.001, 0.9, 0.999, 1e-08, 0.01, 10
LANE = 128
VMEM_LIMIT = 56 * 1024 * 1024
ANY = pl.BlockSpec(memory_space=pl.ANY)


def _nn(a, b):
    return jnp.dot(a, b, preferred_element_type=F32)


def _nt(a, b):
    return lax.dot_general(a, b, (((1,), (1,)), ((), ())), preferred_element_type=F32)


def _tn(a, b):
    return lax.dot_general(a, b, (((0,), (0,)), ((), ())), preferred_element_type=F32)


def _colsum8(v):
    return jnp.sum(v.reshape(v.shape[0] // 8, 8, v.shape[1]), axis=0)


def _sig(v):
    return jax.nn.sigmoid(v)


def _rms(h):
    r = lax.rsqrt(jnp.mean(h * h, axis=-1, keepdims=True) + RMS_EPS)
    return h * r, r


def _rms_bwd(dn, xh, r, gain):
    dxh = dn * gain
    return r * (dxh - xh * jnp.mean(dxh * xh, axis=-1, keepdims=True))


def _ln(c1):
    mu = jnp.mean(c1, axis=-1, keepdims=True)
    cen = c1 - mu
    rstd = lax.rsqrt(jnp.mean(cen * cen, axis=-1, keepdims=True) + LN_EPS)
    return cen * rstd, rstd


def _rows(tm, c):
    return pl.BlockSpec((tm, c), lambda i: (i, 0))


def _const(shape):
    return pl.BlockSpec(shape, lambda i: (0,) * len(shape))


def _params(n_grid=1):
    return pltpu.CompilerParams(dimension_semantics=("arbitrary",) * n_grid, vmem_limit_bytes=VMEM_LIMIT)


BF16_ROWS = 16
MXU_CHUNK = 768
BWD_CHUNK = 512


def _tile(n, want):
    if n <= want:
        return n
    return max(t for t in range(BF16_ROWS, want + 1, BF16_ROWS) if n % t == 0)


def _hidden_chunks(width, chunk=MXU_CHUNK):
    return [(o, min(chunk, width - o)) for o in range(0, width, chunk)]


def _staggered(chunks, project, activate, contract, acc):
    spans = [pl.ds(start, width) for start, width in chunks]
    ahead = project(spans[0])
    for c, span in enumerate(spans):
        projected = ahead
        if c + 1 < len(spans):
            ahead = project(spans[c + 1])
        acc = contract(span, activate(span, projected), acc)
    return acc


def _fetch(g_hbm, specs, sems):
    cps = []
    for wi, (off, rows, dst) in enumerate(specs):
        for dev in range(NDEV):
            cps.append(pltpu.make_async_copy(g_hbm.at[dev, pl.ds(off, rows), :],
                                             dst.at[pl.ds(dev * rows, rows), :], sems.at[wi, dev]))
    for cp in cps:
        cp.start()
    for cp in cps:
        cp.wait()


PIECES = ("A", "B", "C", "D")


class _Layout:
    def __init__(self, D, FS, NS, PD, PG, PGS):
        self.D, self.FS, self.NS, self.PD, self.PG, self.PGS = D, FS, NS, PD, PG, PGS
        assert FS % BF16_ROWS == 0 and (NDEV * FS) % (2 * LANE) == 0, "FFN shard rows must tile as bf16 row blocks"
        self.FB = FS
        self.DS = D // NDEV
        self.pieces = {
            "A": [("g1", self.FB), ("u1", self.FB), ("d1", self.FB)],
            "B": [("g2", self.FB), ("u2", self.FB), ("d2", self.FB)],
            "C": [("win", NS)],
            "D": [("wco", self.DS), ("wout", self.DS), ("wpg", self.DS), ("wpp", self.DS * PD // D),
                  ("pool", 4 * PGS * PG // D)],
            "A1": [("g1", self.FB)], "A2": [("u1", self.FB)], "A3": [("d1", self.FB)]}
        self.off, self.rows = {}, {}
        for pc in PIECES:
            o = 0
            for n, r in self.pieces[pc]:
                self.off[n], self.rows[n] = o, r
                o += r


def _pack_piece(w, l, lay, pc, dtype):
    D = lay.D
    make = {"g1": lambda: w["ffn1_w_gate"][l].T, "u1": lambda: w["ffn1_w_up"][l].T,
            "d1": lambda: w["ffn1_w_down"][l], "g2": lambda: w["ffn2_w_gate"][l].T,
            "u2": lambda: w["ffn2_w_up"][l].T, "d2": lambda: w["ffn2_w_down"][l],
            "win": lambda: w["w_in"][l].T, "wco": lambda: w["conv_w_out"][l], "wout": lambda: w["w_out"][l],
            "wpg": lambda: w["ple_w_gate"][l], "wpp": lambda: w["ple_w_proj"][l].T.reshape(-1, D),
            "pool": lambda: w["pool_w"][l].reshape(-1, D)}
    return jnp.concatenate([make[n]() for n, _ in lay.pieces[pc]], axis=0).astype(dtype)


def _unpack_piece(slab, lay, pc):
    PD, PG, PGS, DS = lay.PD, lay.PG, lay.PGS, lay.DS
    undo = {"g1": ("ffn1_w_gate", lambda a: a.T), "u1": ("ffn1_w_up", lambda a: a.T),
            "d1": ("ffn1_w_down", lambda a: a), "g2": ("ffn2_w_gate", lambda a: a.T),
            "u2": ("ffn2_w_up", lambda a: a.T), "d2": ("ffn2_w_down", lambda a: a),
            "win": ("w_in", lambda a: a.T), "wco": ("conv_w_out", lambda a: a), "wout": ("w_out", lambda a: a),
            "wpg": ("ple_w_gate", lambda a: a), "wpp": ("ple_w_proj", lambda a: a.reshape(DS, PD).T),
            "pool": ("pool_w", lambda a: a.reshape(4, PGS, PG))}
    out, o = {}, 0
    for n, r in lay.pieces[pc]:
        name, fn = undo[n]
        out[name] = fn(slab[o:o + r])
        o += r
    return out


def _place():
    return lax.axis_index("x"), lax.axis_index("y"), lax.axis_index("c")


FLIPS = [(a, b, d) for a in (0, 1) for b in (0, 1) for d in (0, 1)][1:]


class _GatherJob:
    def __init__(self, tag, slab):
        self.tag, self.ins = tag, [slab]
        self.outs = [jax.ShapeDtypeStruct((NDEV,) + slab.shape, slab.dtype)]
        self.scratch = [pltpu.SemaphoreType.DMA((7,)), pltpu.SemaphoreType.DMA((7,)), pltpu.SemaphoreType.DMA]
        self.results = None

    def _plan(self, ins, outs, sems):
        (x_ref,), (out_ref,), (send_sems, recv_sems, local_sem) = ins, outs, sems
        x, y, c = _place()
        me, sibling = (x, y, c), (x, y, 1 - c)
        chips = [(1 - x, y), (x, 1 - y), (1 - x, 1 - y)]

        def rows(px, py, pc):
            return out_ref.at[4 * px + 2 * py + pc]

        def copy(k, block, to, src=None):
            return pltpu.make_async_remote_copy(
                src_ref=rows(*block) if src is None else src, dst_ref=rows(*block),
                send_sem=send_sems.at[k], recv_sem=recv_sems.at[k], device_id=to, device_id_type=MESH)

        mine = pltpu.make_async_copy(x_ref, rows(*me), local_sem)
        first = [copy(0, me, sibling, src=x_ref)]
        first += [copy(1 + j, me, (*chip, c), src=x_ref) for j, chip in enumerate(chips)]
        passed = [copy(4 + j, (*chip, c), sibling) for j, chip in enumerate(chips)]
        landed = [copy(1 + j, (*chip, c), me) for j, chip in enumerate(chips)]
        late = [copy(0, sibling, me)] + [copy(4 + j, (*chip, 1 - c), me) for j, chip in enumerate(chips)]
        return mine, first, passed, landed, late

    def start(self, ins, outs, sems):
        mine, first, _, _, _ = self._plan(ins, outs, sems)
        mine.start()
        for cp in first:
            cp.start()

    def middle(self, ins, outs, sems):
        _, _, passed, landed, _ = self._plan(ins, outs, sems)
        for got, cp in zip(landed, passed):
            got.wait_recv()
            cp.start()

    def finish(self, ins, outs, sems):
        mine, first, passed, _, late = self._plan(ins, outs, sems)
        for got in late:
            got.wait_recv()
        for cp in first + passed:
            cp.wait_send()
        mine.wait()


class _ScatterJob:
    def __init__(self, tag, grads):
        self.tag, self.ins = tag, list(grads)
        self.row_counts = [g.shape[1] for g in grads]
        self.outs = [jax.ShapeDtypeStruct((NDEV, sum(self.row_counts), grads[0].shape[2]), grads[0].dtype)]
        n = len(grads)
        self.scratch = [pltpu.SemaphoreType.DMA((n, 7)), pltpu.SemaphoreType.DMA((n, 7)), pltpu.SemaphoreType.DMA((n,))]
        self.results = None

    def _plan(self, ins, outs, sems):
        (out_ref,), (send_sems, recv_sems, local_sems) = outs, sems
        x, y, c = _place()
        remote, local, off = [], [], 0
        for i, (g_ref, rows) in enumerate(zip(ins, self.row_counts)):
            span = pl.ds(off, rows)
            for k, (a, b, d) in enumerate(FLIPS):
                px, py, pc = x ^ a, y ^ b, c ^ d
                remote.append(pltpu.make_async_remote_copy(
                    src_ref=g_ref.at[4 * px + 2 * py + pc], dst_ref=out_ref.at[k, span, :], send_sem=send_sems.at[i, k],
                    recv_sem=recv_sems.at[i, k], device_id=(px, py, pc), device_id_type=MESH))
            local.append(pltpu.make_async_copy(g_ref.at[4 * x + 2 * y + c], out_ref.at[7, span, :], local_sems.at[i]))
            off += rows
        return remote, local

    def start(self, ins, outs, sems):
        remote, local = self._plan(ins, outs, sems)
        for cp in remote + local:
            cp.start()

    def middle(self, ins, outs, sems):
        pass

    def finish(self, ins, outs, sems):
        remote, local = self._plan(ins, outs, sems)
        for cp in remote:
            cp.wait_recv()
        for cp in remote:
            cp.wait_send()
        for cp in local:
            cp.wait()


class _RowsGatherJob:
    def __init__(self, tag, rows):
        self.tag, self.ins = tag, [rows]
        self.outs = [jax.ShapeDtypeStruct((NDEV,) + rows.shape, rows.dtype)]
        self.scratch = [pltpu.SemaphoreType.DMA((7,)), pltpu.SemaphoreType.DMA((7,)), pltpu.SemaphoreType.DMA]
        self.results = None

    def _plan(self, ins, outs, sems):
        (x_ref,), (out_ref,), (send_sems, recv_sems, local_sem) = ins, outs, sems
        x, y, c = _place()
        me = 4 * x + 2 * y + c
        mine = pltpu.make_async_copy(x_ref, out_ref.at[me], local_sem)
        sends, lands = [], []
        for k, (a, b, d) in enumerate(FLIPS):
            px, py, pc = x ^ a, y ^ b, c ^ d
            for dst, keep in ((out_ref.at[me], sends), (out_ref.at[4 * px + 2 * py + pc], lands)):
                keep.append(pltpu.make_async_remote_copy(
                    src_ref=x_ref, dst_ref=dst, send_sem=send_sems.at[k], recv_sem=recv_sems.at[k],
                    device_id=(px, py, pc), device_id_type=MESH))
        return mine, sends, lands

    def start(self, ins, outs, sems):
        mine, sends, _ = self._plan(ins, outs, sems)
        mine.start()
        for cp in sends:
            cp.start()

    def middle(self, ins, outs, sems):
        pass

    def finish(self, ins, outs, sems):
        mine, sends, lands = self._plan(ins, outs, sems)
        for cp in lands:
            cp.wait_recv()
        for cp in sends:
            cp.wait_send()
        mine.wait()


def _fold_rows(parts):
    D = parts[0].shape[1]
    counts = [a.shape[0] // 8 for a in parts]
    total = -(-sum(counts) // 8) * 8

    def body(*refs):
        out = refs[-1]
        out[...] = jnp.zeros_like(out)
        row = 0
        for ref, k in zip(refs[:-1], counts):
            for j in range(k):
                out[pl.ds(row + j, 1), :] = jnp.sum(ref[pl.ds(8 * j, 8), :], axis=0, keepdims=True)
            row += k

    vm = pl.BlockSpec(memory_space=pltpu.VMEM)
    return pl.pallas_call(body, name="fold_rows", out_shape=jax.ShapeDtypeStruct((total, D), F32),
                          in_specs=[vm] * len(parts), out_specs=vm)(*parts)


def _sum_slots(slots):
    n, S, D = slots.shape

    def body(s_ref, o_ref):
        acc = s_ref[0]
        for j in range(1, n):
            acc = acc + s_ref[j]
        o_ref[...] = acc

    vm = pl.BlockSpec(memory_space=pltpu.VMEM)
    return pl.pallas_call(body, name="sum_slots", out_shape=jax.ShapeDtypeStruct((S, D), slots.dtype),
                          in_specs=[vm], out_specs=vm)(slots)


def _launch(name, body, grid, args, in_specs, out_specs, out_shape, scratch, jobs=()):
    grid = (grid,) if isinstance(grid, int) else tuple(grid)
    steps = grid[0] * (grid[1] if len(grid) == 2 else 1)
    n_in, n_out, n_sc = len(args), len(out_shape), len(scratch)
    j_in = [a for jb in jobs for a in jb.ins]
    j_out = [o for jb in jobs for o in jb.outs]
    j_sc = [s for jb in jobs for s in jb.scratch]
    mid = (17 * steps) // 20

    def wrapped(*refs):
        c_in, refs = refs[:n_in], refs[n_in:]
        m_in, refs = refs[:len(j_in)], refs[len(j_in):]
        c_out, refs = refs[:n_out], refs[n_out:]
        m_out, refs = refs[:len(j_out)], refs[len(j_out):]
        c_sc, m_sc = refs[:n_sc], refs[n_sc:]
        bound, a, b, c = [], 0, 0, 0
        for jb in jobs:
            bound.append((jb, m_in[a:a + len(jb.ins)], m_out[b:b + len(jb.outs)], m_sc[c:c + len(jb.scratch)]))
            a, b, c = a + len(jb.ins), b + len(jb.outs), c + len(jb.scratch)
        i = pl.program_id(0) if len(grid) == 1 else pl.program_id(0) * grid[1] + pl.program_id(1)

        def phase(step, which):
            if jobs:
                @pl.when(i == step)
                def _():
                    for jb, ins, outs, sems in bound:
                        getattr(jb, which)(ins, outs, sems)

        phase(0, "start")
        if body is not None:
            body(*c_in, *c_out, *c_sc)
        phase(mid, "middle")
        phase(steps - 1, "finish")

    outs = pl.pallas_call(
        wrapped, name=name + "".join("_" + jb.tag for jb in jobs), grid=grid,
        in_specs=list(in_specs) + [ANY] * len(j_in), out_specs=list(out_specs) + [ANY] * len(j_out),
        out_shape=list(out_shape) + j_out, scratch_shapes=list(scratch) + j_sc, compiler_params=_params(len(grid)),
    )(*args, *j_in)
    pos = n_out
    for jb in jobs:
        jb.results = list(outs[pos:pos + len(jb.outs)])
        pos += len(jb.outs)
    return list(outs[:n_out])


def _run_jobs(name, jobs):
    _launch(name, None, 1, [], [], [], [], [], jobs)


def _sum_parts(got, name):
    n, rows, D = got.shape
    tr = _tile(rows, 512)
    order = [n - 1] + list(range(n - 1))

    def body(*refs):
        g = refs[0][...].astype(F32)
        for pr in refs[1:n]:
            g = g + pr[...].astype(F32)
        refs[-1][...] = g

    specs = [pl.BlockSpec((None, tr, D), functools.partial(lambda k, i: (k, i, 0), k)) for k in order]
    return pl.pallas_call(
        body, name=name, grid=(rows // tr,), in_specs=specs, out_specs=_rows(tr, D),
        out_shape=jax.ShapeDtypeStruct((rows, D), F32), compiler_params=_params(),
    )(*[got] * n)


def _adamw(name, entries, by_layer):
    n = len(entries)
    L = entries[0][0].shape[0]

    def body(*refs):
        ins, outs = refs[:4 * n], refs[4 * n:]
        for e in range(n):
            w_ref, g_ref, m_ref, v_ref = ins[4 * e:4 * e + 4]
            d_out, m_out, v_out = outs[3 * e:3 * e + 3]
            g = g_ref[...]
            m_new = ADAM_B1 * m_ref[...] + (1.0 - ADAM_B1) * g
            v_new = ADAM_B2 * v_ref[...] + (1.0 - ADAM_B2) * (g * g)
            m_hat = m_new / (1.0 - ADAM_B1 ** ADAM_STEP)
            v_hat = v_new / (1.0 - ADAM_B2 ** ADAM_STEP)
            d_out[...] = -ADAM_LR * (m_hat / (jnp.sqrt(v_hat) + ADAM_EPS) + ADAM_WD * w_ref[...])
            m_out[...] = m_new
            v_out[...] = v_new

    def spec(a):
        rest = (0,) * (a.ndim - 1)
        if by_layer:
            return pl.BlockSpec((None,) + a.shape[1:], lambda l: (l,) + rest)
        return pl.BlockSpec(a.shape, lambda l: (0,) + rest)

    flat = [a for e in entries for a in e]
    outs = pl.pallas_call(
        body, name=name, grid=(L if by_layer else 1,),
        in_specs=[spec(a) for a in flat], out_specs=[spec(e[0]) for e in entries for _ in range(3)],
        out_shape=[jax.ShapeDtypeStruct(e[0].shape, F32) for e in entries for _ in range(3)],
        compiler_params=_params(),
    )(*flat)
    return [tuple(outs[3 * e:3 * e + 3]) for e in range(n)]


def _tn_matmul(xa, ya, name, tmm, y_scale=1.0, jobs=()):
    T, M = xa.shape
    tn = ya.shape[1]
    tt = _tile(T, 1024)
    nb = M // tmm

    def body(x_ref, y_ref, o_ref, acc):
        k = pl.program_id(1)

        @pl.when(k == 0)
        def _():
            acc[...] = jnp.zeros_like(acc)

        y = y_ref[...]
        if y_scale != 1.0 or y.dtype != BF16:
            y = (y_scale * y).astype(BF16)
        acc[...] += _tn(x_ref[...], y)

        @pl.when(k == pl.num_programs(1) - 1)
        def _():
            o_ref[...] = acc[...].astype(o_ref.dtype)

    return _launch(
        name, body, (nb, T // tt), [xa, ya],
        [pl.BlockSpec((tt, tmm), lambda b, k: (k, b)), pl.BlockSpec((tt, tn), lambda b, k: (k, 0))],
        [pl.BlockSpec((tmm, tn), lambda b, k: (b, 0))], [jax.ShapeDtypeStruct((M, tn), BF16)],
        [pltpu.VMEM((tmm, tn), F32)], jobs)[0]


def _tn_groups(xa, ya, groups, name):
    T, M = xa.shape
    w = M // groups
    tt = _tile(T, 1024)
    steps = T // tt

    def body(x_ref, y_ref, o_ref, acc):
        k = pl.program_id(0)

        @pl.when(k == 0)
        def _():
            acc[...] = jnp.zeros_like(acc)

        for g in range(groups):
            cols = pl.ds(g * w, w)
            acc[cols, :] += _tn(x_ref[:, cols], y_ref[:, cols])

        @pl.when(k == steps - 1)
        def _():
            o_ref[...] = acc[...].astype(o_ref.dtype)

    return pl.pallas_call(
        body, name=name, grid=(steps,), in_specs=[_rows(tt, M), _rows(tt, M)], out_specs=_const((M, w)),
        out_shape=jax.ShapeDtypeStruct((M, w), BF16), scratch_shapes=[pltpu.VMEM((M, w), F32)],
        compiler_params=_params(),
    )(xa, ya)


def _ffn_fwd(h, gain, G, lay, jobs=()):
    T, D = h.shape
    FB = lay.FB
    FP = NDEV * FB
    offs = (0, FB, 2 * FB)
    tm = _tile(T, 512)

    def body(h_ref, gain_ref, g_hbm, ho_ref, go_ref, uo_ref, n_ref, wg, wu, wd, sems):
        @pl.when(pl.program_id(0) == 0)
        def _():
            _fetch(g_hbm, [(offs[0], FB, wg), (offs[1], FB, wu), (offs[2], FB, wd)], sems)

        h = h_ref[...]
        xh, _ = _rms(h)
        n = (xh * gain_ref[...]).astype(BF16)
        n_ref[...] = n

        def project(sl):
            return _nt(n, wg[sl, :]), _nt(n, wu[sl, :])

        def activate(sl, gu):
            g, u = gu
            go_ref[:, sl] = g.astype(BF16)
            uo_ref[:, sl] = u.astype(BF16)
            return (g * _sig(g) * u).astype(BF16)

        def contract(sl, a, acc):
            return acc + _nn(a, wd[sl, :])

        acc = _staggered(_hidden_chunks(FP), project, activate, contract, jnp.zeros((tm, D), F32))
        ho_ref[...] = h + 0.5 * acc

    return _launch(
        "ffn_fwd", body, T // tm, [h, gain, G],
        [_rows(tm, D), _const((1, D)), ANY], [_rows(tm, D), _rows(tm, FP), _rows(tm, FP), _rows(tm, D)],
        [jax.ShapeDtypeStruct((T, D), F32), jax.ShapeDtypeStruct((T, FP), BF16), jax.ShapeDtypeStruct((T, FP), BF16),
         jax.ShapeDtypeStruct((T, D), BF16)],
        [pltpu.VMEM((FP, D), BF16)] * 3 + [pltpu.SemaphoreType.DMA((3, NDEV))], jobs)


def _ffn_bwd(d, h, gain, ga, ua, G, lay, jobs=()):
    T, D = h.shape
    FB = lay.FB
    FP = NDEV * FB
    offs = (0, FB, 2 * FB)
    tm = _tile(T, 256)

    def body(d_ref, h_ref, gain_ref, ga_ref, ua_ref, g_hbm,
             do_ref, dg_ref, du_ref, a_ref, gg_ref, wg, wu, wd, sems):
        @pl.when(pl.program_id(0) == 0)
        def _():
            _fetch(g_hbm, [(offs[0], FB, wg), (offs[1], FB, wu), (offs[2], FB, wd)], sems)
            gg_ref[...] = jnp.zeros_like(gg_ref)

        d = d_ref[...]
        gain_v = gain_ref[...]
        xh, r = _rms(h_ref[...])
        dh = (0.5 * d).astype(BF16)

        def project(sl):
            return _nt(dh, wd[sl, :])

        def activate(sl, da):
            g = ga_ref[:, sl].astype(F32)
            u = ua_ref[:, sl].astype(F32)
            s = _sig(g)
            silu = g * s
            a_ref[:, sl] = (silu * u).astype(BF16)
            dgv = (da * u * (s * (1.0 + g * (1.0 - s)))).astype(BF16)
            duv = (da * silu).astype(BF16)
            dg_ref[:, sl] = dgv
            du_ref[:, sl] = duv
            return dgv, duv

        def contract(sl, grads, acc):
            return acc + _nn(grads[0], wg[sl, :]) + _nn(grads[1], wu[sl, :])

        dn = _staggered(_hidden_chunks(FP, BWD_CHUNK), project, activate, contract, jnp.zeros((tm, D), F32))
        gg_ref[...] += _colsum8(dn * xh)
        do_ref[...] = d + _rms_bwd(dn, xh, r, gain_v)

    wide = jax.ShapeDtypeStruct((T, FP), BF16)
    return _launch(
        "ffn_bwd", body, T // tm, [d, h, gain, ga, ua, G],
        [_rows(tm, D), _rows(tm, D), _const((1, D)), _rows(tm, FP), _rows(tm, FP), ANY],
        [_rows(tm, D), _rows(tm, FP), _rows(tm, FP), _rows(tm, FP), _const((8, D))],
        [jax.ShapeDtypeStruct((T, D), F32), wide, wide, wide, jax.ShapeDtypeStruct((8, D), F32)],
        [pltpu.VMEM((FP, D), BF16)] * 3 + [pltpu.SemaphoreType.DMA((3, NDEV))], jobs)


def _inproj_fwd(h, gain, G, lay):
    T, D = h.shape
    NS = lay.NS
    NIN, CH = NDEV * NS, 2 * NS
    tm = _tile(T, 512)

    def body(h_ref, gain_ref, g_hbm, z_ref, win, sems):
        @pl.when(pl.program_id(0) == 0)
        def _():
            _fetch(g_hbm, [(lay.off["win"], NS, win)], sems)

        xh, _ = _rms(h_ref[...])
        n = (xh * gain_ref[...]).astype(BF16)
        for j in range(NIN // CH):
            sl = pl.ds(j * CH, CH)
            z_ref[:, sl] = _nt(n, win[sl, :]).astype(BF16)

    return pl.pallas_call(
        body, name="inproj_fwd", grid=(T // tm,),
        in_specs=[_rows(tm, D), _const((1, D)), ANY], out_specs=_rows(tm, NIN),
        out_shape=jax.ShapeDtypeStruct((T, NIN), BF16),
        scratch_shapes=[pltpu.VMEM((NIN, D), BF16), pltpu.SemaphoreType.DMA((1, NDEV))],
        compiler_params=_params(),
    )(h, gain, G)


def _conv_chunks(tm, D):
    rb, lc = min(tm, 64), min(D, 256)
    return [(r0, l0, rb, lc) for r0 in range(0, tm, rb) for l0 in range(0, D, lc)]


SUBLANES = 8


def _preshift(sh, n_rows):
    for r in range(1, SUBLANES):
        sh[r, pl.ds(0, n_rows), :] = sh[0, pl.ds(r, n_rows), :]


def _window(sh, start, rows, lanes):
    r = start % SUBLANES
    return sh[r, pl.ds(start - r, rows), lanes]


def _mixer_fwd(z, h, pool_w, pscale, cw, cb, lng, lnb, G, lay, jobs=()):
    T, D = h.shape
    PG, DS = lay.PG, lay.DS
    tm = _tile(T, 256)
    hb = tm // HALO

    def body(z_ref, zp_ref, h_ref, pw_ref, ps_ref, cw_ref, cb_ref, lg_ref, lb_ref, g_hbm,
             ho_ref, c1_ref, q_ref, cc_ref, pool_ref, ext_p, sh_c, a_s, wco, wout, sems):
        i = pl.program_id(0)

        @pl.when(i == 0)
        def _():
            _fetch(g_hbm, [(lay.off["wco"], DS, wco), (lay.off["wout"], DS, wout)], sems)

        live = jnp.where(i > 0, 1.0, 0.0).astype(F32)
        ext_p[pl.ds(0, HALO), :] = zp_ref[:, pl.ds(0, D)].astype(F32) * live
        ext_p[pl.ds(HALO, tm), :] = z_ref[:, pl.ds(0, D)].astype(F32)
        sh_c[0, pl.ds(0, HALO), :] = (zp_ref[:, pl.ds(D, D)].astype(F32)
                                      * _sig(zp_ref[:, pl.ds(2 * D, D)].astype(F32)) * live)
        sh_c[0, pl.ds(HALO, tm), :] = z_ref[:, pl.ds(D, D)].astype(F32) * _sig(z_ref[:, pl.ds(2 * D, D)].astype(F32))
        _preshift(sh_c, tm + HALO - SUBLANES)

        t = i * tm + lax.broadcasted_iota(jnp.int32, (tm, 1), 0)
        for g, w in enumerate(POOL_WINDOWS):
            sl = pl.ds(g * PG, PG)
            s = ext_p[pl.ds(HALO, tm), sl]
            zc = s
            for j in range(1, w):
                s = s + ext_p[pl.ds(HALO - j, tm), sl]
            inv = 1.0 / jnp.minimum(t + 1, w).astype(F32)
            pooled = (s * inv - zc).astype(BF16)
            pool_ref[:, sl] = pooled
            qv = _nn(pooled, pw_ref[g])
            q_ref[:, sl] = qv.astype(BF16)
            a_s[:, sl] = qv * ps_ref[:, sl]

        for r0, l0, rb, lc in _conv_chunks(tm, D):
            ls = pl.ds(l0, lc)
            acc = jnp.zeros((rb, lc), F32) + cb_ref[:, ls]
            for k in range(CONV_K):
                acc = acc + cw_ref[pl.ds(k, 1), ls] * _window(sh_c, r0 + HALO - (CONV_K - 1) + k, rb, ls)
            c1_ref[pl.ds(r0, rb), ls] = acc

        xhat, _ = _ln(c1_ref[...])
        c2 = xhat * lg_ref[...] + lb_ref[...]
        c3 = (c2 * _sig(c2)).astype(BF16)
        cc = _nn(c3, wco[...])
        cc_ref[...] = cc.astype(BF16)
        gp = z_ref[:, pl.ds(3 * D, D)].astype(F32)
        gc = z_ref[:, pl.ds(4 * D, D)].astype(F32)
        m = (_sig(gp) * a_s[...] + _sig(gc) * cc).astype(BF16)
        ho_ref[...] = h_ref[...] + _nn(m, wout[...])

    act = jax.ShapeDtypeStruct((T, D), BF16)
    return _launch(
        "mixer_fwd", body, T // tm, [z, z, h, pool_w, pscale, cw, cb, lng, lnb, G],
        [_rows(tm, 5 * D), pl.BlockSpec((HALO, 5 * D), lambda i: (jnp.maximum(i * hb - 1, 0), 0)),
         _rows(tm, D), _const((4, PG, PG)), _const((1, D)), _const((CONV_KP, D)), _const((1, D)),
         _const((1, D)), _const((1, D)), ANY],
        [_rows(tm, D)] * 5,
        [jax.ShapeDtypeStruct((T, D), F32), jax.ShapeDtypeStruct((T, D), F32), act, act, act],
        [pltpu.VMEM((HALO + tm, D), F32), pltpu.VMEM((SUBLANES, HALO + tm, D), F32), pltpu.VMEM((tm, D), F32),
         pltpu.VMEM((D, D), BF16), pltpu.VMEM((D, D), BF16), pltpu.SemaphoreType.DMA((2, NDEV))], jobs)


def _mixer_bwd_rows(d, z, c1, qa, cca, pool_w, pscale, lng, lnb, G, lay):
    T, D = d.shape
    PG, DS = lay.PG, lay.DS
    tm = _tile(T, 256)

    def body(d_ref, zgp_ref, zgc_ref, c1_ref, q_ref, cc_ref, pw_ref, ps_ref, lg_ref, lb_ref, g_hbm,
             m_ref, dcc_ref, c3_ref, dq_ref, dpool_ref, dc1_ref, dzg_ref, gps_ref, glg_ref, glb_ref, gcb_ref,
             wco, wout, sems):
        @pl.when(pl.program_id(0) == 0)
        def _():
            _fetch(g_hbm, [(lay.off["wco"], DS, wco), (lay.off["wout"], DS, wout)], sems)
            for ref in (gps_ref, glg_ref, glb_ref, gcb_ref):
                ref[...] = jnp.zeros_like(ref)

        dm = _nt(d_ref[...].astype(BF16), wout[...])
        q = q_ref[...].astype(F32)
        cc = cc_ref[...].astype(F32)
        ps = ps_ref[...]
        sp = _sig(zgp_ref[...].astype(F32))
        sc = _sig(zgc_ref[...].astype(F32))
        a = q * ps
        m_ref[...] = (sp * a + sc * cc).astype(BF16)
        da = dm * sp
        dzg_ref[:, pl.ds(0, D)] = (dm * a * sp * (1.0 - sp)).astype(BF16)
        dzg_ref[:, pl.ds(D, D)] = (dm * cc * sc * (1.0 - sc)).astype(BF16)
        gps_ref[...] += _colsum8(da * q)
        dq = (da * ps).astype(BF16)
        dq_ref[...] = dq
        for g in range(len(POOL_WINDOWS)):
            sl = pl.ds(g * PG, PG)
            dpool_ref[:, sl] = _nt(dq_ref[:, sl], pw_ref[g]).astype(dpool_ref.dtype)

        dcc = (dm * sc).astype(BF16)
        dcc_ref[...] = dcc
        xhat, rstd = _ln(c1_ref[...])
        lg = lg_ref[...]
        c2 = xhat * lg + lb_ref[...]
        s2 = _sig(c2)
        c3_ref[...] = (c2 * s2).astype(BF16)
        dc2 = _nt(dcc, wco[...]) * (s2 * (1.0 + c2 * (1.0 - s2)))
        glg_ref[...] += _colsum8(dc2 * xhat)
        glb_ref[...] += _colsum8(dc2)
        dxh = dc2 * lg
        dc1 = rstd * (dxh - jnp.mean(dxh, axis=-1, keepdims=True)
                      - xhat * jnp.mean(dxh * xhat, axis=-1, keepdims=True))
        dc1_ref[...] = dc1.astype(dc1_ref.dtype)
        gcb_ref[...] += _colsum8(dc1)

    act = jax.ShapeDtypeStruct((T, D), BF16)
    full = jax.ShapeDtypeStruct((T, D), F32)
    vec = jax.ShapeDtypeStruct((8, D), F32)
    return pl.pallas_call(
        body, name="mixer_bwd_rows", grid=(T // tm,),
        in_specs=[_rows(tm, D), pl.BlockSpec((tm, D), lambda i: (i, 3)), pl.BlockSpec((tm, D), lambda i: (i, 4)),
                  _rows(tm, D), _rows(tm, D), _rows(tm, D),
                  _const((4, PG, PG)), _const((1, D)), _const((1, D)), _const((1, D)), ANY],
        out_specs=[_rows(tm, D)] * 6 + [_rows(tm, 2 * D)] + [_const((8, D))] * 4,
        out_shape=[act, act, act, act, act, act, jax.ShapeDtypeStruct((T, 2 * D), BF16), vec, vec, vec, vec],
        scratch_shapes=[pltpu.VMEM((D, D), BF16), pltpu.VMEM((D, D), BF16), pltpu.SemaphoreType.DMA((2, NDEV))],
        compiler_params=_params(),
    )(d, z, z, c1, qa, cca, pool_w, pscale, lng, lnb, G)


def _mixer_bwd_time(dc1, dpool, z, cw, lay, jobs=()):
    T, D = dc1.shape
    PG = lay.PG
    tm = _tile(T, 256)
    hb = tm // HALO
    nt = T // tm

    def body(dc_ref, dcn_ref, dp_ref, dpn_ref, z_ref, zp_ref, cw_ref, dz_ref, gcw_ref, sh_d, ext_q, sh_c, dc0_s):
        i = pl.program_id(0)

        @pl.when(i == 0)
        def _():
            gcw_ref[...] = jnp.zeros_like(gcw_ref)

        live_p = jnp.where(i > 0, 1.0, 0.0).astype(F32)
        live_n = jnp.where(i < nt - 1, 1.0, 0.0).astype(F32)
        sh_d[0, pl.ds(0, tm), :] = dc_ref[...].astype(F32)
        sh_d[0, pl.ds(tm, HALO), :] = dcn_ref[...].astype(F32) * live_n
        _preshift(sh_d, tm + HALO - SUBLANES)
        zg = z_ref[:, pl.ds(2 * D, D)].astype(F32)
        za = z_ref[:, pl.ds(D, D)].astype(F32)
        sg = _sig(zg)
        sh_c[0, pl.ds(0, HALO), :] = (zp_ref[:, pl.ds(D, D)].astype(F32)
                                      * _sig(zp_ref[:, pl.ds(2 * D, D)].astype(F32)) * live_p)
        sh_c[0, pl.ds(HALO, tm), :] = za * sg
        _preshift(sh_c, tm + HALO - SUBLANES)

        t = i * tm + lax.broadcasted_iota(jnp.int32, (tm, 1), 0)
        tn = (i + 1) * tm + lax.broadcasted_iota(jnp.int32, (HALO, 1), 0)
        for g, w in enumerate(POOL_WINDOWS):
            sl = pl.ds(g * PG, PG)
            ext_q[pl.ds(0, tm), sl] = dp_ref[:, sl] * (1.0 / jnp.minimum(t + 1, w).astype(F32))
            ext_q[pl.ds(tm, HALO), sl] = dpn_ref[:, sl] * (live_n / jnp.minimum(tn + 1, w).astype(F32))
        for g, w in enumerate(POOL_WINDOWS):
            sl = pl.ds(g * PG, PG)
            s = ext_q[pl.ds(0, tm), sl]
            for j in range(1, w):
                s = s + ext_q[pl.ds(j, tm), sl]
            dz_ref[:, sl] = (s - dp_ref[:, sl]).astype(BF16)

        for r0, l0, rb, lc in _conv_chunks(tm, D):
            ls = pl.ds(l0, lc)
            acc = jnp.zeros((rb, lc), F32)
            for j in range(CONV_K):
                acc = acc + cw_ref[pl.ds(CONV_K - 1 - j, 1), ls] * _window(sh_d, r0 + j, rb, ls)
            dc0_s[pl.ds(r0, rb), ls] = acc
        dc0 = dc0_s[...]
        dz_ref[:, pl.ds(D, D)] = (dc0 * sg).astype(BF16)
        dz_ref[:, pl.ds(2 * D, D)] = (dc0 * za * sg * (1.0 - sg)).astype(BF16)

        for r0, l0, rb, lc in _conv_chunks(tm, D):
            ls = pl.ds(l0, lc)
            dcv = sh_d[0, pl.ds(r0, rb), ls]
            for k in range(CONV_K):
                gcw_ref[pl.ds(8 * k, 8), ls] += _colsum8(dcv * _window(sh_c, r0 + HALO - (CONV_K - 1) + k, rb, ls))

    nxt = lambda i: (jnp.minimum((i + 1) * hb, T // HALO - 1), 0)
    return _launch(
        "mixer_bwd_time", body, nt, [dc1, dc1, dpool, dpool, z, z, cw],
        [_rows(tm, D), pl.BlockSpec((HALO, D), nxt), _rows(tm, D), pl.BlockSpec((HALO, D), nxt),
         _rows(tm, 5 * D), pl.BlockSpec((HALO, 5 * D), lambda i: (jnp.maximum(i * hb - 1, 0), 0)),
         _const((CONV_KP, D))],
        [_rows(tm, 3 * D), _const((CONV_KP * 8, D))],
        [jax.ShapeDtypeStruct((T, 3 * D), BF16), jax.ShapeDtypeStruct((CONV_KP * 8, D), F32)],
        [pltpu.VMEM((SUBLANES, tm + HALO, D), F32), pltpu.VMEM((tm + HALO, D), F32),
         pltpu.VMEM((SUBLANES, HALO + tm, D), F32), pltpu.VMEM((tm, D), F32)], jobs)


def _inproj_bwd(d, h, gain, dzm, dzg, G, lay):
    T, D = h.shape
    NS = lay.NS
    NIN = NDEV * NS
    tm = _tile(T, 512)

    def body(d_ref, h_ref, gain_ref, dzm_ref, dzg_ref, g_hbm, do_ref, u_ref, gg_ref, win, sems):
        @pl.when(pl.program_id(0) == 0)
        def _():
            _fetch(g_hbm, [(lay.off["win"], NS, win)], sems)
            gg_ref[...] = jnp.zeros_like(gg_ref)

        gain_v = gain_ref[...]
        xh, r = _rms(h_ref[...])
        u_ref[...] = (xh * gain_v).astype(BF16)
        dn = jnp.zeros((tm, D), F32)
        for j in range(3):
            dn = dn + _nn(dzm_ref[:, pl.ds(j * D, D)], win[pl.ds(j * D, D), :])
        for j in range(2):
            dn = dn + _nn(dzg_ref[:, pl.ds(j * D, D)], win[pl.ds((3 + j) * D, D), :])
        gg_ref[...] += _colsum8(dn * xh)
        do_ref[...] = d_ref[...] + _rms_bwd(dn, xh, r, gain_v)

    return pl.pallas_call(
        body, name="inproj_bwd", grid=(T // tm,),
        in_specs=[_rows(tm, D), _rows(tm, D), _const((1, D)), _rows(tm, 3 * D), _rows(tm, 2 * D), ANY],
        out_specs=[_rows(tm, D), _rows(tm, D), _const((8, D))],
        out_shape=[jax.ShapeDtypeStruct((T, D), F32), jax.ShapeDtypeStruct((T, D), BF16),
                   jax.ShapeDtypeStruct((8, D), F32)],
        scratch_shapes=[pltpu.VMEM((NIN, D), BF16), pltpu.SemaphoreType.DMA((1, NDEV))],
        compiler_params=_params(),
    )(d, h, gain, dzm, dzg, G)


def _ple_fwd(h, pe, gain, wppt, G, lay, head=None):
    T, D = h.shape
    PD, DS = lay.PD, lay.DS
    tm = _tile(T, 512)

    def body(*refs):
        if head is None:
            h_ref, p_ref, gain_ref, wpp_ref, g_hbm, ho_ref, gate_ref, wpg, sems = refs
        else:
            (h_ref, p_ref, gain_ref, wpp_ref, g_hbm, t_ref, fgain_ref,
             do_ref, dpre_ref, de_ref, n_ref, pb_ref, gg_ref, loss_ref, fg_ref, wpg, sems) = refs

        @pl.when(pl.program_id(0) == 0)
        def _():
            _fetch(g_hbm, [(lay.off["wpg"], DS, wpg)], sems)
            if head is not None:
                for ref in (gg_ref, loss_ref, fg_ref):
                    ref[...] = jnp.zeros_like(ref)

        h = h_ref[...]
        gain_v = gain_ref[...]
        xh, r = _rms(h)
        n = (xh * gain_v).astype(BF16)
        gate = _sig(_nn(n, wpg[...]))
        pb = p_ref[...].astype(BF16)
        e = _nt(pb, wpp_ref[...])
        out = h + gate * e
        if head is None:
            gate_ref[...] = gate.astype(BF16)
            ho_ref[...] = out
            return
        fgain = fgain_ref[...]
        yh, ry = _rms(out)
        err = yh * fgain - t_ref[...]
        loss_ref[...] += _colsum8(err * err)
        dy = err * (1.0 / D)
        fg_ref[...] += _colsum8(dy * yh)
        d = _rms_bwd(dy, yh, ry, fgain)
        n_ref[...] = n
        pb_ref[...] = pb
        de_ref[...] = (d * gate).astype(BF16)
        dpre = (d * e * gate * (1.0 - gate)).astype(BF16)
        dpre_ref[...] = dpre
        dn = _nt(dpre, wpg[...])
        gg_ref[...] += _colsum8(dn * xh)
        do_ref[...] = d + _rms_bwd(dn, xh, r, gain_v)

    args = [h, pe, gain, wppt, G]
    in_specs = [_rows(tm, D), _rows(tm, PD), _const((1, D)), _const((D, PD)), ANY]
    act = jax.ShapeDtypeStruct((T, D), BF16)
    if head is None:
        out_specs = [_rows(tm, D), _rows(tm, D)]
        out_shape = [jax.ShapeDtypeStruct((T, D), F32), act]
    else:
        args += list(head)
        in_specs += [_rows(tm, D), _const((1, D))]
        out_specs = [_rows(tm, D)] * 4 + [_rows(tm, PD)] + [_const((8, D))] * 3
        out_shape = [jax.ShapeDtypeStruct((T, D), F32), act, act, act, jax.ShapeDtypeStruct((T, PD), BF16)]
        out_shape += [jax.ShapeDtypeStruct((8, D), F32)] * 3
    return pl.pallas_call(
        body, name="ple_fwd" if head is None else "ple_head", grid=(T // tm,),
        in_specs=in_specs, out_specs=out_specs, out_shape=out_shape,
        scratch_shapes=[pltpu.VMEM((D, D), BF16), pltpu.SemaphoreType.DMA((1, NDEV))],
        compiler_params=_params(),
    )(*args)


def _ple_bwd(d, h, pe, gate_a, gain, wppt, G, lay):
    T, D = h.shape
    PD, DS = lay.PD, lay.DS
    tm = _tile(T, 512)

    def body(d_ref, h_ref, p_ref, gate_ref, gain_ref, wpp_ref, g_hbm,
             do_ref, dpre_ref, de_ref, n_ref, pb_ref, gg_ref, wpg, sems):
        @pl.when(pl.program_id(0) == 0)
        def _():
            _fetch(g_hbm, [(lay.off["wpg"], DS, wpg)], sems)
            gg_ref[...] = jnp.zeros_like(gg_ref)

        d = d_ref[...]
        gain_v = gain_ref[...]
        xh, r = _rms(h_ref[...])
        n_ref[...] = (xh * gain_v).astype(BF16)
        pb = p_ref[...].astype(BF16)
        pb_ref[...] = pb
        e = _nt(pb, wpp_ref[...])
        gate = gate_ref[...].astype(F32)
        de_ref[...] = (d * gate).astype(BF16)
        dpre = (d * e * gate * (1.0 - gate)).astype(BF16)
        dpre_ref[...] = dpre
        dn = _nt(dpre, wpg[...])
        gg_ref[...] += _colsum8(dn * xh)
        do_ref[...] = d + _rms_bwd(dn, xh, r, gain_v)

    act = jax.ShapeDtypeStruct((T, D), BF16)
    return pl.pallas_call(
        body, name="ple_bwd", grid=(T // tm,),
        in_specs=[_rows(tm, D), _rows(tm, D), _rows(tm, PD), _rows(tm, D), _const((1, D)), _const((D, PD)), ANY],
        out_specs=[_rows(tm, D), _rows(tm, D), _rows(tm, D), _rows(tm, D), _rows(tm, PD), _const((8, D))],
        out_shape=[jax.ShapeDtypeStruct((T, D), F32), act, act, act, jax.ShapeDtypeStruct((T, PD), BF16),
                   jax.ShapeDtypeStruct((8, D), F32)],
        scratch_shapes=[pltpu.VMEM((D, D), BF16), pltpu.SemaphoreType.DMA((1, NDEV))],
        compiler_params=_params(),
    )(d, h, pe, gate_a, gain, wppt, G)


_BIG = ["ffn1_w_gate", "ffn1_w_up", "ffn1_w_down", "w_in", "pool_w", "conv_w_out", "w_out", "ffn2_w_gate",
        "ffn2_w_up", "ffn2_w_down", "ple_w_gate", "ple_w_proj"]
_VECS = ["ffn1_norm", "mix_norm", "pool_scale", "conv_dw_b", "conv_ln_g", "conv_ln_b", "ffn2_norm", "ple_norm"]
_WEIGHTS = ["ffn1_norm", "ffn1_w_gate", "ffn1_w_up", "ffn1_w_down", "mix_norm", "w_in", "pool_w", "pool_scale",
            "conv_dw_w", "conv_dw_b", "conv_ln_g", "conv_ln_b", "conv_w_out", "w_out", "ffn2_norm", "ffn2_w_gate",
            "ffn2_w_up", "ffn2_w_down", "ple_norm", "ple_w_gate", "ple_w_proj", "final_norm"]


def _step(x, p, loss_target, w, mom, var):
    T, D = x.shape[1], x.shape[2]
    L = p.shape[0]
    FS, NS = w["ffn1_w_gate"].shape[2], w["w_in"].shape[2]
    PD = p.shape[3]
    PGS, PG = w["pool_w"].shape[2], w["pool_w"].shape[3]
    CS = w["conv_dw_w"].shape[2]
    lay = _Layout(D, FS, NS, PD, PG, PGS)
    FB = lay.FB
    ax, ay, ac = _place()
    me = 4 * ax + 2 * ay + ac

    assert L == 2, "the exchange schedule below is written for two layers"
    gather = {(l, pc): _GatherJob(f"ag{pc}", _pack_piece(w, l, lay, pc, BF16)) for l in range(L) for pc in PIECES}
    fwd_jobs = {("ffn1", 0): [gather[0, "C"], gather[0, "D"]], ("mixer", 0): [gather[0, "B"]],
                ("ffn2", 0): [gather[1, "A"], gather[1, "D"]], ("ffn1", 1): [gather[1, "C"]],
                ("mixer", 1): [gather[1, "B"]]}
    cw_mine = jnp.pad(w["conv_dw_w"], ((0, 0), (0, CONV_KP - CONV_K), (0, 0)))
    taps_job = _RowsGatherJob("agW", cw_mine.reshape(L * CONV_KP * CS // D, D))
    _run_jobs("gather_first", [gather[0, "A"], taps_job])
    cw_full = taps_job.results[0].reshape(NDEV, L, CONV_KP, CS).transpose(1, 2, 0, 3).reshape(L, CONV_KP, D)

    def gathered(l, pc):
        return gather[l, pc].results[0]

    def small_mats(l):
        G = gathered(l, "D")
        wppt = G[:, lay.off["wpp"]:lay.off["wpp"] + lay.rows["wpp"]].reshape(D, PD)
        pw = G[:, lay.off["pool"]:lay.off["pool"] + lay.rows["pool"]].reshape(NDEV, 4, PGS, PG)
        return wppt, pw.transpose(1, 0, 2, 3).reshape(4, PG, PG)

    def vec(name, l):
        return w[name][l].reshape(1, D)

    h = x[0]
    saved = []
    for l in range(L):
        s = {"h0": h}
        h, s["g1"], s["u1"], s["n1"] = _ffn_fwd(h, vec("ffn1_norm", l), gathered(l, "A"), lay,
                                                fwd_jobs.get(("ffn1", l), ()))
        wppt, pw = small_mats(l)
        s["wppt"], s["pw"], s["h1"] = wppt, pw, h
        s["z"] = _inproj_fwd(h, vec("mix_norm", l), gathered(l, "C"), lay)
        h, s["c1"], s["q"], s["cc"], s["pooled"] = _mixer_fwd(
            s["z"], h, pw, vec("pool_scale", l), cw_full[l], vec("conv_dw_b", l), vec("conv_ln_g", l),
            vec("conv_ln_b", l), gathered(l, "D"), lay, fwd_jobs.get(("mixer", l), ()))
        s["h2"] = h
        h, s["g2"], s["u2"], s["n2"] = _ffn_fwd(h, vec("ffn2_norm", l), gathered(l, "B"), lay,
                                                fwd_jobs.get(("ffn2", l), ()))
        s["h3"] = h
        if l + 1 < L:
            h, s["gate"] = _ple_fwd(h, p[l, 0], vec("ple_norm", l), wppt, gathered(l, "D"), lay)
        else:
            *turned, loss_part, g_final = _ple_fwd(h, p[l, 0], vec("ple_norm", l), wppt, gathered(l, "D"), lay,
                                                    head=(loss_target[0], w["final_norm"].reshape(1, D)))
        saved.append(s)

    scatter = {}
    small_parts = [None] * L

    def send(l, pc, blocks):
        scatter[l, pc] = _ScatterJob(f"rs{pc}", [blocks[n].reshape(NDEV, lay.rows[n], D) for n, _ in lay.pieces[pc]])
        return scatter[l, pc]

    def small_rows():
        return _fold_rows([v for l in range(L) for v in small_parts[l]] + [g_final, loss_part])

    held = []
    for l in reversed(range(L)):
        s = saved[l]
        wppt, pw = s["wppt"], s["pw"]
        g = {}
        if l == L - 1:
            d, dpre, de, n_ple, pb, g_ple = turned
        else:
            d, dpre, de, n_ple, pb, g_ple = _ple_bwd(d, s["h3"], p[l, 0], s["gate"], vec("ple_norm", l), wppt,
                                                     gathered(l, "D"), lay)
        g["wpg"] = _tn_matmul(n_ple, dpre, "tn_sq", D)
        g["wpp"] = _tn_matmul(de, pb, "tn_proj", D)

        n1, n2 = s["n1"], s["n2"]
        d_out = d
        d, dg2, du2, a2, g_n2 = _ffn_bwd(d, s["h2"], vec("ffn2_norm", l), s["g2"], s["u2"], gathered(l, "B"), lay, held)
        held = []
        g["g2"] = _tn_matmul(dg2, n2, "tn_ffn", NDEV * FB)
        g["u2"] = _tn_matmul(du2, n2, "tn_ffn", NDEV * FB)
        g["d2"] = _tn_matmul(a2, d_out, "tn_ffn_d", NDEV * FB, y_scale=0.5)
        held.append(send(l, "B", g))

        (m_b, dcc, c3, dq, dpool, dc1, dzg, g_ps, g_lg, g_lb, g_cb) = _mixer_bwd_rows(
            d, s["z"], s["c1"], s["q"], s["cc"], pw, vec("pool_scale", l), vec("conv_ln_g", l), vec("conv_ln_b", l),
            gathered(l, "D"), lay)
        g["wout"] = _tn_matmul(m_b, d, "tn_sq_d", D)
        g["wco"] = _tn_matmul(c3, dcc, "tn_sq", D)
        g_pool = _tn_groups(s["pooled"], dq, len(POOL_WINDOWS), "tn_pool")
        g["pool"] = g_pool.reshape(4, NDEV, PGS, PG).transpose(1, 0, 2, 3)
        held.append(send(l, "D", g))
        dzm, g_cw = _mixer_bwd_time(dc1, dpool, s["z"], cw_full[l], lay, held)
        held = []
        d, u_b, g_mix = _inproj_bwd(d, s["h1"], vec("mix_norm", l), dzm, dzg, gathered(l, "C"), lay)
        g["win"] = jnp.concatenate([_tn_matmul(dzm, u_b, "tn_in3", 3 * D // 2),
                                    _tn_matmul(dzg, u_b, "tn_in2", D)], axis=0)
        held.append(send(l, "C", g))

        d_out = d
        d, dg1, du1, a1, g_n1 = _ffn_bwd(d, s["h0"], vec("ffn1_norm", l), s["g1"], s["u1"], gathered(l, "A"), lay, held)
        held = []
        small_parts[l] = [g_n1, g_mix, g_ps, g_cb, g_lg, g_lb, g_n2, g_ple, g_cw]
        if l > 0:
            g["g1"] = _tn_matmul(dg1, n1, "tn_ffn", NDEV * FB)
            g["u1"] = _tn_matmul(du1, n1, "tn_ffn", NDEV * FB)
            g["d1"] = _tn_matmul(a1, d_out, "tn_ffn_d", NDEV * FB, y_scale=0.5)
            held.append(send(l, "A", g))
        else:
            rows_job = _RowsGatherJob("agS", small_rows())
            g["g1"] = _tn_matmul(dg1, n1, "tn_ffn", NDEV * FB, jobs=[rows_job])
            g["u1"] = _tn_matmul(du1, n1, "tn_ffn", NDEV * FB, jobs=[send(l, "A1", g)])
            g["d1"] = _tn_matmul(a1, d_out, "tn_ffn_d", NDEV * FB, y_scale=0.5, jobs=[send(l, "A2", g)])
            held.append(send(l, "A3", g))
    _run_jobs("scatter_last", held)
    grad_x = d[None]

    per = [{} for _ in range(L)]
    for (l, pc), job in scatter.items():
        per[l].update(_unpack_piece(_sum_parts(job.results[0], "sum_" + pc), lay, pc))
    grads = {n: jnp.stack([per[l][n] for l in range(L)]) for n in _BIG}

    small_sum = _sum_slots(rows_job.results[0])
    per_layer = len(_VECS) + CONV_KP
    for k, n in enumerate(_VECS):
        grads[n] = jnp.stack([small_sum[l * per_layer + k] for l in range(L)])
    g_cw_full = jnp.stack([small_sum[l * per_layer + len(_VECS):l * per_layer + len(_VECS) + CONV_K]
                           for l in range(L)])
    grads["conv_dw_w"] = lax.dynamic_slice_in_dim(g_cw_full, me * CS, CS, axis=2)
    grads["final_norm"] = small_sum[L * per_layer]
    loss = (0.5 / D) * jnp.sum(small_sum[L * per_layer + 1])

    def as2(a):
        return a.reshape(1, -1) if a.ndim == 1 else a

    deltas, new_m, new_v = {}, {}, {}

    def update(name, names, by_layer):
        res = _adamw(name, [(as2(w[n]), as2(grads[n]), as2(mom[n]), as2(var[n])) for n in names], by_layer)
        for n, (delta, m_new, v_new) in zip(names, res):
            deltas[n], new_m[n], new_v[n] = (a.reshape(w[n].shape) for a in (delta, m_new, v_new))

    for n in ("ffn1_w_gate", "ffn1_w_up", "ffn1_w_down", "ffn2_w_gate", "ffn2_w_up", "ffn2_w_down", "w_in"):
        update("adamw_" + n, [n], True)
    update("adamw_mid", ["conv_w_out", "w_out", "ple_w_gate", "ple_w_proj", "pool_w"], True)
    update("adamw_small", _VECS + ["conv_dw_w", "final_norm"], False)
    return loss, grad_x, (grads, deltas, new_m, new_v)


def kernel(x, p, ffn1_norm, ffn1_w_gate, ffn1_w_up, ffn1_w_down, mix_norm, w_in, pool_w, pool_scale, conv_dw_w, conv_dw_b, conv_ln_g, conv_ln_b, conv_w_out, w_out, ffn2_norm, ffn2_w_gate, ffn2_w_up, ffn2_w_down, ple_norm, ple_w_gate, ple_w_proj, final_norm, loss_target, m_ffn1_norm, m_ffn1_w_gate, m_ffn1_w_up, m_ffn1_w_down, m_mix_norm, m_w_in, m_pool_w, m_pool_scale, m_conv_dw_w, m_conv_dw_b, m_conv_ln_g, m_conv_ln_b, m_conv_w_out, m_w_out, m_ffn2_norm, m_ffn2_w_gate, m_ffn2_w_up, m_ffn2_w_down, m_ple_norm, m_ple_w_gate, m_ple_w_proj, m_final_norm, v_ffn1_norm, v_ffn1_w_gate, v_ffn1_w_up, v_ffn1_w_down, v_mix_norm, v_w_in, v_pool_w, v_pool_scale, v_conv_dw_w, v_conv_dw_b, v_conv_ln_g, v_conv_ln_b, v_conv_w_out, v_w_out, v_ffn2_norm, v_ffn2_w_gate, v_ffn2_w_up, v_ffn2_w_down, v_ple_norm, v_ple_w_gate, v_ple_w_proj, v_final_norm):
    given = dict(locals())
    w = {n: given[n] for n in _WEIGHTS}
    mom = {n: given["m_" + n] for n in _WEIGHTS}
    var = {n: given["v_" + n] for n in _WEIGHTS}
    loss, grad_x, (grads, deltas, new_m, new_v) = _step(x, p, loss_target, w, mom, var)
    out = [loss, grad_x]
    for res in (grads, deltas, new_m, new_v):
        out += [res[n] for n in _WEIGHTS]
    return tuple(out)
```

```python
import functools

import jax
import jax.numpy as jnp
from jax import lax
from jax.experimental import pallas as pl
from jax.experimental.pallas import tpu as pltpu

F32, BF16 = jnp.float32, jnp.bfloat16
NDEV = 8
MESH = pl.DeviceIdType.MESH
HALO = 32
POOL_WINDOWS = (2, 4, 8, 16)
CONV_K = 31
CONV_KP = 32
RMS_EPS, LN_EPS = 1e-6, 1e-5
ADAM_LR, ADAM_B1, ADAM_B2, ADAM_EPS, ADAM_WD, ADAM_STEP = 0.001, 0.9, 0.999, 1e-08, 0.01, 10
LANE = 128
VMEM_LIMIT = 56 * 1024 * 1024
ANY = pl.BlockSpec(memory_space=pl.ANY)


def _nn(a, b):
    return jnp.dot(a, b, preferred_element_type=F32)


def _nt(a, b):
    return lax.dot_general(a, b, (((1,), (1,)), ((), ())), preferred_element_type=F32)


def _tn(a, b):
    return lax.dot_general(a, b, (((0,), (0,)), ((), ())), preferred_element_type=F32)


def _colsum8(v):
    return jnp.sum(v.reshape(v.shape[0] // 8, 8, v.shape[1]), axis=0)


def _sig(v):
    return jax.nn.sigmoid(v)


def _rms(h):
    r = lax.rsqrt(jnp.mean(h * h, axis=-1, keepdims=True) + RMS_EPS)
    return h * r, r


def _rms_bwd(dn, xh, r, gain):
    dxh = dn * gain
    return r * (dxh - xh * jnp.mean(dxh * xh, axis=-1, keepdims=True))


def _ln(c1):
    mu = jnp.mean(c1, axis=-1, keepdims=True)
    cen = c1 - mu
    rstd = lax.rsqrt(jnp.mean(cen * cen, axis=-1, keepdims=True) + LN_EPS)
    return cen * rstd, rstd


def _rows(tm, c):
    return pl.BlockSpec((tm, c), lambda i: (i, 0))


def _const(shape):
    return pl.BlockSpec(shape, lambda i: (0,) * len(shape))


def _params(n_grid=1):
    return pltpu.CompilerParams(dimension_semantics=("arbitrary",) * n_grid, vmem_limit_bytes=VMEM_LIMIT)


BF16_ROWS = 16
MXU_CHUNK = 768
BWD_CHUNK = 512


def _tile(n, want):
    if n <= want:
        return n
    return max(t for t in range(BF16_ROWS, want + 1, BF16_ROWS) if n % t == 0)


def _hidden_chunks(width, chunk=MXU_CHUNK):
    return [(o, min(chunk, width - o)) for o in range(0, width, chunk)]


def _staggered(chunks, project, activate, contract, acc):
    spans = [pl.ds(start, width) for start, width in chunks]
    ahead = project(spans[0])
    for c, span in enumerate(spans):
        projected = ahead
        if c + 1 < len(spans):
            ahead = project(spans[c + 1])
        acc = contract(span, activate(span, projected), acc)
    return acc


def _fetch(g_hbm, specs, sems):
    cps = []
    for wi, (off, rows, dst) in enumerate(specs):
        for dev in range(NDEV):
            cps.append(pltpu.make_async_copy(g_hbm.at[dev, pl.ds(off, rows), :],
                                             dst.at[pl.ds(dev * rows, rows), :], sems.at[wi, dev]))
    for cp in cps:
        cp.start()
    for cp in cps:
        cp.wait()


PIECES = ("A", "B", "C", "D")


class _Layout:
    def __init__(self, D, FS, NS, PD, PG, PGS):
        self.D, self.FS, self.NS, self.PD, self.PG, self.PGS = D, FS, NS, PD, PG, PGS
        assert FS % BF16_ROWS == 0 and (NDEV * FS) % (2 * LANE) == 0, "FFN shard rows must tile as bf16 row blocks"
        self.FB = FS
        self.DS = D // NDEV
        self.pieces = {
            "A": [("g1", self.FB), ("u1", self.FB), ("d1", self.FB)],
            "B": [("g2", self.FB), ("u2", self.FB), ("d2", self.FB)],
            "C": [("win", NS)],
            "D": [("wco", self.DS), ("wout", self.DS), ("wpg", self.DS), ("wpp", self.DS * PD // D),
                  ("pool", 4 * PGS * PG // D)],
            "A1": [("g1", self.FB)], "A2": [("u1", self.FB)], "A3": [("d1", self.FB)]}
        self.off, self.rows = {}, {}
        for pc in PIECES:
            o = 0
            for n, r in self.pieces[pc]:
                self.off[n], self.rows[n] = o, r
                o += r


def _pack_piece(w, l, lay, pc, dtype):
    D = lay.D
    make = {"g1": lambda: w["ffn1_w_gate"][l].T, "u1": lambda: w["ffn1_w_up"][l].T,
            "d1": lambda: w["ffn1_w_down"][l], "g2": lambda: w["ffn2_w_gate"][l].T,
            "u2": lambda: w["ffn2_w_up"][l].T, "d2": lambda: w["ffn2_w_down"][l],
            "win": lambda: w["w_in"][l].T, "wco": lambda: w["conv_w_out"][l], "wout": lambda: w["w_out"][l],
            "wpg": lambda: w["ple_w_gate"][l], "wpp": lambda: w["ple_w_proj"][l].T.reshape(-1, D),
            "pool": lambda: w["pool_w"][l].reshape(-1, D)}
    return jnp.concatenate([make[n]() for n, _ in lay.pieces[pc]], axis=0).astype(dtype)


def _unpack_piece(slab, lay, pc):
    PD, PG, PGS, DS = lay.PD, lay.PG, lay.PGS, lay.DS
    undo = {"g1": ("ffn1_w_gate", lambda a: a.T), "u1": ("ffn1_w_up", lambda a: a.T),
            "d1": ("ffn1_w_down", lambda a: a), "g2": ("ffn2_w_gate", lambda a: a.T),
            "u2": ("ffn2_w_up", lambda a: a.T), "d2": ("ffn2_w_down", lambda a: a),
            "win": ("w_in", lambda a: a.T), "wco": ("conv_w_out", lambda a: a), "wout": ("w_out", lambda a: a),
            "wpg": ("ple_w_gate", lambda a: a), "wpp": ("ple_w_proj", lambda a: a.reshape(DS, PD).T),
            "pool": ("pool_w", lambda a: a.reshape(4, PGS, PG))}
    out, o = {}, 0
    for n, r in lay.pieces[pc]:
        name, fn = undo[n]
        out[name] = fn(slab[o:o + r])
        o += r
    return out


def _place():
    return lax.axis_index("x"), lax.axis_index("y"), lax.axis_index("c")


FLIPS = [(a, b, d) for a in (0, 1) for b in (0, 1) for d in (0, 1)][1:]


class _GatherJob:
    def __init__(self, tag, slab):
        self.tag, self.ins = tag, [slab]
        self.outs = [jax.ShapeDtypeStruct((NDEV,) + slab.shape, slab.dtype)]
        self.scratch = [pltpu.SemaphoreType.DMA((7,)), pltpu.SemaphoreType.DMA((7,)), pltpu.SemaphoreType.DMA]
        self.results = None

    def _plan(self, ins, outs, sems):
        (x_ref,), (out_ref,), (send_sems, recv_sems, local_sem) = ins, outs, sems
        x, y, c = _place()
        me, sibling = (x, y, c), (x, y, 1 - c)
        chips = [(1 - x, y), (x, 1 - y), (1 - x, 1 - y)]

        def rows(px, py, pc):
            return out_ref.at[4 * px + 2 * py + pc]

        def copy(k, block, to, src=None):
            return pltpu.make_async_remote_copy(
                src_ref=rows(*block) if src is None else src, dst_ref=rows(*block),
                send_sem=send_sems.at[k], recv_sem=recv_sems.at[k], device_id=to, device_id_type=MESH)

        mine = pltpu.make_async_copy(x_ref, rows(*me), local_sem)
        first = [copy(0, me, sibling, src=x_ref)]
        first += [copy(1 + j, me, (*chip, c), src=x_ref) for j, chip in enumerate(chips)]
        passed = [copy(4 + j, (*chip, c), sibling) for j, chip in enumerate(chips)]
        landed = [copy(1 + j, (*chip, c), me) for j, chip in enumerate(chips)]
        late = [copy(0, sibling, me)] + [copy(4 + j, (*chip, 1 - c), me) for j, chip in enumerate(chips)]
        return mine, first, passed, landed, late

    def start(self, ins, outs, sems):
        mine, first, _, _, _ = self._plan(ins, outs, sems)
        mine.start()
        for cp in first:
            cp.start()

    def middle(self, ins, outs, sems):
        _, _, passed, landed, _ = self._plan(ins, outs, sems)
        for got, cp in zip(landed, passed):
            got.wait_recv()
            cp.start()

    def finish(self, ins, outs, sems):
        mine, first, passed, _, late = self._plan(ins, outs, sems)
        for got in late:
            got.wait_recv()
        for cp in first + passed:
            cp.wait_send()
        mine.wait()


class _ScatterJob:
    def __init__(self, tag, grads):
        self.tag, self.ins = tag, list(grads)
        self.row_counts = [g.shape[1] for g in grads]
        self.outs = [jax.ShapeDtypeStruct((NDEV, sum(self.row_counts), grads[0].shape[2]), grads[0].dtype)]
        n = len(grads)
        self.scratch = [pltpu.SemaphoreType.DMA((n, 7)), pltpu.SemaphoreType.DMA((n, 7)), pltpu.SemaphoreType.DMA((n,))]
        self.results = None

    def _plan(self, ins, outs, sems):
        (out_ref,), (send_sems, recv_sems, local_sems) = outs, sems
        x, y, c = _place()
        remote, local, off = [], [], 0
        for i, (g_ref, rows) in enumerate(zip(ins, self.row_counts)):
            span = pl.ds(off, rows)
            for k, (a, b, d) in enumerate(FLIPS):
                px, py, pc = x ^ a, y ^ b, c ^ d
                remote.append(pltpu.make_async_remote_copy(
                    src_ref=g_ref.at[4 * px + 2 * py + pc], dst_ref=out_ref.at[k, span, :], send_sem=send_sems.at[i, k],
                    recv_sem=recv_sems.at[i, k], device_id=(px, py, pc), device_id_type=MESH))
            local.append(pltpu.make_async_copy(g_ref.at[4 * x + 2 * y + c], out_ref.at[7, span, :], local_sems.at[i]))
            off += rows
        return remote, local

    def start(self, ins, outs, sems):
        remote, local = self._plan(ins, outs, sems)
        for cp in remote + local:
            cp.start()

    def middle(self, ins, outs, sems):
        pass

    def finish(self, ins, outs, sems):
        remote, local = self._plan(ins, outs, sems)
        for cp in remote:
            cp.wait_recv()
        for cp in remote:
            cp.wait_send()
        for cp in local:
            cp.wait()


class _RowsGatherJob:
    def __init__(self, tag, rows):
        self.tag, self.ins = tag, [rows]
        self.outs = [jax.ShapeDtypeStruct((NDEV,) + rows.shape, rows.dtype)]
        self.scratch = [pltpu.SemaphoreType.DMA((7,)), pltpu.SemaphoreType.DMA((7,)), pltpu.SemaphoreType.DMA]
        self.results = None

    def _plan(self, ins, outs, sems):
        (x_ref,), (out_ref,), (send_sems, recv_sems, local_sem) = ins, outs, sems
        x, y, c = _place()
        me = 4 * x + 2 * y + c
        mine = pltpu.make_async_copy(x_ref, out_ref.at[me], local_sem)
        sends, lands = [], []
        for k, (a, b, d) in enumerate(FLIPS):
            px, py, pc = x ^ a, y ^ b, c ^ d
            for dst, keep in ((out_ref.at[me], sends), (out_ref.at[4 * px + 2 * py + pc], lands)):
                keep.append(pltpu.make_async_remote_copy(
                    src_ref=x_ref, dst_ref=dst, send_sem=send_sems.at[k], recv_sem=recv_sems.at[k],
                    device_id=(px, py, pc), device_id_type=MESH))
        return mine, sends, lands

    def start(self, ins, outs, sems):
        mine, sends, _ = self._plan(ins, outs, sems)
        mine.start()
        for cp in sends:
            cp.start()

    def middle(self, ins, outs, sems):
        pass

    def finish(self, ins, outs, sems):
        mine, sends, lands = self._plan(ins, outs, sems)
        for cp in lands:
            cp.wait_recv()
        for cp in sends:
            cp.wait_send()
        mine.wait()


def _fold_rows(parts):
    D = parts[0].shape[1]
    counts = [a.shape[0] // 8 for a in parts]
    total = -(-sum(counts) // 8) * 8

    def body(*refs):
        out = refs[-1]
        out[...] = jnp.zeros_like(out)
        row = 0
        for ref, k in zip(refs[:-1], counts):
            for j in range(k):
                out[pl.ds(row + j, 1), :] = jnp.sum(ref[pl.ds(8 * j, 8), :], axis=0, keepdims=True)
            row += k

    vm = pl.BlockSpec(memory_space=pltpu.VMEM)
    return pl.pallas_call(body, name="fold_rows", out_shape=jax.ShapeDtypeStruct((total, D), F32),
                          in_specs=[vm] * len(parts), out_specs=vm)(*parts)


def _sum_slots(slots):
    n, S, D = slots.shape

    def body(s_ref, o_ref):
        acc = s_ref[0]
        for j in range(1, n):
            acc = acc + s_ref[j]
        o_ref[...] = acc

    vm = pl.BlockSpec(memory_space=pltpu.VMEM)
    return pl.pallas_call(body, name="sum_slots", out_shape=jax.ShapeDtypeStruct((S, D), slots.dtype),
                          in_specs=[vm], out_specs=vm)(slots)


def _launch(name, body, grid, args, in_specs, out_specs, out_shape, scratch, jobs=()):
    grid = (grid,) if isinstance(grid, int) else tuple(grid)
    steps = grid[0] * (grid[1] if len(grid) == 2 else 1)
    n_in, n_out, n_sc = len(args), len(out_shape), len(scratch)
    j_in = [a for jb in jobs for a in jb.ins]
    j_out = [o for jb in jobs for o in jb.outs]
    j_sc = [s for jb in jobs for s in jb.scratch]
    mid = (17 * steps) // 20

    def wrapped(*refs):
        c_in, refs = refs[:n_in], refs[n_in:]
        m_in, refs = refs[:len(j_in)], refs[len(j_in):]
        c_out, refs = refs[:n_out], refs[n_out:]
        m_out, refs = refs[:len(j_out)], refs[len(j_out):]
        c_sc, m_sc = refs[:n_sc], refs[n_sc:]
        bound, a, b, c = [], 0, 0, 0
        for jb in jobs:
            bound.append((jb, m_in[a:a + len(jb.ins)], m_out[b:b + len(jb.outs)], m_sc[c:c + len(jb.scratch)]))
            a, b, c = a + len(jb.ins), b + len(jb.outs), c + len(jb.scratch)
        i = pl.program_id(0) if len(grid) == 1 else pl.program_id(0) * grid[1] + pl.program_id(1)

        def phase(step, which):
            if jobs:
                @pl.when(i == step)
                def _():
                    for jb, ins, outs, sems in bound:
                        getattr(jb, which)(ins, outs, sems)

        phase(0, "start")
        if body is not None:
            body(*c_in, *c_out, *c_sc)
        phase(mid, "middle")
        phase(steps - 1, "finish")

    outs = pl.pallas_call(
        wrapped, name=name + "".join("_" + jb.tag for jb in jobs), grid=grid,
        in_specs=list(in_specs) + [ANY] * len(j_in), out_specs=list(out_specs) + [ANY] * len(j_out),
        out_shape=list(out_shape) + j_out, scratch_shapes=list(scratch) + j_sc, compiler_params=_params(len(grid)),
    )(*args, *j_in)
    pos = n_out
    for jb in jobs:
        jb.results = list(outs[pos:pos + len(jb.outs)])
        pos += len(jb.outs)
    return list(outs[:n_out])


def _run_jobs(name, jobs):
    _launch(name, None, 1, [], [], [], [], [], jobs)


def _sum_parts(got, name):
    n, rows, D = got.shape
    tr = _tile(rows, 512)
    order = [n - 1] + list(range(n - 1))

    def body(*refs):
        g = refs[0][...].astype(F32)
        for pr in refs[1:n]:
            g = g + pr[...].astype(F32)
        refs[-1][...] = g

    specs = [pl.BlockSpec((None, tr, D), functools.partial(lambda k, i: (k, i, 0), k)) for k in order]
    return pl.pallas_call(
        body, name=name, grid=(rows // tr,), in_specs=specs, out_specs=_rows(tr, D),
        out_shape=jax.ShapeDtypeStruct((rows, D), F32), compiler_params=_params(),
    )(*[got] * n)


def _adamw(name, entries, by_layer):
    n = len(entries)
    L = entries[0][0].shape[0]

    def body(*refs):
        ins, outs = refs[:4 * n], refs[4 * n:]
        for e in range(n):
            w_ref, g_ref, m_ref, v_ref = ins[4 * e:4 * e + 4]
            d_out, m_out, v_out = outs[3 * e:3 * e + 3]
            g = g_ref[...]
            m_new = ADAM_B1 * m_ref[...] + (1.0 - ADAM_B1) * g
            v_new = ADAM_B2 * v_ref[...] + (1.0 - ADAM_B2) * (g * g)
            m_hat = m_new / (1.0 - ADAM_B1 ** ADAM_STEP)
            v_hat = v_new / (1.0 - ADAM_B2 ** ADAM_STEP)
            d_out[...] = -ADAM_LR * (m_hat / (jnp.sqrt(v_hat) + ADAM_EPS) + ADAM_WD * w_ref[...])
            m_out[...] = m_new
            v_out[...] = v_new

    def spec(a):
        rest = (0,) * (a.ndim - 1)
        if by_layer:
            return pl.BlockSpec((None,) + a.shape[1:], lambda l: (l,) + rest)
        return pl.BlockSpec(a.shape, lambda l: (0,) + rest)

    flat = [a for e in entries for a in e]
    outs = pl.pallas_call(
        body, name=name, grid=(L if by_layer else 1,),
        in_specs=[spec(a) for a in flat], out_specs=[spec(e[0]) for e in entries for _ in range(3)],
        out_shape=[jax.ShapeDtypeStruct(e[0].shape, F32) for e in entries for _ in range(3)],
        compiler_params=_params(),
    )(*flat)
    return [tuple(outs[3 * e:3 * e + 3]) for e in range(n)]


def _tn_matmul(xa, ya, name, tmm, y_scale=1.0, jobs=()):
    T, M = xa.shape
    tn = ya.shape[1]
    tt = _tile(T, 1024)
    nb = M // tmm

    def body(x_ref, y_ref, o_ref, acc):
        k = pl.program_id(1)

        @pl.when(k == 0)
        def _():
            acc[...] = jnp.zeros_like(acc)

        y = y_ref[...]
        if y_scale != 1.0 or y.dtype != BF16:
            y = (y_scale * y).astype(BF16)
        acc[...] += _tn(x_ref[...], y)

        @pl.when(k == pl.num_programs(1) - 1)
        def _():
            o_ref[...] = acc[...].astype(o_ref.dtype)

    return _launch(
        name, body, (nb, T // tt), [xa, ya],
        [pl.BlockSpec((tt, tmm), lambda b, k: (k, b)), pl.BlockSpec((tt, tn), lambda b, k: (k, 0))],
        [pl.BlockSpec((tmm, tn), lambda b, k: (b, 0))], [jax.ShapeDtypeStruct((M, tn), BF16)],
        [pltpu.VMEM((tmm, tn), F32)], jobs)[0]


def _tn_groups(xa, ya, groups, name):
    T, M = xa.shape
    w = M // groups
    tt = _tile(T, 1024)
    steps = T // tt

    def body(x_ref, y_ref, o_ref, acc):
        k = pl.program_id(0)

        @pl.when(k == 0)
        def _():
            acc[...] = jnp.zeros_like(acc)

        for g in range(groups):
            cols = pl.ds(g * w, w)
            acc[cols, :] += _tn(x_ref[:, cols], y_ref[:, cols])

        @pl.when(k == steps - 1)
        def _():
            o_ref[...] = acc[...].astype(o_ref.dtype)

    return pl.pallas_call(
        body, name=name, grid=(steps,), in_specs=[_rows(tt, M), _rows(tt, M)], out_specs=_const((M, w)),
        out_shape=jax.ShapeDtypeStruct((M, w), BF16), scratch_shapes=[pltpu.VMEM((M, w), F32)],
        compiler_params=_params(),
    )(xa, ya)


def _ffn_fwd(h, gain, G, lay, jobs=()):
    T, D = h.shape
    FB = lay.FB
    FP = NDEV * FB
    offs = (0, FB, 2 * FB)
    tm = _tile(T, 512)

    def body(h_ref, gain_ref, g_hbm, ho_ref, go_ref, uo_ref, n_ref, wg, wu, wd, sems):
        @pl.when(pl.program_id(0) == 0)
        def _():
            _fetch(g_hbm, [(offs[0], FB, wg), (offs[1], FB, wu), (offs[2], FB, wd)], sems)

        h = h_ref[...]
        xh, _ = _rms(h)
        n = (xh * gain_ref[...]).astype(BF16)
        n_ref[...] = n

        def project(sl):
            return _nt(n, wg[sl, :]), _nt(n, wu[sl, :])

        def activate(sl, gu):
            g, u = gu
            go_ref[:, sl] = g.astype(BF16)
            uo_ref[:, sl] = u.astype(BF16)
            return (g * _sig(g) * u).astype(BF16)

        def contract(sl, a, acc):
            return acc + _nn(a, wd[sl, :])

        acc = _staggered(_hidden_chunks(FP), project, activate, contract, jnp.zeros((tm, D), F32))
        ho_ref[...] = h + 0.5 * acc

    return _launch(
        "ffn_fwd", body, T // tm, [h, gain, G],
        [_rows(tm, D), _const((1, D)), ANY], [_rows(tm, D), _rows(tm, FP), _rows(tm, FP), _rows(tm, D)],
        [jax.ShapeDtypeStruct((T, D), F32), jax.ShapeDtypeStruct((T, FP), BF16), jax.ShapeDtypeStruct((T, FP), BF16),
         jax.ShapeDtypeStruct((T, D), BF16)],
        [pltpu.VMEM((FP, D), BF16)] * 3 + [pltpu.SemaphoreType.DMA((3, NDEV))], jobs)


def _ffn_bwd(d, h, gain, ga, ua, G, lay, jobs=()):
    T, D = h.shape
    FB = lay.FB
    FP = NDEV * FB
    offs = (0, FB, 2 * FB)
    tm = _tile(T, 256)

    def body(d_ref, h_ref, gain_ref, ga_ref, ua_ref, g_hbm,
             do_ref, dg_ref, du_ref, a_ref, gg_ref, wg, wu, wd, sems, dh_s):
        @pl.when(pl.program_id(0) == 0)
        def _():
            _fetch(g_hbm, [(offs[0], FB, wg), (offs[1], FB, wu), (offs[2], FB, wd)], sems)
            gg_ref[...] = jnp.zeros_like(gg_ref)

        dh_s[...] = (0.5 * d_ref[...]).astype(BF16)

        def project(sl):
            return _nt(dh_s[...], wd[sl, :])

        def activate(sl, da):
            g = ga_ref[:, sl].astype(F32)
            u = ua_ref[:, sl].astype(F32)
            s = _sig(g)
            silu = g * s
            a_ref[:, sl] = (silu * u).astype(BF16)
            dgv = (da * u * (s * (1.0 + g * (1.0 - s)))).astype(BF16)
            duv = (da * silu).astype(BF16)
            dg_ref[:, sl] = dgv
            du_ref[:, sl] = duv
            return dgv, duv

        def contract(sl, grads, acc):
            return acc + _nn(grads[0], wg[sl, :]) + _nn(grads[1], wu[sl, :])

        dn = _staggered(_hidden_chunks(FP, BWD_CHUNK), project, activate, contract, jnp.zeros((tm, D), F32))
        xh, r = _rms(h_ref[...])
        gg_ref[...] += _colsum8(dn * xh)
        do_ref[...] = d_ref[...] + _rms_bwd(dn, xh, r, gain_ref[...])

    wide = jax.ShapeDtypeStruct((T, FP), BF16)
    return _launch(
        "ffn_bwd", body, T // tm, [d, h, gain, ga, ua, G],
        [_rows(tm, D), _rows(tm, D), _const((1, D)), _rows(tm, FP), _rows(tm, FP), ANY],
        [_rows(tm, D), _rows(tm, FP), _rows(tm, FP), _rows(tm, FP), _const((8, D))],
        [jax.ShapeDtypeStruct((T, D), F32), wide, wide, wide, jax.ShapeDtypeStruct((8, D), F32)],
        [pltpu.VMEM((FP, D), BF16)] * 3 + [pltpu.SemaphoreType.DMA((3, NDEV)), pltpu.VMEM((tm, D), BF16)], jobs)


def _inproj_fwd(h, gain, G, lay):
    T, D = h.shape
    NS = lay.NS
    NIN, CH = NDEV * NS, 2 * NS
    tm = _tile(T, 512)

    def body(h_ref, gain_ref, g_hbm, z_ref, win, sems):
        @pl.when(pl.program_id(0) == 0)
        def _():
            _fetch(g_hbm, [(lay.off["win"], NS, win)], sems)

        xh, _ = _rms(h_ref[...])
        n = (xh * gain_ref[...]).astype(BF16)
        for j in range(NIN // CH):
            sl = pl.ds(j * CH, CH)
            z_ref[:, sl] = _nt(n, win[sl, :]).astype(BF16)

    return pl.pallas_call(
        body, name="inproj_fwd", grid=(T // tm,),
        in_specs=[_rows(tm, D), _const((1, D)), ANY], out_specs=_rows(tm, NIN),
        out_shape=jax.ShapeDtypeStruct((T, NIN), BF16),
        scratch_shapes=[pltpu.VMEM((NIN, D), BF16), pltpu.SemaphoreType.DMA((1, NDEV))],
        compiler_params=_params(),
    )(h, gain, G)


def _conv_chunks(tm, D):
    rb, lc = min(tm, 64), min(D, 256)
    return [(r0, l0, rb, lc) for r0 in range(0, tm, rb) for l0 in range(0, D, lc)]


SUBLANES = 8


def _preshift(sh, n_rows):
    for r in range(1, SUBLANES):
        sh[r, pl.ds(0, n_rows), :] = sh[0, pl.ds(r, n_rows), :]


def _window(sh, start, rows, lanes):
    r = start % SUBLANES
    return sh[r, pl.ds(start - r, rows), lanes]


def _mixer_fwd(z, h, pool_w, pscale, cw, cb, lng, lnb, G, lay, jobs=()):
    T, D = h.shape
    PG, DS = lay.PG, lay.DS
    tm = _tile(T, 256)
    hb = tm // HALO

    def body(z_ref, zp_ref, h_ref, pw_ref, ps_ref, cw_ref, cb_ref, lg_ref, lb_ref, g_hbm,
             ho_ref, c1_ref, q_ref, cc_ref, pool_ref, ext_p, sh_c, a_s, wco, wout, sems):
        i = pl.program_id(0)

        @pl.when(i == 0)
        def _():
            _fetch(g_hbm, [(lay.off["wco"], DS, wco), (lay.off["wout"], DS, wout)], sems)

        live = jnp.where(i > 0, 1.0, 0.0).astype(F32)
        ext_p[pl.ds(0, HALO), :] = zp_ref[:, pl.ds(0, D)].astype(F32) * live
        ext_p[pl.ds(HALO, tm), :] = z_ref[:, pl.ds(0, D)].astype(F32)
        sh_c[0, pl.ds(0, HALO), :] = (zp_ref[:, pl.ds(D, D)].astype(F32)
                                      * _sig(zp_ref[:, pl.ds(2 * D, D)].astype(F32)) * live)
        sh_c[0, pl.ds(HALO, tm), :] = z_ref[:, pl.ds(D, D)].astype(F32) * _sig(z_ref[:, pl.ds(2 * D, D)].astype(F32))
        _preshift(sh_c, tm + HALO - SUBLANES)

        t = i * tm + lax.broadcasted_iota(jnp.int32, (tm, 1), 0)
        for g, w in enumerate(POOL_WINDOWS):
            sl = pl.ds(g * PG, PG)
            s = ext_p[pl.ds(HALO, tm), sl]
            zc = s
            for j in range(1, w):
                s = s + ext_p[pl.ds(HALO - j, tm), sl]
            inv = 1.0 / jnp.minimum(t + 1, w).astype(F32)
            pooled = (s * inv - zc).astype(BF16)
            pool_ref[:, sl] = pooled
            qv = _nn(pooled, pw_ref[g])
            q_ref[:, sl] = qv.astype(BF16)
            a_s[:, sl] = qv * ps_ref[:, sl]

        for r0, l0, rb, lc in _conv_chunks(tm, D):
            ls = pl.ds(l0, lc)
            acc = jnp.zeros((rb, lc), F32) + cb_ref[:, ls]
            for k in range(CONV_K):
                acc = acc + cw_ref[pl.ds(k, 1), ls] * _window(sh_c, r0 + HALO - (CONV_K - 1) + k, rb, ls)
            c1_ref[pl.ds(r0, rb), ls] = acc

        xhat, _ = _ln(c1_ref[...])
        c2 = xhat * lg_ref[...] + lb_ref[...]
        c3 = (c2 * _sig(c2)).astype(BF16)
        cc = _nn(c3, wco[...])
        cc_ref[...] = cc.astype(BF16)
        gp = z_ref[:, pl.ds(3 * D, D)].astype(F32)
        gc = z_ref[:, pl.ds(4 * D, D)].astype(F32)
        m = (_sig(gp) * a_s[...] + _sig(gc) * cc).astype(BF16)
        ho_ref[...] = h_ref[...] + _nn(m, wout[...])

    act = jax.ShapeDtypeStruct((T, D), BF16)
    return _launch(
        "mixer_fwd", body, T // tm, [z, z, h, pool_w, pscale, cw, cb, lng, lnb, G],
        [_rows(tm, 5 * D), pl.BlockSpec((HALO, 5 * D), lambda i: (jnp.maximum(i * hb - 1, 0), 0)),
         _rows(tm, D), _const((4, PG, PG)), _const((1, D)), _const((CONV_KP, D)), _const((1, D)),
         _const((1, D)), _const((1, D)), ANY],
        [_rows(tm, D)] * 5,
        [jax.ShapeDtypeStruct((T, D), F32), jax.ShapeDtypeStruct((T, D), F32), act, act, act],
        [pltpu.VMEM((HALO + tm, D), F32), pltpu.VMEM((SUBLANES, HALO + tm, D), F32), pltpu.VMEM((tm, D), F32),
         pltpu.VMEM((D, D), BF16), pltpu.VMEM((D, D), BF16), pltpu.SemaphoreType.DMA((2, NDEV))], jobs)


def _mixer_bwd_rows(d, z, c1, qa, cca, pool_w, pscale, lng, lnb, G, lay):
    T, D = d.shape
    PG, DS = lay.PG, lay.DS
    tm = _tile(T, 256)

    def body(d_ref, zgp_ref, zgc_ref, c1_ref, q_ref, cc_ref, pw_ref, ps_ref, lg_ref, lb_ref, g_hbm,
             m_ref, dcc_ref, c3_ref, dq_ref, dpool_ref, dc1_ref, dzg_ref, gps_ref, glg_ref, glb_ref, gcb_ref,
             wco, wout, sems):
        @pl.when(pl.program_id(0) == 0)
        def _():
            _fetch(g_hbm, [(lay.off["wco"], DS, wco), (lay.off["wout"], DS, wout)], sems)
            for ref in (gps_ref, glg_ref, glb_ref, gcb_ref):
                ref[...] = jnp.zeros_like(ref)

        dm = _nt(d_ref[...].astype(BF16), wout[...])
        q = q_ref[...].astype(F32)
        cc = cc_ref[...].astype(F32)
        ps = ps_ref[...]
        sp = _sig(zgp_ref[...].astype(F32))
        sc = _sig(zgc_ref[...].astype(F32))
        a = q * ps
        m_ref[...] = (sp * a + sc * cc).astype(BF16)
        da = dm * sp
        dzg_ref[:, pl.ds(0, D)] = (dm * a * sp * (1.0 - sp)).astype(BF16)
        dzg_ref[:, pl.ds(D, D)] = (dm * cc * sc * (1.0 - sc)).astype(BF16)
        gps_ref[...] += _colsum8(da * q)
        dq = (da * ps).astype(BF16)
        dq_ref[...] = dq
        for g in range(len(POOL_WINDOWS)):
            sl = pl.ds(g * PG, PG)
            dpool_ref[:, sl] = _nt(dq_ref[:, sl], pw_ref[g]).astype(dpool_ref.dtype)

        dcc = (dm * sc).astype(BF16)
        dcc_ref[...] = dcc
        xhat, rstd = _ln(c1_ref[...])
        lg = lg_ref[...]
        c2 = xhat * lg + lb_ref[...]
        s2 = _sig(c2)
        c3_ref[...] = (c2 * s2).astype(BF16)
        dc2 = _nt(dcc, wco[...]) * (s2 * (1.0 + c2 * (1.0 - s2)))
        glg_ref[...] += _colsum8(dc2 * xhat)
        glb_ref[...] += _colsum8(dc2)
        dxh = dc2 * lg
        dc1 = rstd * (dxh - jnp.mean(dxh, axis=-1, keepdims=True)
                      - xhat * jnp.mean(dxh * xhat, axis=-1, keepdims=True))
        dc1_ref[...] = dc1.astype(dc1_ref.dtype)
        gcb_ref[...] += _colsum8(dc1)

    act = jax.ShapeDtypeStruct((T, D), BF16)
    full = jax.ShapeDtypeStruct((T, D), F32)
    vec = jax.ShapeDtypeStruct((8, D), F32)
    return pl.pallas_call(
        body, name="mixer_bwd_rows", grid=(T // tm,),
        in_specs=[_rows(tm, D), pl.BlockSpec((tm, D), lambda i: (i, 3)), pl.BlockSpec((tm, D), lambda i: (i, 4)),
                  _rows(tm, D), _rows(tm, D), _rows(tm, D),
                  _const((4, PG, PG)), _const((1, D)), _const((1, D)), _const((1, D)), ANY],
        out_specs=[_rows(tm, D)] * 6 + [_rows(tm, 2 * D)] + [_const((8, D))] * 4,
        out_shape=[act, act, act, act, act, act, jax.ShapeDtypeStruct((T, 2 * D), BF16), vec, vec, vec, vec],
        scratch_shapes=[pltpu.VMEM((D, D), BF16), pltpu.VMEM((D, D), BF16), pltpu.SemaphoreType.DMA((2, NDEV))],
        compiler_params=_params(),
    )(d, z, z, c1, qa, cca, pool_w, pscale, lng, lnb, G)


def _mixer_bwd_time(dc1, dpool, z, cw, lay, jobs=()):
    T, D = dc1.shape
    PG = lay.PG
    tm = _tile(T, 256)
    hb = tm // HALO
    nt = T // tm

    def body(dc_ref, dcn_ref, dp_ref, dpn_ref, z_ref, zp_ref, cw_ref, dz_ref, gcw_ref, sh_d, ext_q, sh_c, dc0_s):
        i = pl.program_id(0)

        @pl.when(i == 0)
        def _():
            gcw_ref[...] = jnp.zeros_like(gcw_ref)

        live_p = jnp.where(i > 0, 1.0, 0.0).astype(F32)
        live_n = jnp.where(i < nt - 1, 1.0, 0.0).astype(F32)
        sh_d[0, pl.ds(0, tm), :] = dc_ref[...].astype(F32)
        sh_d[0, pl.ds(tm, HALO), :] = dcn_ref[...].astype(F32) * live_n
        _preshift(sh_d, tm + HALO - SUBLANES)
        zg = z_ref[:, pl.ds(2 * D, D)].astype(F32)
        za = z_ref[:, pl.ds(D, D)].astype(F32)
        sg = _sig(zg)
        sh_c[0, pl.ds(0, HALO), :] = (zp_ref[:, pl.ds(D, D)].astype(F32)
                                      * _sig(zp_ref[:, pl.ds(2 * D, D)].astype(F32)) * live_p)
        sh_c[0, pl.ds(HALO, tm), :] = za * sg
        _preshift(sh_c, tm + HALO - SUBLANES)

        t = i * tm + lax.broadcasted_iota(jnp.int32, (tm, 1), 0)
        tn = (i + 1) * tm + lax.broadcasted_iota(jnp.int32, (HALO, 1), 0)
        for g, w in enumerate(POOL_WINDOWS):
            sl = pl.ds(g * PG, PG)
            ext_q[pl.ds(0, tm), sl] = dp_ref[:, sl] * (1.0 / jnp.minimum(t + 1, w).astype(F32))
            ext_q[pl.ds(tm, HALO), sl] = dpn_ref[:, sl] * (live_n / jnp.minimum(tn + 1, w).astype(F32))
        for g, w in enumerate(POOL_WINDOWS):
            sl = pl.ds(g * PG, PG)
            s = ext_q[pl.ds(0, tm), sl]
            for j in range(1, w):
                s = s + ext_q[pl.ds(j, tm), sl]
            dz_ref[:, sl] = (s - dp_ref[:, sl]).astype(BF16)

        for r0, l0, rb, lc in _conv_chunks(tm, D):
            ls = pl.ds(l0, lc)
            acc = jnp.zeros((rb, lc), F32)
            for j in range(CONV_K):
                acc = acc + cw_ref[pl.ds(CONV_K - 1 - j, 1), ls] * _window(sh_d, r0 + j, rb, ls)
            dc0_s[pl.ds(r0, rb), ls] = acc
        dc0 = dc0_s[...]
        dz_ref[:, pl.ds(D, D)] = (dc0 * sg).astype(BF16)
        dz_ref[:, pl.ds(2 * D, D)] = (dc0 * za * sg * (1.0 - sg)).astype(BF16)

        for r0, l0, rb, lc in _conv_chunks(tm, D):
            ls = pl.ds(l0, lc)
            dcv = sh_d[0, pl.ds(r0, rb), ls]
            for k in range(CONV_K):
                gcw_ref[pl.ds(8 * k, 8), ls] += _colsum8(dcv * _window(sh_c, r0 + HALO - (CONV_K - 1) + k, rb, ls))

    nxt = lambda i: (jnp.minimum((i + 1) * hb, T // HALO - 1), 0)
    return _launch(
        "mixer_bwd_time", body, nt, [dc1, dc1, dpool, dpool, z, z, cw],
        [_rows(tm, D), pl.BlockSpec((HALO, D), nxt), _rows(tm, D), pl.BlockSpec((HALO, D), nxt),
         _rows(tm, 5 * D), pl.BlockSpec((HALO, 5 * D), lambda i: (jnp.maximum(i * hb - 1, 0), 0)),
         _const((CONV_KP, D))],
        [_rows(tm, 3 * D), _const((CONV_KP * 8, D))],
        [jax.ShapeDtypeStruct((T, 3 * D), BF16), jax.ShapeDtypeStruct((CONV_KP * 8, D), F32)],
        [pltpu.VMEM((SUBLANES, tm + HALO, D), F32), pltpu.VMEM((tm + HALO, D), F32),
         pltpu.VMEM((SUBLANES, HALO + tm, D), F32), pltpu.VMEM((tm, D), F32)], jobs)


def _inproj_bwd(d, h, gain, dzm, dzg, G, lay):
    T, D = h.shape
    NS = lay.NS
    NIN = NDEV * NS
    tm = _tile(T, 512)

    def body(d_ref, h_ref, gain_ref, dzm_ref, dzg_ref, g_hbm, do_ref, u_ref, gg_ref, win, sems):
        @pl.when(pl.program_id(0) == 0)
        def _():
            _fetch(g_hbm, [(lay.off["win"], NS, win)], sems)
            gg_ref[...] = jnp.zeros_like(gg_ref)

        gain_v = gain_ref[...]
        xh, r = _rms(h_ref[...])
        u_ref[...] = (xh * gain_v).astype(BF16)
        dn = jnp.zeros((tm, D), F32)
        for j in range(3):
            dn = dn + _nn(dzm_ref[:, pl.ds(j * D, D)], win[pl.ds(j * D, D), :])
        for j in range(2):
            dn = dn + _nn(dzg_ref[:, pl.ds(j * D, D)], win[pl.ds((3 + j) * D, D), :])
        gg_ref[...] += _colsum8(dn * xh)
        do_ref[...] = d_ref[...] + _rms_bwd(dn, xh, r, gain_v)

    return pl.pallas_call(
        body, name="inproj_bwd", grid=(T // tm,),
        in_specs=[_rows(tm, D), _rows(tm, D), _const((1, D)), _rows(tm, 3 * D), _rows(tm, 2 * D), ANY],
        out_specs=[_rows(tm, D), _rows(tm, D), _const((8, D))],
        out_shape=[jax.ShapeDtypeStruct((T, D), F32), jax.ShapeDtypeStruct((T, D), BF16),
                   jax.ShapeDtypeStruct((8, D), F32)],
        scratch_shapes=[pltpu.VMEM((NIN, D), BF16), pltpu.SemaphoreType.DMA((1, NDEV))],
        compiler_params=_params(),
    )(d, h, gain, dzm, dzg, G)


def _ple_fwd(h, pe, gain, wppt, G, lay, head=None):
    T, D = h.shape
    PD, DS = lay.PD, lay.DS
    tm = _tile(T, 512)

    def body(*refs):
        if head is None:
            h_ref, p_ref, gain_ref, wpp_ref, g_hbm, ho_ref, gate_ref, wpg, sems = refs
        else:
            (h_ref, p_ref, gain_ref, wpp_ref, g_hbm, t_ref, fgain_ref,
             do_ref, dpre_ref, de_ref, n_ref, pb_ref, gg_ref, loss_ref, fg_ref, wpg, sems) = refs

        @pl.when(pl.program_id(0) == 0)
        def _():
            _fetch(g_hbm, [(lay.off["wpg"], DS, wpg)], sems)
            if head is not None:
                for ref in (gg_ref, loss_ref, fg_ref):
                    ref[...] = jnp.zeros_like(ref)

        h = h_ref[...]
        gain_v = gain_ref[...]
        xh, r = _rms(h)
        n = (xh * gain_v).astype(BF16)
        gate = _sig(_nn(n, wpg[...]))
        pb = p_ref[...].astype(BF16)
        e = _nt(pb, wpp_ref[...])
        out = h + gate * e
        if head is None:
            gate_ref[...] = gate.astype(BF16)
            ho_ref[...] = out
            return
        fgain = fgain_ref[...]
        yh, ry = _rms(out)
        err = yh * fgain - t_ref[...]
        loss_ref[...] += _colsum8(err * err)
        dy = err * (1.0 / D)
        fg_ref[...] += _colsum8(dy * yh)
        d = _rms_bwd(dy, yh, ry, fgain)
        n_ref[...] = n
        pb_ref[...] = pb
        de_ref[...] = (d * gate).astype(BF16)
        dpre = (d * e * gate * (1.0 - gate)).astype(BF16)
        dpre_ref[...] = dpre
        dn = _nt(dpre, wpg[...])
        gg_ref[...] += _colsum8(dn * xh)
        do_ref[...] = d + _rms_bwd(dn, xh, r, gain_v)

    args = [h, pe, gain, wppt, G]
    in_specs = [_rows(tm, D), _rows(tm, PD), _const((1, D)), _const((D, PD)), ANY]
    act = jax.ShapeDtypeStruct((T, D), BF16)
    if head is None:
        out_specs = [_rows(tm, D), _rows(tm, D)]
        out_shape = [jax.ShapeDtypeStruct((T, D), F32), act]
    else:
        args += list(head)
        in_specs += [_rows(tm, D), _const((1, D))]
        out_specs = [_rows(tm, D)] * 4 + [_rows(tm, PD)] + [_const((8, D))] * 3
        out_shape = [jax.ShapeDtypeStruct((T, D), F32), act, act, act, jax.ShapeDtypeStruct((T, PD), BF16)]
        out_shape += [jax.ShapeDtypeStruct((8, D), F32)] * 3
    return pl.pallas_call(
        body, name="ple_fwd" if head is None else "ple_head", grid=(T // tm,),
        in_specs=in_specs, out_specs=out_specs, out_shape=out_shape,
        scratch_shapes=[pltpu.VMEM((D, D), BF16), pltpu.SemaphoreType.DMA((1, NDEV))],
        compiler_params=_params(),
    )(*args)


def _ple_bwd(d, h, pe, gate_a, gain, wppt, G, lay):
    T, D = h.shape
    PD, DS = lay.PD, lay.DS
    tm = _tile(T, 512)

    def body(d_ref, h_ref, p_ref, gate_ref, gain_ref, wpp_ref, g_hbm,
             do_ref, dpre_ref, de_ref, n_ref, pb_ref, gg_ref, wpg, sems):
        @pl.when(pl.program_id(0) == 0)
        def _():
            _fetch(g_hbm, [(lay.off["wpg"], DS, wpg)], sems)
            gg_ref[...] = jnp.zeros_like(gg_ref)

        d = d_ref[...]
        gain_v = gain_ref[...]
        xh, r = _rms(h_ref[...])
        n_ref[...] = (xh * gain_v).astype(BF16)
        pb = p_ref[...].astype(BF16)
        pb_ref[...] = pb
        e = _nt(pb, wpp_ref[...])
        gate = gate_ref[...].astype(F32)
        de_ref[...] = (d * gate).astype(BF16)
        dpre = (d * e * gate * (1.0 - gate)).astype(BF16)
        dpre_ref[...] = dpre
        dn = _nt(dpre, wpg[...])
        gg_ref[...] += _colsum8(dn * xh)
        do_ref[...] = d + _rms_bwd(dn, xh, r, gain_v)

    act = jax.ShapeDtypeStruct((T, D), BF16)
    return pl.pallas_call(
        body, name="ple_bwd", grid=(T // tm,),
        in_specs=[_rows(tm, D), _rows(tm, D), _rows(tm, PD), _rows(tm, D), _const((1, D)), _const((D, PD)), ANY],
        out_specs=[_rows(tm, D), _rows(tm, D), _rows(tm, D), _rows(tm, D), _rows(tm, PD), _const((8, D))],
        out_shape=[jax.ShapeDtypeStruct((T, D), F32), act, act, act, jax.ShapeDtypeStruct((T, PD), BF16),
                   jax.ShapeDtypeStruct((8, D), F32)],
        scratch_shapes=[pltpu.VMEM((D, D), BF16), pltpu.SemaphoreType.DMA((1, NDEV))],
        compiler_params=_params(),
    )(d, h, pe, gate_a, gain, wppt, G)


_BIG = ["ffn1_w_gate", "ffn1_w_up", "ffn1_w_down", "w_in", "pool_w", "conv_w_out", "w_out", "ffn2_w_gate",
        "ffn2_w_up", "ffn2_w_down", "ple_w_gate", "ple_w_proj"]
_VECS = ["ffn1_norm", "mix_norm", "pool_scale", "conv_dw_b", "conv_ln_g", "conv_ln_b", "ffn2_norm", "ple_norm"]
_WEIGHTS = ["ffn1_norm", "ffn1_w_gate", "ffn1_w_up", "ffn1_w_down", "mix_norm", "w_in", "pool_w", "pool_scale",
            "conv_dw_w", "conv_dw_b", "conv_ln_g", "conv_ln_b", "conv_w_out", "w_out", "ffn2_norm", "ffn2_w_gate",
            "ffn2_w_up", "ffn2_w_down", "ple_norm", "ple_w_gate", "ple_w_proj", "final_norm"]


def _step(x, p, loss_target, w, mom, var):
    T, D = x.shape[1], x.shape[2]
    L = p.shape[0]
    FS, NS = w["ffn1_w_gate"].shape[2], w["w_in"].shape[2]
    PD = p.shape[3]
    PGS, PG = w["pool_w"].shape[2], w["pool_w"].shape[3]
    CS = w["conv_dw_w"].shape[2]
    lay = _Layout(D, FS, NS, PD, PG, PGS)
    FB = lay.FB
    ax, ay, ac = _place()
    me = 4 * ax + 2 * ay + ac

    assert L == 2, "the exchange schedule below is written for two layers"
    gather = {(l, pc): _GatherJob(f"ag{pc}", _pack_piece(w, l, lay, pc, BF16)) for l in range(L) for pc in PIECES}
    fwd_jobs = {("ffn1", 0): [gather[0, "C"], gather[0, "D"]], ("mixer", 0): [gather[0, "B"]],
                ("ffn2", 0): [gather[1, "A"], gather[1, "D"]], ("ffn1", 1): [gather[1, "C"]],
                ("mixer", 1): [gather[1, "B"]]}
    cw_mine = jnp.pad(w["conv_dw_w"], ((0, 0), (0, CONV_KP - CONV_K), (0, 0)))
    taps_job = _RowsGatherJob("agW", cw_mine.reshape(L * CONV_KP * CS // D, D))
    _run_jobs("gather_first", [gather[0, "A"], taps_job])
    cw_full = taps_job.results[0].reshape(NDEV, L, CONV_KP, CS).transpose(1, 2, 0, 3).reshape(L, CONV_KP, D)

    def gathered(l, pc):
        return gather[l, pc].results[0]

    def small_mats(l):
        G = gathered(l, "D")
        wppt = G[:, lay.off["wpp"]:lay.off["wpp"] + lay.rows["wpp"]].reshape(D, PD)
        pw = G[:, lay.off["pool"]:lay.off["pool"] + lay.rows["pool"]].reshape(NDEV, 4, PGS, PG)
        return wppt, pw.transpose(1, 0, 2, 3).reshape(4, PG, PG)

    def vec(name, l):
        return w[name][l].reshape(1, D)

    h = x[0]
    saved = []
    for l in range(L):
        s = {"h0": h}
        h, s["g1"], s["u1"], s["n1"] = _ffn_fwd(h, vec("ffn1_norm", l), gathered(l, "A"), lay,
                                                fwd_jobs.get(("ffn1", l), ()))
        wppt, pw = small_mats(l)
        s["wppt"], s["pw"], s["h1"] = wppt, pw, h
        s["z"] = _inproj_fwd(h, vec("mix_norm", l), gathered(l, "C"), lay)
        h, s["c1"], s["q"], s["cc"], s["pooled"] = _mixer_fwd(
            s["z"], h, pw, vec("pool_scale", l), cw_full[l], vec("conv_dw_b", l), vec("conv_ln_g", l),
            vec("conv_ln_b", l), gathered(l, "D"), lay, fwd_jobs.get(("mixer", l), ()))
        s["h2"] = h
        h, s["g2"], s["u2"], s["n2"] = _ffn_fwd(h, vec("ffn2_norm", l), gathered(l, "B"), lay,
                                                fwd_jobs.get(("ffn2", l), ()))
        s["h3"] = h
        if l + 1 < L:
            h, s["gate"] = _ple_fwd(h, p[l, 0], vec("ple_norm", l), wppt, gathered(l, "D"), lay)
        else:
            *turned, loss_part, g_final = _ple_fwd(h, p[l, 0], vec("ple_norm", l), wppt, gathered(l, "D"), lay,
                                                    head=(loss_target[0], w["final_norm"].reshape(1, D)))
        saved.append(s)

    scatter = {}
    small_parts = [None] * L

    def send(l, pc, blocks):
        scatter[l, pc] = _ScatterJob(f"rs{pc}", [blocks[n].reshape(NDEV, lay.rows[n], D) for n, _ in lay.pieces[pc]])
        return scatter[l, pc]

    def small_rows():
        return _fold_rows([v for l in range(L) for v in small_parts[l]] + [g_final, loss_part])

    held = []
    for l in reversed(range(L)):
        s = saved[l]
        wppt, pw = s["wppt"], s["pw"]
        g = {}
        if l == L - 1:
            d, dpre, de, n_ple, pb, g_ple = turned
        else:
            d, dpre, de, n_ple, pb, g_ple = _ple_bwd(d, s["h3"], p[l, 0], s["gate"], vec("ple_norm", l), wppt,
                                                     gathered(l, "D"), lay)
        g["wpg"] = _tn_matmul(n_ple, dpre, "tn_sq", D)
        g["wpp"] = _tn_matmul(de, pb, "tn_proj", D)

        n1, n2 = s["n1"], s["n2"]
        d_out = d
        d, dg2, du2, a2, g_n2 = _ffn_bwd(d, s["h2"], vec("ffn2_norm", l), s["g2"], s["u2"], gathered(l, "B"), lay, held)
        held = []
        g["g2"] = _tn_matmul(dg2, n2, "tn_ffn", NDEV * FB)
        g["u2"] = _tn_matmul(du2, n2, "tn_ffn", NDEV * FB)
        g["d2"] = _tn_matmul(a2, d_out, "tn_ffn_d", NDEV * FB, y_scale=0.5)
        held.append(send(l, "B", g))

        (m_b, dcc, c3, dq, dpool, dc1, dzg, g_ps, g_lg, g_lb, g_cb) = _mixer_bwd_rows(
            d, s["z"], s["c1"], s["q"], s["cc"], pw, vec("pool_scale", l), vec("conv_ln_g", l), vec("conv_ln_b", l),
            gathered(l, "D"), lay)
        g["wout"] = _tn_matmul(m_b, d, "tn_sq_d", D)
        g["wco"] = _tn_matmul(c3, dcc, "tn_sq", D)
        g_pool = _tn_groups(s["pooled"], dq, len(POOL_WINDOWS), "tn_pool")
        g["pool"] = g_pool.reshape(4, NDEV, PGS, PG).transpose(1, 0, 2, 3)
        held.append(send(l, "D", g))
        dzm, g_cw = _mixer_bwd_time(dc1, dpool, s["z"], cw_full[l], lay, held)
        held = []
        d, u_b, g_mix = _inproj_bwd(d, s["h1"], vec("mix_norm", l), dzm, dzg, gathered(l, "C"), lay)
        g["win"] = jnp.concatenate([_tn_matmul(dzm, u_b, "tn_in3", 3 * D // 2),
                                    _tn_matmul(dzg, u_b, "tn_in2", D)], axis=0)
        held.append(send(l, "C", g))

        d_out = d
        d, dg1, du1, a1, g_n1 = _ffn_bwd(d, s["h0"], vec("ffn1_norm", l), s["g1"], s["u1"], gathered(l, "A"), lay, held)
        held = []
        small_parts[l] = [g_n1, g_mix, g_ps, g_cb, g_lg, g_lb, g_n2, g_ple, g_cw]
        if l > 0:
            g["g1"] = _tn_matmul(dg1, n1, "tn_ffn", NDEV * FB)
            g["u1"] = _tn_matmul(du1, n1, "tn_ffn", NDEV * FB)
            g["d1"] = _tn_matmul(a1, d_out, "tn_ffn_d", NDEV * FB, y_scale=0.5)
            held.append(send(l, "A", g))
        else:
            rows_job = _RowsGatherJob("agS", small_rows())
            g["g1"] = _tn_matmul(dg1, n1, "tn_ffn", NDEV * FB, jobs=[rows_job])
            g["u1"] = _tn_matmul(du1, n1, "tn_ffn", NDEV * FB, jobs=[send(l, "A1", g)])
            g["d1"] = _tn_matmul(a1, d_out, "tn_ffn_d", NDEV * FB, y_scale=0.5, jobs=[send(l, "A2", g)])
            held.append(send(l, "A3", g))
    _run_jobs("scatter_last", held)
    grad_x = d[None]

    per = [{} for _ in range(L)]
    for (l, pc), job in scatter.items():
        per[l].update(_unpack_piece(_sum_parts(job.results[0], "sum_" + pc), lay, pc))
    grads = {n: jnp.stack([per[l][n] for l in range(L)]) for n in _BIG}

    small_sum = _sum_slots(rows_job.results[0])
    per_layer = len(_VECS) + CONV_KP
    for k, n in enumerate(_VECS):
        grads[n] = jnp.stack([small_sum[l * per_layer + k] for l in range(L)])
    g_cw_full = jnp.stack([small_sum[l * per_layer + len(_VECS):l * per_layer + len(_VECS) + CONV_K]
                           for l in range(L)])
    grads["conv_dw_w"] = lax.dynamic_slice_in_dim(g_cw_full, me * CS, CS, axis=2)
    grads["final_norm"] = small_sum[L * per_layer]
    loss = (0.5 / D) * jnp.sum(small_sum[L * per_layer + 1])

    def as2(a):
        return a.reshape(1, -1) if a.ndim == 1 else a

    deltas, new_m, new_v = {}, {}, {}

    def update(name, names, by_layer):
        res = _adamw(name, [(as2(w[n]), as2(grads[n]), as2(mom[n]), as2(var[n])) for n in names], by_layer)
        for n, (delta, m_new, v_new) in zip(names, res):
            deltas[n], new_m[n], new_v[n] = (a.reshape(w[n].shape) for a in (delta, m_new, v_new))

    for n in ("ffn1_w_gate", "ffn1_w_up", "ffn1_w_down", "ffn2_w_gate", "ffn2_w_up", "ffn2_w_down", "w_in"):
        update("adamw_" + n, [n], True)
    update("adamw_mid", ["conv_w_out", "w_out", "ple_w_gate", "ple_w_proj", "pool_w"], True)
    update("adamw_small", _VECS + ["conv_dw_w", "final_norm"], False)
    return loss, grad_x, (grads, deltas, new_m, new_v)


def kernel(x, p, ffn1_norm, ffn1_w_gate, ffn1_w_up, ffn1_w_down, mix_norm, w_in, pool_w, pool_scale, conv_dw_w, conv_dw_b, conv_ln_g, conv_ln_b, conv_w_out, w_out, ffn2_norm, ffn2_w_gate, ffn2_w_up, ffn2_w_down, ple_norm, ple_w_gate, ple_w_proj, final_norm, loss_target, m_ffn1_norm, m_ffn1_w_gate, m_ffn1_w_up, m_ffn1_w_down, m_mix_norm, m_w_in, m_pool_w, m_pool_scale, m_conv_dw_w, m_conv_dw_b, m_conv_ln_g, m_conv_ln_b, m_conv_w_out, m_w_out, m_ffn2_norm, m_ffn2_w_gate, m_ffn2_w_up, m_ffn2_w_down, m_ple_norm, m_ple_w_gate, m_ple_w_proj, m_final_norm, v_ffn1_norm, v_ffn1_w_gate, v_ffn1_w_up, v_ffn1_w_down, v_mix_norm, v_w_in, v_pool_w, v_pool_scale, v_conv_dw_w, v_conv_dw_b, v_conv_ln_g, v_conv_ln_b, v_conv_w_out, v_w_out, v_ffn2_norm, v_ffn2_w_gate, v_ffn2_w_up, v_ffn2_w_down, v_ple_norm, v_ple_w_gate, v_ple_w_proj, v_final_norm):
    given = dict(locals())
    w = {n: given[n] for n in _WEIGHTS}
    mom = {n: given["m_" + n] for n in _WEIGHTS}
    var = {n: given["v_" + n] for n in _WEIGHTS}
    loss, grad_x, (grads, deltas, new_m, new_v) = _step(x, p, loss_target, w, mom, var)
    out = [loss, grad_x]
    for res in (grads, deltas, new_m, new_v):
        out += [res[n] for n in _WEIGHTS]
    return tuple(out)
```
